```python
import math
import jax, jax.numpy as jnp
from jax import lax
import numpy as np

D_MODEL = 1024
BATCH = 2
SEQ = 8192
DEPTH = 2

N_MIXERS = 2
D_INNER = 2 * D_MODEL
RET_HEADS = 4
RET_QK_DIM = D_MODEL // RET_HEADS
RET_V_DIM = D_INNER // RET_HEADS
RET_CHUNK = 128
RET_IN_COLS = 2 * RET_HEADS * RET_QK_DIM + 2 * D_INNER
DIFF_HEADS = 16
DIFF_HEAD_DIM = D_INNER // DIFF_HEADS // 2
DIFF_V_DIM = 2 * DIFF_HEAD_DIM
Q_BLOCK = 128
DIFF_IN_COLS = 3 * DIFF_HEADS * 2 * DIFF_HEAD_DIM + D_INNER
REL_BUCKETS = 32
REL_MAX_DIST = 128
ROPE_BASE = 10000.0
RMS_EPS = 1e-6
GN_EPS = 1e-5

N_RET_LAYERS = (DEPTH + 1) // 2
N_DIFF_LAYERS = DEPTH // 2

kernel_name = "hybrid_retention_diffattn_trunk"


def rmsnorm(x, g, eps=RMS_EPS):
    xf = x.astype(jnp.float32)
    y = xf * lax.rsqrt(jnp.mean(xf * xf, axis=-1, keepdims=True) + eps)
    return (y * g.astype(jnp.float32)).astype(x.dtype)


def rope(x, pos):
    half = x.shape[-1] // 2
    inv = ROPE_BASE ** (-jnp.arange(half, dtype=jnp.float32) / half)
    ang = pos[:, None] * inv[None, :]
    c, s = jnp.cos(ang), jnp.sin(ang)
    x1, x2 = x[..., :half], x[..., half:]
    return jnp.concatenate([x1 * c - x2 * s, x1 * s + x2 * c], axis=-1)


def retention_mixer(h, w_in, w_out):
    B, S, _ = h.shape
    H, dk, dv, C = RET_HEADS, RET_QK_DIM, RET_V_DIM, RET_CHUNK
    N = S // C
    proj = h @ w_in
    qw = H * dk
    q, k, v, g = jnp.split(proj, [qw, 2 * qw, 2 * qw + D_INNER], axis=-1)
    q = q.reshape(B, S, H, dk).transpose(0, 2, 1, 3).astype(jnp.float32)
    k = k.reshape(B, S, H, dk).transpose(0, 2, 1, 3).astype(jnp.float32)
    v = v.reshape(B, S, H, dv).transpose(0, 2, 1, 3).astype(jnp.float32)
    pos = jnp.arange(S, dtype=jnp.float32)
    q = rope(q, pos)
    k = rope(k, pos) * (dk ** -0.5)
    log_gamma = jnp.log1p(-jnp.exp2(-5.0 - jnp.arange(H, dtype=jnp.float32)))
    idx = jnp.arange(C, dtype=jnp.float32)
    rel = idx[:, None] - idx[None, :]
    dmat = jnp.exp(log_gamma[:, None, None] * jnp.maximum(rel, 0.0)) * (rel >= 0)
    xi = jnp.exp(log_gamma[:, None] * (idx + 1.0))
    zeta = jnp.exp(log_gamma[:, None] * (C - 1.0 - idx))
    chunk_decay = jnp.exp(log_gamma * C)
    qc = q.reshape(B, H, N, C, dk)
    kc = k.reshape(B, H, N, C, dk)
    vc = v.reshape(B, H, N, C, dv)
    scores = jnp.einsum('bhncd,bhnmd->bhncm', qc, kc) * dmat[None, :, None]
    inner = jnp.einsum('bhncm,bhnme->bhnce', scores, vc)

    def step(R, xs):
        qn, kn, vn = xs
        cross = jnp.einsum('bhcd,bhde->bhce', qn, R) * xi[None, :, :, None]
        R_new = R * chunk_decay[None, :, None, None] + jnp.einsum(
            'bhcd,bhce->bhde', kn, vn * zeta[None, :, :, None])
        return R_new, cross

    R0 = jnp.zeros((B, H, dk, dv), jnp.float32)
    xs = (qc.transpose(2, 0, 1, 3, 4), kc.transpose(2, 0, 1, 3, 4), vc.transpose(2, 0, 1, 3, 4))
    _, cross = lax.scan(step, R0, xs)
    ret = (inner + cross.transpose(1, 2, 0, 3, 4)).reshape(B, H, S, dv)
    mu = jnp.mean(ret, axis=-1, keepdims=True)
    var = jnp.mean(jnp.square(ret - mu), axis=-1, keepdims=True)
    ret = (ret - mu) * lax.rsqrt(var + GN_EPS)
    y = ret.transpose(0, 2, 1, 3).reshape(B, S, D_INNER)
    gated = jax.nn.silu(g.astype(jnp.float32)) * y
    return gated.astype(w_out.dtype) @ w_out


def t5_bucket(q_pos, k_pos):
    n = jnp.maximum(q_pos[:, None] - k_pos[None, :], 0)
    max_exact = REL_BUCKETS // 2
    nf = jnp.maximum(n, max_exact).astype(jnp.float32)
    large = max_exact + (jnp.log(nf / max_exact) / math.log(REL_MAX_DIST / max_exact)
                         * (REL_BUCKETS - max_exact)).astype(jnp.int32)
    large = jnp.minimum(large, REL_BUCKETS - 1)
    return jnp.where(n < max_exact, n, large)


def diff_attn_mixer(h, w_in, w_out, lq1, lk1, lq2, lk2, subln_g, rel_bias, lambda_init):
    B, S, _ = h.shape
    H, d, dv = DIFF_HEADS, DIFF_HEAD_DIM, DIFF_V_DIM
    proj = h @ w_in
    qw = H * 2 * d
    q, k, v, g = jnp.split(proj, [qw, 2 * qw, 2 * qw + H * dv], axis=-1)
    q = q.reshape(B, S, H, 2, d).transpose(0, 2, 3, 1, 4).astype(jnp.float32) * (d ** -0.5)
    k = k.reshape(B, S, H, 2, d).transpose(0, 2, 3, 1, 4).astype(jnp.float32)
    v = v.reshape(B, S, H, dv).transpose(0, 2, 1, 3).astype(jnp.float32)
    lam = (jnp.exp(jnp.sum(lq1.astype(jnp.float32) * lk1.astype(jnp.float32)))
           - jnp.exp(jnp.sum(lq2.astype(jnp.float32) * lk2.astype(jnp.float32)))
           + lambda_init)
    NB = S // Q_BLOCK
    qb = q.reshape(B, H, 2, NB, Q_BLOCK, d).transpose(3, 0, 1, 2, 4, 5)
    starts = jnp.arange(NB, dtype=jnp.int32) * Q_BLOCK
    k_pos = jnp.arange(S, dtype=jnp.int32)
    table = rel_bias.astype(jnp.float32)

    def block(args):
        qblk, start = args
        q_pos = start + jnp.arange(Q_BLOCK, dtype=jnp.int32)
        logits = jnp.einsum('bhmqd,bhmkd->bhmqk', qblk, k)
        bias = table[t5_bucket(q_pos, k_pos)].transpose(2, 0, 1)
        logits = logits + bias[None, :, None]
        mask = k_pos[None, :] <= q_pos[:, None]
        logits = jnp.where(mask, logits, -jnp.inf)
        probs = jax.nn.softmax(logits, axis=-1)
        attn = probs[:, :, 0] - lam * probs[:, :, 1]
        return jnp.einsum('bhqk,bhke->bhqe', attn, v)

    out = lax.map(block, (qb, starts))
    out = out.transpose(1, 2, 0, 3, 4).reshape(B, H, S, dv)
    out = rmsnorm(out, subln_g, eps=GN_EPS) * (1.0 - lambda_init)
    y = out.transpose(0, 2, 1, 3).reshape(B, S, D_INNER)
    gated = jax.nn.silu(g.astype(jnp.float32)) * y
    return gated.astype(w_out.dtype) @ w_out


def setup_inputs(seed: int = 0) -> dict:
    key = jax.random.key(seed)
    ks = jax.random.split(key, 14)
    f32 = jnp.float32
    x = jax.random.normal(ks[0], (BATCH, SEQ, D_MODEL), f32)
    pre_norm_g = 1.0 + 0.02 * jax.random.normal(ks[1], (DEPTH, D_MODEL), f32)
    post_norm_g = 1.0 + 0.02 * jax.random.normal(ks[2], (DEPTH, D_MODEL), f32)
    ret_w_in = jax.random.normal(ks[3], (N_RET_LAYERS, D_MODEL, RET_IN_COLS), f32) * D_MODEL ** -0.5
    ret_w_out = jax.random.normal(ks[4], (N_RET_LAYERS, D_INNER, D_MODEL), f32) * D_INNER ** -0.5
    diff_w_in = jax.random.normal(ks[5], (N_DIFF_LAYERS, D_MODEL, DIFF_IN_COLS), f32) * D_MODEL ** -0.5
    diff_w_out = jax.random.normal(ks[6], (N_DIFF_LAYERS, D_INNER, D_MODEL), f32) * D_INNER ** -0.5
    diff_lambda_q1 = 0.1 * jax.random.normal(ks[7], (N_DIFF_LAYERS, DIFF_HEAD_DIM), f32)
    diff_lambda_k1 = 0.1 * jax.random.normal(ks[8], (N_DIFF_LAYERS, DIFF_HEAD_DIM), f32)
    diff_lambda_q2 = 0.1 * jax.random.normal(ks[9], (N_DIFF_LAYERS, DIFF_HEAD_DIM), f32)
    diff_lambda_k2 = 0.1 * jax.random.normal(ks[10], (N_DIFF_LAYERS, DIFF_HEAD_DIM), f32)
    diff_subln_g = 1.0 + 0.02 * jax.random.normal(ks[11], (N_DIFF_LAYERS, DIFF_V_DIM), f32)
    rel_bias = 0.5 * jax.random.normal(ks[12], (REL_BUCKETS, DIFF_HEADS), f32)
    return {"x": x, "pre_norm_g": pre_norm_g, "post_norm_g": post_norm_g,
            "ret_w_in": ret_w_in, "ret_w_out": ret_w_out,
            "diff_w_in": diff_w_in, "diff_w_out": diff_w_out,
            "diff_lambda_q1": diff_lambda_q1, "diff_lambda_k1": diff_lambda_k1,
            "diff_lambda_q2": diff_lambda_q2, "diff_lambda_k2": diff_lambda_k2,
            "diff_subln_g": diff_subln_g, "rel_bias": rel_bias}


def reference(x, pre_norm_g, post_norm_g, ret_w_in, ret_w_out, diff_w_in, diff_w_out,
              diff_lambda_q1, diff_lambda_k1, diff_lambda_q2, diff_lambda_k2,
              diff_subln_g, rel_bias):
    for i in range(DEPTH):
        h = rmsnorm(x, pre_norm_g[i])
        j = i // N_MIXERS
        if i % N_MIXERS == 0:
            y = retention_mixer(h, ret_w_in[j], ret_w_out[j])
        else:
            lambda_init = 0.8 - 0.6 * math.exp(-0.3 * i)
            y = diff_attn_mixer(h, diff_w_in[j], diff_w_out[j],
                                diff_lambda_q1[j], diff_lambda_k1[j],
                                diff_lambda_q2[j], diff_lambda_k2[j],
                                diff_subln_g[j], rel_bias, lambda_init)
        x = x + rmsnorm(y, post_norm_g[i])
    return x
```

```python
import functools
import math

import jax
import jax.numpy as jnp
from jax import lax
from jax.experimental import pallas as pl
from jax.experimental.pallas import tpu as pltpu

F32 = jnp.float32
BF16 = jnp.bfloat16

DEPTH = 2
N_MIXERS = 2
RET_HEADS = 4
RET_CHUNK = 128
DIFF_HEADS = 16
REL_BUCKETS = 32
REL_MAX_DIST = 128
ROPE_BASE = 10000.0
RMS_EPS = 1e-6
GN_EPS = 1e-5

V7X_VMEM_BYTES = 64 * 1024 * 1024
VMEM_LIMIT_BYTES = V7X_VMEM_BYTES * 3 // 4

PROJ_ROWS = 1024
PROJ_COLS = 1024
OUT_ROWS = 512
RET_CHUNKS_PER_STEP = 4
ATT_TILE = 256


def _compiler_params(semantics):
    return pltpu.CompilerParams(dimension_semantics=semantics,
                                vmem_limit_bytes=VMEM_LIMIT_BYTES)


def _norm_matmul_kernel(x_ref, g_ref, w_ref, o_ref, h_ref, *, transpose_out):
    @pl.when(pl.program_id(2) == 0)
    def _():
        x = x_ref[0]
        ms = jnp.mean(x * x, axis=-1, keepdims=True)
        h_ref[...] = (x * lax.rsqrt(ms + RMS_EPS) * g_ref[...]).astype(BF16)

    h = h_ref[...]
    if transpose_out:
        o = lax.dot_general(w_ref[...], h, (((1,), (1,)), ((), ())),
                            preferred_element_type=F32)
    else:
        o = jnp.dot(h, w_ref[...], preferred_element_type=F32)
    o_ref[0] = o.astype(o_ref.dtype)


def _norm_matmul(x, g, w, *, transpose_out):
    B, S, D = x.shape
    N = w.shape[0] if transpose_out else w.shape[1]
    ts, tn = PROJ_ROWS, PROJ_COLS
    grid = (B, S // ts, N // tn)
    if transpose_out:
        w_spec = pl.BlockSpec((tn, D), lambda b, i, j: (j, 0))
        o_spec = pl.BlockSpec((1, tn, ts), lambda b, i, j: (b, j, i))
        o_shape = jax.ShapeDtypeStruct((B, N, S), BF16)
    else:
        w_spec = pl.BlockSpec((D, tn), lambda b, i, j: (0, j))
        o_spec = pl.BlockSpec((1, ts, tn), lambda b, i, j: (b, i, j))
        o_shape = jax.ShapeDtypeStruct((B, S, N), BF16)
    return pl.pallas_call(
        functools.partial(_norm_matmul_kernel, transpose_out=transpose_out),
        grid=grid,
        in_specs=[pl.BlockSpec((1, ts, D), lambda b, i, j: (b, i, 0)),
                  pl.BlockSpec((1, D), lambda b, i, j: (0, 0)),
                  w_spec],
        out_specs=o_spec,
        out_shape=o_shape,
        scratch_shapes=[pltpu.VMEM((ts, D), BF16)],
        compiler_params=_compiler_params(("parallel", "parallel", "arbitrary")),
        name="norm_matmul_t" if transpose_out else "norm_matmul",
    )(x, g.reshape(1, D), w)


def _out_proj_kernel(y_ref, w_ref, g_ref, x_ref, o_ref, *, transposed_in):
    y = y_ref[0]
    if transposed_in:
        o = lax.dot_general(y, w_ref[...], (((0,), (0,)), ((), ())),
                            preferred_element_type=F32)
    else:
        o = jnp.dot(y, w_ref[...], preferred_element_type=F32)
    ms = jnp.mean(o * o, axis=-1, keepdims=True)
    o_ref[0] = x_ref[0] + o * lax.rsqrt(ms + RMS_EPS) * g_ref[...]


def _out_proj(y, w, g, x, *, transposed_in):
    B, S, D = x.shape
    Di = w.shape[0]
    ts = OUT_ROWS
    if transposed_in:
        y_spec = pl.BlockSpec((1, Di, ts), lambda b, i: (b, 0, i))
    else:
        y_spec = pl.BlockSpec((1, ts, Di), lambda b, i: (b, i, 0))
    return pl.pallas_call(
        functools.partial(_out_proj_kernel, transposed_in=transposed_in),
        grid=(B, S // ts),
        in_specs=[y_spec,
                  pl.BlockSpec((Di, D), lambda b, i: (0, 0)),
                  pl.BlockSpec((1, D), lambda b, i: (0, 0)),
                  pl.BlockSpec((1, ts, D), lambda b, i: (b, i, 0))],
        out_specs=pl.BlockSpec((1, ts, D), lambda b, i: (b, i, 0)),
        out_shape=jax.ShapeDtypeStruct((B, S, D), F32),
        compiler_params=_compiler_params(("parallel", "parallel")),
        name="out_proj_t" if transposed_in else "out_proj",
    )(y, w, g.reshape(1, D), x)


def _retention_kernel(q_ref, k_ref, v_ref, g_ref, cos_ref, sin_ref, dmat_ref,
                      xi_ref, zeta_ref, cdec_ref, o_ref, state_ref, *, dk):
    C = RET_CHUNK
    half = dk // 2

    @pl.when(pl.program_id(2) == 0)
    def _():
        state_ref[...] = jnp.zeros_like(state_ref)

    dmat = dmat_ref[0]
    xi = xi_ref[0]
    zeta = zeta_ref[0]
    cdec = cdec_ref[0]

    def rope(t, c, s):
        t1, t2 = t[:, :half], t[:, half:]
        return jnp.concatenate([t1 * c - t2 * s, t1 * s + t2 * c], axis=-1)

    for ci in range(RET_CHUNKS_PER_STEP):
        rows = pl.ds(ci * C, C)
        c, s = cos_ref[rows, :], sin_ref[rows, :]
        q = rope(q_ref[0, rows, :].astype(F32), c, s).astype(BF16)
        k = (rope(k_ref[0, rows, :].astype(F32), c, s) * (dk ** -0.5)).astype(BF16)
        v = v_ref[0, rows, :]
        state = state_ref[...]
        scores = lax.dot_general(q, k, (((1,), (1,)), ((), ())),
                                 preferred_element_type=F32) * dmat
        inner = jnp.dot(scores.astype(BF16), v, preferred_element_type=F32)
        cross = jnp.dot(q, state.astype(BF16), preferred_element_type=F32) * xi
        vz = (v.astype(F32) * zeta).astype(BF16)
        state_ref[...] = state * cdec + lax.dot_general(
            k, vz, (((0,), (0,)), ((), ())), preferred_element_type=F32)
        ret = inner + cross
        mu = jnp.mean(ret, axis=-1, keepdims=True)
        cen = ret - mu
        var = jnp.mean(cen * cen, axis=-1, keepdims=True)
        y = cen * lax.rsqrt(var + GN_EPS)
        gate = g_ref[0, rows, :].astype(F32)
        o_ref[0, rows, :] = (gate * jax.nn.sigmoid(gate) * y).astype(o_ref.dtype)


def _retention(proj, d_model, d_inner):
    B, S, _ = proj.shape
    H, C = RET_HEADS, RET_CHUNK
    dk, dv = d_model // H, d_inner // H
    half = dk // 2
    rows = C * RET_CHUNKS_PER_STEP
    k_off = d_model // dk
    v_off = 2 * d_model // dv
    g_off = (2 * d_model + d_inner) // dv

    pos = jnp.arange(S, dtype=F32)
    inv = ROPE_BASE ** (-jnp.arange(half, dtype=F32) / half)
    ang = pos[:, None] * inv[None, :]
    cos, sin = jnp.cos(ang), jnp.sin(ang)

    log_gamma = jnp.log1p(-jnp.exp2(-5.0 - jnp.arange(H, dtype=F32)))
    idx = jnp.arange(C, dtype=F32)
    rel = idx[:, None] - idx[None, :]
    dmat = jnp.exp(log_gamma[:, None, None] * jnp.maximum(rel, 0.0)) * (rel >= 0)
    xi = jnp.exp(log_gamma[:, None] * (idx + 1.0))
    zeta = jnp.exp(log_gamma[:, None] * (C - 1.0 - idx))
    cdec = jnp.exp(log_gamma * C)
    xi = jnp.broadcast_to(xi[:, :, None], (H, C, dv))
    zeta = jnp.broadcast_to(zeta[:, :, None], (H, C, dv))
    cdec = jnp.broadcast_to(cdec[:, None, None], (H, 1, dv))

    return pl.pallas_call(
        functools.partial(_retention_kernel, dk=dk),
        grid=(B, H, S // rows),
        in_specs=[
            pl.BlockSpec((1, rows, dk), lambda b, h, n: (b, n, h)),
            pl.BlockSpec((1, rows, dk), lambda b, h, n: (b, n, k_off + h)),
            pl.BlockSpec((1, rows, dv), lambda b, h, n: (b, n, v_off + h)),
            pl.BlockSpec((1, rows, dv), lambda b, h, n: (b, n, g_off + h)),
            pl.BlockSpec((rows, half), lambda b, h, n: (n, 0)),
            pl.BlockSpec((rows, half), lambda b, h, n: (n, 0)),
            pl.BlockSpec((1, C, C), lambda b, h, n: (h, 0, 0)),
            pl.BlockSpec((1, C, dv), lambda b, h, n: (h, 0, 0)),
            pl.BlockSpec((1, C, dv), lambda b, h, n: (h, 0, 0)),
            pl.BlockSpec((1, 1, dv), lambda b, h, n: (h, 0, 0)),
        ],
        out_specs=pl.BlockSpec((1, rows, dv), lambda b, h, n: (b, n, h)),
        out_shape=jax.ShapeDtypeStruct((B, S, d_inner), BF16),
        scratch_shapes=[pltpu.VMEM((dk, dv), F32)],
        compiler_params=_compiler_params(("parallel", "parallel", "arbitrary")),
        name="retention",
    )(proj, proj, proj, proj, cos, sin, dmat, xi, zeta, cdec)


def _diff_attn_kernel(cfar_ref, q_ref, k_ref, vt_ref, gt_ref, bias0_ref,
                      bias1_ref, lq1_ref, lk1_ref, lq2_ref, lk2_ref, sg_ref,
                      o_ref, qbd_ref, m_ref, l_ref, acc_ref, *, d, lambda_init):
    T = ATT_TILE
    i = pl.program_id(2)

    q = q_ref[0].astype(F32) * (d ** -0.5)
    lane = lax.broadcasted_iota(jnp.int32, q.shape, 1)
    qbd_ref[0:T, :] = jnp.where(lane < d, q, 0.0).astype(BF16)
    qbd_ref[T:2 * T, :] = jnp.where(lane >= d, q, 0.0).astype(BF16)
    m_ref[...] = jnp.full_like(m_ref, -jnp.inf)
    l_ref[...] = jnp.zeros_like(l_ref)
    acc_ref[...] = jnp.zeros_like(acc_ref)

    def tile(j, bias):
        start = pl.multiple_of(j * T, T)
        kt = k_ref[0, pl.ds(start, T), :]
        s = lax.dot_general(kt, qbd_ref[...], (((1,), (1,)), ((), ())),
                            preferred_element_type=F32) + bias
        m_old = m_ref[...]
        m_new = jnp.maximum(m_old, jnp.max(s, axis=0, keepdims=True))
        alpha = jnp.exp(m_old - m_new)
        p = jnp.exp(s - m_new)
        l_ref[...] = alpha * l_ref[...] + jnp.sum(p, axis=0, keepdims=True)
        vt = vt_ref[0, :, pl.ds(start, T)]
        acc_ref[...] = acc_ref[...] * alpha + jnp.dot(
            vt, p.astype(BF16), preferred_element_type=F32)
        m_ref[...] = m_new

    cfar = cfar_ref[pl.program_id(1)]

    def far_body(j, carry):
        tile(j, cfar)
        return carry

    lax.fori_loop(0, i - 1, far_body, 0)

    @pl.when(i >= 1)
    def _():
        b1 = bias1_ref[0]
        tile(i - 1, jnp.concatenate([b1, b1], axis=1))

    b0 = bias0_ref[0]
    tile(i, jnp.concatenate([b0, b0], axis=1))

    lam = (jnp.exp(jnp.sum(lq1_ref[...] * lk1_ref[...], keepdims=True))
           - jnp.exp(jnp.sum(lq2_ref[...] * lk2_ref[...], keepdims=True))
           + lambda_init)
    l = l_ref[...]
    acc = acc_ref[...]
    o = acc[:, :T] / l[:, :T] - lam * (acc[:, T:] / l[:, T:])
    ms = jnp.mean(o * o, axis=0, keepdims=True)
    on = o * lax.rsqrt(ms + GN_EPS) * sg_ref[...] * (1.0 - lambda_init)
    gate = gt_ref[0].astype(F32)
    o_ref[0] = (gate * jax.nn.sigmoid(gate) * on).astype(o_ref.dtype)


def _t5_bucket(n):
    max_exact = REL_BUCKETS // 2
    nf = jnp.maximum(n, max_exact).astype(F32)
    large = max_exact + (jnp.log(nf / max_exact) / math.log(REL_MAX_DIST / max_exact)
                         * (REL_BUCKETS - max_exact)).astype(jnp.int32)
    large = jnp.minimum(large, REL_BUCKETS - 1)
    return jnp.where(n < max_exact, n, large)


def _diff_attn(qk, vgt, lq1, lk1, lq2, lk2, subln_g, rel_bias, lambda_init):
    B, S, _ = qk.shape
    H, T = DIFF_HEADS, ATT_TILE
    d_inner = vgt.shape[1] // 2
    dv = d_inner // H
    d = dv // 2

    table = rel_bias.astype(F32)
    kk = jnp.arange(T, dtype=jnp.int32)[:, None]
    qq = jnp.arange(T, dtype=jnp.int32)[None, :]
    n0 = qq - kk
    bias0 = table[_t5_bucket(jnp.maximum(n0, 0))].transpose(2, 0, 1)
    bias0 = jnp.where((n0 >= 0)[None], bias0, -jnp.inf)
    bias1 = table[_t5_bucket(n0 + T)].transpose(2, 0, 1)
    cfar = table[_t5_bucket(jnp.full((), T + 1, jnp.int32))]

    smem = pl.BlockSpec(memory_space=pltpu.SMEM)
    vec = pl.BlockSpec((1, d), lambda b, h, i: (0, 0))
    return pl.pallas_call(
        functools.partial(_diff_attn_kernel, d=d, lambda_init=lambda_init),
        grid=(B, H, S // T),
        in_specs=[
            smem,
            pl.BlockSpec((1, T, dv), lambda b, h, i: (b, i, h)),
            pl.BlockSpec((1, S, dv), lambda b, h, i: (b, 0, H + h)),
            pl.BlockSpec((1, dv, S), lambda b, h, i: (b, h, 0)),
            pl.BlockSpec((1, dv, T), lambda b, h, i: (b, H + h, i)),
            pl.BlockSpec((1, T, T), lambda b, h, i: (h, 0, 0)),
            pl.BlockSpec((1, T, T), lambda b, h, i: (h, 0, 0)),
            vec, vec, vec, vec,
            pl.BlockSpec((dv, 1), lambda b, h, i: (0, 0)),
        ],
        out_specs=pl.BlockSpec((1, dv, T), lambda b, h, i: (b, h, i)),
        out_shape=jax.ShapeDtypeStruct((B, d_inner, S), BF16),
        scratch_shapes=[pltpu.VMEM((2 * T, dv), BF16),
                        pltpu.VMEM((1, 2 * T), F32),
                        pltpu.VMEM((1, 2 * T), F32),
                        pltpu.VMEM((dv, 2 * T), F32)],
        compiler_params=_compiler_params(("parallel", "parallel", "arbitrary")),
        name="diff_attn",
    )(cfar, qk, qk, vgt, vgt, bias0, bias1,
      lq1.reshape(1, d), lk1.reshape(1, d), lq2.reshape(1, d), lk2.reshape(1, d),
      subln_g.reshape(dv, 1))


def kernel(x, pre_norm_g, post_norm_g, ret_w_in, ret_w_out, diff_w_in, diff_w_out,
           diff_lambda_q1, diff_lambda_k1, diff_lambda_q2, diff_lambda_k2,
           diff_subln_g, rel_bias):
    d_model = x.shape[-1]
    d_inner = ret_w_out.shape[1]
    for i in range(DEPTH):
        j = i // N_MIXERS
        if i % N_MIXERS == 0:
            proj = _norm_matmul(x, pre_norm_g[i], ret_w_in[j].astype(BF16),
                                transpose_out=False)
            y = _retention(proj, d_model, d_inner)
            x = _out_proj(y, ret_w_out[j].astype(BF16), post_norm_g[i], x,
                          transposed_in=False)
        else:
            lambda_init = 0.8 - 0.6 * math.exp(-0.3 * i)
            w = diff_w_in[j]
            qk = _norm_matmul(x, pre_norm_g[i], w[:, :2 * d_inner].astype(BF16),
                              transpose_out=False)
            vgt = _norm_matmul(x, pre_norm_g[i], w[:, 2 * d_inner:].T.astype(BF16),
                               transpose_out=True)
            yt = _diff_attn(qk, vgt, diff_lambda_q1[j], diff_lambda_k1[j],
                            diff_lambda_q2[j], diff_lambda_k2[j], diff_subln_g[j],
                            rel_bias, lambda_init)
            x = _out_proj(yt, diff_w_out[j].astype(BF16), post_norm_g[i], x,
                          transposed_in=True)
    return x
```

```python
import functools
import math

import jax
import jax.numpy as jnp
from jax import lax
from jax.experimental import pallas as pl
from jax.experimental.pallas import tpu as pltpu

F32 = jnp.float32
BF16 = jnp.bfloat16

DEPTH = 2
N_MIXERS = 2
RET_HEADS = 4
RET_CHUNK = 128
DIFF_HEADS = 16
REL_BUCKETS = 32
REL_MAX_DIST = 128
ROPE_BASE = 10000.0
RMS_EPS = 1e-6
GN_EPS = 1e-5

V7X_VMEM_BYTES = 64 * 1024 * 1024
VMEM_LIMIT_BYTES = V7X_VMEM_BYTES * 3 // 4

PROJ_ROWS = 1024
PROJ_COLS = 1024
OUT_ROWS = 512
RET_CHUNKS_PER_STEP = 4
ATT_TILE = 256
ATT_HEADS_PER_STEP = 4
ONES_ROWS = 16
LOG2E = math.log2(math.e)


def _compiler_params(semantics):
    return pltpu.CompilerParams(dimension_semantics=semantics,
                                vmem_limit_bytes=VMEM_LIMIT_BYTES)


def _norm_matmul_kernel(x_ref, g_ref, w_ref, o_ref, h_ref, *, transpose_out):
    @pl.when(pl.program_id(2) == 0)
    def _():
        x = x_ref[0]
        ms = jnp.mean(x * x, axis=-1, keepdims=True)
        h_ref[...] = (x * lax.rsqrt(ms + RMS_EPS) * g_ref[...]).astype(BF16)

    h = h_ref[...]
    if transpose_out:
        o = lax.dot_general(w_ref[...], h, (((1,), (1,)), ((), ())),
                            preferred_element_type=F32)
    else:
        o = jnp.dot(h, w_ref[...], preferred_element_type=F32)
    o_ref[0] = o.astype(o_ref.dtype)


def _norm_matmul(x, g, w, *, transpose_out):
    B, S, D = x.shape
    N = w.shape[0] if transpose_out else w.shape[1]
    ts, tn = PROJ_ROWS, PROJ_COLS
    grid = (B, S // ts, N // tn)
    if transpose_out:
        w_spec = pl.BlockSpec((tn, D), lambda b, i, j: (j, 0))
        o_spec = pl.BlockSpec((1, tn, ts), lambda b, i, j: (b, j, i))
        o_shape = jax.ShapeDtypeStruct((B, N, S), BF16)
    else:
        w_spec = pl.BlockSpec((D, tn), lambda b, i, j: (0, j))
        o_spec = pl.BlockSpec((1, ts, tn), lambda b, i, j: (b, i, j))
        o_shape = jax.ShapeDtypeStruct((B, S, N), BF16)
    return pl.pallas_call(
        functools.partial(_norm_matmul_kernel, transpose_out=transpose_out),
        grid=grid,
        in_specs=[pl.BlockSpec((1, ts, D), lambda b, i, j: (b, i, 0)),
                  pl.BlockSpec((1, D), lambda b, i, j: (0, 0)),
                  w_spec],
        out_specs=o_spec,
        out_shape=o_shape,
        scratch_shapes=[pltpu.VMEM((ts, D), BF16)],
        compiler_params=_compiler_params(("parallel", "parallel", "arbitrary")),
        name="norm_matmul_t" if transpose_out else "norm_matmul",
    )(x, g.reshape(1, D), w)


def _out_proj_kernel(y_ref, w_ref, g_ref, x_ref, o_ref, *, transposed_in):
    y = y_ref[0]
    if transposed_in:
        o = lax.dot_general(y, w_ref[...], (((0,), (0,)), ((), ())),
                            preferred_element_type=F32)
    else:
        o = jnp.dot(y, w_ref[...], preferred_element_type=F32)
    ms = jnp.mean(o * o, axis=-1, keepdims=True)
    o_ref[0] = x_ref[0] + o * lax.rsqrt(ms + RMS_EPS) * g_ref[...]


def _out_proj(y, w, g, x, *, transposed_in):
    B, S, D = x.shape
    Di = w.shape[0]
    ts = OUT_ROWS
    if transposed_in:
        y_spec = pl.BlockSpec((1, Di, ts), lambda b, i: (b, 0, i))
    else:
        y_spec = pl.BlockSpec((1, ts, Di), lambda b, i: (b, i, 0))
    return pl.pallas_call(
        functools.partial(_out_proj_kernel, transposed_in=transposed_in),
        grid=(B, S // ts),
        in_specs=[y_spec,
                  pl.BlockSpec((Di, D), lambda b, i: (0, 0)),
                  pl.BlockSpec((1, D), lambda b, i: (0, 0)),
                  pl.BlockSpec((1, ts, D), lambda b, i: (b, i, 0))],
        out_specs=pl.BlockSpec((1, ts, D), lambda b, i: (b, i, 0)),
        out_shape=jax.ShapeDtypeStruct((B, S, D), F32),
        compiler_params=_compiler_params(("parallel", "parallel")),
        name="out_proj_t" if transposed_in else "out_proj",
    )(y, w, g.reshape(1, D), x)


def _retention_kernel(q_ref, k_ref, v_ref, g_ref, cos_ref, sin_ref, dmat_ref,
                      xi_ref, zeta_ref, cdec_ref, o_ref, state_ref, *, dk):
    C = RET_CHUNK
    half = dk // 2

    @pl.when(pl.program_id(2) == 0)
    def _():
        state_ref[...] = jnp.zeros_like(state_ref)

    dmat = dmat_ref[0]
    xi = xi_ref[0]
    zeta = zeta_ref[0]
    cdec = cdec_ref[0]

    def rope(t, c, s):
        t1, t2 = t[:, :half], t[:, half:]
        return jnp.concatenate([t1 * c - t2 * s, t1 * s + t2 * c], axis=-1)

    for ci in range(RET_CHUNKS_PER_STEP):
        rows = pl.ds(ci * C, C)
        c, s = cos_ref[rows, :], sin_ref[rows, :]
        q = rope(q_ref[0, rows, :].astype(F32), c, s).astype(BF16)
        k = (rope(k_ref[0, rows, :].astype(F32), c, s) * (dk ** -0.5)).astype(BF16)
        v = v_ref[0, rows, :]
        state = state_ref[...]
        scores = lax.dot_general(q, k, (((1,), (1,)), ((), ())),
                                 preferred_element_type=F32) * dmat
        inner = jnp.dot(scores.astype(BF16), v, preferred_element_type=F32)
        cross = jnp.dot(q, state.astype(BF16), preferred_element_type=F32) * xi
        vz = (v.astype(F32) * zeta).astype(BF16)
        state_ref[...] = state * cdec + lax.dot_general(
            k, vz, (((0,), (0,)), ((), ())), preferred_element_type=F32)
        ret = inner + cross
        mu = jnp.mean(ret, axis=-1, keepdims=True)
        cen = ret - mu
        var = jnp.mean(cen * cen, axis=-1, keepdims=True)
        y = cen * lax.rsqrt(var + GN_EPS)
        gate = g_ref[0, rows, :].astype(F32)
        o_ref[0, rows, :] = (gate * jax.nn.sigmoid(gate) * y).astype(o_ref.dtype)


def _retention(proj, d_model, d_inner):
    B, S, _ = proj.shape
    H, C = RET_HEADS, RET_CHUNK
    dk, dv = d_model // H, d_inner // H
    half = dk // 2
    rows = C * RET_CHUNKS_PER_STEP
    k_off = d_model // dk
    v_off = 2 * d_model // dv
    g_off = (2 * d_model + d_inner) // dv

    pos = jnp.arange(S, dtype=F32)
    inv = ROPE_BASE ** (-jnp.arange(half, dtype=F32) / half)
    ang = pos[:, None] * inv[None, :]
    cos, sin = jnp.cos(ang), jnp.sin(ang)

    log_gamma = jnp.log1p(-jnp.exp2(-5.0 - jnp.arange(H, dtype=F32)))
    idx = jnp.arange(C, dtype=F32)
    rel = idx[:, None] - idx[None, :]
    dmat = jnp.exp(log_gamma[:, None, None] * jnp.maximum(rel, 0.0)) * (rel >= 0)
    xi = jnp.exp(log_gamma[:, None] * (idx + 1.0))
    zeta = jnp.exp(log_gamma[:, None] * (C - 1.0 - idx))
    cdec = jnp.exp(log_gamma * C)
    xi = jnp.broadcast_to(xi[:, :, None], (H, C, dv))
    zeta = jnp.broadcast_to(zeta[:, :, None], (H, C, dv))
    cdec = jnp.broadcast_to(cdec[:, None, None], (H, 1, dv))

    return pl.pallas_call(
        functools.partial(_retention_kernel, dk=dk),
        grid=(B, H, S // rows),
        in_specs=[
            pl.BlockSpec((1, rows, dk), lambda b, h, n: (b, n, h)),
            pl.BlockSpec((1, rows, dk), lambda b, h, n: (b, n, k_off + h)),
            pl.BlockSpec((1, rows, dv), lambda b, h, n: (b, n, v_off + h)),
            pl.BlockSpec((1, rows, dv), lambda b, h, n: (b, n, g_off + h)),
            pl.BlockSpec((rows, half), lambda b, h, n: (n, 0)),
            pl.BlockSpec((rows, half), lambda b, h, n: (n, 0)),
            pl.BlockSpec((1, C, C), lambda b, h, n: (h, 0, 0)),
            pl.BlockSpec((1, C, dv), lambda b, h, n: (h, 0, 0)),
            pl.BlockSpec((1, C, dv), lambda b, h, n: (h, 0, 0)),
            pl.BlockSpec((1, 1, dv), lambda b, h, n: (h, 0, 0)),
        ],
        out_specs=pl.BlockSpec((1, rows, dv), lambda b, h, n: (b, n, h)),
        out_shape=jax.ShapeDtypeStruct((B, S, d_inner), BF16),
        scratch_shapes=[pltpu.VMEM((dk, dv), F32)],
        compiler_params=_compiler_params(("parallel", "parallel", "arbitrary")),
        name="retention",
    )(proj, proj, proj, proj, cos, sin, dmat, xi, zeta, cdec)


def _diff_attn_kernel(tab_ref, bfar_ref, q_ref, k_ref, vt_ref, gt_ref, bk0_ref,
                      bk1_ref, lq1_ref, lk1_ref, lq2_ref, lk2_ref, sg_ref, o_ref,
                      qbd_ref, vaug_ref, bias_ref, m_ref, acc_ref,
                      *, d, lambda_init):
    T, G, H = ATT_TILE, ATT_HEADS_PER_STEP, DIFF_HEADS
    dv = 2 * d
    S = vt_ref.shape[2]
    hp = pl.program_id(1)
    i = pl.program_id(2)

    @pl.when(i == 0)
    def _():
        rows = lax.broadcasted_iota(jnp.int32, (T, T), 0)
        cols = lax.broadcasted_iota(jnp.int32, (T, T), 1)
        for u in range(G):
            h = hp * G + u
            vaug_ref[u, 0:dv, :] = vt_ref[0, u * dv:(u + 1) * dv, :]
            vaug_ref[u, dv:dv + ONES_ROWS, :] = jnp.ones((ONES_ROWS, S), BF16)
            cfar = tab_ref[bfar_ref[0] * H + h]
            for t, bk_ref in enumerate((bk0_ref, bk1_ref)):
                bk = bk_ref[...]
                b = jnp.zeros((T, T), F32)
                for bucket in range(REL_BUCKETS):
                    b = jnp.where(bk == bucket, tab_ref[bucket * H + h], b)
                b = (b - cfar) * LOG2E
                if t == 0:
                    b = jnp.where(rows <= cols, b, -jnp.inf)
                bias_ref[u, t] = b

    for u in range(G):
        q = q_ref[0, :, u * dv:(u + 1) * dv].astype(F32) * (d ** -0.5 * LOG2E)
        lane = lax.broadcasted_iota(jnp.int32, q.shape, 1)
        qbd_ref[u, 0:T, :] = jnp.where(lane < d, q, 0.0).astype(BF16)
        qbd_ref[u, T:2 * T, :] = jnp.where(lane >= d, q, 0.0).astype(BF16)
    m_ref[...] = jnp.full_like(m_ref, -jnp.inf)
    acc_ref[...] = jnp.zeros_like(acc_ref)

    def tiles(j, bias_index):
        start = pl.multiple_of(j * T, T)
        logits = []
        for u in range(G):
            kt = k_ref[0, pl.ds(start, T), u * dv:(u + 1) * dv]
            logits.append(lax.dot_general(
                kt, qbd_ref[u], (((1,), (1,)), ((), ())),
                preferred_element_type=F32))
        for u in range(G):
            s = logits[u]
            if bias_index is not None:
                bias = bias_ref[u, bias_index]
                s = s + jnp.concatenate([bias, bias], axis=1)
            m_old = m_ref[u]
            m_new = jnp.maximum(m_old, jnp.max(s, axis=0, keepdims=True))
            p = jnp.exp2(s - m_new).astype(BF16)
            alpha = jnp.exp2(m_old - m_new)
            pv = jnp.dot(vaug_ref[u, :, pl.ds(start, T)], p,
                         preferred_element_type=F32)
            acc_ref[u] = acc_ref[u] * alpha + pv
            m_ref[u] = m_new

    def far_body(j, carry):
        tiles(j, None)
        return carry

    lax.fori_loop(0, i - 1, far_body, 0)

    @pl.when(i >= 1)
    def _():
        tiles(i - 1, 1)

    tiles(i, 0)

    lam = (jnp.exp(jnp.sum(lq1_ref[...] * lk1_ref[...], keepdims=True))
           - jnp.exp(jnp.sum(lq2_ref[...] * lk2_ref[...], keepdims=True))
           + lambda_init)
    for u in range(G):
        acc = acc_ref[u]
        l = acc[dv:dv + 1, :]
        o = acc[:dv, :T] / l[:, :T] - lam * (acc[:dv, T:] / l[:, T:])
        ms = jnp.mean(o * o, axis=0, keepdims=True)
        on = o * lax.rsqrt(ms + GN_EPS) * sg_ref[...] * (1.0 - lambda_init)
        gate = gt_ref[0, u * dv:(u + 1) * dv, :].astype(F32)
        o_ref[0, u * dv:(u + 1) * dv, :] = (
            gate * jax.nn.sigmoid(gate) * on).astype(o_ref.dtype)


def _t5_bucket(n):
    max_exact = REL_BUCKETS // 2
    nf = jnp.maximum(n, max_exact).astype(F32)
    large = max_exact + (jnp.log(nf / max_exact) / math.log(REL_MAX_DIST / max_exact)
                         * (REL_BUCKETS - max_exact)).astype(jnp.int32)
    large = jnp.minimum(large, REL_BUCKETS - 1)
    return jnp.where(n < max_exact, n, large)


def _diff_attn(qk, vgt, lq1, lk1, lq2, lk2, subln_g, rel_bias, lambda_init):
    B, S, _ = qk.shape
    H, T, G = DIFF_HEADS, ATT_TILE, ATT_HEADS_PER_STEP
    d_inner = vgt.shape[1] // 2
    dv = d_inner // H
    d = dv // 2
    gw = G * dv
    n_groups = H // G

    kk = jnp.arange(T, dtype=jnp.int32)[:, None]
    qq = jnp.arange(T, dtype=jnp.int32)[None, :]
    bk0 = _t5_bucket(jnp.maximum(qq - kk, 0))
    bk1 = _t5_bucket(qq - kk + T)
    bfar = _t5_bucket(jnp.full((1,), T + 1, jnp.int32))

    smem = pl.BlockSpec(memory_space=pltpu.SMEM)
    vec = pl.BlockSpec((1, d), lambda b, h, i: (0, 0))
    tile_spec = pl.BlockSpec((T, T), lambda b, h, i: (0, 0))
    return pl.pallas_call(
        functools.partial(_diff_attn_kernel, d=d, lambda_init=lambda_init),
        grid=(B, n_groups, S // T),
        in_specs=[
            smem, smem,
            pl.BlockSpec((1, T, gw), lambda b, h, i: (b, i, h)),
            pl.BlockSpec((1, S, gw), lambda b, h, i: (b, 0, n_groups + h),
                         pipeline_mode=pl.Buffered(1)),
            pl.BlockSpec((1, gw, S), lambda b, h, i: (b, h, 0),
                         pipeline_mode=pl.Buffered(1)),
            pl.BlockSpec((1, gw, T), lambda b, h, i: (b, n_groups + h, i)),
            tile_spec, tile_spec,
            vec, vec, vec, vec,
            pl.BlockSpec((dv, 1), lambda b, h, i: (0, 0)),
        ],
        out_specs=pl.BlockSpec((1, gw, T), lambda b, h, i: (b, h, i)),
        out_shape=jax.ShapeDtypeStruct((B, d_inner, S), BF16),
        scratch_shapes=[pltpu.VMEM((G, 2 * T, dv), BF16),
                        pltpu.VMEM((G, dv + ONES_ROWS, S), BF16),
                        pltpu.VMEM((G, 2, T, T), F32),
                        pltpu.VMEM((G, 1, 2 * T), F32),
                        pltpu.VMEM((G, dv + ONES_ROWS, 2 * T), F32)],
        compiler_params=_compiler_params(("parallel", "arbitrary", "arbitrary")),
        name="diff_attn",
    )(rel_bias.astype(F32).reshape(-1), bfar, qk, qk, vgt, vgt, bk0, bk1,
      lq1.reshape(1, d), lk1.reshape(1, d), lq2.reshape(1, d), lk2.reshape(1, d),
      subln_g.reshape(dv, 1))


def kernel(x, pre_norm_g, post_norm_g, ret_w_in, ret_w_out, diff_w_in, diff_w_out,
           diff_lambda_q1, diff_lambda_k1, diff_lambda_q2, diff_lambda_k2,
           diff_subln_g, rel_bias):
    d_model = x.shape[-1]
    d_inner = ret_w_out.shape[1]
    for i in range(DEPTH):
        j = i // N_MIXERS
        if i % N_MIXERS == 0:
            proj = _norm_matmul(x, pre_norm_g[i], ret_w_in[j].astype(BF16),
                                transpose_out=False)
            y = _retention(proj, d_model, d_inner)
            x = _out_proj(y, ret_w_out[j].astype(BF16), post_norm_g[i], x,
                          transposed_in=False)
        else:
            lambda_init = 0.8 - 0.6 * math.exp(-0.3 * i)
            w = diff_w_in[j]
            qk = _norm_matmul(x, pre_norm_g[i], w[:, :2 * d_inner].astype(BF16),
                              transpose_out=False)
            vgt = _norm_matmul(x, pre_norm_g[i], w[:, 2 * d_inner:].T.astype(BF16),
                               transpose_out=True)
            yt = _diff_attn(qk, vgt, diff_lambda_q1[j], diff_lambda_k1[j],
                            diff_lambda_q2[j], diff_lambda_k2[j], diff_subln_g[j],
                            rel_bias, lambda_init)
            x = _out_proj(yt, diff_w_out[j].astype(BF16), post_norm_g[i], x,
                          transposed_in=True)
    return x
```

```python
import functools
import math

import jax
import jax.numpy as jnp
from jax import lax
from jax.experimental import pallas as pl
from jax.experimental.pallas import tpu as pltpu

F32 = jnp.float32
BF16 = jnp.bfloat16

DEPTH = 2
N_MIXERS = 2
RET_HEADS = 4
RET_CHUNK = 128
DIFF_HEADS = 16
REL_BUCKETS = 32
REL_MAX_DIST = 128
ROPE_BASE = 10000.0
RMS_EPS = 1e-6
GN_EPS = 1e-5

V7X_VMEM_BYTES = 64 * 1024 * 1024
VMEM_LIMIT_BYTES = V7X_VMEM_BYTES * 3 // 4

PROJ_ROWS = 1024
PROJ_COLS = 1024
OUT_ROWS = 512
RET_CHUNKS_PER_STEP = 4
ATT_TILE = 256
ATT_HEADS_PER_STEP = 4
ONES_ROWS = 16
LOG2E = math.log2(math.e)


def _compiler_params(semantics):
    return pltpu.CompilerParams(dimension_semantics=semantics,
                                vmem_limit_bytes=VMEM_LIMIT_BYTES)


def _norm_matmul_kernel(x_ref, g_ref, w_ref, o_ref, h_ref, *, transpose_out):
    @pl.when(pl.program_id(2) == 0)
    def _():
        x = x_ref[0]
        ms = jnp.mean(x * x, axis=-1, keepdims=True)
        h_ref[...] = (x * lax.rsqrt(ms + RMS_EPS) * g_ref[...]).astype(BF16)

    h = h_ref[...]
    if transpose_out:
        o = lax.dot_general(w_ref[...], h, (((1,), (1,)), ((), ())),
                            preferred_element_type=F32)
    else:
        o = jnp.dot(h, w_ref[...], preferred_element_type=F32)
    o_ref[0] = o.astype(o_ref.dtype)


def _norm_matmul(x, g, w, *, transpose_out):
    B, S, D = x.shape
    N = w.shape[0] if transpose_out else w.shape[1]
    ts, tn = PROJ_ROWS, PROJ_COLS
    grid = (B, S // ts, N // tn)
    if transpose_out:
        w_spec = pl.BlockSpec((tn, D), lambda b, i, j: (j, 0))
        o_spec = pl.BlockSpec((1, tn, ts), lambda b, i, j: (b, j, i))
        o_shape = jax.ShapeDtypeStruct((B, N, S), BF16)
    else:
        w_spec = pl.BlockSpec((D, tn), lambda b, i, j: (0, j))
        o_spec = pl.BlockSpec((1, ts, tn), lambda b, i, j: (b, i, j))
        o_shape = jax.ShapeDtypeStruct((B, S, N), BF16)
    return pl.pallas_call(
        functools.partial(_norm_matmul_kernel, transpose_out=transpose_out),
        grid=grid,
        in_specs=[pl.BlockSpec((1, ts, D), lambda b, i, j: (b, i, 0)),
                  pl.BlockSpec((1, D), lambda b, i, j: (0, 0)),
                  w_spec],
        out_specs=o_spec,
        out_shape=o_shape,
        scratch_shapes=[pltpu.VMEM((ts, D), BF16)],
        compiler_params=_compiler_params(("parallel", "parallel", "arbitrary")),
        name="norm_matmul_t" if transpose_out else "norm_matmul",
    )(x, g.reshape(1, D), w)


def _out_proj_kernel(y_ref, w_ref, g_ref, x_ref, o_ref, *, transposed_in):
    y = y_ref[0]
    if transposed_in:
        o = lax.dot_general(y, w_ref[...], (((0,), (0,)), ((), ())),
                            preferred_element_type=F32)
    else:
        o = jnp.dot(y, w_ref[...], preferred_element_type=F32)
    ms = jnp.mean(o * o, axis=-1, keepdims=True)
    o_ref[0] = x_ref[0] + o * lax.rsqrt(ms + RMS_EPS) * g_ref[...]


def _out_proj(y, w, g, x, *, transposed_in):
    B, S, D = x.shape
    Di = w.shape[0]
    ts = OUT_ROWS
    if transposed_in:
        y_spec = pl.BlockSpec((1, Di, ts), lambda b, i: (b, 0, i))
    else:
        y_spec = pl.BlockSpec((1, ts, Di), lambda b, i: (b, i, 0))
    return pl.pallas_call(
        functools.partial(_out_proj_kernel, transposed_in=transposed_in),
        grid=(B, S // ts),
        in_specs=[y_spec,
                  pl.BlockSpec((Di, D), lambda b, i: (0, 0)),
                  pl.BlockSpec((1, D), lambda b, i: (0, 0)),
                  pl.BlockSpec((1, ts, D), lambda b, i: (b, i, 0))],
        out_specs=pl.BlockSpec((1, ts, D), lambda b, i: (b, i, 0)),
        out_shape=jax.ShapeDtypeStruct((B, S, D), F32),
        compiler_params=_compiler_params(("parallel", "parallel")),
        name="out_proj_t" if transposed_in else "out_proj",
    )(y, w, g.reshape(1, D), x)


def _retention_kernel(q_ref, k_ref, v_ref, g_ref, cos_ref, sin_ref, dmat_ref,
                      xi_ref, zeta_ref, cdec_ref, o_ref, state_ref, *, dk):
    C = RET_CHUNK
    half = dk // 2

    @pl.when(pl.program_id(2) == 0)
    def _():
        state_ref[...] = jnp.zeros_like(state_ref)

    dmat = dmat_ref[0]
    xi = xi_ref[0]
    zeta = zeta_ref[0]
    cdec = cdec_ref[0]

    def rope(t, c, s):
        t1, t2 = t[:, :half], t[:, half:]
        return jnp.concatenate([t1 * c - t2 * s, t1 * s + t2 * c], axis=-1)

    for ci in range(RET_CHUNKS_PER_STEP):
        rows = pl.ds(ci * C, C)
        c, s = cos_ref[rows, :], sin_ref[rows, :]
        q = rope(q_ref[0, rows, :].astype(F32), c, s).astype(BF16)
        k = (rope(k_ref[0, rows, :].astype(F32), c, s) * (dk ** -0.5)).astype(BF16)
        v = v_ref[0, rows, :]
        state = state_ref[...]
        scores = lax.dot_general(q, k, (((1,), (1,)), ((), ())),
                                 preferred_element_type=F32) * dmat
        inner = jnp.dot(scores.astype(BF16), v, preferred_element_type=F32)
        cross = jnp.dot(q, state.astype(BF16), preferred_element_type=F32) * xi
        vz = (v.astype(F32) * zeta).astype(BF16)
        state_ref[...] = state * cdec + lax.dot_general(
            k, vz, (((0,), (0,)), ((), ())), preferred_element_type=F32)
        ret = inner + cross
        mu = jnp.mean(ret, axis=-1, keepdims=True)
        cen = ret - mu
        var = jnp.mean(cen * cen, axis=-1, keepdims=True)
        y = cen * lax.rsqrt(var + GN_EPS)
        gate = g_ref[0, rows, :].astype(F32)
        o_ref[0, rows, :] = (gate * jax.nn.sigmoid(gate) * y).astype(o_ref.dtype)


def _retention(proj, d_model, d_inner):
    B, S, _ = proj.shape
    H, C = RET_HEADS, RET_CHUNK
    dk, dv = d_model // H, d_inner // H
    half = dk // 2
    rows = C * RET_CHUNKS_PER_STEP
    k_off = d_model // dk
    v_off = 2 * d_model // dv
    g_off = (2 * d_model + d_inner) // dv

    pos = jnp.arange(S, dtype=F32)
    inv = ROPE_BASE ** (-jnp.arange(half, dtype=F32) / half)
    ang = pos[:, None] * inv[None, :]
    cos, sin = jnp.cos(ang), jnp.sin(ang)

    log_gamma = jnp.log1p(-jnp.exp2(-5.0 - jnp.arange(H, dtype=F32)))
    idx = jnp.arange(C, dtype=F32)
    rel = idx[:, None] - idx[None, :]
    dmat = jnp.exp(log_gamma[:, None, None] * jnp.maximum(rel, 0.0)) * (rel >= 0)
    xi = jnp.exp(log_gamma[:, None] * (idx + 1.0))
    zeta = jnp.exp(log_gamma[:, None] * (C - 1.0 - idx))
    cdec = jnp.exp(log_gamma * C)
    xi = jnp.broadcast_to(xi[:, :, None], (H, C, dv))
    zeta = jnp.broadcast_to(zeta[:, :, None], (H, C, dv))
    cdec = jnp.broadcast_to(cdec[:, None, None], (H, 1, dv))

    return pl.pallas_call(
        functools.partial(_retention_kernel, dk=dk),
        grid=(B, H, S // rows),
        in_specs=[
            pl.BlockSpec((1, rows, dk), lambda b, h, n: (b, n, h)),
            pl.BlockSpec((1, rows, dk), lambda b, h, n: (b, n, k_off + h)),
            pl.BlockSpec((1, rows, dv), lambda b, h, n: (b, n, v_off + h)),
            pl.BlockSpec((1, rows, dv), lambda b, h, n: (b, n, g_off + h)),
            pl.BlockSpec((rows, half), lambda b, h, n: (n, 0)),
            pl.BlockSpec((rows, half), lambda b, h, n: (n, 0)),
            pl.BlockSpec((1, C, C), lambda b, h, n: (h, 0, 0)),
            pl.BlockSpec((1, C, dv), lambda b, h, n: (h, 0, 0)),
            pl.BlockSpec((1, C, dv), lambda b, h, n: (h, 0, 0)),
            pl.BlockSpec((1, 1, dv), lambda b, h, n: (h, 0, 0)),
        ],
        out_specs=pl.BlockSpec((1, rows, dv), lambda b, h, n: (b, n, h)),
        out_shape=jax.ShapeDtypeStruct((B, S, d_inner), BF16),
        scratch_shapes=[pltpu.VMEM((dk, dv), F32)],
        compiler_params=_compiler_params(("parallel", "parallel", "arbitrary")),
        name="retention",
    )(proj, proj, proj, proj, cos, sin, dmat, xi, zeta, cdec)


def _diff_attn_kernel(tab_ref, bfar_ref, q_ref, k_ref, vt_ref, gt_ref, bk0_ref,
                      bk1_ref, lq1_ref, lk1_ref, lq2_ref, lk2_ref, sg_ref, o_ref,
                      qbd_ref, vaug_ref, bias_ref, m_ref, acc_ref,
                      *, d, lambda_init):
    T, G, H = ATT_TILE, ATT_HEADS_PER_STEP, DIFF_HEADS
    dv = 2 * d
    S = vt_ref.shape[2]
    hp = pl.program_id(1)
    i = pl.program_id(2)

    @pl.when(i == 0)
    def _():
        rows = lax.broadcasted_iota(jnp.int32, (T, T), 0)
        cols = lax.broadcasted_iota(jnp.int32, (T, T), 1)
        for u in range(G):
            h = hp * G + u
            vaug_ref[u, 0:dv, :] = vt_ref[0, u * dv:(u + 1) * dv, :]
            vaug_ref[u, dv:dv + ONES_ROWS, :] = jnp.ones((ONES_ROWS, S), BF16)
            cfar = tab_ref[bfar_ref[0] * H + h]
            for t, bk_ref in enumerate((bk0_ref, bk1_ref)):
                bk = bk_ref[...]
                b = jnp.zeros((T, T), F32)
                for bucket in range(REL_BUCKETS):
                    b = jnp.where(bk == bucket, tab_ref[bucket * H + h], b)
                b = (b - cfar) * LOG2E
                if t == 0:
                    b = jnp.where(rows <= cols, b, -jnp.inf)
                bias_ref[u, t] = b

    for u in range(G):
        q = q_ref[0, :, u * dv:(u + 1) * dv].astype(F32) * (d ** -0.5 * LOG2E)
        qt = q.T
        row = lax.broadcasted_iota(jnp.int32, qt.shape, 0)
        qbd_ref[u, :, 0:T] = jnp.where(row < d, qt, 0.0).astype(BF16)
        qbd_ref[u, :, T:2 * T] = jnp.where(row >= d, qt, 0.0).astype(BF16)
    m_ref[...] = jnp.full_like(m_ref, -jnp.inf)
    acc_ref[...] = jnp.zeros_like(acc_ref)

    def tiles(j, bias_index):
        start = pl.multiple_of(j * T, T)
        logits = []
        for u in range(G):
            kt = k_ref[0, pl.ds(start, T), u * dv:(u + 1) * dv]
            logits.append(jnp.dot(kt, qbd_ref[u],
                                  preferred_element_type=F32))
        for u in range(G):
            s = logits[u]
            if bias_index is not None:
                bias = bias_ref[u, bias_index]
                s = s + jnp.concatenate([bias, bias], axis=1)
            m_old = m_ref[u]
            m_new = jnp.maximum(m_old, jnp.max(s, axis=0, keepdims=True))
            p = jnp.exp2(s - m_new).astype(BF16)
            alpha = jnp.exp2(m_old - m_new)
            pv = jnp.dot(vaug_ref[u, :, pl.ds(start, T)], p,
                         preferred_element_type=F32)
            acc_ref[u] = acc_ref[u] * alpha + pv
            m_ref[u] = m_new

    def far_body(j, carry):
        tiles(j, None)
        return carry

    lax.fori_loop(0, i - 1, far_body, 0)

    @pl.when(i >= 1)
    def _():
        tiles(i - 1, 1)

    tiles(i, 0)

    lam = (jnp.exp(jnp.sum(lq1_ref[...] * lk1_ref[...], keepdims=True))
           - jnp.exp(jnp.sum(lq2_ref[...] * lk2_ref[...], keepdims=True))
           + lambda_init)
    for u in range(G):
        acc = acc_ref[u]
        l = acc[dv:dv + 1, :]
        o = acc[:dv, :T] / l[:, :T] - lam * (acc[:dv, T:] / l[:, T:])
        ms = jnp.mean(o * o, axis=0, keepdims=True)
        on = o * lax.rsqrt(ms + GN_EPS) * sg_ref[...] * (1.0 - lambda_init)
        gate = gt_ref[0, u * dv:(u + 1) * dv, :].astype(F32)
        o_ref[0, u * dv:(u + 1) * dv, :] = (
            gate * jax.nn.sigmoid(gate) * on).astype(o_ref.dtype)


def _t5_bucket(n):
    max_exact = REL_BUCKETS // 2
    nf = jnp.maximum(n, max_exact).astype(F32)
    large = max_exact + (jnp.log(nf / max_exact) / math.log(REL_MAX_DIST / max_exact)
                         * (REL_BUCKETS - max_exact)).astype(jnp.int32)
    large = jnp.minimum(large, REL_BUCKETS - 1)
    return jnp.where(n < max_exact, n, large)


def _diff_attn(qk, vgt, lq1, lk1, lq2, lk2, subln_g, rel_bias, lambda_init):
    B, S, _ = qk.shape
    H, T, G = DIFF_HEADS, ATT_TILE, ATT_HEADS_PER_STEP
    d_inner = vgt.shape[1] // 2
    dv = d_inner // H
    d = dv // 2
    gw = G * dv
    n_groups = H // G

    kk = jnp.arange(T, dtype=jnp.int32)[:, None]
    qq = jnp.arange(T, dtype=jnp.int32)[None, :]
    bk0 = _t5_bucket(jnp.maximum(qq - kk, 0))
    bk1 = _t5_bucket(qq - kk + T)
    bfar = _t5_bucket(jnp.full((1,), T + 1, jnp.int32))

    smem = pl.BlockSpec(memory_space=pltpu.SMEM)
    vec = pl.BlockSpec((1, d), lambda b, h, i: (0, 0))
    tile_spec = pl.BlockSpec((T, T), lambda b, h, i: (0, 0))
    return pl.pallas_call(
        functools.partial(_diff_attn_kernel, d=d, lambda_init=lambda_init),
        grid=(B, n_groups, S // T),
        in_specs=[
            smem, smem,
            pl.BlockSpec((1, T, gw), lambda b, h, i: (b, i, h)),
            pl.BlockSpec((1, S, gw), lambda b, h, i: (b, 0, n_groups + h),
                         pipeline_mode=pl.Buffered(1)),
            pl.BlockSpec((1, gw, S), lambda b, h, i: (b, h, 0),
                         pipeline_mode=pl.Buffered(1)),
            pl.BlockSpec((1, gw, T), lambda b, h, i: (b, n_groups + h, i)),
            tile_spec, tile_spec,
            vec, vec, vec, vec,
            pl.BlockSpec((dv, 1), lambda b, h, i: (0, 0)),
        ],
        out_specs=pl.BlockSpec((1, gw, T), lambda b, h, i: (b, h, i)),
        out_shape=jax.ShapeDtypeStruct((B, d_inner, S), BF16),
        scratch_shapes=[pltpu.VMEM((G, dv, 2 * T), BF16),
                        pltpu.VMEM((G, dv + ONES_ROWS, S), BF16),
                        pltpu.VMEM((G, 2, T, T), F32),
                        pltpu.VMEM((G, 1, 2 * T), F32),
                        pltpu.VMEM((G, dv + ONES_ROWS, 2 * T), F32)],
        compiler_params=_compiler_params(("parallel", "arbitrary", "arbitrary")),
        name="diff_attn",
    )(rel_bias.astype(F32).reshape(-1), bfar, qk, qk, vgt, vgt, bk0, bk1,
      lq1.reshape(1, d), lk1.reshape(1, d), lq2.reshape(1, d), lk2.reshape(1, d),
      subln_g.reshape(dv, 1))


def kernel(x, pre_norm_g, post_norm_g, ret_w_in, ret_w_out, diff_w_in, diff_w_out,
           diff_lambda_q1, diff_lambda_k1, diff_lambda_q2, diff_lambda_k2,
           diff_subln_g, rel_bias):
    d_model = x.shape[-1]
    d_inner = ret_w_out.shape[1]
    for i in range(DEPTH):
        j = i // N_MIXERS
        if i % N_MIXERS == 0:
            proj = _norm_matmul(x, pre_norm_g[i], ret_w_in[j].astype(BF16),
                                transpose_out=False)
            y = _retention(proj, d_model, d_inner)
            x = _out_proj(y, ret_w_out[j].astype(BF16), post_norm_g[i], x,
                          transposed_in=False)
        else:
            lambda_init = 0.8 - 0.6 * math.exp(-0.3 * i)
            w = diff_w_in[j]
            qk = _norm_matmul(x, pre_norm_g[i], w[:, :2 * d_inner].astype(BF16),
                              transpose_out=False)
            vgt = _norm_matmul(x, pre_norm_g[i], w[:, 2 * d_inner:].T.astype(BF16),
                               transpose_out=True)
            yt = _diff_attn(qk, vgt, diff_lambda_q1[j], diff_lambda_k1[j],
                            diff_lambda_q2[j], diff_lambda_k2[j], diff_subln_g[j],
                            rel_bias, lambda_init)
            x = _out_proj(yt, diff_w_out[j].astype(BF16), post_norm_g[i], x,
                          transposed_in=True)
    return x
```

```python
import functools
import math

import jax
import jax.numpy as jnp
from jax import lax
from jax.experimental import pallas as pl
from jax.experimental.pallas import tpu as pltpu

F32 = jnp.float32
BF16 = jnp.bfloat16

DEPTH = 2
N_MIXERS = 2
RET_HEADS = 4
RET_CHUNK = 128
DIFF_HEADS = 16
REL_BUCKETS = 32
REL_MAX_DIST = 128
ROPE_BASE = 10000.0
RMS_EPS = 1e-6
GN_EPS = 1e-5

V7X_VMEM_BYTES = 64 * 1024 * 1024
VMEM_LIMIT_BYTES = V7X_VMEM_BYTES * 3 // 4

PROJ_ROWS = 1024
PROJ_COLS = 1024
OUT_ROWS = 512
RET_CHUNKS_PER_STEP = 4
ATT_TILE = 256
ATT_HEADS_PER_STEP = 4
ONES_ROWS = 16
LOG2E = math.log2(math.e)


def _compiler_params(semantics):
    return pltpu.CompilerParams(dimension_semantics=semantics,
                                vmem_limit_bytes=VMEM_LIMIT_BYTES)


def _norm_matmul_kernel(x_ref, g_ref, w_ref, o_ref, h_ref, *, transpose_out):
    @pl.when(pl.program_id(2) == 0)
    def _():
        x = x_ref[0]
        ms = jnp.mean(x * x, axis=-1, keepdims=True)
        h_ref[...] = (x * lax.rsqrt(ms + RMS_EPS) * g_ref[...]).astype(BF16)

    h = h_ref[...]
    if transpose_out:
        o = lax.dot_general(w_ref[...], h, (((1,), (1,)), ((), ())),
                            preferred_element_type=F32)
    else:
        o = jnp.dot(h, w_ref[...], preferred_element_type=F32)
    o_ref[0] = o.astype(o_ref.dtype)


def _norm_matmul(x, g, w, *, transpose_out):
    B, S, D = x.shape
    N = w.shape[0] if transpose_out else w.shape[1]
    ts, tn = PROJ_ROWS, PROJ_COLS
    grid = (B, S // ts, N // tn)
    if transpose_out:
        w_spec = pl.BlockSpec((tn, D), lambda b, i, j: (j, 0))
        o_spec = pl.BlockSpec((1, tn, ts), lambda b, i, j: (b, j, i))
        o_shape = jax.ShapeDtypeStruct((B, N, S), BF16)
    else:
        w_spec = pl.BlockSpec((D, tn), lambda b, i, j: (0, j))
        o_spec = pl.BlockSpec((1, ts, tn), lambda b, i, j: (b, i, j))
        o_shape = jax.ShapeDtypeStruct((B, S, N), BF16)
    return pl.pallas_call(
        functools.partial(_norm_matmul_kernel, transpose_out=transpose_out),
        grid=grid,
        in_specs=[pl.BlockSpec((1, ts, D), lambda b, i, j: (b, i, 0)),
                  pl.BlockSpec((1, D), lambda b, i, j: (0, 0)),
                  w_spec],
        out_specs=o_spec,
        out_shape=o_shape,
        scratch_shapes=[pltpu.VMEM((ts, D), BF16)],
        compiler_params=_compiler_params(("parallel", "parallel", "arbitrary")),
        name="norm_matmul_t" if transpose_out else "norm_matmul",
    )(x, g.reshape(1, D), w)


def _out_proj_kernel(y_ref, w_ref, g_ref, x_ref, o_ref, *, transposed_in):
    y = y_ref[0]
    if transposed_in:
        o = lax.dot_general(y, w_ref[...], (((0,), (0,)), ((), ())),
                            preferred_element_type=F32)
    else:
        o = jnp.dot(y, w_ref[...], preferred_element_type=F32)
    ms = jnp.mean(o * o, axis=-1, keepdims=True)
    o_ref[0] = x_ref[0] + o * lax.rsqrt(ms + RMS_EPS) * g_ref[...]


def _out_proj(y, w, g, x, *, transposed_in):
    B, S, D = x.shape
    Di = w.shape[0]
    ts = OUT_ROWS
    if transposed_in:
        y_spec = pl.BlockSpec((1, Di, ts), lambda b, i: (b, 0, i))
    else:
        y_spec = pl.BlockSpec((1, ts, Di), lambda b, i: (b, i, 0))
    return pl.pallas_call(
        functools.partial(_out_proj_kernel, transposed_in=transposed_in),
        grid=(B, S // ts),
        in_specs=[y_spec,
                  pl.BlockSpec((Di, D), lambda b, i: (0, 0)),
                  pl.BlockSpec((1, D), lambda b, i: (0, 0)),
                  pl.BlockSpec((1, ts, D), lambda b, i: (b, i, 0))],
        out_specs=pl.BlockSpec((1, ts, D), lambda b, i: (b, i, 0)),
        out_shape=jax.ShapeDtypeStruct((B, S, D), F32),
        compiler_params=_compiler_params(("parallel", "parallel")),
        name="out_proj_t" if transposed_in else "out_proj",
    )(y, w, g.reshape(1, D), x)


def _retention_kernel(q_ref, k_ref, v_ref, g_ref, cos_ref, sin_ref, dmat_ref,
                      xi_ref, zeta_ref, cdec_ref, o_ref, state_ref, *, dk):
    C = RET_CHUNK
    half = dk // 2

    @pl.when(pl.program_id(2) == 0)
    def _():
        state_ref[...] = jnp.zeros_like(state_ref)

    dmat = dmat_ref[0]
    xi = xi_ref[0]
    zeta = zeta_ref[0]
    cdec = cdec_ref[0]

    def rope(t, c, s):
        t1, t2 = t[:, :half], t[:, half:]
        return jnp.concatenate([t1 * c - t2 * s, t1 * s + t2 * c], axis=-1)

    for ci in range(RET_CHUNKS_PER_STEP):
        rows = pl.ds(ci * C, C)
        c, s = cos_ref[rows, :], sin_ref[rows, :]
        q = rope(q_ref[0, rows, :].astype(F32), c, s).astype(BF16)
        k = (rope(k_ref[0, rows, :].astype(F32), c, s) * (dk ** -0.5)).astype(BF16)
        v = v_ref[0, rows, :]
        state = state_ref[...]
        scores = lax.dot_general(q, k, (((1,), (1,)), ((), ())),
                                 preferred_element_type=F32) * dmat
        inner = jnp.dot(scores.astype(BF16), v, preferred_element_type=F32)
        cross = jnp.dot(q, state.astype(BF16), preferred_element_type=F32) * xi
        vz = (v.astype(F32) * zeta).astype(BF16)
        state_ref[...] = state * cdec + lax.dot_general(
            k, vz, (((0,), (0,)), ((), ())), preferred_element_type=F32)
        ret = inner + cross
        mu = jnp.mean(ret, axis=-1, keepdims=True)
        cen = ret - mu
        var = jnp.mean(cen * cen, axis=-1, keepdims=True)
        y = cen * lax.rsqrt(var + GN_EPS)
        gate = g_ref[0, rows, :].astype(F32)
        o_ref[0, rows, :] = (gate * jax.nn.sigmoid(gate) * y).astype(o_ref.dtype)


def _retention(proj, d_model, d_inner):
    B, S, _ = proj.shape
    H, C = RET_HEADS, RET_CHUNK
    dk, dv = d_model // H, d_inner // H
    half = dk // 2
    rows = C * RET_CHUNKS_PER_STEP
    k_off = d_model // dk
    v_off = 2 * d_model // dv
    g_off = (2 * d_model + d_inner) // dv

    pos = jnp.arange(S, dtype=F32)
    inv = ROPE_BASE ** (-jnp.arange(half, dtype=F32) / half)
    ang = pos[:, None] * inv[None, :]
    cos, sin = jnp.cos(ang), jnp.sin(ang)

    log_gamma = jnp.log1p(-jnp.exp2(-5.0 - jnp.arange(H, dtype=F32)))
    idx = jnp.arange(C, dtype=F32)
    rel = idx[:, None] - idx[None, :]
    dmat = jnp.exp(log_gamma[:, None, None] * jnp.maximum(rel, 0.0)) * (rel >= 0)
    xi = jnp.exp(log_gamma[:, None] * (idx + 1.0))
    zeta = jnp.exp(log_gamma[:, None] * (C - 1.0 - idx))
    cdec = jnp.exp(log_gamma * C)
    xi = jnp.broadcast_to(xi[:, :, None], (H, C, dv))
    zeta = jnp.broadcast_to(zeta[:, :, None], (H, C, dv))
    cdec = jnp.broadcast_to(cdec[:, None, None], (H, 1, dv))

    return pl.pallas_call(
        functools.partial(_retention_kernel, dk=dk),
        grid=(B, H, S // rows),
        in_specs=[
            pl.BlockSpec((1, rows, dk), lambda b, h, n: (b, n, h)),
            pl.BlockSpec((1, rows, dk), lambda b, h, n: (b, n, k_off + h)),
            pl.BlockSpec((1, rows, dv), lambda b, h, n: (b, n, v_off + h)),
            pl.BlockSpec((1, rows, dv), lambda b, h, n: (b, n, g_off + h)),
            pl.BlockSpec((rows, half), lambda b, h, n: (n, 0)),
            pl.BlockSpec((rows, half), lambda b, h, n: (n, 0)),
            pl.BlockSpec((1, C, C), lambda b, h, n: (h, 0, 0)),
            pl.BlockSpec((1, C, dv), lambda b, h, n: (h, 0, 0)),
            pl.BlockSpec((1, C, dv), lambda b, h, n: (h, 0, 0)),
            pl.BlockSpec((1, 1, dv), lambda b, h, n: (h, 0, 0)),
        ],
        out_specs=pl.BlockSpec((1, rows, dv), lambda b, h, n: (b, n, h)),
        out_shape=jax.ShapeDtypeStruct((B, S, d_inner), BF16),
        scratch_shapes=[pltpu.VMEM((dk, dv), F32)],
        compiler_params=_compiler_params(("parallel", "parallel", "arbitrary")),
        name="retention",
    )(proj, proj, proj, proj, cos, sin, dmat, xi, zeta, cdec)


def _diff_attn_kernel(tab_ref, bfar_ref, q_ref, k_ref, vt_ref, gt_ref, bk0_ref,
                      bk1_ref, lq1_ref, lk1_ref, lq2_ref, lk2_ref, sg_ref, o_ref,
                      qbd_ref, vaug_ref, bias_ref, m_ref, acc_ref, sa_ref, sb_ref,
                      *, d, lambda_init):
    T, G, H = ATT_TILE, ATT_HEADS_PER_STEP, DIFF_HEADS
    dv = 2 * d
    S = vt_ref.shape[2]
    hp = pl.program_id(1)
    i = pl.program_id(2)

    @pl.when(i == 0)
    def _():
        rows = lax.broadcasted_iota(jnp.int32, (T, T), 0)
        cols = lax.broadcasted_iota(jnp.int32, (T, T), 1)
        for u in range(G):
            h = hp * G + u
            vaug_ref[u, 0:dv, :] = vt_ref[0, u * dv:(u + 1) * dv, :]
            vaug_ref[u, dv:dv + ONES_ROWS, :] = jnp.ones((ONES_ROWS, S), BF16)
            cfar = tab_ref[bfar_ref[0] * H + h]
            for t, bk_ref in enumerate((bk0_ref, bk1_ref)):
                bk = bk_ref[...]
                b = jnp.zeros((T, T), F32)
                for bucket in range(REL_BUCKETS):
                    b = jnp.where(bk == bucket, tab_ref[bucket * H + h], b)
                b = (b - cfar) * LOG2E
                if t == 0:
                    b = jnp.where(rows <= cols, b, -jnp.inf)
                bias_ref[u, t, :, 0:T] = b
                bias_ref[u, t, :, T:2 * T] = b

    for u in range(G):
        q = q_ref[0, :, u * dv:(u + 1) * dv].astype(F32) * (d ** -0.5 * LOG2E)
        qt = q.T
        row = lax.broadcasted_iota(jnp.int32, qt.shape, 0)
        qbd_ref[u, :, 0:T] = jnp.where(row < d, qt, 0.0).astype(BF16)
        qbd_ref[u, :, T:2 * T] = jnp.where(row >= d, qt, 0.0).astype(BF16)
    m_ref[...] = jnp.full_like(m_ref, -jnp.inf)
    acc_ref[...] = jnp.zeros_like(acc_ref)

    def logits_into(dst_ref, u, j, bias_index):
        start = pl.multiple_of(j * T, T)
        kt = k_ref[0, pl.ds(start, T), u * dv:(u + 1) * dv]
        s = jnp.dot(kt, qbd_ref[u], preferred_element_type=F32)
        if bias_index is not None:
            s = s + bias_ref[u, bias_index]
        dst_ref[u] = s

    def consume(src_ref, u, j):
        start = pl.multiple_of(j * T, T)
        s = src_ref[u]
        m_old = m_ref[u]
        m_new = jnp.maximum(m_old, jnp.max(s, axis=0, keepdims=True))
        p = jnp.exp2(s - m_new).astype(BF16)
        alpha = jnp.exp2(m_old - m_new)
        pv = jnp.dot(vaug_ref[u, :, pl.ds(start, T)], p,
                     preferred_element_type=F32)
        acc_ref[u] = acc_ref[u] * alpha + pv
        m_ref[u] = m_new

    def step(dst_ref, j_next, bias_index, src_ref, j_cur):
        for u in range(G):
            logits_into(dst_ref, u, j_next, bias_index)
            consume(src_ref, u, j_cur)

    def drain(src_ref, j_cur):
        for u in range(G):
            consume(src_ref, u, j_cur)

    for u in range(G):
        logits_into(sa_ref, u, i, 0)

    @pl.when(i == 0)
    def _():
        drain(sa_ref, i)

    @pl.when(i >= 1)
    def _():
        step(sb_ref, i - 1, 1, sa_ref, i)
        n_far = i - 1
        n_pairs = n_far // 2

        def pair_body(r, carry):
            in_b = jnp.where(r == 0, i - 1, 2 * r - 1)
            step(sa_ref, 2 * r, None, sb_ref, in_b)
            step(sb_ref, 2 * r + 1, None, sa_ref, 2 * r)
            return carry

        lax.fori_loop(0, n_pairs, pair_body, 0)
        in_b = jnp.where(n_pairs == 0, i - 1, 2 * n_pairs - 1)

        @pl.when(n_far % 2 == 1)
        def _():
            step(sa_ref, n_far - 1, None, sb_ref, in_b)
            drain(sa_ref, n_far - 1)

        @pl.when(n_far % 2 == 0)
        def _():
            drain(sb_ref, in_b)

    lam = (jnp.exp(jnp.sum(lq1_ref[...] * lk1_ref[...], keepdims=True))
           - jnp.exp(jnp.sum(lq2_ref[...] * lk2_ref[...], keepdims=True))
           + lambda_init)
    for u in range(G):
        acc = acc_ref[u]
        l = acc[dv:dv + 1, :]
        o = acc[:dv, :T] / l[:, :T] - lam * (acc[:dv, T:] / l[:, T:])
        ms = jnp.mean(o * o, axis=0, keepdims=True)
        on = o * lax.rsqrt(ms + GN_EPS) * sg_ref[...] * (1.0 - lambda_init)
        gate = gt_ref[0, u * dv:(u + 1) * dv, :].astype(F32)
        o_ref[0, u * dv:(u + 1) * dv, :] = (
            gate * jax.nn.sigmoid(gate) * on).astype(o_ref.dtype)


def _t5_bucket(n):
    max_exact = REL_BUCKETS // 2
    nf = jnp.maximum(n, max_exact).astype(F32)
    large = max_exact + (jnp.log(nf / max_exact) / math.log(REL_MAX_DIST / max_exact)
                         * (REL_BUCKETS - max_exact)).astype(jnp.int32)
    large = jnp.minimum(large, REL_BUCKETS - 1)
    return jnp.where(n < max_exact, n, large)


def _diff_attn(qk, vgt, lq1, lk1, lq2, lk2, subln_g, rel_bias, lambda_init):
    B, S, _ = qk.shape
    H, T, G = DIFF_HEADS, ATT_TILE, ATT_HEADS_PER_STEP
    d_inner = vgt.shape[1] // 2
    dv = d_inner // H
    d = dv // 2
    gw = G * dv
    n_groups = H // G

    kk = jnp.arange(T, dtype=jnp.int32)[:, None]
    qq = jnp.arange(T, dtype=jnp.int32)[None, :]
    bk0 = _t5_bucket(jnp.maximum(qq - kk, 0))
    bk1 = _t5_bucket(qq - kk + T)
    bfar = _t5_bucket(jnp.full((1,), T + 1, jnp.int32))

    smem = pl.BlockSpec(memory_space=pltpu.SMEM)
    vec = pl.BlockSpec((1, d), lambda b, h, i: (0, 0))
    tile_spec = pl.BlockSpec((T, T), lambda b, h, i: (0, 0))
    return pl.pallas_call(
        functools.partial(_diff_attn_kernel, d=d, lambda_init=lambda_init),
        grid=(B, n_groups, S // T),
        in_specs=[
            smem, smem,
            pl.BlockSpec((1, T, gw), lambda b, h, i: (b, i, h)),
            pl.BlockSpec((1, S, gw), lambda b, h, i: (b, 0, n_groups + h),
                         pipeline_mode=pl.Buffered(1)),
            pl.BlockSpec((1, gw, S), lambda b, h, i: (b, h, 0),
                         pipeline_mode=pl.Buffered(1)),
            pl.BlockSpec((1, gw, T), lambda b, h, i: (b, n_groups + h, i)),
            tile_spec, tile_spec,
            vec, vec, vec, vec,
            pl.BlockSpec((dv, 1), lambda b, h, i: (0, 0)),
        ],
        out_specs=pl.BlockSpec((1, gw, T), lambda b, h, i: (b, h, i)),
        out_shape=jax.ShapeDtypeStruct((B, d_inner, S), BF16),
        scratch_shapes=[pltpu.VMEM((G, dv, 2 * T), BF16),
                        pltpu.VMEM((G, dv + ONES_ROWS, S), BF16),
                        pltpu.VMEM((G, 2, T, 2 * T), F32),
                        pltpu.VMEM((G, 1, 2 * T), F32),
                        pltpu.VMEM((G, dv + ONES_ROWS, 2 * T), F32),
                        pltpu.VMEM((G, T, 2 * T), F32),
                        pltpu.VMEM((G, T, 2 * T), F32)],
        compiler_params=_compiler_params(("parallel", "arbitrary", "arbitrary")),
        name="diff_attn",
    )(rel_bias.astype(F32).reshape(-1), bfar, qk, qk, vgt, vgt, bk0, bk1,
      lq1.reshape(1, d), lk1.reshape(1, d), lq2.reshape(1, d), lk2.reshape(1, d),
      subln_g.reshape(dv, 1))


def kernel(x, pre_norm_g, post_norm_g, ret_w_in, ret_w_out, diff_w_in, diff_w_out,
           diff_lambda_q1, diff_lambda_k1, diff_lambda_q2, diff_lambda_k2,
           diff_subln_g, rel_bias):
    d_model = x.shape[-1]
    d_inner = ret_w_out.shape[1]
    for i in range(DEPTH):
        j = i // N_MIXERS
        if i % N_MIXERS == 0:
            proj = _norm_matmul(x, pre_norm_g[i], ret_w_in[j].astype(BF16),
                                transpose_out=False)
            y = _retention(proj, d_model, d_inner)
            x = _out_proj(y, ret_w_out[j].astype(BF16), post_norm_g[i], x,
                          transposed_in=False)
        else:
            lambda_init = 0.8 - 0.6 * math.exp(-0.3 * i)
            w = diff_w_in[j]
            qk = _norm_matmul(x, pre_norm_g[i], w[:, :2 * d_inner].astype(BF16),
                              transpose_out=False)
            vgt = _norm_matmul(x, pre_norm_g[i], w[:, 2 * d_inner:].T.astype(BF16),
                               transpose_out=True)
            yt = _diff_attn(qk, vgt, diff_lambda_q1[j], diff_lambda_k1[j],
                            diff_lambda_q2[j], diff_lambda_k2[j], diff_subln_g[j],
                            rel_bias, lambda_init)
            x = _out_proj(yt, diff_w_out[j].astype(BF16), post_norm_g[i], x,
                          transposed_in=True)
    return x
```

```python
import functools
import math

import jax
import jax.numpy as jnp
from jax import lax
from jax.experimental import pallas as pl
from jax.experimental.pallas import tpu as pltpu

F32 = jnp.float32
BF16 = jnp.bfloat16

DEPTH = 2
N_MIXERS = 2
RET_HEADS = 4
RET_CHUNK = 128
DIFF_HEADS = 16
REL_BUCKETS = 32
REL_MAX_DIST = 128
ROPE_BASE = 10000.0
RMS_EPS = 1e-6
GN_EPS = 1e-5

V7X_VMEM_BYTES = 64 * 1024 * 1024
VMEM_LIMIT_BYTES = V7X_VMEM_BYTES * 3 // 4

PROJ_ROWS = 1024
PROJ_COLS = 1024
OUT_ROWS = 512
RET_CHUNKS_PER_STEP = 4
ATT_TILE = 256
ATT_HEADS_PER_STEP = 4
ONES_ROWS = 16
LOG2E = math.log2(math.e)


def _compiler_params(semantics):
    return pltpu.CompilerParams(dimension_semantics=semantics,
                                vmem_limit_bytes=VMEM_LIMIT_BYTES)


def _norm_matmul_kernel(x_ref, g_ref, w_ref, o_ref, h_ref, *, transpose_out):
    @pl.when(pl.program_id(2) == 0)
    def _():
        x = x_ref[0]
        ms = jnp.mean(x * x, axis=-1, keepdims=True)
        h_ref[...] = (x * lax.rsqrt(ms + RMS_EPS) * g_ref[...]).astype(BF16)

    h = h_ref[...]
    if transpose_out:
        o = lax.dot_general(w_ref[...], h, (((1,), (1,)), ((), ())),
                            preferred_element_type=F32)
    else:
        o = jnp.dot(h, w_ref[...], preferred_element_type=F32)
    o_ref[0] = o.astype(o_ref.dtype)


def _norm_matmul(x, g, w, *, transpose_out):
    B, S, D = x.shape
    N = w.shape[0] if transpose_out else w.shape[1]
    ts, tn = PROJ_ROWS, PROJ_COLS
    grid = (B, S // ts, N // tn)
    if transpose_out:
        w_spec = pl.BlockSpec((tn, D), lambda b, i, j: (j, 0))
        o_spec = pl.BlockSpec((1, tn, ts), lambda b, i, j: (b, j, i))
        o_shape = jax.ShapeDtypeStruct((B, N, S), BF16)
    else:
        w_spec = pl.BlockSpec((D, tn), lambda b, i, j: (0, j))
        o_spec = pl.BlockSpec((1, ts, tn), lambda b, i, j: (b, i, j))
        o_shape = jax.ShapeDtypeStruct((B, S, N), BF16)
    return pl.pallas_call(
        functools.partial(_norm_matmul_kernel, transpose_out=transpose_out),
        grid=grid,
        in_specs=[pl.BlockSpec((1, ts, D), lambda b, i, j: (b, i, 0)),
                  pl.BlockSpec((1, D), lambda b, i, j: (0, 0)),
                  w_spec],
        out_specs=o_spec,
        out_shape=o_shape,
        scratch_shapes=[pltpu.VMEM((ts, D), BF16)],
        compiler_params=_compiler_params(("parallel", "parallel", "arbitrary")),
        name="norm_matmul_t" if transpose_out else "norm_matmul",
    )(x, g.reshape(1, D), w)


def _out_proj_kernel(y_ref, w_ref, g_ref, x_ref, o_ref, *, transposed_in):
    y = y_ref[0]
    if transposed_in:
        o = lax.dot_general(y, w_ref[...], (((0,), (0,)), ((), ())),
                            preferred_element_type=F32)
    else:
        o = jnp.dot(y, w_ref[...], preferred_element_type=F32)
    ms = jnp.mean(o * o, axis=-1, keepdims=True)
    o_ref[0] = x_ref[0] + o * lax.rsqrt(ms + RMS_EPS) * g_ref[...]


def _out_proj(y, w, g, x, *, transposed_in):
    B, S, D = x.shape
    Di = w.shape[0]
    ts = OUT_ROWS
    if transposed_in:
        y_spec = pl.BlockSpec((1, Di, ts), lambda b, i: (b, 0, i))
    else:
        y_spec = pl.BlockSpec((1, ts, Di), lambda b, i: (b, i, 0))
    return pl.pallas_call(
        functools.partial(_out_proj_kernel, transposed_in=transposed_in),
        grid=(B, S // ts),
        in_specs=[y_spec,
                  pl.BlockSpec((Di, D), lambda b, i: (0, 0)),
                  pl.BlockSpec((1, D), lambda b, i: (0, 0)),
                  pl.BlockSpec((1, ts, D), lambda b, i: (b, i, 0))],
        out_specs=pl.BlockSpec((1, ts, D), lambda b, i: (b, i, 0)),
        out_shape=jax.ShapeDtypeStruct((B, S, D), F32),
        compiler_params=_compiler_params(("parallel", "parallel")),
        name="out_proj_t" if transposed_in else "out_proj",
    )(y, w, g.reshape(1, D), x)


def _retention_kernel(q_ref, k_ref, v_ref, g_ref, cos_ref, sin_ref, dmat_ref,
                      xi_ref, zeta_ref, cdec_ref, o_ref, state_ref, *, dk):
    C = RET_CHUNK
    half = dk // 2

    @pl.when(pl.program_id(2) == 0)
    def _():
        state_ref[...] = jnp.zeros_like(state_ref)

    dmat = dmat_ref[0]
    xi = xi_ref[0]
    zeta = zeta_ref[0]
    cdec = cdec_ref[0]

    def rope(t, c, s):
        t1, t2 = t[:, :half], t[:, half:]
        return jnp.concatenate([t1 * c - t2 * s, t1 * s + t2 * c], axis=-1)

    for ci in range(RET_CHUNKS_PER_STEP):
        rows = pl.ds(ci * C, C)
        c, s = cos_ref[rows, :], sin_ref[rows, :]
        q = rope(q_ref[0, rows, :].astype(F32), c, s).astype(BF16)
        k = (rope(k_ref[0, rows, :].astype(F32), c, s) * (dk ** -0.5)).astype(BF16)
        v = v_ref[0, rows, :]
        state = state_ref[...]
        scores = lax.dot_general(q, k, (((1,), (1,)), ((), ())),
                                 preferred_element_type=F32) * dmat
        inner = jnp.dot(scores.astype(BF16), v, preferred_element_type=F32)
        cross = jnp.dot(q, state.astype(BF16), preferred_element_type=F32) * xi
        vz = (v.astype(F32) * zeta).astype(BF16)
        state_ref[...] = state * cdec + lax.dot_general(
            k, vz, (((0,), (0,)), ((), ())), preferred_element_type=F32)
        ret = inner + cross
        mu = jnp.mean(ret, axis=-1, keepdims=True)
        cen = ret - mu
        var = jnp.mean(cen * cen, axis=-1, keepdims=True)
        y = cen * lax.rsqrt(var + GN_EPS)
        gate = g_ref[0, rows, :].astype(F32)
        o_ref[0, rows, :] = (gate * jax.nn.sigmoid(gate) * y).astype(o_ref.dtype)


def _retention(proj, d_model, d_inner):
    B, S, _ = proj.shape
    H, C = RET_HEADS, RET_CHUNK
    dk, dv = d_model // H, d_inner // H
    half = dk // 2
    rows = C * RET_CHUNKS_PER_STEP
    k_off = d_model // dk
    v_off = 2 * d_model // dv
    g_off = (2 * d_model + d_inner) // dv

    pos = jnp.arange(S, dtype=F32)
    inv = ROPE_BASE ** (-jnp.arange(half, dtype=F32) / half)
    ang = pos[:, None] * inv[None, :]
    cos, sin = jnp.cos(ang), jnp.sin(ang)

    log_gamma = jnp.log1p(-jnp.exp2(-5.0 - jnp.arange(H, dtype=F32)))
    idx = jnp.arange(C, dtype=F32)
    rel = idx[:, None] - idx[None, :]
    dmat = jnp.exp(log_gamma[:, None, None] * jnp.maximum(rel, 0.0)) * (rel >= 0)
    xi = jnp.exp(log_gamma[:, None] * (idx + 1.0))
    zeta = jnp.exp(log_gamma[:, None] * (C - 1.0 - idx))
    cdec = jnp.exp(log_gamma * C)
    xi = jnp.broadcast_to(xi[:, :, None], (H, C, dv))
    zeta = jnp.broadcast_to(zeta[:, :, None], (H, C, dv))
    cdec = jnp.broadcast_to(cdec[:, None, None], (H, 1, dv))

    return pl.pallas_call(
        functools.partial(_retention_kernel, dk=dk),
        grid=(B, H, S // rows),
        in_specs=[
            pl.BlockSpec((1, rows, dk), lambda b, h, n: (b, n, h)),
            pl.BlockSpec((1, rows, dk), lambda b, h, n: (b, n, k_off + h)),
            pl.BlockSpec((1, rows, dv), lambda b, h, n: (b, n, v_off + h)),
            pl.BlockSpec((1, rows, dv), lambda b, h, n: (b, n, g_off + h)),
            pl.BlockSpec((rows, half), lambda b, h, n: (n, 0)),
            pl.BlockSpec((rows, half), lambda b, h, n: (n, 0)),
            pl.BlockSpec((1, C, C), lambda b, h, n: (h, 0, 0)),
            pl.BlockSpec((1, C, dv), lambda b, h, n: (h, 0, 0)),
            pl.BlockSpec((1, C, dv), lambda b, h, n: (h, 0, 0)),
            pl.BlockSpec((1, 1, dv), lambda b, h, n: (h, 0, 0)),
        ],
        out_specs=pl.BlockSpec((1, rows, dv), lambda b, h, n: (b, n, h)),
        out_shape=jax.ShapeDtypeStruct((B, S, d_inner), BF16),
        scratch_shapes=[pltpu.VMEM((dk, dv), F32)],
        compiler_params=_compiler_params(("parallel", "parallel", "arbitrary")),
        name="retention",
    )(proj, proj, proj, proj, cos, sin, dmat, xi, zeta, cdec)


def _diff_attn_kernel(tab_ref, bfar_ref, q_ref, k_ref, vt_ref, gt_ref, bk0_ref,
                      bk1_ref, lq1_ref, lk1_ref, lq2_ref, lk2_ref, sg_ref, o_ref,
                      qbd_ref, vaug_ref, bias_ref, m_ref, acc_ref, sa_ref, sb_ref,
                      *, d, lambda_init):
    T, G, H = ATT_TILE, ATT_HEADS_PER_STEP, DIFF_HEADS
    dv = 2 * d
    S = vt_ref.shape[2]
    hp = pl.program_id(1)
    i = pl.program_id(2)

    @pl.when(i == 0)
    def _():
        rows = lax.broadcasted_iota(jnp.int32, (T, T), 0)
        cols = lax.broadcasted_iota(jnp.int32, (T, T), 1)
        for u in range(G):
            h = hp * G + u
            vaug_ref[u, 0:dv, :] = vt_ref[0, u * dv:(u + 1) * dv, :]
            vaug_ref[u, dv:dv + ONES_ROWS, :] = jnp.ones((ONES_ROWS, S), BF16)
            cfar = tab_ref[bfar_ref[0] * H + h]
            for t, bk_ref in enumerate((bk0_ref, bk1_ref)):
                bk = bk_ref[...]
                b = jnp.zeros((T, T), F32)
                for bucket in range(REL_BUCKETS):
                    b = jnp.where(bk == bucket, tab_ref[bucket * H + h], b)
                b = (b - cfar) * LOG2E
                if t == 0:
                    b = jnp.where(rows <= cols, b, -jnp.inf)
                bias_ref[u, t] = b

    for w in range(2 * G):
        sub, u = divmod(w, G)
        q = q_ref[0, sub * T:(sub + 1) * T, u * dv:(u + 1) * dv].astype(F32)
        qt = (q * (d ** -0.5 * LOG2E)).T
        row = lax.broadcasted_iota(jnp.int32, qt.shape, 0)
        qbd_ref[w, :, 0:T] = jnp.where(row < d, qt, 0.0).astype(BF16)
        qbd_ref[w, :, T:2 * T] = jnp.where(row >= d, qt, 0.0).astype(BF16)
    m_ref[...] = jnp.full_like(m_ref, -jnp.inf)
    acc_ref[...] = jnp.zeros_like(acc_ref)

    def logits_into(dst_ref, w, j, bias_index):
        u = w % G
        start = pl.multiple_of(j * T, T)
        kt = k_ref[0, pl.ds(start, T), u * dv:(u + 1) * dv]
        s = jnp.dot(kt, qbd_ref[w], preferred_element_type=F32)
        if bias_index is None:
            dst_ref[w] = s
        else:
            bias = bias_ref[u, bias_index]
            dst_ref[w, :, 0:T] = s[:, 0:T] + bias
            dst_ref[w, :, T:2 * T] = s[:, T:2 * T] + bias

    def consume(src_ref, w, j):
        u = w % G
        start = pl.multiple_of(j * T, T)
        s = src_ref[w]
        m_old = m_ref[w]
        m_new = jnp.maximum(m_old, jnp.max(s, axis=0, keepdims=True))
        p = jnp.exp2(s - m_new).astype(BF16)
        alpha = jnp.exp2(m_old - m_new)
        pv = jnp.dot(vaug_ref[u, :, pl.ds(start, T)], p,
                     preferred_element_type=F32)
        acc_ref[w] = acc_ref[w] * alpha + pv
        m_ref[w] = m_new

    first, second, both = range(G), range(G, 2 * G), range(2 * G)

    def step(dst_ref, j_next, bias_of, src_ref, j_cur, cur_units=both):
        for w in both:
            logits_into(dst_ref, w, j_next, bias_of(w))
            if w in cur_units:
                consume(src_ref, w, j_cur)

    def drain(src_ref, j_cur):
        for w in both:
            consume(src_ref, w, j_cur)

    no_bias = lambda w: None
    for w in second:
        logits_into(sa_ref, w, 2 * i + 1, 0)
    step(sb_ref, 2 * i, lambda w: 0 if w in first else 1,
         sa_ref, 2 * i + 1, cur_units=second)

    @pl.when(i == 0)
    def _():
        drain(sb_ref, 2 * i)

    @pl.when(i >= 1)
    def _():
        step(sa_ref, 2 * i - 1, lambda w: 1 if w in first else None,
             sb_ref, 2 * i)
        step(sb_ref, 0, no_bias, sa_ref, 2 * i - 1)

        def pair_body(r, carry):
            step(sa_ref, 2 * r - 1, no_bias, sb_ref, 2 * r - 2)
            step(sb_ref, 2 * r, no_bias, sa_ref, 2 * r - 1)
            return carry

        lax.fori_loop(1, i, pair_body, 0)
        drain(sb_ref, 2 * i - 2)

    lam = (jnp.exp(jnp.sum(lq1_ref[...] * lk1_ref[...], keepdims=True))
           - jnp.exp(jnp.sum(lq2_ref[...] * lk2_ref[...], keepdims=True))
           + lambda_init)
    for w in both:
        sub, u = divmod(w, G)
        acc = acc_ref[w]
        l = acc[dv:dv + 1, :]
        o = acc[:dv, :T] / l[:, :T] - lam * (acc[:dv, T:] / l[:, T:])
        ms = jnp.mean(o * o, axis=0, keepdims=True)
        on = o * lax.rsqrt(ms + GN_EPS) * sg_ref[...] * (1.0 - lambda_init)
        head_rows = slice(u * dv, (u + 1) * dv)
        sub_cols = slice(sub * T, (sub + 1) * T)
        gate = gt_ref[0, head_rows, sub_cols].astype(F32)
        o_ref[0, head_rows, sub_cols] = (
            gate * jax.nn.sigmoid(gate) * on).astype(o_ref.dtype)


def _t5_bucket(n):
    max_exact = REL_BUCKETS // 2
    nf = jnp.maximum(n, max_exact).astype(F32)
    large = max_exact + (jnp.log(nf / max_exact) / math.log(REL_MAX_DIST / max_exact)
                         * (REL_BUCKETS - max_exact)).astype(jnp.int32)
    large = jnp.minimum(large, REL_BUCKETS - 1)
    return jnp.where(n < max_exact, n, large)


def _diff_attn(qk, vgt, lq1, lk1, lq2, lk2, subln_g, rel_bias, lambda_init):
    B, S, _ = qk.shape
    H, T, G = DIFF_HEADS, ATT_TILE, ATT_HEADS_PER_STEP
    d_inner = vgt.shape[1] // 2
    dv = d_inner // H
    d = dv // 2
    gw = G * dv
    n_groups = H // G

    kk = jnp.arange(T, dtype=jnp.int32)[:, None]
    qq = jnp.arange(T, dtype=jnp.int32)[None, :]
    bk0 = _t5_bucket(jnp.maximum(qq - kk, 0))
    bk1 = _t5_bucket(qq - kk + T)
    bfar = _t5_bucket(jnp.full((1,), T + 1, jnp.int32))

    smem = pl.BlockSpec(memory_space=pltpu.SMEM)
    vec = pl.BlockSpec((1, d), lambda b, h, i: (0, 0))
    tile_spec = pl.BlockSpec((T, T), lambda b, h, i: (0, 0))
    return pl.pallas_call(
        functools.partial(_diff_attn_kernel, d=d, lambda_init=lambda_init),
        grid=(B, n_groups, S // (2 * T)),
        in_specs=[
            smem, smem,
            pl.BlockSpec((1, 2 * T, gw), lambda b, h, i: (b, i, h)),
            pl.BlockSpec((1, S, gw), lambda b, h, i: (b, 0, n_groups + h),
                         pipeline_mode=pl.Buffered(1)),
            pl.BlockSpec((1, gw, S), lambda b, h, i: (b, h, 0),
                         pipeline_mode=pl.Buffered(1)),
            pl.BlockSpec((1, gw, 2 * T), lambda b, h, i: (b, n_groups + h, i)),
            tile_spec, tile_spec,
            vec, vec, vec, vec,
            pl.BlockSpec((dv, 1), lambda b, h, i: (0, 0)),
        ],
        out_specs=pl.BlockSpec((1, gw, 2 * T), lambda b, h, i: (b, h, i)),
        out_shape=jax.ShapeDtypeStruct((B, d_inner, S), BF16),
        scratch_shapes=[pltpu.VMEM((2 * G, dv, 2 * T), BF16),
                        pltpu.VMEM((G, dv + ONES_ROWS, S), BF16),
                        pltpu.VMEM((G, 2, T, T), F32),
                        pltpu.VMEM((2 * G, 1, 2 * T), F32),
                        pltpu.VMEM((2 * G, dv + ONES_ROWS, 2 * T), F32),
                        pltpu.VMEM((2 * G, T, 2 * T), F32),
                        pltpu.VMEM((2 * G, T, 2 * T), F32)],
        compiler_params=_compiler_params(("parallel", "arbitrary", "arbitrary")),
        name="diff_attn",
    )(rel_bias.astype(F32).reshape(-1), bfar, qk, qk, vgt, vgt, bk0, bk1,
      lq1.reshape(1, d), lk1.reshape(1, d), lq2.reshape(1, d), lk2.reshape(1, d),
      subln_g.reshape(dv, 1))


def kernel(x, pre_norm_g, post_norm_g, ret_w_in, ret_w_out, diff_w_in, diff_w_out,
           diff_lambda_q1, diff_lambda_k1, diff_lambda_q2, diff_lambda_k2,
           diff_subln_g, rel_bias):
    d_model = x.shape[-1]
    d_inner = ret_w_out.shape[1]
    for i in range(DEPTH):
        j = i // N_MIXERS
        if i % N_MIXERS == 0:
            proj = _norm_matmul(x, pre_norm_g[i], ret_w_in[j].astype(BF16),
                                transpose_out=False)
            y = _retention(proj, d_model, d_inner)
            x = _out_proj(y, ret_w_out[j].astype(BF16), post_norm_g[i], x,
                          transposed_in=False)
        else:
            lambda_init = 0.8 - 0.6 * math.exp(-0.3 * i)
            w = diff_w_in[j]
            qk = _norm_matmul(x, pre_norm_g[i], w[:, :2 * d_inner].astype(BF16),
                              transpose_out=False)
            vgt = _norm_matmul(x, pre_norm_g[i], w[:, 2 * d_inner:].T.astype(BF16),
                               transpose_out=True)
            yt = _diff_attn(qk, vgt, diff_lambda_q1[j], diff_lambda_k1[j],
                            diff_lambda_q2[j], diff_lambda_k2[j], diff_subln_g[j],
                            rel_bias, lambda_init)
            x = _out_proj(yt, diff_w_out[j].astype(BF16), post_norm_g[i], x,
                          transposed_in=True)
    return x
```

```python
import functools
import math

import jax
import jax.numpy as jnp
from jax import lax
from jax.experimental import pallas as pl
from jax.experimental.pallas import tpu as pltpu

F32 = jnp.float32
BF16 = jnp.bfloat16

DEPTH = 2
N_MIXERS = 2
RET_HEADS = 4
RET_CHUNK = 128
DIFF_HEADS = 16
REL_BUCKETS = 32
REL_MAX_DIST = 128
ROPE_BASE = 10000.0
RMS_EPS = 1e-6
GN_EPS = 1e-5

V7X_VMEM_BYTES = 64 * 1024 * 1024
VMEM_LIMIT_BYTES = V7X_VMEM_BYTES * 3 // 4

PROJ_ROWS = 1024
PROJ_COLS = 2048
OUT_ROWS = 1024
RET_CHUNKS_PER_STEP = 8
ATT_TILE = 256
ATT_HEADS_PER_STEP = 4
ONES_ROWS = 16
LOG2E = math.log2(math.e)


def _compiler_params(semantics):
    return pltpu.CompilerParams(dimension_semantics=semantics,
                                vmem_limit_bytes=VMEM_LIMIT_BYTES)


def _norm_matmul_kernel(x_ref, g_ref, w_ref, o_ref, h_ref, *, transpose_out):
    @pl.when(pl.program_id(2) == 0)
    def _():
        x = x_ref[0]
        ms = jnp.mean(x * x, axis=-1, keepdims=True)
        h_ref[...] = (x * lax.rsqrt(ms + RMS_EPS) * g_ref[...]).astype(BF16)

    h = h_ref[...]
    if transpose_out:
        o = lax.dot_general(w_ref[...], h, (((1,), (1,)), ((), ())),
                            preferred_element_type=F32)
    else:
        o = jnp.dot(h, w_ref[...], preferred_element_type=F32)
    o_ref[0] = o.astype(o_ref.dtype)


def _norm_matmul(x, g, w, *, transpose_out):
    B, S, D = x.shape
    N = w.shape[0] if transpose_out else w.shape[1]
    ts, tn = PROJ_ROWS, PROJ_COLS
    grid = (B, S // ts, N // tn)
    if transpose_out:
        w_spec = pl.BlockSpec((tn, D), lambda b, i, j: (j, 0))
        o_spec = pl.BlockSpec((1, tn, ts), lambda b, i, j: (b, j, i))
        o_shape = jax.ShapeDtypeStruct((B, N, S), BF16)
    else:
        w_spec = pl.BlockSpec((D, tn), lambda b, i, j: (0, j))
        o_spec = pl.BlockSpec((1, ts, tn), lambda b, i, j: (b, i, j))
        o_shape = jax.ShapeDtypeStruct((B, S, N), BF16)
    return pl.pallas_call(
        functools.partial(_norm_matmul_kernel, transpose_out=transpose_out),
        grid=grid,
        in_specs=[pl.BlockSpec((1, ts, D), lambda b, i, j: (b, i, 0)),
                  pl.BlockSpec((1, D), lambda b, i, j: (0, 0)),
                  w_spec],
        out_specs=o_spec,
        out_shape=o_shape,
        scratch_shapes=[pltpu.VMEM((ts, D), BF16)],
        compiler_params=_compiler_params(("parallel", "parallel", "arbitrary")),
        name="norm_matmul_t" if transpose_out else "norm_matmul",
    )(x, g.reshape(1, D), w)


def _out_proj_kernel(y_ref, w_ref, g_ref, x_ref, o_ref, *, transposed_in):
    y = y_ref[0]
    if transposed_in:
        o = lax.dot_general(y, w_ref[...], (((0,), (0,)), ((), ())),
                            preferred_element_type=F32)
    else:
        o = jnp.dot(y, w_ref[...], preferred_element_type=F32)
    ms = jnp.mean(o * o, axis=-1, keepdims=True)
    o_ref[0] = x_ref[0] + o * lax.rsqrt(ms + RMS_EPS) * g_ref[...]


def _out_proj(y, w, g, x, *, transposed_in):
    B, S, D = x.shape
    Di = w.shape[0]
    ts = OUT_ROWS
    if transposed_in:
        y_spec = pl.BlockSpec((1, Di, ts), lambda b, i: (b, 0, i))
    else:
        y_spec = pl.BlockSpec((1, ts, Di), lambda b, i: (b, i, 0))
    return pl.pallas_call(
        functools.partial(_out_proj_kernel, transposed_in=transposed_in),
        grid=(B, S // ts),
        in_specs=[y_spec,
                  pl.BlockSpec((Di, D), lambda b, i: (0, 0)),
                  pl.BlockSpec((1, D), lambda b, i: (0, 0)),
                  pl.BlockSpec((1, ts, D), lambda b, i: (b, i, 0))],
        out_specs=pl.BlockSpec((1, ts, D), lambda b, i: (b, i, 0)),
        out_shape=jax.ShapeDtypeStruct((B, S, D), F32),
        compiler_params=_compiler_params(("parallel", "parallel")),
        name="out_proj_t" if transposed_in else "out_proj",
    )(y, w, g.reshape(1, D), x)


def _retention_kernel(q_ref, k_ref, v_ref, g_ref, cos_ref, sin_ref, dmat_ref,
                      xi_ref, zeta_ref, cdec_ref, o_ref, state_ref, *, dk):
    C = RET_CHUNK
    half = dk // 2

    @pl.when(pl.program_id(2) == 0)
    def _():
        state_ref[...] = jnp.zeros_like(state_ref)

    dmat = dmat_ref[0]
    xi = xi_ref[0]
    zeta = zeta_ref[0]
    cdec = cdec_ref[0]

    def rope(t, c, s):
        t1, t2 = t[:, :half], t[:, half:]
        return jnp.concatenate([t1 * c - t2 * s, t1 * s + t2 * c], axis=-1)

    for ci in range(RET_CHUNKS_PER_STEP):
        rows = pl.ds(ci * C, C)
        c, s = cos_ref[rows, :], sin_ref[rows, :]
        qf = rope(q_ref[0, rows, :].astype(F32), c, s)
        kf = rope(k_ref[0, rows, :].astype(F32), c, s) * (dk ** -0.5)
        q, k = qf.astype(BF16), kf.astype(BF16)
        v = v_ref[0, rows, :]
        state = state_ref[...]
        scores = lax.dot_general(q, k, (((1,), (1,)), ((), ())),
                                 preferred_element_type=F32) * dmat
        inner = jnp.dot(scores.astype(BF16), v, preferred_element_type=F32)
        cross = jnp.dot((qf * xi).astype(BF16), state.astype(BF16),
                        preferred_element_type=F32)
        state_ref[...] = state * cdec + lax.dot_general(
            (kf * zeta).astype(BF16), v, (((0,), (0,)), ((), ())),
            preferred_element_type=F32)
        ret = inner + cross
        mu = jnp.mean(ret, axis=-1, keepdims=True)
        cen = ret - mu
        var = jnp.mean(cen * cen, axis=-1, keepdims=True)
        y = cen * lax.rsqrt(var + GN_EPS)
        gate = g_ref[0, rows, :].astype(F32)
        o_ref[0, rows, :] = (gate * jax.nn.sigmoid(gate) * y).astype(o_ref.dtype)


def _retention(proj, d_model, d_inner):
    B, S, _ = proj.shape
    H, C = RET_HEADS, RET_CHUNK
    dk, dv = d_model // H, d_inner // H
    half = dk // 2
    rows = C * RET_CHUNKS_PER_STEP
    k_off = d_model // dk
    v_off = 2 * d_model // dv
    g_off = (2 * d_model + d_inner) // dv

    pos = jnp.arange(S, dtype=F32)
    inv = ROPE_BASE ** (-jnp.arange(half, dtype=F32) / half)
    ang = pos[:, None] * inv[None, :]
    cos, sin = jnp.cos(ang), jnp.sin(ang)

    log_gamma = jnp.log1p(-jnp.exp2(-5.0 - jnp.arange(H, dtype=F32)))
    idx = jnp.arange(C, dtype=F32)
    rel = idx[:, None] - idx[None, :]
    dmat = jnp.exp(log_gamma[:, None, None] * jnp.maximum(rel, 0.0)) * (rel >= 0)
    xi = jnp.exp(log_gamma[:, None] * (idx + 1.0))
    zeta = jnp.exp(log_gamma[:, None] * (C - 1.0 - idx))
    cdec = jnp.exp(log_gamma * C)
    xi = jnp.broadcast_to(xi[:, :, None], (H, C, dk))
    zeta = jnp.broadcast_to(zeta[:, :, None], (H, C, dk))
    cdec = jnp.broadcast_to(cdec[:, None, None], (H, 1, dv))

    return pl.pallas_call(
        functools.partial(_retention_kernel, dk=dk),
        grid=(B, H, S // rows),
        in_specs=[
            pl.BlockSpec((1, rows, dk), lambda b, h, n: (b, n, h)),
            pl.BlockSpec((1, rows, dk), lambda b, h, n: (b, n, k_off + h)),
            pl.BlockSpec((1, rows, dv), lambda b, h, n: (b, n, v_off + h)),
            pl.BlockSpec((1, rows, dv), lambda b, h, n: (b, n, g_off + h)),
            pl.BlockSpec((rows, half), lambda b, h, n: (n, 0)),
            pl.BlockSpec((rows, half), lambda b, h, n: (n, 0)),
            pl.BlockSpec((1, C, C), lambda b, h, n: (h, 0, 0)),
            pl.BlockSpec((1, C, dk), lambda b, h, n: (h, 0, 0)),
            pl.BlockSpec((1, C, dk), lambda b, h, n: (h, 0, 0)),
            pl.BlockSpec((1, 1, dv), lambda b, h, n: (h, 0, 0)),
        ],
        out_specs=pl.BlockSpec((1, rows, dv), lambda b, h, n: (b, n, h)),
        out_shape=jax.ShapeDtypeStruct((B, S, d_inner), BF16),
        scratch_shapes=[pltpu.VMEM((dk, dv), F32)],
        compiler_params=_compiler_params(("parallel", "parallel", "arbitrary")),
        name="retention",
    )(proj, proj, proj, proj, cos, sin, dmat, xi, zeta, cdec)


def _diff_attn_kernel(tab_ref, bfar_ref, q_ref, k_ref, vt_ref, gt_ref, bk0_ref,
                      bk1_ref, lq1_ref, lk1_ref, lq2_ref, lk2_ref, sg_ref, o_ref,
                      qbd_ref, vaug_ref, bias_ref, m_ref, acc_ref, sa_ref, sb_ref,
                      *, d, lambda_init):
    T, G, H = ATT_TILE, ATT_HEADS_PER_STEP, DIFF_HEADS
    dv = 2 * d
    S = vt_ref.shape[2]
    hp = pl.program_id(1)
    i = pl.program_id(2)

    @pl.when(i == 0)
    def _():
        rows = lax.broadcasted_iota(jnp.int32, (T, T), 0)
        cols = lax.broadcasted_iota(jnp.int32, (T, T), 1)
        for u in range(G):
            h = hp * G + u
            vaug_ref[u, 0:dv, :] = vt_ref[0, u * dv:(u + 1) * dv, :]
            vaug_ref[u, dv:dv + ONES_ROWS, :] = jnp.ones((ONES_ROWS, S), BF16)
            cfar = tab_ref[bfar_ref[0] * H + h]
            for t, bk_ref in enumerate((bk0_ref, bk1_ref)):
                bk = bk_ref[...]
                b = jnp.zeros((T, T), F32)
                for bucket in range(REL_BUCKETS):
                    b = jnp.where(bk == bucket, tab_ref[bucket * H + h], b)
                b = (b - cfar) * LOG2E
                if t == 0:
                    b = jnp.where(rows <= cols, b, -jnp.inf)
                bias_ref[u, t] = b

    for w in range(2 * G):
        sub, u = divmod(w, G)
        q = q_ref[0, sub * T:(sub + 1) * T, u * dv:(u + 1) * dv].astype(F32)
        qt = (q * (d ** -0.5 * LOG2E)).T
        row = lax.broadcasted_iota(jnp.int32, qt.shape, 0)
        qbd_ref[w, :, 0:T] = jnp.where(row < d, qt, 0.0).astype(BF16)
        qbd_ref[w, :, T:2 * T] = jnp.where(row >= d, qt, 0.0).astype(BF16)
    m_ref[...] = jnp.full_like(m_ref, -jnp.inf)
    acc_ref[...] = jnp.zeros_like(acc_ref)

    def logits_into(dst_ref, w, j, bias_index):
        u = w % G
        start = pl.multiple_of(j * T, T)
        kt = k_ref[0, pl.ds(start, T), u * dv:(u + 1) * dv]
        s = jnp.dot(kt, qbd_ref[w], preferred_element_type=F32)
        if bias_index is None:
            dst_ref[w] = s
        else:
            bias = bias_ref[u, bias_index]
            dst_ref[w, :, 0:T] = s[:, 0:T] + bias
            dst_ref[w, :, T:2 * T] = s[:, T:2 * T] + bias

    def consume(src_ref, w, j):
        u = w % G
        start = pl.multiple_of(j * T, T)
        s = src_ref[w]
        m_old = m_ref[w]
        m_new = jnp.maximum(m_old, jnp.max(s, axis=0, keepdims=True))
        p = jnp.exp2(s - m_new).astype(BF16)
        alpha = jnp.exp2(m_old - m_new)
        pv = jnp.dot(vaug_ref[u, :, pl.ds(start, T)], p,
                     preferred_element_type=F32)
        acc_ref[w] = acc_ref[w] * alpha + pv
        m_ref[w] = m_new

    first, second, both = range(G), range(G, 2 * G), range(2 * G)

    def step(dst_ref, j_next, bias_of, src_ref, j_cur, cur_units=both):
        for w in both:
            logits_into(dst_ref, w, j_next, bias_of(w))
            if w in cur_units:
                consume(src_ref, w, j_cur)

    def drain(src_ref, j_cur):
        for w in both:
            consume(src_ref, w, j_cur)

    no_bias = lambda w: None
    for w in second:
        logits_into(sa_ref, w, 2 * i + 1, 0)
    step(sb_ref, 2 * i, lambda w: 0 if w in first else 1,
         sa_ref, 2 * i + 1, cur_units=second)

    @pl.when(i == 0)
    def _():
        drain(sb_ref, 2 * i)

    @pl.when(i >= 1)
    def _():
        step(sa_ref, 2 * i - 1, lambda w: 1 if w in first else None,
             sb_ref, 2 * i)
        step(sb_ref, 0, no_bias, sa_ref, 2 * i - 1)

        def pair_body(r, carry):
            step(sa_ref, 2 * r - 1, no_bias, sb_ref, 2 * r - 2)
            step(sb_ref, 2 * r, no_bias, sa_ref, 2 * r - 1)
            return carry

        lax.fori_loop(1, i, pair_body, 0)
        drain(sb_ref, 2 * i - 2)

    lam = (jnp.exp(jnp.sum(lq1_ref[...] * lk1_ref[...], keepdims=True))
           - jnp.exp(jnp.sum(lq2_ref[...] * lk2_ref[...], keepdims=True))
           + lambda_init)
    for w in both:
        sub, u = divmod(w, G)
        acc = acc_ref[w]
        l = acc[dv:dv + 1, :]
        o = acc[:dv, :T] / l[:, :T] - lam * (acc[:dv, T:] / l[:, T:])
        ms = jnp.mean(o * o, axis=0, keepdims=True)
        on = o * lax.rsqrt(ms + GN_EPS) * sg_ref[...] * (1.0 - lambda_init)
        head_rows = slice(u * dv, (u + 1) * dv)
        sub_cols = slice(sub * T, (sub + 1) * T)
        gate = gt_ref[0, head_rows, sub_cols].astype(F32)
        o_ref[0, head_rows, sub_cols] = (
            gate * jax.nn.sigmoid(gate) * on).astype(o_ref.dtype)


def _t5_bucket(n):
    max_exact = REL_BUCKETS // 2
    nf = jnp.maximum(n, max_exact).astype(F32)
    large = max_exact + (jnp.log(nf / max_exact) / math.log(REL_MAX_DIST / max_exact)
                         * (REL_BUCKETS - max_exact)).astype(jnp.int32)
    large = jnp.minimum(large, REL_BUCKETS - 1)
    return jnp.where(n < max_exact, n, large)


def _diff_attn(qk, vgt, lq1, lk1, lq2, lk2, subln_g, rel_bias, lambda_init):
    B, S, _ = qk.shape
    H, T, G = DIFF_HEADS, ATT_TILE, ATT_HEADS_PER_STEP
    d_inner = vgt.shape[1] // 2
    dv = d_inner // H
    d = dv // 2
    gw = G * dv
    n_groups = H // G

    kk = jnp.arange(T, dtype=jnp.int32)[:, None]
    qq = jnp.arange(T, dtype=jnp.int32)[None, :]
    bk0 = _t5_bucket(jnp.maximum(qq - kk, 0))
    bk1 = _t5_bucket(qq - kk + T)
    bfar = _t5_bucket(jnp.full((1,), T + 1, jnp.int32))

    smem = pl.BlockSpec(memory_space=pltpu.SMEM)
    vec = pl.BlockSpec((1, d), lambda b, h, i: (0, 0))
    tile_spec = pl.BlockSpec((T, T), lambda b, h, i: (0, 0))
    return pl.pallas_call(
        functools.partial(_diff_attn_kernel, d=d, lambda_init=lambda_init),
        grid=(B, n_groups, S // (2 * T)),
        in_specs=[
            smem, smem,
            pl.BlockSpec((1, 2 * T, gw), lambda b, h, i: (b, i, h)),
            pl.BlockSpec((1, S, gw), lambda b, h, i: (b, 0, n_groups + h),
                         pipeline_mode=pl.Buffered(1)),
            pl.BlockSpec((1, gw, S), lambda b, h, i: (b, h, 0),
                         pipeline_mode=pl.Buffered(1)),
            pl.BlockSpec((1, gw, 2 * T), lambda b, h, i: (b, n_groups + h, i)),
            tile_spec, tile_spec,
            vec, vec, vec, vec,
            pl.BlockSpec((dv, 1), lambda b, h, i: (0, 0)),
        ],
        out_specs=pl.BlockSpec((1, gw, 2 * T), lambda b, h, i: (b, h, i)),
        out_shape=jax.ShapeDtypeStruct((B, d_inner, S), BF16),
        scratch_shapes=[pltpu.VMEM((2 * G, dv, 2 * T), BF16),
                        pltpu.VMEM((G, dv + ONES_ROWS, S), BF16),
                        pltpu.VMEM((G, 2, T, T), F32),
                        pltpu.VMEM((2 * G, 1, 2 * T), F32),
                        pltpu.VMEM((2 * G, dv + ONES_ROWS, 2 * T), F32),
                        pltpu.VMEM((2 * G, T, 2 * T), F32),
                        pltpu.VMEM((2 * G, T, 2 * T), F32)],
        compiler_params=_compiler_params(("parallel", "arbitrary", "arbitrary")),
        name="diff_attn",
    )(rel_bias.astype(F32).reshape(-1), bfar, qk, qk, vgt, vgt, bk0, bk1,
      lq1.reshape(1, d), lk1.reshape(1, d), lq2.reshape(1, d), lk2.reshape(1, d),
      subln_g.reshape(dv, 1))


def kernel(x, pre_norm_g, post_norm_g, ret_w_in, ret_w_out, diff_w_in, diff_w_out,
           diff_lambda_q1, diff_lambda_k1, diff_lambda_q2, diff_lambda_k2,
           diff_subln_g, rel_bias):
    d_model = x.shape[-1]
    d_inner = ret_w_out.shape[1]
    for i in range(DEPTH):
        j = i // N_MIXERS
        if i % N_MIXERS == 0:
            proj = _norm_matmul(x, pre_norm_g[i], ret_w_in[j].astype(BF16),
                                transpose_out=False)
            y = _retention(proj, d_model, d_inner)
            x = _out_proj(y, ret_w_out[j].astype(BF16), post_norm_g[i], x,
                          transposed_in=False)
        else:
            lambda_init = 0.8 - 0.6 * math.exp(-0.3 * i)
            w = diff_w_in[j]
            qk = _norm_matmul(x, pre_norm_g[i], w[:, :2 * d_inner].astype(BF16),
                              transpose_out=False)
            vgt = _norm_matmul(x, pre_norm_g[i], w[:, 2 * d_inner:].T.astype(BF16),
                               transpose_out=True)
            yt = _diff_attn(qk, vgt, diff_lambda_q1[j], diff_lambda_k1[j],
                            diff_lambda_q2[j], diff_lambda_k2[j], diff_subln_g[j],
                            rel_bias, lambda_init)
            x = _out_proj(yt, diff_w_out[j].astype(BF16), post_norm_g[i], x,
                          transposed_in=True)
    return x
```

```python
import functools
import math

import jax
import jax.numpy as jnp
from jax import lax
from jax.experimental import pallas as pl
from jax.experimental.pallas import tpu as pltpu

F32 = jnp.float32
BF16 = jnp.bfloat16

DEPTH = 2
N_MIXERS = 2
RET_HEADS = 4
RET_CHUNK = 128
DIFF_HEADS = 16
REL_BUCKETS = 32
REL_MAX_DIST = 128
ROPE_BASE = 10000.0
RMS_EPS = 1e-6
GN_EPS = 1e-5

V7X_VMEM_BYTES = 64 * 1024 * 1024
VMEM_LIMIT_BYTES = V7X_VMEM_BYTES * 3 // 4

PROJ_ROWS = 1024
PROJ_COLS = 2048
OUT_ROWS = 1024
RET_CHUNKS_PER_STEP = 8
ATT_TILE = 256
ATT_HEADS_PER_STEP = 4
ONES_ROWS = 16
LOGITS_PAD = 128
LOG2E = math.log2(math.e)


def _compiler_params(semantics):
    return pltpu.CompilerParams(dimension_semantics=semantics,
                                vmem_limit_bytes=VMEM_LIMIT_BYTES)


def _norm_matmul_kernel(x_ref, g_ref, w_ref, o_ref, h_ref, *, transpose_out):
    @pl.when(pl.program_id(2) == 0)
    def _():
        x = x_ref[0]
        ms = jnp.mean(x * x, axis=-1, keepdims=True)
        h_ref[...] = (x * lax.rsqrt(ms + RMS_EPS) * g_ref[...]).astype(BF16)

    h = h_ref[...]
    if transpose_out:
        o = lax.dot_general(w_ref[...], h, (((1,), (1,)), ((), ())),
                            preferred_element_type=F32)
    else:
        o = jnp.dot(h, w_ref[...], preferred_element_type=F32)
    o_ref[0] = o.astype(o_ref.dtype)


def _norm_matmul(x, g, w, *, transpose_out):
    B, S, D = x.shape
    N = w.shape[0] if transpose_out else w.shape[1]
    ts, tn = PROJ_ROWS, PROJ_COLS
    grid = (B, S // ts, N // tn)
    if transpose_out:
        w_spec = pl.BlockSpec((tn, D), lambda b, i, j: (j, 0))
        o_spec = pl.BlockSpec((1, tn, ts), lambda b, i, j: (b, j, i))
        o_shape = jax.ShapeDtypeStruct((B, N, S), BF16)
    else:
        w_spec = pl.BlockSpec((D, tn), lambda b, i, j: (0, j))
        o_spec = pl.BlockSpec((1, ts, tn), lambda b, i, j: (b, i, j))
        o_shape = jax.ShapeDtypeStruct((B, S, N), BF16)
    return pl.pallas_call(
        functools.partial(_norm_matmul_kernel, transpose_out=transpose_out),
        grid=grid,
        in_specs=[pl.BlockSpec((1, ts, D), lambda b, i, j: (b, i, 0)),
                  pl.BlockSpec((1, D), lambda b, i, j: (0, 0)),
                  w_spec],
        out_specs=o_spec,
        out_shape=o_shape,
        scratch_shapes=[pltpu.VMEM((ts, D), BF16)],
        compiler_params=_compiler_params(("parallel", "parallel", "arbitrary")),
        name="norm_matmul_t" if transpose_out else "norm_matmul",
    )(x, g.reshape(1, D), w)


def _out_proj_kernel(y_ref, w_ref, g_ref, x_ref, o_ref, *, transposed_in):
    y = y_ref[0]
    if transposed_in:
        o = lax.dot_general(y, w_ref[...], (((0,), (0,)), ((), ())),
                            preferred_element_type=F32)
    else:
        o = jnp.dot(y, w_ref[...], preferred_element_type=F32)
    ms = jnp.mean(o * o, axis=-1, keepdims=True)
    o_ref[0] = x_ref[0] + o * lax.rsqrt(ms + RMS_EPS) * g_ref[...]


def _out_proj(y, w, g, x, *, transposed_in):
    B, S, D = x.shape
    Di = w.shape[0]
    ts = OUT_ROWS
    if transposed_in:
        y_spec = pl.BlockSpec((1, Di, ts), lambda b, i: (b, 0, i))
    else:
        y_spec = pl.BlockSpec((1, ts, Di), lambda b, i: (b, i, 0))
    return pl.pallas_call(
        functools.partial(_out_proj_kernel, transposed_in=transposed_in),
        grid=(B, S // ts),
        in_specs=[y_spec,
                  pl.BlockSpec((Di, D), lambda b, i: (0, 0)),
                  pl.BlockSpec((1, D), lambda b, i: (0, 0)),
                  pl.BlockSpec((1, ts, D), lambda b, i: (b, i, 0))],
        out_specs=pl.BlockSpec((1, ts, D), lambda b, i: (b, i, 0)),
        out_shape=jax.ShapeDtypeStruct((B, S, D), F32),
        compiler_params=_compiler_params(("parallel", "parallel")),
        name="out_proj_t" if transposed_in else "out_proj",
    )(y, w, g.reshape(1, D), x)


def _retention_kernel(q_ref, k_ref, v_ref, g_ref, cos_ref, sin_ref, dmat_ref,
                      xi_ref, zeta_ref, cdec_ref, o_ref, state_ref, *, dk):
    C = RET_CHUNK
    half = dk // 2

    @pl.when(pl.program_id(2) == 0)
    def _():
        state_ref[...] = jnp.zeros_like(state_ref)

    dmat = dmat_ref[0]
    xi = xi_ref[0]
    zeta = zeta_ref[0]
    cdec = cdec_ref[0]

    def rope(t, c, s):
        t1, t2 = t[:, :half], t[:, half:]
        return jnp.concatenate([t1 * c - t2 * s, t1 * s + t2 * c], axis=-1)

    for ci in range(RET_CHUNKS_PER_STEP):
        rows = pl.ds(ci * C, C)
        c, s = cos_ref[rows, :], sin_ref[rows, :]
        qf = rope(q_ref[0, rows, :].astype(F32), c, s)
        kf = rope(k_ref[0, rows, :].astype(F32), c, s) * (dk ** -0.5)
        q, k = qf.astype(BF16), kf.astype(BF16)
        v = v_ref[0, rows, :]
        state = state_ref[...]
        scores = lax.dot_general(q, k, (((1,), (1,)), ((), ())),
                                 preferred_element_type=F32) * dmat
        inner = jnp.dot(scores.astype(BF16), v, preferred_element_type=F32)
        cross = jnp.dot((qf * xi).astype(BF16), state.astype(BF16),
                        preferred_element_type=F32)
        state_ref[...] = state * cdec + lax.dot_general(
            (kf * zeta).astype(BF16), v, (((0,), (0,)), ((), ())),
            preferred_element_type=F32)
        ret = inner + cross
        mu = jnp.mean(ret, axis=-1, keepdims=True)
        cen = ret - mu
        var = jnp.mean(cen * cen, axis=-1, keepdims=True)
        y = cen * lax.rsqrt(var + GN_EPS)
        gate = g_ref[0, rows, :].astype(F32)
        o_ref[0, rows, :] = (gate * jax.nn.sigmoid(gate) * y).astype(o_ref.dtype)


def _retention(proj, d_model, d_inner):
    B, S, _ = proj.shape
    H, C = RET_HEADS, RET_CHUNK
    dk, dv = d_model // H, d_inner // H
    half = dk // 2
    rows = C * RET_CHUNKS_PER_STEP
    k_off = d_model // dk
    v_off = 2 * d_model // dv
    g_off = (2 * d_model + d_inner) // dv

    pos = jnp.arange(S, dtype=F32)
    inv = ROPE_BASE ** (-jnp.arange(half, dtype=F32) / half)
    ang = pos[:, None] * inv[None, :]
    cos, sin = jnp.cos(ang), jnp.sin(ang)

    log_gamma = jnp.log1p(-jnp.exp2(-5.0 - jnp.arange(H, dtype=F32)))
    idx = jnp.arange(C, dtype=F32)
    rel = idx[:, None] - idx[None, :]
    dmat = jnp.exp(log_gamma[:, None, None] * jnp.maximum(rel, 0.0)) * (rel >= 0)
    xi = jnp.exp(log_gamma[:, None] * (idx + 1.0))
    zeta = jnp.exp(log_gamma[:, None] * (C - 1.0 - idx))
    cdec = jnp.exp(log_gamma * C)
    xi = jnp.broadcast_to(xi[:, :, None], (H, C, dk))
    zeta = jnp.broadcast_to(zeta[:, :, None], (H, C, dk))
    cdec = jnp.broadcast_to(cdec[:, None, None], (H, 1, dv))

    return pl.pallas_call(
        functools.partial(_retention_kernel, dk=dk),
        grid=(B, H, S // rows),
        in_specs=[
            pl.BlockSpec((1, rows, dk), lambda b, h, n: (b, n, h)),
            pl.BlockSpec((1, rows, dk), lambda b, h, n: (b, n, k_off + h)),
            pl.BlockSpec((1, rows, dv), lambda b, h, n: (b, n, v_off + h)),
            pl.BlockSpec((1, rows, dv), lambda b, h, n: (b, n, g_off + h)),
            pl.BlockSpec((rows, half), lambda b, h, n: (n, 0)),
            pl.BlockSpec((rows, half), lambda b, h, n: (n, 0)),
            pl.BlockSpec((1, C, C), lambda b, h, n: (h, 0, 0)),
            pl.BlockSpec((1, C, dk), lambda b, h, n: (h, 0, 0)),
            pl.BlockSpec((1, C, dk), lambda b, h, n: (h, 0, 0)),
            pl.BlockSpec((1, 1, dv), lambda b, h, n: (h, 0, 0)),
        ],
        out_specs=pl.BlockSpec((1, rows, dv), lambda b, h, n: (b, n, h)),
        out_shape=jax.ShapeDtypeStruct((B, S, d_inner), BF16),
        scratch_shapes=[pltpu.VMEM((dk, dv), F32)],
        compiler_params=_compiler_params(("parallel", "parallel", "arbitrary")),
        name="retention",
    )(proj, proj, proj, proj, cos, sin, dmat, xi, zeta, cdec)


def _diff_attn_kernel(tab_ref, bfar_ref, q_ref, k_ref, vt_ref, gt_ref, bk0_ref,
                      bk1_ref, lq1_ref, lk1_ref, lq2_ref, lk2_ref, sg_ref, o_ref,
                      qbd_ref, vaug_ref, bias_ref, m_ref, acc_ref, sa_ref, sb_ref,
                      *, d, lambda_init):
    T, G, H = ATT_TILE, ATT_HEADS_PER_STEP, DIFF_HEADS
    dv = 2 * d
    S = vt_ref.shape[2]
    hp = pl.program_id(1)
    i = pl.program_id(2)

    @pl.when(i == 0)
    def _():
        rows = lax.broadcasted_iota(jnp.int32, (T, T), 0)
        cols = lax.broadcasted_iota(jnp.int32, (T, T), 1)
        for u in range(G):
            h = hp * G + u
            vaug_ref[u, 0:dv, :] = vt_ref[0, u * dv:(u + 1) * dv, :]
            vaug_ref[u, dv:dv + ONES_ROWS, :] = jnp.ones((ONES_ROWS, S), BF16)
            cfar = tab_ref[bfar_ref[0] * H + h]
            for t, bk_ref in enumerate((bk0_ref, bk1_ref)):
                bk = bk_ref[...]
                b = jnp.zeros((T, T), F32)
                for bucket in range(REL_BUCKETS):
                    b = jnp.where(bk == bucket, tab_ref[bucket * H + h], b)
                b = (b - cfar) * LOG2E
                if t == 0:
                    b = jnp.where(rows <= cols, b, -jnp.inf)
                bias_ref[u, t] = b

    for w in range(2 * G):
        sub, u = divmod(w, G)
        q = q_ref[0, sub * T:(sub + 1) * T, u * dv:(u + 1) * dv].astype(F32)
        qt = (q * (d ** -0.5 * LOG2E)).T
        row = lax.broadcasted_iota(jnp.int32, qt.shape, 0)
        qbd_ref[w, :, 0:T] = jnp.where(row < d, qt, 0.0).astype(BF16)
        qbd_ref[w, :, T:2 * T] = jnp.where(row >= d, qt, 0.0).astype(BF16)
    m_ref[...] = jnp.full_like(m_ref, -jnp.inf)
    acc_ref[...] = jnp.zeros_like(acc_ref)

    def logits_into(dst_ref, w, j, bias_index):
        u = w % G
        start = pl.multiple_of(j * T, T)
        kt = k_ref[0, pl.ds(start, T), u * dv:(u + 1) * dv]
        s = jnp.dot(kt, qbd_ref[w], preferred_element_type=F32)
        if bias_index is None:
            dst_ref[w, :, 0:2 * T] = s
        else:
            bias = bias_ref[u, bias_index]
            dst_ref[w, :, 0:T] = s[:, 0:T] + bias
            dst_ref[w, :, T:2 * T] = s[:, T:2 * T] + bias

    def consume(src_ref, w, j):
        u = w % G
        start = pl.multiple_of(j * T, T)
        s = src_ref[w, :, 0:2 * T]
        m_old = m_ref[w]
        m_new = jnp.maximum(m_old, jnp.max(s, axis=0, keepdims=True))
        p = jnp.exp2(s - m_new).astype(BF16)
        alpha = jnp.exp2(m_old - m_new)
        pv = jnp.dot(vaug_ref[u, :, pl.ds(start, T)], p,
                     preferred_element_type=F32)
        acc_ref[w] = acc_ref[w] * alpha + pv
        m_ref[w] = m_new

    first, second, both = range(G), range(G, 2 * G), range(2 * G)

    def step(dst_ref, j_next, bias_of, src_ref, j_cur, cur_units=both):
        for w in both:
            logits_into(dst_ref, w, j_next, bias_of(w))
            if w in cur_units:
                consume(src_ref, w, j_cur)

    def drain(src_ref, j_cur):
        for w in both:
            consume(src_ref, w, j_cur)

    no_bias = lambda w: None
    for w in second:
        logits_into(sa_ref, w, 2 * i + 1, 0)
    step(sb_ref, 2 * i, lambda w: 0 if w in first else 1,
         sa_ref, 2 * i + 1, cur_units=second)

    @pl.when(i == 0)
    def _():
        drain(sb_ref, 2 * i)

    @pl.when(i >= 1)
    def _():
        step(sa_ref, 2 * i - 1, lambda w: 1 if w in first else None,
             sb_ref, 2 * i)
        step(sb_ref, 0, no_bias, sa_ref, 2 * i - 1)

        def pair_body(r, carry):
            step(sa_ref, 2 * r - 1, no_bias, sb_ref, 2 * r - 2)
            step(sb_ref, 2 * r, no_bias, sa_ref, 2 * r - 1)
            return carry

        lax.fori_loop(1, i, pair_body, 0)
        drain(sb_ref, 2 * i - 2)

    lam = (jnp.exp(jnp.sum(lq1_ref[...] * lk1_ref[...], keepdims=True))
           - jnp.exp(jnp.sum(lq2_ref[...] * lk2_ref[...], keepdims=True))
           + lambda_init)
    for w in both:
        sub, u = divmod(w, G)
        acc = acc_ref[w]
        l = acc[dv:dv + 1, :]
        o = acc[:dv, :T] / l[:, :T] - lam * (acc[:dv, T:] / l[:, T:])
        ms = jnp.mean(o * o, axis=0, keepdims=True)
        on = o * lax.rsqrt(ms + GN_EPS) * sg_ref[...] * (1.0 - lambda_init)
        head_rows = slice(u * dv, (u + 1) * dv)
        sub_cols = slice(sub * T, (sub + 1) * T)
        gate = gt_ref[0, head_rows, sub_cols].astype(F32)
        o_ref[0, head_rows, sub_cols] = (
            gate * jax.nn.sigmoid(gate) * on).astype(o_ref.dtype)


def _t5_bucket(n):
    max_exact = REL_BUCKETS // 2
    nf = jnp.maximum(n, max_exact).astype(F32)
    large = max_exact + (jnp.log(nf / max_exact) / math.log(REL_MAX_DIST / max_exact)
                         * (REL_BUCKETS - max_exact)).astype(jnp.int32)
    large = jnp.minimum(large, REL_BUCKETS - 1)
    return jnp.where(n < max_exact, n, large)


def _diff_attn(qk, vgt, lq1, lk1, lq2, lk2, subln_g, rel_bias, lambda_init):
    B, S, _ = qk.shape
    H, T, G = DIFF_HEADS, ATT_TILE, ATT_HEADS_PER_STEP
    d_inner = vgt.shape[1] // 2
    dv = d_inner // H
    d = dv // 2
    gw = G * dv
    n_groups = H // G

    kk = jnp.arange(T, dtype=jnp.int32)[:, None]
    qq = jnp.arange(T, dtype=jnp.int32)[None, :]
    bk0 = _t5_bucket(jnp.maximum(qq - kk, 0))
    bk1 = _t5_bucket(qq - kk + T)
    bfar = _t5_bucket(jnp.full((1,), T + 1, jnp.int32))

    smem = pl.BlockSpec(memory_space=pltpu.SMEM)
    vec = pl.BlockSpec((1, d), lambda b, h, i: (0, 0))
    tile_spec = pl.BlockSpec((T, T), lambda b, h, i: (0, 0))
    return pl.pallas_call(
        functools.partial(_diff_attn_kernel, d=d, lambda_init=lambda_init),
        grid=(B, n_groups, S // (2 * T)),
        in_specs=[
            smem, smem,
            pl.BlockSpec((1, 2 * T, gw), lambda b, h, i: (b, i, h)),
            pl.BlockSpec((1, S, gw), lambda b, h, i: (b, 0, n_groups + h),
                         pipeline_mode=pl.Buffered(1)),
            pl.BlockSpec((1, gw, S), lambda b, h, i: (b, h, 0),
                         pipeline_mode=pl.Buffered(1)),
            pl.BlockSpec((1, gw, 2 * T), lambda b, h, i: (b, n_groups + h, i)),
            tile_spec, tile_spec,
            vec, vec, vec, vec,
            pl.BlockSpec((dv, 1), lambda b, h, i: (0, 0)),
        ],
        out_specs=pl.BlockSpec((1, gw, 2 * T), lambda b, h, i: (b, h, i)),
        out_shape=jax.ShapeDtypeStruct((B, d_inner, S), BF16),
        scratch_shapes=[pltpu.VMEM((2 * G, dv, 2 * T), BF16),
                        pltpu.VMEM((G, dv + ONES_ROWS, S), BF16),
                        pltpu.VMEM((G, 2, T, T), F32),
                        pltpu.VMEM((2 * G, 1, 2 * T), F32),
                        pltpu.VMEM((2 * G, dv + ONES_ROWS, 2 * T), F32),
                        pltpu.VMEM((2 * G, T, 2 * T + LOGITS_PAD), F32),
                        pltpu.VMEM((2 * G, T, 2 * T + LOGITS_PAD), F32)],
        compiler_params=_compiler_params(("parallel", "arbitrary", "arbitrary")),
        name="diff_attn",
    )(rel_bias.astype(F32).reshape(-1), bfar, qk, qk, vgt, vgt, bk0, bk1,
      lq1.reshape(1, d), lk1.reshape(1, d), lq2.reshape(1, d), lk2.reshape(1, d),
      subln_g.reshape(dv, 1))


def kernel(x, pre_norm_g, post_norm_g, ret_w_in, ret_w_out, diff_w_in, diff_w_out,
           diff_lambda_q1, diff_lambda_k1, diff_lambda_q2, diff_lambda_k2,
           diff_subln_g, rel_bias):
    d_model = x.shape[-1]
    d_inner = ret_w_out.shape[1]
    for i in range(DEPTH):
        j = i // N_MIXERS
        if i % N_MIXERS == 0:
            proj = _norm_matmul(x, pre_norm_g[i], ret_w_in[j].astype(BF16),
                                transpose_out=False)
            y = _retention(proj, d_model, d_inner)
            x = _out_proj(y, ret_w_out[j].astype(BF16), post_norm_g[i], x,
                          transposed_in=False)
        else:
            lambda_init = 0.8 - 0.6 * math.exp(-0.3 * i)
            w = diff_w_in[j]
            qk = _norm_matmul(x, pre_norm_g[i], w[:, :2 * d_inner].astype(BF16),
                              transpose_out=False)
            vgt = _norm_matmul(x, pre_norm_g[i], w[:, 2 * d_inner:].T.astype(BF16),
                               transpose_out=True)
            yt = _diff_attn(qk, vgt, diff_lambda_q1[j], diff_lambda_k1[j],
                            diff_lambda_q2[j], diff_lambda_k2[j], diff_subln_g[j],
                            rel_bias, lambda_init)
            x = _out_proj(yt, diff_w_out[j].astype(BF16), post_norm_g[i], x,
                          transposed_in=True)
    return x
```

```python
import functools
import math

import jax
import jax.numpy as jnp
from jax import lax
from jax.experimental import pallas as pl
from jax.experimental.pallas import tpu as pltpu

F32 = jnp.float32
BF16 = jnp.bfloat16

DEPTH = 2
N_MIXERS = 2
RET_HEADS = 4
RET_CHUNK = 128
DIFF_HEADS = 16
REL_BUCKETS = 32
REL_MAX_DIST = 128
ROPE_BASE = 10000.0
RMS_EPS = 1e-6
GN_EPS = 1e-5

V7X_VMEM_BYTES = 64 * 1024 * 1024
VMEM_LIMIT_BYTES = V7X_VMEM_BYTES * 3 // 4

PROJ_ROWS = 1024
PROJ_COLS = 2048
OUT_ROWS = 1024
RET_CHUNKS_PER_STEP = 8
ATT_TILE = 256
ATT_HEADS_PER_STEP = 4
ONES_ROWS = 16
LOG2E = math.log2(math.e)


def _compiler_params(semantics):
    return pltpu.CompilerParams(dimension_semantics=semantics,
                                vmem_limit_bytes=VMEM_LIMIT_BYTES)


def _norm_matmul_kernel(x_ref, g_ref, w_ref, o_ref, h_ref, *, transpose_out):
    @pl.when(pl.program_id(2) == 0)
    def _():
        x = x_ref[0]
        ms = jnp.mean(x * x, axis=-1, keepdims=True)
        h_ref[...] = (x * lax.rsqrt(ms + RMS_EPS) * g_ref[...]).astype(BF16)

    h = h_ref[...]
    if transpose_out:
        o = lax.dot_general(w_ref[...], h, (((1,), (1,)), ((), ())),
                            preferred_element_type=F32)
    else:
        o = jnp.dot(h, w_ref[...], preferred_element_type=F32)
    o_ref[0] = o.astype(o_ref.dtype)


def _norm_matmul(x, g, w, *, transpose_out):
    B, S, D = x.shape
    N = w.shape[0] if transpose_out else w.shape[1]
    ts, tn = PROJ_ROWS, PROJ_COLS
    grid = (B, S // ts, N // tn)
    if transpose_out:
        w_spec = pl.BlockSpec((tn, D), lambda b, i, j: (j, 0))
        o_spec = pl.BlockSpec((1, tn, ts), lambda b, i, j: (b, j, i))
        o_shape = jax.ShapeDtypeStruct((B, N, S), BF16)
    else:
        w_spec = pl.BlockSpec((D, tn), lambda b, i, j: (0, j))
        o_spec = pl.BlockSpec((1, ts, tn), lambda b, i, j: (b, i, j))
        o_shape = jax.ShapeDtypeStruct((B, S, N), BF16)
    return pl.pallas_call(
        functools.partial(_norm_matmul_kernel, transpose_out=transpose_out),
        grid=grid,
        in_specs=[pl.BlockSpec((1, ts, D), lambda b, i, j: (b, i, 0)),
                  pl.BlockSpec((1, D), lambda b, i, j: (0, 0)),
                  w_spec],
        out_specs=o_spec,
        out_shape=o_shape,
        scratch_shapes=[pltpu.VMEM((ts, D), BF16)],
        compiler_params=_compiler_params(("parallel", "parallel", "arbitrary")),
        name="norm_matmul_t" if transpose_out else "norm_matmul",
    )(x, g.reshape(1, D), w)


def _out_proj_kernel(y_ref, w_ref, g_ref, x_ref, o_ref, *, transposed_in):
    y = y_ref[0]
    if transposed_in:
        o = lax.dot_general(y, w_ref[...], (((0,), (0,)), ((), ())),
                            preferred_element_type=F32)
    else:
        o = jnp.dot(y, w_ref[...], preferred_element_type=F32)
    ms = jnp.mean(o * o, axis=-1, keepdims=True)
    o_ref[0] = x_ref[0] + o * lax.rsqrt(ms + RMS_EPS) * g_ref[...]


def _out_proj(y, w, g, x, *, transposed_in):
    B, S, D = x.shape
    Di = w.shape[0]
    ts = OUT_ROWS
    if transposed_in:
        y_spec = pl.BlockSpec((1, Di, ts), lambda b, i: (b, 0, i))
    else:
        y_spec = pl.BlockSpec((1, ts, Di), lambda b, i: (b, i, 0))
    return pl.pallas_call(
        functools.partial(_out_proj_kernel, transposed_in=transposed_in),
        grid=(B, S // ts),
        in_specs=[y_spec,
                  pl.BlockSpec((Di, D), lambda b, i: (0, 0)),
                  pl.BlockSpec((1, D), lambda b, i: (0, 0)),
                  pl.BlockSpec((1, ts, D), lambda b, i: (b, i, 0))],
        out_specs=pl.BlockSpec((1, ts, D), lambda b, i: (b, i, 0)),
        out_shape=jax.ShapeDtypeStruct((B, S, D), F32),
        compiler_params=_compiler_params(("parallel", "parallel")),
        name="out_proj_t" if transposed_in else "out_proj",
    )(y, w, g.reshape(1, D), x)


def _retention_kernel(q_ref, k_ref, v_ref, g_ref, cos_ref, sin_ref, dmat_ref,
                      xi_ref, zeta_ref, cdec_ref, o_ref, state_ref, *, dk):
    C = RET_CHUNK
    half = dk // 2

    @pl.when(pl.program_id(2) == 0)
    def _():
        state_ref[...] = jnp.zeros_like(state_ref)

    dmat = dmat_ref[0]
    xi = xi_ref[0]
    zeta = zeta_ref[0]
    cdec = cdec_ref[0]

    def rope(t, c, s):
        t1, t2 = t[:, :half], t[:, half:]
        return jnp.concatenate([t1 * c - t2 * s, t1 * s + t2 * c], axis=-1)

    for ci in range(RET_CHUNKS_PER_STEP):
        rows = pl.ds(ci * C, C)
        c, s = cos_ref[rows, :], sin_ref[rows, :]
        qf = rope(q_ref[0, rows, :].astype(F32), c, s)
        kf = rope(k_ref[0, rows, :].astype(F32), c, s) * (dk ** -0.5)
        q, k = qf.astype(BF16), kf.astype(BF16)
        v = v_ref[0, rows, :]
        state = state_ref[...]
        scores = lax.dot_general(q, k, (((1,), (1,)), ((), ())),
                                 preferred_element_type=F32) * dmat
        inner = jnp.dot(scores.astype(BF16), v, preferred_element_type=F32)
        cross = jnp.dot((qf * xi).astype(BF16), state.astype(BF16),
                        preferred_element_type=F32)
        state_ref[...] = state * cdec + lax.dot_general(
            (kf * zeta).astype(BF16), v, (((0,), (0,)), ((), ())),
            preferred_element_type=F32)
        ret = inner + cross
        mu = jnp.mean(ret, axis=-1, keepdims=True)
        cen = ret - mu
        var = jnp.mean(cen * cen, axis=-1, keepdims=True)
        y = cen * lax.rsqrt(var + GN_EPS)
        gate = g_ref[0, rows, :].astype(F32)
        o_ref[0, rows, :] = (gate * jax.nn.sigmoid(gate) * y).astype(o_ref.dtype)


def _retention(proj, d_model, d_inner):
    B, S, _ = proj.shape
    H, C = RET_HEADS, RET_CHUNK
    dk, dv = d_model // H, d_inner // H
    half = dk // 2
    rows = C * RET_CHUNKS_PER_STEP
    k_off = d_model // dk
    v_off = 2 * d_model // dv
    g_off = (2 * d_model + d_inner) // dv

    pos = jnp.arange(S, dtype=F32)
    inv = ROPE_BASE ** (-jnp.arange(half, dtype=F32) / half)
    ang = pos[:, None] * inv[None, :]
    cos, sin = jnp.cos(ang), jnp.sin(ang)

    log_gamma = jnp.log1p(-jnp.exp2(-5.0 - jnp.arange(H, dtype=F32)))
    idx = jnp.arange(C, dtype=F32)
    rel = idx[:, None] - idx[None, :]
    dmat = jnp.exp(log_gamma[:, None, None] * jnp.maximum(rel, 0.0)) * (rel >= 0)
    xi = jnp.exp(log_gamma[:, None] * (idx + 1.0))
    zeta = jnp.exp(log_gamma[:, None] * (C - 1.0 - idx))
    cdec = jnp.exp(log_gamma * C)
    xi = jnp.broadcast_to(xi[:, :, None], (H, C, dk))
    zeta = jnp.broadcast_to(zeta[:, :, None], (H, C, dk))
    cdec = jnp.broadcast_to(cdec[:, None, None], (H, 1, dv))

    return pl.pallas_call(
        functools.partial(_retention_kernel, dk=dk),
        grid=(B, H, S // rows),
        in_specs=[
            pl.BlockSpec((1, rows, dk), lambda b, h, n: (b, n, h)),
            pl.BlockSpec((1, rows, dk), lambda b, h, n: (b, n, k_off + h)),
            pl.BlockSpec((1, rows, dv), lambda b, h, n: (b, n, v_off + h)),
            pl.BlockSpec((1, rows, dv), lambda b, h, n: (b, n, g_off + h)),
            pl.BlockSpec((rows, half), lambda b, h, n: (n, 0)),
            pl.BlockSpec((rows, half), lambda b, h, n: (n, 0)),
            pl.BlockSpec((1, C, C), lambda b, h, n: (h, 0, 0)),
            pl.BlockSpec((1, C, dk), lambda b, h, n: (h, 0, 0)),
            pl.BlockSpec((1, C, dk), lambda b, h, n: (h, 0, 0)),
            pl.BlockSpec((1, 1, dv), lambda b, h, n: (h, 0, 0)),
        ],
        out_specs=pl.BlockSpec((1, rows, dv), lambda b, h, n: (b, n, h)),
        out_shape=jax.ShapeDtypeStruct((B, S, d_inner), BF16),
        scratch_shapes=[pltpu.VMEM((dk, dv), F32)],
        compiler_params=_compiler_params(("parallel", "parallel", "arbitrary")),
        name="retention",
    )(proj, proj, proj, proj, cos, sin, dmat, xi, zeta, cdec)


def _diff_attn_kernel(tab_ref, bfar_ref, q_ref, k_ref, vt_ref, gt_ref, bk0_ref,
                      bk1_ref, lq1_ref, lk1_ref, lq2_ref, lk2_ref, sg_ref, o_ref,
                      qbd_ref, vaug_ref, bias_ref, m_ref, acc_ref, sa_ref, sb_ref,
                      *, d, lambda_init):
    T, G, H = ATT_TILE, ATT_HEADS_PER_STEP, DIFF_HEADS
    dv = 2 * d
    S = vt_ref.shape[2]
    hp = pl.program_id(1)
    i = pl.program_id(2)

    @pl.when(i == 0)
    def _():
        rows = lax.broadcasted_iota(jnp.int32, (T, T), 0)
        cols = lax.broadcasted_iota(jnp.int32, (T, T), 1)
        for u in range(G):
            h = hp * G + u
            vaug_ref[u, 0:dv, :] = vt_ref[0, u * dv:(u + 1) * dv, :]
            vaug_ref[u, dv:dv + ONES_ROWS, :] = jnp.ones((ONES_ROWS, S), BF16)
            cfar = tab_ref[bfar_ref[0] * H + h]
            for t, bk_ref in enumerate((bk0_ref, bk1_ref)):
                bk = bk_ref[...]
                b = jnp.zeros((T, T), F32)
                for bucket in range(REL_BUCKETS):
                    b = jnp.where(bk == bucket, tab_ref[bucket * H + h], b)
                b = (b - cfar) * LOG2E
                if t == 0:
                    b = jnp.where(rows <= cols, b, -jnp.inf)
                bias_ref[u, t] = b

    for w in range(2 * G):
        sub, u = divmod(w, G)
        q = q_ref[0, sub * T:(sub + 1) * T, u * dv:(u + 1) * dv].astype(F32)
        qt = (q * (d ** -0.5 * LOG2E)).T
        row = lax.broadcasted_iota(jnp.int32, qt.shape, 0)
        qbd_ref[w, :, 0:T] = jnp.where(row < d, qt, 0.0).astype(BF16)
        qbd_ref[w, :, T:2 * T] = jnp.where(row >= d, qt, 0.0).astype(BF16)
    m_ref[...] = jnp.full_like(m_ref, -jnp.inf)
    acc_ref[...] = jnp.zeros_like(acc_ref)

    def logits_into(dst_ref, w, j, bias_index):
        u = w % G
        start = pl.multiple_of(j * T, T)
        kt = k_ref[0, pl.ds(start, T), u * dv:(u + 1) * dv]
        s = jnp.dot(kt, qbd_ref[w], preferred_element_type=F32)
        if bias_index is None:
            dst_ref[w] = s
        else:
            bias = bias_ref[u, bias_index]
            dst_ref[w, :, 0:T] = s[:, 0:T] + bias
            dst_ref[w, :, T:2 * T] = s[:, T:2 * T] + bias

    def consume(src_ref, w, j):
        u = w % G
        start = pl.multiple_of(j * T, T)
        s = src_ref[w]
        m_old = m_ref[w]
        m_new = jnp.maximum(m_old, jnp.max(s, axis=0, keepdims=True))
        p = jnp.exp2(s - m_new).astype(BF16)
        alpha = jnp.exp2(m_old - m_new)
        pv = jnp.dot(vaug_ref[u, :, pl.ds(start, T)], p,
                     preferred_element_type=F32)
        acc_ref[w] = acc_ref[w] * alpha + pv
        m_ref[w] = m_new

    first, second, both = range(G), range(G, 2 * G), range(2 * G)

    def step(dst_ref, j_next, bias_of, src_ref, j_cur, cur_units=both):
        for w in both:
            logits_into(dst_ref, w, j_next, bias_of(w))
            if w in cur_units:
                consume(src_ref, w, j_cur)

    def drain(src_ref, j_cur):
        for w in both:
            consume(src_ref, w, j_cur)

    no_bias = lambda w: None
    for w in second:
        logits_into(sa_ref, w, 2 * i + 1, 0)
    step(sb_ref, 2 * i, lambda w: 0 if w in first else 1,
         sa_ref, 2 * i + 1, cur_units=second)

    @pl.when(i == 0)
    def _():
        drain(sb_ref, 2 * i)

    @pl.when(i >= 1)
    def _():
        step(sa_ref, 2 * i - 1, lambda w: 1 if w in first else None,
             sb_ref, 2 * i)
        step(sb_ref, 0, no_bias, sa_ref, 2 * i - 1)

        def pair(r):
            step(sa_ref, 2 * r - 1, no_bias, sb_ref, 2 * r - 2)
            step(sb_ref, 2 * r, no_bias, sa_ref, 2 * r - 1)

        def quad_body(t, carry):
            pair(2 * t + 1)
            pair(2 * t + 2)
            return carry

        lax.fori_loop(0, (i - 1) // 2, quad_body, 0)

        @pl.when((i - 1) % 2 == 1)
        def _():
            pair(i - 1)

        drain(sb_ref, 2 * i - 2)

    lam = (jnp.exp(jnp.sum(lq1_ref[...] * lk1_ref[...], keepdims=True))
           - jnp.exp(jnp.sum(lq2_ref[...] * lk2_ref[...], keepdims=True))
           + lambda_init)
    for w in both:
        sub, u = divmod(w, G)
        acc = acc_ref[w]
        l = acc[dv:dv + 1, :]
        o = acc[:dv, :T] / l[:, :T] - lam * (acc[:dv, T:] / l[:, T:])
        ms = jnp.mean(o * o, axis=0, keepdims=True)
        on = o * lax.rsqrt(ms + GN_EPS) * sg_ref[...] * (1.0 - lambda_init)
        head_rows = slice(u * dv, (u + 1) * dv)
        sub_cols = slice(sub * T, (sub + 1) * T)
        gate = gt_ref[0, head_rows, sub_cols].astype(F32)
        o_ref[0, head_rows, sub_cols] = (
            gate * jax.nn.sigmoid(gate) * on).astype(o_ref.dtype)


def _t5_bucket(n):
    max_exact = REL_BUCKETS // 2
    nf = jnp.maximum(n, max_exact).astype(F32)
    large = max_exact + (jnp.log(nf / max_exact) / math.log(REL_MAX_DIST / max_exact)
                         * (REL_BUCKETS - max_exact)).astype(jnp.int32)
    large = jnp.minimum(large, REL_BUCKETS - 1)
    return jnp.where(n < max_exact, n, large)


def _diff_attn(qk, vgt, lq1, lk1, lq2, lk2, subln_g, rel_bias, lambda_init):
    B, S, _ = qk.shape
    H, T, G = DIFF_HEADS, ATT_TILE, ATT_HEADS_PER_STEP
    d_inner = vgt.shape[1] // 2
    dv = d_inner // H
    d = dv // 2
    gw = G * dv
    n_groups = H // G

    kk = jnp.arange(T, dtype=jnp.int32)[:, None]
    qq = jnp.arange(T, dtype=jnp.int32)[None, :]
    bk0 = _t5_bucket(jnp.maximum(qq - kk, 0))
    bk1 = _t5_bucket(qq - kk + T)
    bfar = _t5_bucket(jnp.full((1,), T + 1, jnp.int32))

    smem = pl.BlockSpec(memory_space=pltpu.SMEM)
    vec = pl.BlockSpec((1, d), lambda b, h, i: (0, 0))
    tile_spec = pl.BlockSpec((T, T), lambda b, h, i: (0, 0))
    return pl.pallas_call(
        functools.partial(_diff_attn_kernel, d=d, lambda_init=lambda_init),
        grid=(B, n_groups, S // (2 * T)),
        in_specs=[
            smem, smem,
            pl.BlockSpec((1, 2 * T, gw), lambda b, h, i: (b, i, h)),
            pl.BlockSpec((1, S, gw), lambda b, h, i: (b, 0, n_groups + h),
                         pipeline_mode=pl.Buffered(1)),
            pl.BlockSpec((1, gw, S), lambda b, h, i: (b, h, 0),
                         pipeline_mode=pl.Buffered(1)),
            pl.BlockSpec((1, gw, 2 * T), lambda b, h, i: (b, n_groups + h, i)),
            tile_spec, tile_spec,
            vec, vec, vec, vec,
            pl.BlockSpec((dv, 1), lambda b, h, i: (0, 0)),
        ],
        out_specs=pl.BlockSpec((1, gw, 2 * T), lambda b, h, i: (b, h, i)),
        out_shape=jax.ShapeDtypeStruct((B, d_inner, S), BF16),
        scratch_shapes=[pltpu.VMEM((2 * G, dv, 2 * T), BF16),
                        pltpu.VMEM((G, dv + ONES_ROWS, S), BF16),
                        pltpu.VMEM((G, 2, T, T), F32),
                        pltpu.VMEM((2 * G, 1, 2 * T), F32),
                        pltpu.VMEM((2 * G, dv + ONES_ROWS, 2 * T), F32),
                        pltpu.VMEM((2 * G, T, 2 * T), F32),
                        pltpu.VMEM((2 * G, T, 2 * T), F32)],
        compiler_params=_compiler_params(("parallel", "arbitrary", "arbitrary")),
        name="diff_attn",
    )(rel_bias.astype(F32).reshape(-1), bfar, qk, qk, vgt, vgt, bk0, bk1,
      lq1.reshape(1, d), lk1.reshape(1, d), lq2.reshape(1, d), lk2.reshape(1, d),
      subln_g.reshape(dv, 1))


def kernel(x, pre_norm_g, post_norm_g, ret_w_in, ret_w_out, diff_w_in, diff_w_out,
           diff_lambda_q1, diff_lambda_k1, diff_lambda_q2, diff_lambda_k2,
           diff_subln_g, rel_bias):
    d_model = x.shape[-1]
    d_inner = ret_w_out.shape[1]
    for i in range(DEPTH):
        j = i // N_MIXERS
        if i % N_MIXERS == 0:
            proj = _norm_matmul(x, pre_norm_g[i], ret_w_in[j].astype(BF16),
                                transpose_out=False)
            y = _retention(proj, d_model, d_inner)
            x = _out_proj(y, ret_w_out[j].astype(BF16), post_norm_g[i], x,
                          transposed_in=False)
        else:
            lambda_init = 0.8 - 0.6 * math.exp(-0.3 * i)
            w = diff_w_in[j]
            qk = _norm_matmul(x, pre_norm_g[i], w[:, :2 * d_inner].astype(BF16),
                              transpose_out=False)
            vgt = _norm_matmul(x, pre_norm_g[i], w[:, 2 * d_inner:].T.astype(BF16),
                               transpose_out=True)
            yt = _diff_attn(qk, vgt, diff_lambda_q1[j], diff_lambda_k1[j],
                            diff_lambda_q2[j], diff_lambda_k2[j], diff_subln_g[j],
                            rel_bias, lambda_init)
            x = _out_proj(yt, diff_w_out[j].astype(BF16), post_norm_g[i], x,
                          transposed_in=True)
    return x
```

```python
import functools
import math

import jax
import jax.numpy as jnp
from jax import lax
from jax.experimental import pallas as pl
from jax.experimental.pallas import tpu as pltpu

F32 = jnp.float32
BF16 = jnp.bfloat16

DEPTH = 2
N_MIXERS = 2
RET_HEADS = 4
RET_CHUNK = 128
DIFF_HEADS = 16
REL_BUCKETS = 32
REL_MAX_DIST = 128
ROPE_BASE = 10000.0
RMS_EPS = 1e-6
GN_EPS = 1e-5

V7X_VMEM_BYTES = 64 * 1024 * 1024
VMEM_LIMIT_BYTES = V7X_VMEM_BYTES * 7 // 8

PROJ_ROWS = 1024
PROJ_COLS = 2048
OUT_ROWS = 1024
RET_CHUNKS_PER_STEP = 8
ATT_TILE = 256
ATT_HEADS_PER_STEP = 4
ONES_ROWS = 16
LOG2E = math.log2(math.e)


def _compiler_params(semantics):
    return pltpu.CompilerParams(dimension_semantics=semantics,
                                vmem_limit_bytes=VMEM_LIMIT_BYTES)


def _norm_matmul_kernel(x_ref, g_ref, w_ref, o_ref, h_ref, *, transpose_out):
    @pl.when(pl.program_id(2) == 0)
    def _():
        x = x_ref[0]
        ms = jnp.mean(x * x, axis=-1, keepdims=True)
        h_ref[...] = (x * lax.rsqrt(ms + RMS_EPS) * g_ref[...]).astype(BF16)

    h = h_ref[...]
    if transpose_out:
        o = lax.dot_general(w_ref[...], h, (((1,), (1,)), ((), ())),
                            preferred_element_type=F32)
    else:
        o = jnp.dot(h, w_ref[...], preferred_element_type=F32)
    o_ref[0] = o.astype(o_ref.dtype)


def _norm_matmul(x, g, w, *, transpose_out):
    B, S, D = x.shape
    N = w.shape[0] if transpose_out else w.shape[1]
    ts, tn = PROJ_ROWS, PROJ_COLS
    grid = (B, S // ts, N // tn)
    if transpose_out:
        w_spec = pl.BlockSpec((tn, D), lambda b, i, j: (j, 0))
        o_spec = pl.BlockSpec((1, tn, ts), lambda b, i, j: (b, j, i))
        o_shape = jax.ShapeDtypeStruct((B, N, S), BF16)
    else:
        w_spec = pl.BlockSpec((D, tn), lambda b, i, j: (0, j))
        o_spec = pl.BlockSpec((1, ts, tn), lambda b, i, j: (b, i, j))
        o_shape = jax.ShapeDtypeStruct((B, S, N), BF16)
    return pl.pallas_call(
        functools.partial(_norm_matmul_kernel, transpose_out=transpose_out),
        grid=grid,
        in_specs=[pl.BlockSpec((1, ts, D), lambda b, i, j: (b, i, 0)),
                  pl.BlockSpec((1, D), lambda b, i, j: (0, 0)),
                  w_spec],
        out_specs=o_spec,
        out_shape=o_shape,
        scratch_shapes=[pltpu.VMEM((ts, D), BF16)],
        compiler_params=_compiler_params(("parallel", "parallel", "arbitrary")),
        name="norm_matmul_t" if transpose_out else "norm_matmul",
    )(x, g.reshape(1, D), w)


def _out_proj_kernel(y_ref, w_ref, g_ref, x_ref, o_ref, *, transposed_in):
    y = y_ref[0]
    if transposed_in:
        o = lax.dot_general(y, w_ref[...], (((0,), (0,)), ((), ())),
                            preferred_element_type=F32)
    else:
        o = jnp.dot(y, w_ref[...], preferred_element_type=F32)
    ms = jnp.mean(o * o, axis=-1, keepdims=True)
    o_ref[0] = x_ref[0] + o * lax.rsqrt(ms + RMS_EPS) * g_ref[...]


def _out_proj(y, w, g, x, *, transposed_in):
    B, S, D = x.shape
    Di = w.shape[0]
    ts = OUT_ROWS
    if transposed_in:
        y_spec = pl.BlockSpec((1, Di, ts), lambda b, i: (b, 0, i))
    else:
        y_spec = pl.BlockSpec((1, ts, Di), lambda b, i: (b, i, 0))
    return pl.pallas_call(
        functools.partial(_out_proj_kernel, transposed_in=transposed_in),
        grid=(B, S // ts),
        in_specs=[y_spec,
                  pl.BlockSpec((Di, D), lambda b, i: (0, 0)),
                  pl.BlockSpec((1, D), lambda b, i: (0, 0)),
                  pl.BlockSpec((1, ts, D), lambda b, i: (b, i, 0))],
        out_specs=pl.BlockSpec((1, ts, D), lambda b, i: (b, i, 0)),
        out_shape=jax.ShapeDtypeStruct((B, S, D), F32),
        compiler_params=_compiler_params(("parallel", "parallel")),
        name="out_proj_t" if transposed_in else "out_proj",
    )(y, w, g.reshape(1, D), x)


def _retention_kernel(q_ref, k_ref, v_ref, g_ref, cos_ref, sin_ref, dmat_ref,
                      xi_ref, zeta_ref, cdec_ref, o_ref, state_ref, *, dk):
    C = RET_CHUNK
    half = dk // 2

    @pl.when(pl.program_id(2) == 0)
    def _():
        state_ref[...] = jnp.zeros_like(state_ref)

    dmat = dmat_ref[0]
    xi = xi_ref[0]
    zeta = zeta_ref[0]
    cdec = cdec_ref[0]

    def rope(t, c, s):
        t1, t2 = t[:, :half], t[:, half:]
        return jnp.concatenate([t1 * c - t2 * s, t1 * s + t2 * c], axis=-1)

    for ci in range(RET_CHUNKS_PER_STEP):
        rows = pl.ds(ci * C, C)
        c, s = cos_ref[rows, :], sin_ref[rows, :]
        qf = rope(q_ref[0, rows, :].astype(F32), c, s)
        kf = rope(k_ref[0, rows, :].astype(F32), c, s) * (dk ** -0.5)
        q, k = qf.astype(BF16), kf.astype(BF16)
        v = v_ref[0, rows, :]
        state = state_ref[...]
        scores = lax.dot_general(q, k, (((1,), (1,)), ((), ())),
                                 preferred_element_type=F32) * dmat
        inner = jnp.dot(scores.astype(BF16), v, preferred_element_type=F32)
        cross = jnp.dot((qf * xi).astype(BF16), state.astype(BF16),
                        preferred_element_type=F32)
        state_ref[...] = state * cdec + lax.dot_general(
            (kf * zeta).astype(BF16), v, (((0,), (0,)), ((), ())),
            preferred_element_type=F32)
        ret = inner + cross
        mu = jnp.mean(ret, axis=-1, keepdims=True)
        cen = ret - mu
        var = jnp.mean(cen * cen, axis=-1, keepdims=True)
        y = cen * lax.rsqrt(var + GN_EPS)
        gate = g_ref[0, rows, :].astype(F32)
        o_ref[0, rows, :] = (gate * jax.nn.sigmoid(gate) * y).astype(o_ref.dtype)


def _retention(proj, d_model, d_inner):
    B, S, _ = proj.shape
    H, C = RET_HEADS, RET_CHUNK
    dk, dv = d_model // H, d_inner // H
    half = dk // 2
    rows = C * RET_CHUNKS_PER_STEP
    k_off = d_model // dk
    v_off = 2 * d_model // dv
    g_off = (2 * d_model + d_inner) // dv

    pos = jnp.arange(S, dtype=F32)
    inv = ROPE_BASE ** (-jnp.arange(half, dtype=F32) / half)
    ang = pos[:, None] * inv[None, :]
    cos, sin = jnp.cos(ang), jnp.sin(ang)

    log_gamma = jnp.log1p(-jnp.exp2(-5.0 - jnp.arange(H, dtype=F32)))
    idx = jnp.arange(C, dtype=F32)
    rel = idx[:, None] - idx[None, :]
    dmat = jnp.exp(log_gamma[:, None, None] * jnp.maximum(rel, 0.0)) * (rel >= 0)
    xi = jnp.exp(log_gamma[:, None] * (idx + 1.0))
    zeta = jnp.exp(log_gamma[:, None] * (C - 1.0 - idx))
    cdec = jnp.exp(log_gamma * C)
    xi = jnp.broadcast_to(xi[:, :, None], (H, C, dk))
    zeta = jnp.broadcast_to(zeta[:, :, None], (H, C, dk))
    cdec = jnp.broadcast_to(cdec[:, None, None], (H, 1, dv))

    return pl.pallas_call(
        functools.partial(_retention_kernel, dk=dk),
        grid=(B, H, S // rows),
        in_specs=[
            pl.BlockSpec((1, rows, dk), lambda b, h, n: (b, n, h)),
            pl.BlockSpec((1, rows, dk), lambda b, h, n: (b, n, k_off + h)),
            pl.BlockSpec((1, rows, dv), lambda b, h, n: (b, n, v_off + h)),
            pl.BlockSpec((1, rows, dv), lambda b, h, n: (b, n, g_off + h)),
            pl.BlockSpec((rows, half), lambda b, h, n: (n, 0)),
            pl.BlockSpec((rows, half), lambda b, h, n: (n, 0)),
            pl.BlockSpec((1, C, C), lambda b, h, n: (h, 0, 0)),
            pl.BlockSpec((1, C, dk), lambda b, h, n: (h, 0, 0)),
            pl.BlockSpec((1, C, dk), lambda b, h, n: (h, 0, 0)),
            pl.BlockSpec((1, 1, dv), lambda b, h, n: (h, 0, 0)),
        ],
        out_specs=pl.BlockSpec((1, rows, dv), lambda b, h, n: (b, n, h)),
        out_shape=jax.ShapeDtypeStruct((B, S, d_inner), BF16),
        scratch_shapes=[pltpu.VMEM((dk, dv), F32)],
        compiler_params=_compiler_params(("parallel", "parallel", "arbitrary")),
        name="retention",
    )(proj, proj, proj, proj, cos, sin, dmat, xi, zeta, cdec)


def _diff_attn_kernel(tab_ref, bfar_ref, q_ref, k_ref, vt_ref, gt_ref, bk0_ref,
                      bk1_ref, lq1_ref, lk1_ref, lq2_ref, lk2_ref, sg_ref, o_ref,
                      qbd_ref, vaug_ref, bias_ref, m_ref, acc_ref, sa_ref, sb_ref,
                      *, d, lambda_init):
    T, G, H = ATT_TILE, ATT_HEADS_PER_STEP, DIFF_HEADS
    dv = 2 * d
    S = vt_ref.shape[2]
    hp = pl.program_id(1)
    i = pl.program_id(2)

    @pl.when(i == 0)
    def _():
        rows = lax.broadcasted_iota(jnp.int32, (T, T), 0)
        cols = lax.broadcasted_iota(jnp.int32, (T, T), 1)
        for u in range(G):
            h = hp * G + u
            vaug_ref[u, 0:dv, :] = vt_ref[0, u * dv:(u + 1) * dv, :]
            vaug_ref[u, dv:dv + ONES_ROWS, :] = jnp.ones((ONES_ROWS, S), BF16)
            cfar = tab_ref[bfar_ref[0] * H + h]
            for t, bk_ref in enumerate((bk0_ref, bk1_ref)):
                bk = bk_ref[...]
                b = jnp.zeros((T, T), F32)
                for bucket in range(REL_BUCKETS):
                    b = jnp.where(bk == bucket, tab_ref[bucket * H + h], b)
                b = (b - cfar) * LOG2E
                if t == 0:
                    b = jnp.where(rows <= cols, b, -jnp.inf)
                bias_ref[u, t] = b

    def logits_into(dst_ref, w, j, bias_index):
        u = w % G
        start = pl.multiple_of(j * T, T)
        kt = k_ref[0, pl.ds(start, T), u * dv:(u + 1) * dv]
        s = jnp.dot(kt, qbd_ref[w], preferred_element_type=F32)
        if bias_index is None:
            dst_ref[w] = s
        else:
            bias = bias_ref[u, bias_index]
            dst_ref[w, :, 0:T] = s[:, 0:T] + bias
            dst_ref[w, :, T:2 * T] = s[:, T:2 * T] + bias

    def consume(src_ref, w, j):
        u = w % G
        start = pl.multiple_of(j * T, T)
        s = src_ref[w]
        m_old = m_ref[w]
        m_new = jnp.maximum(m_old, jnp.max(s, axis=0, keepdims=True))
        p = jnp.exp2(s - m_new).astype(BF16)
        alpha = jnp.exp2(m_old - m_new)
        pv = jnp.dot(vaug_ref[u, :, pl.ds(start, T)], p,
                     preferred_element_type=F32)
        acc_ref[w] = acc_ref[w] * alpha + pv
        m_ref[w] = m_new

    first, second, both = range(G), range(G, 2 * G), range(2 * G)

    def step(dst_ref, j_next, bias_of, src_ref, j_cur, cur_units=both):
        for w in both:
            logits_into(dst_ref, w, j_next, bias_of(w))
            if w in cur_units:
                consume(src_ref, w, j_cur)

    no_bias = lambda w: None

    def start():
        for w in both:
            sub, u = divmod(w, G)
            q = q_ref[0, sub * T:(sub + 1) * T, u * dv:(u + 1) * dv].astype(F32)
            qt = (q * (d ** -0.5 * LOG2E)).T
            row = lax.broadcasted_iota(jnp.int32, qt.shape, 0)
            qbd_ref[w, :, 0:T] = jnp.where(row < d, qt, 0.0).astype(BF16)
            qbd_ref[w, :, T:2 * T] = jnp.where(row >= d, qt, 0.0).astype(BF16)
        m_ref[...] = jnp.full_like(m_ref, -jnp.inf)
        acc_ref[...] = jnp.zeros_like(acc_ref)
        for w in second:
            logits_into(sa_ref, w, 2 * i + 1, 0)
        step(sb_ref, 2 * i, lambda w: 0 if w in first else 1,
             sa_ref, 2 * i + 1, cur_units=second)

    def finish(src_ref, j_cur):
        lam = (jnp.exp(jnp.sum(lq1_ref[...] * lk1_ref[...], keepdims=True))
               - jnp.exp(jnp.sum(lq2_ref[...] * lk2_ref[...], keepdims=True))
               + lambda_init)
        for w in both:
            sub, u = divmod(w, G)
            consume(src_ref, w, j_cur)
            acc = acc_ref[w]
            l = acc[dv:dv + 1, :]
            o = acc[:dv, :T] / l[:, :T] - lam * (acc[:dv, T:] / l[:, T:])
            ms = jnp.mean(o * o, axis=0, keepdims=True)
            on = o * lax.rsqrt(ms + GN_EPS) * sg_ref[...] * (1.0 - lambda_init)
            head_rows = slice(u * dv, (u + 1) * dv)
            sub_cols = slice(sub * T, (sub + 1) * T)
            gate = gt_ref[0, head_rows, sub_cols].astype(F32)
            o_ref[0, head_rows, sub_cols] = (
                gate * jax.nn.sigmoid(gate) * on).astype(o_ref.dtype)

    @pl.when(i == 0)
    def _():
        start()
        finish(sb_ref, 2 * i)

    @pl.when(i >= 1)
    def _():
        start()
        step(sa_ref, 2 * i - 1, lambda w: 1 if w in first else None,
             sb_ref, 2 * i)
        step(sb_ref, 0, no_bias, sa_ref, 2 * i - 1)

        def pair(r):
            step(sa_ref, 2 * r - 1, no_bias, sb_ref, 2 * r - 2)
            step(sb_ref, 2 * r, no_bias, sa_ref, 2 * r - 1)

        def quad_body(t, carry):
            pair(2 * t + 1)
            pair(2 * t + 2)
            return carry

        lax.fori_loop(0, (i - 1) // 2, quad_body, 0)

        @pl.when((i - 1) % 2 == 1)
        def _():
            pair(i - 1)
            finish(sb_ref, 2 * i - 2)

        @pl.when((i - 1) % 2 == 0)
        def _():
            finish(sb_ref, 2 * i - 2)


def _t5_bucket(n):
    max_exact = REL_BUCKETS // 2
    nf = jnp.maximum(n, max_exact).astype(F32)
    large = max_exact + (jnp.log(nf / max_exact) / math.log(REL_MAX_DIST / max_exact)
                         * (REL_BUCKETS - max_exact)).astype(jnp.int32)
    large = jnp.minimum(large, REL_BUCKETS - 1)
    return jnp.where(n < max_exact, n, large)


def _diff_attn(qk, vgt, lq1, lk1, lq2, lk2, subln_g, rel_bias, lambda_init):
    B, S, _ = qk.shape
    H, T, G = DIFF_HEADS, ATT_TILE, ATT_HEADS_PER_STEP
    d_inner = vgt.shape[1] // 2
    dv = d_inner // H
    d = dv // 2
    gw = G * dv
    n_groups = H // G

    kk = jnp.arange(T, dtype=jnp.int32)[:, None]
    qq = jnp.arange(T, dtype=jnp.int32)[None, :]
    bk0 = _t5_bucket(jnp.maximum(qq - kk, 0))
    bk1 = _t5_bucket(qq - kk + T)
    bfar = _t5_bucket(jnp.full((1,), T + 1, jnp.int32))

    smem = pl.BlockSpec(memory_space=pltpu.SMEM)
    vec = pl.BlockSpec((1, d), lambda b, h, i: (0, 0))
    tile_spec = pl.BlockSpec((T, T), lambda b, h, i: (0, 0))
    return pl.pallas_call(
        functools.partial(_diff_attn_kernel, d=d, lambda_init=lambda_init),
        grid=(B, n_groups, S // (2 * T)),
        in_specs=[
            smem, smem,
            pl.BlockSpec((1, 2 * T, gw), lambda b, h, i: (b, i, h)),
            pl.BlockSpec((1, S, gw), lambda b, h, i: (b, 0, n_groups + h),
                         pipeline_mode=pl.Buffered(1)),
            pl.BlockSpec((1, gw, S), lambda b, h, i: (b, h, 0),
                         pipeline_mode=pl.Buffered(1)),
            pl.BlockSpec((1, gw, 2 * T), lambda b, h, i: (b, n_groups + h, i)),
            tile_spec, tile_spec,
            vec, vec, vec, vec,
            pl.BlockSpec((dv, 1), lambda b, h, i: (0, 0)),
        ],
        out_specs=pl.BlockSpec((1, gw, 2 * T), lambda b, h, i: (b, h, i)),
        out_shape=jax.ShapeDtypeStruct((B, d_inner, S), BF16),
        scratch_shapes=[pltpu.VMEM((2 * G, dv, 2 * T), BF16),
                        pltpu.VMEM((G, dv + ONES_ROWS, S), BF16),
                        pltpu.VMEM((G, 2, T, T), F32),
                        pltpu.VMEM((2 * G, 1, 2 * T), F32),
                        pltpu.VMEM((2 * G, dv + ONES_ROWS, 2 * T), F32),
                        pltpu.VMEM((2 * G, T, 2 * T), F32),
                        pltpu.VMEM((2 * G, T, 2 * T), F32)],
        compiler_params=_compiler_params(("parallel", "arbitrary", "arbitrary")),
        name="diff_attn",
    )(rel_bias.astype(F32).reshape(-1), bfar, qk, qk, vgt, vgt, bk0, bk1,
      lq1.reshape(1, d), lk1.reshape(1, d), lq2.reshape(1, d), lk2.reshape(1, d),
      subln_g.reshape(dv, 1))


def kernel(x, pre_norm_g, post_norm_g, ret_w_in, ret_w_out, diff_w_in, diff_w_out,
           diff_lambda_q1, diff_lambda_k1, diff_lambda_q2, diff_lambda_k2,
           diff_subln_g, rel_bias):
    d_model = x.shape[-1]
    d_inner = ret_w_out.shape[1]
    for i in range(DEPTH):
        j = i // N_MIXERS
        if i % N_MIXERS == 0:
            proj = _norm_matmul(x, pre_norm_g[i], ret_w_in[j].astype(BF16),
                                transpose_out=False)
            y = _retention(proj, d_model, d_inner)
            x = _out_proj(y, ret_w_out[j].astype(BF16), post_norm_g[i], x,
                          transposed_in=False)
        else:
            lambda_init = 0.8 - 0.6 * math.exp(-0.3 * i)
            w = diff_w_in[j]
            qk = _norm_matmul(x, pre_norm_g[i], w[:, :2 * d_inner].astype(BF16),
                              transpose_out=False)
            vgt = _norm_matmul(x, pre_norm_g[i], w[:, 2 * d_inner:].T.astype(BF16),
                               transpose_out=True)
            yt = _diff_attn(qk, vgt, diff_lambda_q1[j], diff_lambda_k1[j],
                            diff_lambda_q2[j], diff_lambda_k2[j], diff_subln_g[j],
                            rel_bias, lambda_init)
            x = _out_proj(yt, diff_w_out[j].astype(BF16), post_norm_g[i], x,
                          transposed_in=True)
    return x
```

```python
import functools
import math

import jax
import jax.numpy as jnp
from jax import lax
from jax.experimental import pallas as pl
from jax.experimental.pallas import tpu as pltpu

F32 = jnp.float32
BF16 = jnp.bfloat16

DEPTH = 2
N_MIXERS = 2
RET_HEADS = 4
RET_CHUNK = 128
DIFF_HEADS = 16
REL_BUCKETS = 32
REL_MAX_DIST = 128
ROPE_BASE = 10000.0
RMS_EPS = 1e-6
GN_EPS = 1e-5

V7X_VMEM_BYTES = 64 * 1024 * 1024
VMEM_LIMIT_BYTES = V7X_VMEM_BYTES * 7 // 8

PROJ_ROWS = 1024
PROJ_COLS = 2048
OUT_ROWS = 1024
RET_CHUNKS_PER_STEP = 8
ATT_TILE = 256
ATT_HEADS_PER_STEP = 4
ONES_ROWS = 16
LOG2E = math.log2(math.e)


def _compiler_params(semantics):
    return pltpu.CompilerParams(dimension_semantics=semantics,
                                vmem_limit_bytes=VMEM_LIMIT_BYTES)


def _store_normed(x_ref, g_ref, h_ref):
    x = x_ref[0]
    ms = jnp.mean(x * x, axis=-1, keepdims=True)
    h_ref[...] = (x * lax.rsqrt(ms + RMS_EPS) * g_ref[...]).astype(BF16)


def _norm_matmul_kernel(x_ref, g_ref, w_ref, o_ref, h_ref):
    @pl.when(pl.program_id(2) == 0)
    def _():
        _store_normed(x_ref, g_ref, h_ref)

    o_ref[0] = jnp.dot(h_ref[...], w_ref[...],
                       preferred_element_type=F32).astype(o_ref.dtype)


def _norm_matmul(x, g, w):
    B, S, D = x.shape
    N = w.shape[1]
    ts, tn = PROJ_ROWS, PROJ_COLS
    return pl.pallas_call(
        _norm_matmul_kernel,
        grid=(B, S // ts, N // tn),
        in_specs=[pl.BlockSpec((1, ts, D), lambda b, i, j: (b, i, 0)),
                  pl.BlockSpec((1, D), lambda b, i, j: (0, 0)),
                  pl.BlockSpec((D, tn), lambda b, i, j: (0, j))],
        out_specs=pl.BlockSpec((1, ts, tn), lambda b, i, j: (b, i, j)),
        out_shape=jax.ShapeDtypeStruct((B, S, N), BF16),
        scratch_shapes=[pltpu.VMEM((ts, D), BF16)],
        compiler_params=_compiler_params(("parallel", "parallel", "arbitrary")),
        name="norm_matmul",
    )(x, g.reshape(1, D), w)


def _norm_matmul_diff_kernel(x_ref, g_ref, w_ref, qk_ref, vaug_ref, gt_ref, h_ref):
    j = pl.program_id(2)

    @pl.when(j == 0)
    def _():
        _store_normed(x_ref, g_ref, h_ref)

    def project():
        return jnp.dot(h_ref[...], w_ref[...], preferred_element_type=F32)

    @pl.when(j < 2)
    def _():
        qk_ref[0] = project().astype(BF16)

    @pl.when(j == 2)
    def _():
        n_heads, rows_aug, ts = vaug_ref.shape[1:]
        dv = w_ref.shape[1] // n_heads
        for h in range(0, n_heads, 2):
            ot = jnp.dot(h_ref[...], w_ref[:, h * dv:(h + 2) * dv],
                         preferred_element_type=F32).T.astype(BF16)
            for e in range(2):
                vaug_ref[0, h + e, 0:dv, :] = ot[e * dv:(e + 1) * dv, :]
                vaug_ref[0, h + e, dv:rows_aug, :] = jnp.ones(
                    (rows_aug - dv, ts), BF16)

    @pl.when(j == 3)
    def _():
        gt_ref[0] = project().T.astype(BF16)


def _norm_matmul_diff(x, g, w, n_heads):
    B, S, D = x.shape
    Di = w.shape[1] // 4
    dv = Di // n_heads
    ts = PROJ_ROWS
    return pl.pallas_call(
        _norm_matmul_diff_kernel,
        grid=(B, S // ts, 4),
        in_specs=[pl.BlockSpec((1, ts, D), lambda b, i, j: (b, i, 0)),
                  pl.BlockSpec((1, D), lambda b, i, j: (0, 0)),
                  pl.BlockSpec((D, Di), lambda b, i, j: (0, j))],
        out_specs=[
            pl.BlockSpec((1, ts, Di), lambda b, i, j: (b, i, jnp.minimum(j, 1))),
            pl.BlockSpec((1, n_heads, dv + ONES_ROWS, ts),
                         lambda b, i, j: (b, 0, 0, i)),
            pl.BlockSpec((1, Di, ts), lambda b, i, j: (b, 0, i)),
        ],
        out_shape=[jax.ShapeDtypeStruct((B, S, 2 * Di), BF16),
                   jax.ShapeDtypeStruct((B, n_heads, dv + ONES_ROWS, S), BF16),
                   jax.ShapeDtypeStruct((B, Di, S), BF16)],
        scratch_shapes=[pltpu.VMEM((ts, D), BF16)],
        compiler_params=_compiler_params(("parallel", "parallel", "arbitrary")),
        name="norm_matmul_diff",
    )(x, g.reshape(1, D), w)


def _out_proj_kernel(y_ref, w_ref, g_ref, x_ref, o_ref, *, transposed_in):
    y = y_ref[0]
    if transposed_in:
        o = lax.dot_general(y, w_ref[...], (((0,), (0,)), ((), ())),
                            preferred_element_type=F32)
    else:
        o = jnp.dot(y, w_ref[...], preferred_element_type=F32)
    ms = jnp.mean(o * o, axis=-1, keepdims=True)
    o_ref[0] = x_ref[0] + o * lax.rsqrt(ms + RMS_EPS) * g_ref[...]


def _out_proj(y, w, g, x, *, transposed_in):
    B, S, D = x.shape
    Di = w.shape[0]
    ts = OUT_ROWS
    if transposed_in:
        y_spec = pl.BlockSpec((1, Di, ts), lambda b, i: (b, 0, i))
    else:
        y_spec = pl.BlockSpec((1, ts, Di), lambda b, i: (b, i, 0))
    return pl.pallas_call(
        functools.partial(_out_proj_kernel, transposed_in=transposed_in),
        grid=(B, S // ts),
        in_specs=[y_spec,
                  pl.BlockSpec((Di, D), lambda b, i: (0, 0)),
                  pl.BlockSpec((1, D), lambda b, i: (0, 0)),
                  pl.BlockSpec((1, ts, D), lambda b, i: (b, i, 0))],
        out_specs=pl.BlockSpec((1, ts, D), lambda b, i: (b, i, 0)),
        out_shape=jax.ShapeDtypeStruct((B, S, D), F32),
        compiler_params=_compiler_params(("parallel", "parallel")),
        name="out_proj_t" if transposed_in else "out_proj",
    )(y, w, g.reshape(1, D), x)


def _retention_kernel(q_ref, k_ref, v_ref, g_ref, cos_ref, sin_ref, dmat_ref,
                      xi_ref, zeta_ref, cdec_ref, o_ref, state_ref, *, dk):
    C = RET_CHUNK
    half = dk // 2

    @pl.when(pl.program_id(2) == 0)
    def _():
        state_ref[...] = jnp.zeros_like(state_ref)

    dmat = dmat_ref[0]
    xi = xi_ref[0]
    zeta = zeta_ref[0]
    cdec = cdec_ref[0]

    def rope(t, c, s):
        t1, t2 = t[:, :half], t[:, half:]
        return jnp.concatenate([t1 * c - t2 * s, t1 * s + t2 * c], axis=-1)

    for ci in range(RET_CHUNKS_PER_STEP):
        rows = pl.ds(ci * C, C)
        c, s = cos_ref[rows, :], sin_ref[rows, :]
        qf = rope(q_ref[0, rows, :].astype(F32), c, s)
        kf = rope(k_ref[0, rows, :].astype(F32), c, s) * (dk ** -0.5)
        q, k = qf.astype(BF16), kf.astype(BF16)
        v = v_ref[0, rows, :]
        state = state_ref[...]
        scores = lax.dot_general(q, k, (((1,), (1,)), ((), ())),
                                 preferred_element_type=F32) * dmat
        inner = jnp.dot(scores.astype(BF16), v, preferred_element_type=F32)
        cross = jnp.dot((qf * xi).astype(BF16), state.astype(BF16),
                        preferred_element_type=F32)
        state_ref[...] = state * cdec + lax.dot_general(
            (kf * zeta).astype(BF16), v, (((0,), (0,)), ((), ())),
            preferred_element_type=F32)
        ret = inner + cross
        mu = jnp.mean(ret, axis=-1, keepdims=True)
        cen = ret - mu
        var = jnp.mean(cen * cen, axis=-1, keepdims=True)
        y = cen * lax.rsqrt(var + GN_EPS)
        gate = g_ref[0, rows, :].astype(F32)
        o_ref[0, rows, :] = (gate * jax.nn.sigmoid(gate) * y).astype(o_ref.dtype)


def _retention(proj, d_model, d_inner):
    B, S, _ = proj.shape
    H, C = RET_HEADS, RET_CHUNK
    dk, dv = d_model // H, d_inner // H
    half = dk // 2
    rows = C * RET_CHUNKS_PER_STEP
    k_off = d_model // dk
    v_off = 2 * d_model // dv
    g_off = (2 * d_model + d_inner) // dv

    pos = jnp.arange(S, dtype=F32)
    inv = ROPE_BASE ** (-jnp.arange(half, dtype=F32) / half)
    ang = pos[:, None] * inv[None, :]
    cos, sin = jnp.cos(ang), jnp.sin(ang)

    log_gamma = jnp.log1p(-jnp.exp2(-5.0 - jnp.arange(H, dtype=F32)))
    idx = jnp.arange(C, dtype=F32)
    rel = idx[:, None] - idx[None, :]
    dmat = jnp.exp(log_gamma[:, None, None] * jnp.maximum(rel, 0.0)) * (rel >= 0)
    xi = jnp.exp(log_gamma[:, None] * (idx + 1.0))
    zeta = jnp.exp(log_gamma[:, None] * (C - 1.0 - idx))
    cdec = jnp.exp(log_gamma * C)
    xi = jnp.broadcast_to(xi[:, :, None], (H, C, dk))
    zeta = jnp.broadcast_to(zeta[:, :, None], (H, C, dk))
    cdec = jnp.broadcast_to(cdec[:, None, None], (H, 1, dv))

    return pl.pallas_call(
        functools.partial(_retention_kernel, dk=dk),
        grid=(B, H, S // rows),
        in_specs=[
            pl.BlockSpec((1, rows, dk), lambda b, h, n: (b, n, h)),
            pl.BlockSpec((1, rows, dk), lambda b, h, n: (b, n, k_off + h)),
            pl.BlockSpec((1, rows, dv), lambda b, h, n: (b, n, v_off + h)),
            pl.BlockSpec((1, rows, dv), lambda b, h, n: (b, n, g_off + h)),
            pl.BlockSpec((rows, half), lambda b, h, n: (n, 0)),
            pl.BlockSpec((rows, half), lambda b, h, n: (n, 0)),
            pl.BlockSpec((1, C, C), lambda b, h, n: (h, 0, 0)),
            pl.BlockSpec((1, C, dk), lambda b, h, n: (h, 0, 0)),
            pl.BlockSpec((1, C, dk), lambda b, h, n: (h, 0, 0)),
            pl.BlockSpec((1, 1, dv), lambda b, h, n: (h, 0, 0)),
        ],
        out_specs=pl.BlockSpec((1, rows, dv), lambda b, h, n: (b, n, h)),
        out_shape=jax.ShapeDtypeStruct((B, S, d_inner), BF16),
        scratch_shapes=[pltpu.VMEM((dk, dv), F32)],
        compiler_params=_compiler_params(("parallel", "parallel", "arbitrary")),
        name="retention",
    )(proj, proj, proj, proj, cos, sin, dmat, xi, zeta, cdec)


def _diff_attn_kernel(tab_ref, bfar_ref, q_ref, k_ref, vaug_ref, gt_ref, bk0_ref,
                      bk1_ref, lq1_ref, lk1_ref, lq2_ref, lk2_ref, sg_ref, o_ref,
                      qbd_ref, bias_ref, m_ref, acc_ref, sa_ref, sb_ref,
                      *, d, lambda_init):
    T, G, H = ATT_TILE, ATT_HEADS_PER_STEP, DIFF_HEADS
    dv = 2 * d
    hp = pl.program_id(1)
    i = pl.program_id(2)

    @pl.when(i == 0)
    def _():
        rows = lax.broadcasted_iota(jnp.int32, (T, T), 0)
        cols = lax.broadcasted_iota(jnp.int32, (T, T), 1)
        for u in range(G):
            h = hp * G + u
            cfar = tab_ref[bfar_ref[0] * H + h]
            for t, bk_ref in enumerate((bk0_ref, bk1_ref)):
                bk = bk_ref[...]
                b = jnp.zeros((T, T), F32)
                for bucket in range(REL_BUCKETS):
                    b = jnp.where(bk == bucket, tab_ref[bucket * H + h], b)
                b = (b - cfar) * LOG2E
                if t == 0:
                    b = jnp.where(rows <= cols, b, -jnp.inf)
                bias_ref[u, t] = b

    def logits_into(dst_ref, w, j, bias_index):
        u = w % G
        start = pl.multiple_of(j * T, T)
        kt = k_ref[0, pl.ds(start, T), u * dv:(u + 1) * dv]
        s = jnp.dot(kt, qbd_ref[w], preferred_element_type=F32)
        if bias_index is None:
            dst_ref[w] = s
        else:
            bias = bias_ref[u, bias_index]
            dst_ref[w, :, 0:T] = s[:, 0:T] + bias
            dst_ref[w, :, T:2 * T] = s[:, T:2 * T] + bias

    def consume(src_ref, w, j):
        u = w % G
        start = pl.multiple_of(j * T, T)
        s = src_ref[w]
        m_old = m_ref[w]
        m_new = jnp.maximum(m_old, jnp.max(s, axis=0, keepdims=True))
        p = jnp.exp2(s - m_new).astype(BF16)
        alpha = jnp.exp2(m_old - m_new)
        pv = jnp.dot(vaug_ref[0, u, :, pl.ds(start, T)], p,
                     preferred_element_type=F32)
        acc_ref[w] = acc_ref[w] * alpha + pv
        m_ref[w] = m_new

    first, second, both = range(G), range(G, 2 * G), range(2 * G)

    def step(dst_ref, j_next, bias_of, src_ref, j_cur, cur_units=both):
        for w in both:
            logits_into(dst_ref, w, j_next, bias_of(w))
            if w in cur_units:
                consume(src_ref, w, j_cur)

    no_bias = lambda w: None

    def start():
        for w in both:
            sub, u = divmod(w, G)
            q = q_ref[0, sub * T:(sub + 1) * T, u * dv:(u + 1) * dv].astype(F32)
            qt = (q * (d ** -0.5 * LOG2E)).T
            row = lax.broadcasted_iota(jnp.int32, qt.shape, 0)
            qbd_ref[w, :, 0:T] = jnp.where(row < d, qt, 0.0).astype(BF16)
            qbd_ref[w, :, T:2 * T] = jnp.where(row >= d, qt, 0.0).astype(BF16)
        m_ref[...] = jnp.full_like(m_ref, -jnp.inf)
        acc_ref[...] = jnp.zeros_like(acc_ref)
        for w in second:
            logits_into(sa_ref, w, 2 * i + 1, 0)
        step(sb_ref, 2 * i, lambda w: 0 if w in first else 1,
             sa_ref, 2 * i + 1, cur_units=second)

    def finish(src_ref, j_cur):
        lam = (jnp.exp(jnp.sum(lq1_ref[...] * lk1_ref[...], keepdims=True))
               - jnp.exp(jnp.sum(lq2_ref[...] * lk2_ref[...], keepdims=True))
               + lambda_init)
        for w in both:
            sub, u = divmod(w, G)
            consume(src_ref, w, j_cur)
            acc = acc_ref[w]
            l = acc[dv:dv + 1, :]
            o = acc[:dv, :T] / l[:, :T] - lam * (acc[:dv, T:] / l[:, T:])
            ms = jnp.mean(o * o, axis=0, keepdims=True)
            on = o * lax.rsqrt(ms + GN_EPS) * sg_ref[...] * (1.0 - lambda_init)
            head_rows = slice(u * dv, (u + 1) * dv)
            sub_cols = slice(sub * T, (sub + 1) * T)
            gate = gt_ref[0, head_rows, sub_cols].astype(F32)
            o_ref[0, head_rows, sub_cols] = (
                gate * jax.nn.sigmoid(gate) * on).astype(o_ref.dtype)

    @pl.when(i == 0)
    def _():
        start()
        finish(sb_ref, 2 * i)

    @pl.when(i >= 1)
    def _():
        start()
        step(sa_ref, 2 * i - 1, lambda w: 1 if w in first else None,
             sb_ref, 2 * i)
        step(sb_ref, 0, no_bias, sa_ref, 2 * i - 1)

        def pair(r):
            step(sa_ref, 2 * r - 1, no_bias, sb_ref, 2 * r - 2)
            step(sb_ref, 2 * r, no_bias, sa_ref, 2 * r - 1)

        def quad_body(t, carry):
            pair(2 * t + 1)
            pair(2 * t + 2)
            return carry

        lax.fori_loop(0, (i - 1) // 2, quad_body, 0)

        @pl.when((i - 1) % 2 == 1)
        def _():
            pair(i - 1)
            finish(sb_ref, 2 * i - 2)

        @pl.when((i - 1) % 2 == 0)
        def _():
            finish(sb_ref, 2 * i - 2)


def _t5_bucket(n):
    max_exact = REL_BUCKETS // 2
    nf = jnp.maximum(n, max_exact).astype(F32)
    large = max_exact + (jnp.log(nf / max_exact) / math.log(REL_MAX_DIST / max_exact)
                         * (REL_BUCKETS - max_exact)).astype(jnp.int32)
    large = jnp.minimum(large, REL_BUCKETS - 1)
    return jnp.where(n < max_exact, n, large)


def _diff_attn(qk, vaug, gt, lq1, lk1, lq2, lk2, subln_g, rel_bias, lambda_init):
    B, S, _ = qk.shape
    H, T, G = DIFF_HEADS, ATT_TILE, ATT_HEADS_PER_STEP
    d_inner = gt.shape[1]
    dv = d_inner // H
    d = dv // 2
    gw = G * dv
    n_groups = H // G

    kk = jnp.arange(T, dtype=jnp.int32)[:, None]
    qq = jnp.arange(T, dtype=jnp.int32)[None, :]
    bk0 = _t5_bucket(jnp.maximum(qq - kk, 0))
    bk1 = _t5_bucket(qq - kk + T)
    bfar = _t5_bucket(jnp.full((1,), T + 1, jnp.int32))

    smem = pl.BlockSpec(memory_space=pltpu.SMEM)
    vec = pl.BlockSpec((1, d), lambda b, h, i: (0, 0))
    tile_spec = pl.BlockSpec((T, T), lambda b, h, i: (0, 0))
    return pl.pallas_call(
        functools.partial(_diff_attn_kernel, d=d, lambda_init=lambda_init),
        grid=(B, n_groups, S // (2 * T)),
        in_specs=[
            smem, smem,
            pl.BlockSpec((1, 2 * T, gw), lambda b, h, i: (b, i, h)),
            pl.BlockSpec((1, S, gw), lambda b, h, i: (b, 0, n_groups + h),
                         pipeline_mode=pl.Buffered(1)),
            pl.BlockSpec((1, G, dv + ONES_ROWS, S), lambda b, h, i: (b, h, 0, 0),
                         pipeline_mode=pl.Buffered(1)),
            pl.BlockSpec((1, gw, 2 * T), lambda b, h, i: (b, h, i)),
            tile_spec, tile_spec,
            vec, vec, vec, vec,
            pl.BlockSpec((dv, 1), lambda b, h, i: (0, 0)),
        ],
        out_specs=pl.BlockSpec((1, gw, 2 * T), lambda b, h, i: (b, h, i)),
        out_shape=jax.ShapeDtypeStruct((B, d_inner, S), BF16),
        scratch_shapes=[pltpu.VMEM((2 * G, dv, 2 * T), BF16),
                        pltpu.VMEM((G, 2, T, T), F32),
                        pltpu.VMEM((2 * G, 1, 2 * T), F32),
                        pltpu.VMEM((2 * G, dv + ONES_ROWS, 2 * T), F32),
                        pltpu.VMEM((2 * G, T, 2 * T), F32),
                        pltpu.VMEM((2 * G, T, 2 * T), F32)],
        compiler_params=_compiler_params(("parallel", "arbitrary", "arbitrary")),
        name="diff_attn",
    )(rel_bias.astype(F32).reshape(-1), bfar, qk, qk, vaug, gt, bk0, bk1,
      lq1.reshape(1, d), lk1.reshape(1, d), lq2.reshape(1, d), lk2.reshape(1, d),
      subln_g.reshape(dv, 1))


def kernel(x, pre_norm_g, post_norm_g, ret_w_in, ret_w_out, diff_w_in, diff_w_out,
           diff_lambda_q1, diff_lambda_k1, diff_lambda_q2, diff_lambda_k2,
           diff_subln_g, rel_bias):
    d_model = x.shape[-1]
    d_inner = ret_w_out.shape[1]
    for i in range(DEPTH):
        j = i // N_MIXERS
        if i % N_MIXERS == 0:
            proj = _norm_matmul(x, pre_norm_g[i], ret_w_in[j].astype(BF16))
            y = _retention(proj, d_model, d_inner)
            x = _out_proj(y, ret_w_out[j].astype(BF16), post_norm_g[i], x,
                          transposed_in=False)
        else:
            lambda_init = 0.8 - 0.6 * math.exp(-0.3 * i)
            qk, vaug, gt = _norm_matmul_diff(
                x, pre_norm_g[i], diff_w_in[j].astype(BF16), DIFF_HEADS)
            yt = _diff_attn(qk, vaug, gt, diff_lambda_q1[j], diff_lambda_k1[j],
                            diff_lambda_q2[j], diff_lambda_k2[j], diff_subln_g[j],
                            rel_bias, lambda_init)
            x = _out_proj(yt, diff_w_out[j].astype(BF16), post_norm_g[i], x,
                          transposed_in=True)
    return x
```

```python
import functools
import math

import jax
import jax.numpy as jnp
from jax import lax
from jax.experimental import pallas as pl
from jax.experimental.pallas import tpu as pltpu

F32 = jnp.float32
BF16 = jnp.bfloat16

DEPTH = 2
N_MIXERS = 2
RET_HEADS = 4
RET_CHUNK = 128
DIFF_HEADS = 16
REL_BUCKETS = 32
REL_MAX_DIST = 128
ROPE_BASE = 10000.0
RMS_EPS = 1e-6
GN_EPS = 1e-5

V7X_VMEM_BYTES = 64 * 1024 * 1024
VMEM_LIMIT_BYTES = V7X_VMEM_BYTES * 7 // 8

PROJ_ROWS = 1024
PROJ_COLS = 2048
OUT_ROWS = 1024
RET_CHUNKS_PER_STEP = 8
ATT_TILE = 256
ATT_HEADS_PER_STEP = 4
ONES_ROWS = 16
LOG2E = math.log2(math.e)


def _compiler_params(semantics):
    return pltpu.CompilerParams(dimension_semantics=semantics,
                                vmem_limit_bytes=VMEM_LIMIT_BYTES)


def _store_normed(x_ref, g_ref, h_ref):
    x = x_ref[0]
    ms = jnp.mean(x * x, axis=-1, keepdims=True)
    h_ref[...] = (x * lax.rsqrt(ms + RMS_EPS) * g_ref[...]).astype(BF16)


def _silu(x):
    return x * jax.nn.sigmoid(x)


def _norm_matmul_ret_kernel(x_ref, g_ref, w_ref, cos_ref, sin_ref, o_ref, h_ref,
                            *, dk):
    j = pl.program_id(2)

    def project(cols=slice(None)):
        return jnp.dot(h_ref[...], w_ref[:, cols], preferred_element_type=F32)

    @pl.when(j == 0)
    def _():
        _store_normed(x_ref, g_ref, h_ref)
        half = dk // 2
        n_blocks = w_ref.shape[1] // dk
        c, s = cos_ref[...], sin_ref[...]
        for blk in range(n_blocks):
            o = project(slice(blk * dk, (blk + 1) * dk))
            scale = 1.0 if blk < n_blocks // 2 else dk ** -0.5
            x1, x2 = o[:, :half], o[:, half:]
            o_ref[0, :, blk * dk:blk * dk + half] = (
                (x1 * c - x2 * s) * scale).astype(BF16)
            o_ref[0, :, blk * dk + half:(blk + 1) * dk] = (
                (x1 * s + x2 * c) * scale).astype(BF16)

    @pl.when(j == 1)
    def _():
        o_ref[0] = project().astype(BF16)

    @pl.when(j == 2)
    def _():
        o_ref[0] = _silu(project()).astype(BF16)


def _norm_matmul_ret(x, g, w, d_inner, n_heads):
    B, S, D = x.shape
    N = w.shape[1]
    ts, tn = PROJ_ROWS, d_inner
    dk = (N - 2 * d_inner) // 2 // n_heads
    half = dk // 2
    assert N == 3 * tn, "q | k must fill exactly one column tile"

    pos = jnp.arange(S, dtype=F32)
    inv = ROPE_BASE ** (-jnp.arange(half, dtype=F32) / half)
    ang = pos[:, None] * inv[None, :]
    cos, sin = jnp.cos(ang), jnp.sin(ang)

    return pl.pallas_call(
        functools.partial(_norm_matmul_ret_kernel, dk=dk),
        grid=(B, S // ts, N // tn),
        in_specs=[pl.BlockSpec((1, ts, D), lambda b, i, j: (b, i, 0)),
                  pl.BlockSpec((1, D), lambda b, i, j: (0, 0)),
                  pl.BlockSpec((D, tn), lambda b, i, j: (0, j)),
                  pl.BlockSpec((ts, half), lambda b, i, j: (i, 0)),
                  pl.BlockSpec((ts, half), lambda b, i, j: (i, 0))],
        out_specs=pl.BlockSpec((1, ts, tn), lambda b, i, j: (b, i, j)),
        out_shape=jax.ShapeDtypeStruct((B, S, N), BF16),
        scratch_shapes=[pltpu.VMEM((ts, D), BF16)],
        compiler_params=_compiler_params(("parallel", "parallel", "arbitrary")),
        name="norm_matmul_ret",
    )(x, g.reshape(1, D), w, cos, sin)


def _norm_matmul_diff_kernel(x_ref, g_ref, w_ref, qk_ref, vaug_ref, gt_ref, h_ref):
    j = pl.program_id(2)

    @pl.when(j == 0)
    def _():
        _store_normed(x_ref, g_ref, h_ref)

    def project():
        return jnp.dot(h_ref[...], w_ref[...], preferred_element_type=F32)

    @pl.when(j < 2)
    def _():
        qk_ref[0] = project().astype(BF16)

    @pl.when(j == 2)
    def _():
        n_heads, rows_aug, ts = vaug_ref.shape[1:]
        dv = w_ref.shape[1] // n_heads
        for h in range(0, n_heads, 2):
            ot = jnp.dot(h_ref[...], w_ref[:, h * dv:(h + 2) * dv],
                         preferred_element_type=F32).T.astype(BF16)
            for e in range(2):
                vaug_ref[0, h + e, 0:dv, :] = ot[e * dv:(e + 1) * dv, :]
                vaug_ref[0, h + e, dv:rows_aug, :] = jnp.ones(
                    (rows_aug - dv, ts), BF16)

    @pl.when(j == 3)
    def _():
        gt_ref[0] = _silu(project()).T.astype(BF16)


def _norm_matmul_diff(x, g, w, n_heads):
    B, S, D = x.shape
    Di = w.shape[1] // 4
    dv = Di // n_heads
    ts = PROJ_ROWS
    return pl.pallas_call(
        _norm_matmul_diff_kernel,
        grid=(B, S // ts, 4),
        in_specs=[pl.BlockSpec((1, ts, D), lambda b, i, j: (b, i, 0)),
                  pl.BlockSpec((1, D), lambda b, i, j: (0, 0)),
                  pl.BlockSpec((D, Di), lambda b, i, j: (0, j))],
        out_specs=[
            pl.BlockSpec((1, ts, Di), lambda b, i, j: (b, i, jnp.minimum(j, 1))),
            pl.BlockSpec((1, n_heads, dv + ONES_ROWS, ts),
                         lambda b, i, j: (b, 0, 0, i)),
            pl.BlockSpec((1, Di, ts), lambda b, i, j: (b, 0, i)),
        ],
        out_shape=[jax.ShapeDtypeStruct((B, S, 2 * Di), BF16),
                   jax.ShapeDtypeStruct((B, n_heads, dv + ONES_ROWS, S), BF16),
                   jax.ShapeDtypeStruct((B, Di, S), BF16)],
        scratch_shapes=[pltpu.VMEM((ts, D), BF16)],
        compiler_params=_compiler_params(("parallel", "parallel", "arbitrary")),
        name="norm_matmul_diff",
    )(x, g.reshape(1, D), w)


def _out_proj_kernel(y_ref, w_ref, g_ref, x_ref, o_ref, *, transposed_in):
    y = y_ref[0]
    if transposed_in:
        o = lax.dot_general(y, w_ref[...], (((0,), (0,)), ((), ())),
                            preferred_element_type=F32)
    else:
        o = jnp.dot(y, w_ref[...], preferred_element_type=F32)
    ms = jnp.mean(o * o, axis=-1, keepdims=True)
    o_ref[0] = x_ref[0] + o * lax.rsqrt(ms + RMS_EPS) * g_ref[...]


def _out_proj(y, w, g, x, *, transposed_in):
    B, S, D = x.shape
    Di = w.shape[0]
    ts = OUT_ROWS
    if transposed_in:
        y_spec = pl.BlockSpec((1, Di, ts), lambda b, i: (b, 0, i))
    else:
        y_spec = pl.BlockSpec((1, ts, Di), lambda b, i: (b, i, 0))
    return pl.pallas_call(
        functools.partial(_out_proj_kernel, transposed_in=transposed_in),
        grid=(B, S // ts),
        in_specs=[y_spec,
                  pl.BlockSpec((Di, D), lambda b, i: (0, 0)),
                  pl.BlockSpec((1, D), lambda b, i: (0, 0)),
                  pl.BlockSpec((1, ts, D), lambda b, i: (b, i, 0))],
        out_specs=pl.BlockSpec((1, ts, D), lambda b, i: (b, i, 0)),
        out_shape=jax.ShapeDtypeStruct((B, S, D), F32),
        compiler_params=_compiler_params(("parallel", "parallel")),
        name="out_proj_t" if transposed_in else "out_proj",
    )(y, w, g.reshape(1, D), x)


def _retention_kernel(q_ref, k_ref, v_ref, g_ref, dmat_ref, xi_ref, zeta_ref,
                      cdec_ref, o_ref, state_ref):
    C = RET_CHUNK

    @pl.when(pl.program_id(2) == 0)
    def _():
        state_ref[...] = jnp.zeros_like(state_ref)

    dmat = dmat_ref[0]
    xi = xi_ref[0]
    zeta = zeta_ref[0]
    cdec = cdec_ref[0]

    for ci in range(RET_CHUNKS_PER_STEP):
        rows = pl.ds(ci * C, C)
        q, k, v = q_ref[0, rows, :], k_ref[0, rows, :], v_ref[0, rows, :]
        state = state_ref[...]
        scores = lax.dot_general(q, k, (((1,), (1,)), ((), ())),
                                 preferred_element_type=F32) * dmat
        inner = jnp.dot(scores.astype(BF16), v, preferred_element_type=F32)
        cross = jnp.dot(q, state.astype(BF16), preferred_element_type=F32) * xi
        kz = (k.astype(F32) * zeta).astype(BF16)
        state_ref[...] = state * cdec + lax.dot_general(
            kz, v, (((0,), (0,)), ((), ())), preferred_element_type=F32)
        ret = inner + cross
        mu = jnp.mean(ret, axis=-1, keepdims=True)
        cen = ret - mu
        var = jnp.mean(cen * cen, axis=-1, keepdims=True)
        y = cen * lax.rsqrt(var + GN_EPS)
        o_ref[0, rows, :] = (g_ref[0, rows, :].astype(F32) * y).astype(o_ref.dtype)


def _retention(proj, d_model, d_inner):
    B, S, _ = proj.shape
    H, C = RET_HEADS, RET_CHUNK
    dk, dv = d_model // H, d_inner // H
    rows = C * RET_CHUNKS_PER_STEP
    k_off = d_model // dk
    v_off = 2 * d_model // dv
    g_off = (2 * d_model + d_inner) // dv

    log_gamma = jnp.log1p(-jnp.exp2(-5.0 - jnp.arange(H, dtype=F32)))
    idx = jnp.arange(C, dtype=F32)
    rel = idx[:, None] - idx[None, :]
    dmat = jnp.exp(log_gamma[:, None, None] * jnp.maximum(rel, 0.0)) * (rel >= 0)
    xi = jnp.exp(log_gamma[:, None] * (idx + 1.0))
    zeta = jnp.exp(log_gamma[:, None] * (C - 1.0 - idx))
    cdec = jnp.exp(log_gamma * C)
    xi = jnp.broadcast_to(xi[:, :, None], (H, C, dv))
    zeta = jnp.broadcast_to(zeta[:, :, None], (H, C, dk))
    cdec = jnp.broadcast_to(cdec[:, None, None], (H, 1, dv))

    return pl.pallas_call(
        _retention_kernel,
        grid=(B, H, S // rows),
        in_specs=[
            pl.BlockSpec((1, rows, dk), lambda b, h, n: (b, n, h)),
            pl.BlockSpec((1, rows, dk), lambda b, h, n: (b, n, k_off + h)),
            pl.BlockSpec((1, rows, dv), lambda b, h, n: (b, n, v_off + h)),
            pl.BlockSpec((1, rows, dv), lambda b, h, n: (b, n, g_off + h)),
            pl.BlockSpec((1, C, C), lambda b, h, n: (h, 0, 0)),
            pl.BlockSpec((1, C, dv), lambda b, h, n: (h, 0, 0)),
            pl.BlockSpec((1, C, dk), lambda b, h, n: (h, 0, 0)),
            pl.BlockSpec((1, 1, dv), lambda b, h, n: (h, 0, 0)),
        ],
        out_specs=pl.BlockSpec((1, rows, dv), lambda b, h, n: (b, n, h)),
        out_shape=jax.ShapeDtypeStruct((B, S, d_inner), BF16),
        scratch_shapes=[pltpu.VMEM((dk, dv), F32)],
        compiler_params=_compiler_params(("parallel", "parallel", "arbitrary")),
        name="retention",
    )(proj, proj, proj, proj, dmat, xi, zeta, cdec)


def _diff_attn_kernel(tab_ref, bfar_ref, q_ref, k_ref, vaug_ref, gt_ref, bk0_ref,
                      bk1_ref, lq1_ref, lk1_ref, lq2_ref, lk2_ref, sg_ref, o_ref,
                      qbd_ref, bias_ref, m_ref, acc_ref, sa_ref, sb_ref,
                      *, d, lambda_init):
    T, G, H = ATT_TILE, ATT_HEADS_PER_STEP, DIFF_HEADS
    dv = 2 * d
    hp = pl.program_id(1)
    i = pl.program_id(2)

    @pl.when(i == 0)
    def _():
        rows = lax.broadcasted_iota(jnp.int32, (T, T), 0)
        cols = lax.broadcasted_iota(jnp.int32, (T, T), 1)
        for u in range(G):
            h = hp * G + u
            cfar = tab_ref[bfar_ref[0] * H + h]
            for t, bk_ref in enumerate((bk0_ref, bk1_ref)):
                bk = bk_ref[...]
                b = jnp.zeros((T, T), F32)
                for bucket in range(REL_BUCKETS):
                    b = jnp.where(bk == bucket, tab_ref[bucket * H + h], b)
                b = (b - cfar) * LOG2E
                if t == 0:
                    b = jnp.where(rows <= cols, b, -jnp.inf)
                bias_ref[u, t] = b

    def logits_into(dst_ref, w, j, bias_index):
        u = w % G
        start = pl.multiple_of(j * T, T)
        kt = k_ref[0, pl.ds(start, T), u * dv:(u + 1) * dv]
        s = jnp.dot(kt, qbd_ref[w], preferred_element_type=F32)
        if bias_index is None:
            dst_ref[w] = s
        else:
            bias = bias_ref[u, bias_index]
            dst_ref[w, :, 0:T] = s[:, 0:T] + bias
            dst_ref[w, :, T:2 * T] = s[:, T:2 * T] + bias

    def consume(src_ref, w, j):
        u = w % G
        start = pl.multiple_of(j * T, T)
        s = src_ref[w]
        m_old = m_ref[w]
        m_new = jnp.maximum(m_old, jnp.max(s, axis=0, keepdims=True))
        p = jnp.exp2(s - m_new).astype(BF16)
        alpha = jnp.exp2(m_old - m_new)
        pv = jnp.dot(vaug_ref[0, u, :, pl.ds(start, T)], p,
                     preferred_element_type=F32)
        acc_ref[w] = acc_ref[w] * alpha + pv
        m_ref[w] = m_new

    first, second, both = range(G), range(G, 2 * G), range(2 * G)

    def step(dst_ref, j_next, bias_of, src_ref, j_cur, cur_units=both):
        for w in both:
            logits_into(dst_ref, w, j_next, bias_of(w))
            if w in cur_units:
                consume(src_ref, w, j_cur)

    no_bias = lambda w: None

    def start():
        for w in both:
            sub, u = divmod(w, G)
            q = q_ref[0, sub * T:(sub + 1) * T, u * dv:(u + 1) * dv].astype(F32)
            qt = (q * (d ** -0.5 * LOG2E)).T
            row = lax.broadcasted_iota(jnp.int32, qt.shape, 0)
            qbd_ref[w, :, 0:T] = jnp.where(row < d, qt, 0.0).astype(BF16)
            qbd_ref[w, :, T:2 * T] = jnp.where(row >= d, qt, 0.0).astype(BF16)
        m_ref[...] = jnp.full_like(m_ref, -jnp.inf)
        acc_ref[...] = jnp.zeros_like(acc_ref)
        for w in second:
            logits_into(sa_ref, w, 2 * i + 1, 0)
        step(sb_ref, 2 * i, lambda w: 0 if w in first else 1,
             sa_ref, 2 * i + 1, cur_units=second)

    def finish(src_ref, j_cur):
        lam = (jnp.exp(jnp.sum(lq1_ref[...] * lk1_ref[...], keepdims=True))
               - jnp.exp(jnp.sum(lq2_ref[...] * lk2_ref[...], keepdims=True))
               + lambda_init)
        for w in both:
            sub, u = divmod(w, G)
            consume(src_ref, w, j_cur)
            acc = acc_ref[w]
            l = acc[dv:dv + 1, :]
            o = acc[:dv, :T] / l[:, :T] - lam * (acc[:dv, T:] / l[:, T:])
            ms = jnp.mean(o * o, axis=0, keepdims=True)
            on = o * lax.rsqrt(ms + GN_EPS) * sg_ref[...] * (1.0 - lambda_init)
            head_rows = slice(u * dv, (u + 1) * dv)
            sub_cols = slice(sub * T, (sub + 1) * T)
            gate = gt_ref[0, head_rows, sub_cols].astype(F32)
            o_ref[0, head_rows, sub_cols] = (gate * on).astype(o_ref.dtype)

    @pl.when(i == 0)
    def _():
        start()
        finish(sb_ref, 2 * i)

    @pl.when(i >= 1)
    def _():
        start()
        step(sa_ref, 2 * i - 1, lambda w: 1 if w in first else None,
             sb_ref, 2 * i)
        step(sb_ref, 0, no_bias, sa_ref, 2 * i - 1)

        def pair(r):
            step(sa_ref, 2 * r - 1, no_bias, sb_ref, 2 * r - 2)
            step(sb_ref, 2 * r, no_bias, sa_ref, 2 * r - 1)

        def quad_body(t, carry):
            pair(2 * t + 1)
            pair(2 * t + 2)
            return carry

        lax.fori_loop(0, (i - 1) // 2, quad_body, 0)

        @pl.when((i - 1) % 2 == 1)
        def _():
            pair(i - 1)
            finish(sb_ref, 2 * i - 2)

        @pl.when((i - 1) % 2 == 0)
        def _():
            finish(sb_ref, 2 * i - 2)


def _t5_bucket(n):
    max_exact = REL_BUCKETS // 2
    nf = jnp.maximum(n, max_exact).astype(F32)
    large = max_exact + (jnp.log(nf / max_exact) / math.log(REL_MAX_DIST / max_exact)
                         * (REL_BUCKETS - max_exact)).astype(jnp.int32)
    large = jnp.minimum(large, REL_BUCKETS - 1)
    return jnp.where(n < max_exact, n, large)


def _diff_attn(qk, vaug, gt, lq1, lk1, lq2, lk2, subln_g, rel_bias, lambda_init):
    B, S, _ = qk.shape
    H, T, G = DIFF_HEADS, ATT_TILE, ATT_HEADS_PER_STEP
    d_inner = gt.shape[1]
    dv = d_inner // H
    d = dv // 2
    gw = G * dv
    n_groups = H // G

    kk = jnp.arange(T, dtype=jnp.int32)[:, None]
    qq = jnp.arange(T, dtype=jnp.int32)[None, :]
    bk0 = _t5_bucket(jnp.maximum(qq - kk, 0))
    bk1 = _t5_bucket(qq - kk + T)
    bfar = _t5_bucket(jnp.full((1,), T + 1, jnp.int32))

    smem = pl.BlockSpec(memory_space=pltpu.SMEM)
    vec = pl.BlockSpec((1, d), lambda b, h, i: (0, 0))
    tile_spec = pl.BlockSpec((T, T), lambda b, h, i: (0, 0))
    return pl.pallas_call(
        functools.partial(_diff_attn_kernel, d=d, lambda_init=lambda_init),
        grid=(B, n_groups, S // (2 * T)),
        in_specs=[
            smem, smem,
            pl.BlockSpec((1, 2 * T, gw), lambda b, h, i: (b, i, h)),
            pl.BlockSpec((1, S, gw), lambda b, h, i: (b, 0, n_groups + h),
                         pipeline_mode=pl.Buffered(1)),
            pl.BlockSpec((1, G, dv + ONES_ROWS, S), lambda b, h, i: (b, h, 0, 0),
                         pipeline_mode=pl.Buffered(1)),
            pl.BlockSpec((1, gw, 2 * T), lambda b, h, i: (b, h, i)),
            tile_spec, tile_spec,
            vec, vec, vec, vec,
            pl.BlockSpec((dv, 1), lambda b, h, i: (0, 0)),
        ],
        out_specs=pl.BlockSpec((1, gw, 2 * T), lambda b, h, i: (b, h, i)),
        out_shape=jax.ShapeDtypeStruct((B, d_inner, S), BF16),
        scratch_shapes=[pltpu.VMEM((2 * G, dv, 2 * T), BF16),
                        pltpu.VMEM((G, 2, T, T), F32),
                        pltpu.VMEM((2 * G, 1, 2 * T), F32),
                        pltpu.VMEM((2 * G, dv + ONES_ROWS, 2 * T), F32),
                        pltpu.VMEM((2 * G, T, 2 * T), F32),
                        pltpu.VMEM((2 * G, T, 2 * T), F32)],
        compiler_params=_compiler_params(("parallel", "arbitrary", "arbitrary")),
        name="diff_attn",
    )(rel_bias.astype(F32).reshape(-1), bfar, qk, qk, vaug, gt, bk0, bk1,
      lq1.reshape(1, d), lk1.reshape(1, d), lq2.reshape(1, d), lk2.reshape(1, d),
      subln_g.reshape(dv, 1))


def kernel(x, pre_norm_g, post_norm_g, ret_w_in, ret_w_out, diff_w_in, diff_w_out,
           diff_lambda_q1, diff_lambda_k1, diff_lambda_q2, diff_lambda_k2,
           diff_subln_g, rel_bias):
    d_model = x.shape[-1]
    d_inner = ret_w_out.shape[1]
    for i in range(DEPTH):
        j = i // N_MIXERS
        if i % N_MIXERS == 0:
            proj = _norm_matmul_ret(x, pre_norm_g[i], ret_w_in[j].astype(BF16),
                                    d_inner, RET_HEADS)
            y = _retention(proj, d_model, d_inner)
            x = _out_proj(y, ret_w_out[j].astype(BF16), post_norm_g[i], x,
                          transposed_in=False)
        else:
            lambda_init = 0.8 - 0.6 * math.exp(-0.3 * i)
            qk, vaug, gt = _norm_matmul_diff(
                x, pre_norm_g[i], diff_w_in[j].astype(BF16), DIFF_HEADS)
            yt = _diff_attn(qk, vaug, gt, diff_lambda_q1[j], diff_lambda_k1[j],
                            diff_lambda_q2[j], diff_lambda_k2[j], diff_subln_g[j],
                            rel_bias, lambda_init)
            x = _out_proj(yt, diff_w_out[j].astype(BF16), post_norm_g[i], x,
                          transposed_in=True)
    return x
```

```python
import functools
import math

import jax
import jax.numpy as jnp
from jax import lax
from jax.experimental import pallas as pl
from jax.experimental.pallas import tpu as pltpu

F32 = jnp.float32
BF16 = jnp.bfloat16

DEPTH = 2
N_MIXERS = 2
RET_HEADS = 4
RET_CHUNK = 128
DIFF_HEADS = 16
REL_BUCKETS = 32
REL_MAX_DIST = 128
ROPE_BASE = 10000.0
RMS_EPS = 1e-6
GN_EPS = 1e-5

V7X_VMEM_BYTES = 64 * 1024 * 1024
VMEM_LIMIT_BYTES = V7X_VMEM_BYTES * 7 // 8

PROJ_ROWS = 1024
PROJ_COLS = 2048
OUT_ROWS = 1024
RET_CHUNKS_PER_STEP = 8
ATT_TILE = 256
ATT_HEADS_PER_STEP = 4
ONES_ROWS = 16
LOG2E = math.log2(math.e)


def _compiler_params(semantics):
    return pltpu.CompilerParams(dimension_semantics=semantics,
                                vmem_limit_bytes=VMEM_LIMIT_BYTES)


def _store_normed(x_ref, g_ref, h_ref):
    x = x_ref[0]
    ms = jnp.mean(x * x, axis=-1, keepdims=True)
    h_ref[...] = (x * lax.rsqrt(ms + RMS_EPS) * g_ref[...]).astype(BF16)


def _silu(x):
    return x * jax.nn.sigmoid(x)


def _norm_matmul_ret_kernel(x_ref, g_ref, w_ref, cos_ref, sin_ref, o_ref, h_ref,
                            *, dk):
    j = pl.program_id(2)

    def project(cols=slice(None)):
        return jnp.dot(h_ref[...], w_ref[:, cols], preferred_element_type=F32)

    @pl.when(j == 0)
    def _():
        _store_normed(x_ref, g_ref, h_ref)
        half = dk // 2
        n_blocks = w_ref.shape[1] // dk
        c, s = cos_ref[...], sin_ref[...]
        for blk in range(n_blocks):
            o = project(slice(blk * dk, (blk + 1) * dk))
            scale = 1.0 if blk < n_blocks // 2 else dk ** -0.5
            x1, x2 = o[:, :half], o[:, half:]
            o_ref[0, :, blk * dk:blk * dk + half] = (
                (x1 * c - x2 * s) * scale).astype(BF16)
            o_ref[0, :, blk * dk + half:(blk + 1) * dk] = (
                (x1 * s + x2 * c) * scale).astype(BF16)

    @pl.when(j == 1)
    def _():
        o_ref[0] = project().astype(BF16)

    @pl.when(j == 2)
    def _():
        o_ref[0] = _silu(project()).astype(BF16)


def _norm_matmul_ret(x, g, w, d_inner, n_heads):
    B, S, D = x.shape
    N = w.shape[1]
    ts, tn = PROJ_ROWS, d_inner
    dk = (N - 2 * d_inner) // 2 // n_heads
    half = dk // 2
    assert N == 3 * tn, "q | k must fill exactly one column tile"

    pos = jnp.arange(S, dtype=F32)
    inv = ROPE_BASE ** (-jnp.arange(half, dtype=F32) / half)
    ang = pos[:, None] * inv[None, :]
    cos, sin = jnp.cos(ang), jnp.sin(ang)

    return pl.pallas_call(
        functools.partial(_norm_matmul_ret_kernel, dk=dk),
        grid=(B, S // ts, N // tn),
        in_specs=[pl.BlockSpec((1, ts, D), lambda b, i, j: (b, i, 0)),
                  pl.BlockSpec((1, D), lambda b, i, j: (0, 0)),
                  pl.BlockSpec((D, tn), lambda b, i, j: (0, j)),
                  pl.BlockSpec((ts, half), lambda b, i, j: (i, 0)),
                  pl.BlockSpec((ts, half), lambda b, i, j: (i, 0))],
        out_specs=pl.BlockSpec((1, ts, tn), lambda b, i, j: (b, i, j)),
        out_shape=jax.ShapeDtypeStruct((B, S, N), BF16),
        scratch_shapes=[pltpu.VMEM((ts, D), BF16)],
        compiler_params=_compiler_params(("parallel", "parallel", "arbitrary")),
        name="norm_matmul_ret",
    )(x, g.reshape(1, D), w, cos, sin)


def _norm_matmul_diff_kernel(x_ref, g_ref, w_ref, qk_ref, vaug_ref, gt_ref, h_ref):
    j = pl.program_id(2)

    @pl.when(j == 0)
    def _():
        _store_normed(x_ref, g_ref, h_ref)

    def project():
        return jnp.dot(h_ref[...], w_ref[...], preferred_element_type=F32)

    @pl.when(j < 2)
    def _():
        qk_ref[0] = project().astype(BF16)

    @pl.when(j == 2)
    def _():
        n_heads, rows_aug, ts = vaug_ref.shape[1:]
        dv = w_ref.shape[1] // n_heads
        for h in range(0, n_heads, 2):
            ot = jnp.dot(h_ref[...], w_ref[:, h * dv:(h + 2) * dv],
                         preferred_element_type=F32).T.astype(BF16)
            for e in range(2):
                vaug_ref[0, h + e, 0:dv, :] = ot[e * dv:(e + 1) * dv, :]
                vaug_ref[0, h + e, dv:rows_aug, :] = jnp.ones(
                    (rows_aug - dv, ts), BF16)

    @pl.when(j == 3)
    def _():
        gt_ref[0] = _silu(project()).T.astype(BF16)


def _norm_matmul_diff(x, g, w, n_heads):
    B, S, D = x.shape
    Di = w.shape[1] // 4
    dv = Di // n_heads
    ts = PROJ_ROWS
    return pl.pallas_call(
        _norm_matmul_diff_kernel,
        grid=(B, S // ts, 4),
        in_specs=[pl.BlockSpec((1, ts, D), lambda b, i, j: (b, i, 0)),
                  pl.BlockSpec((1, D), lambda b, i, j: (0, 0)),
                  pl.BlockSpec((D, Di), lambda b, i, j: (0, j))],
        out_specs=[
            pl.BlockSpec((1, ts, Di), lambda b, i, j: (b, i, jnp.minimum(j, 1))),
            pl.BlockSpec((1, n_heads, dv + ONES_ROWS, ts),
                         lambda b, i, j: (b, 0, 0, i)),
            pl.BlockSpec((1, Di, ts), lambda b, i, j: (b, 0, i)),
        ],
        out_shape=[jax.ShapeDtypeStruct((B, S, 2 * Di), BF16),
                   jax.ShapeDtypeStruct((B, n_heads, dv + ONES_ROWS, S), BF16),
                   jax.ShapeDtypeStruct((B, Di, S), BF16)],
        scratch_shapes=[pltpu.VMEM((ts, D), BF16)],
        compiler_params=_compiler_params(("parallel", "parallel", "arbitrary")),
        name="norm_matmul_diff",
    )(x, g.reshape(1, D), w)


def _out_proj_kernel(y_ref, w_ref, g_ref, x_ref, o_ref, *, transposed_in):
    y = y_ref[0]
    if transposed_in:
        o = lax.dot_general(y, w_ref[...], (((0,), (0,)), ((), ())),
                            preferred_element_type=F32)
    else:
        o = jnp.dot(y, w_ref[...], preferred_element_type=F32)
    ms = jnp.mean(o * o, axis=-1, keepdims=True)
    o_ref[0] = x_ref[0] + o * lax.rsqrt(ms + RMS_EPS) * g_ref[...]


def _out_proj(y, w, g, x, *, transposed_in):
    B, S, D = x.shape
    Di = w.shape[0]
    ts = OUT_ROWS
    if transposed_in:
        y_spec = pl.BlockSpec((1, Di, ts), lambda b, i: (b, 0, i))
    else:
        y_spec = pl.BlockSpec((1, ts, Di), lambda b, i: (b, i, 0))
    return pl.pallas_call(
        functools.partial(_out_proj_kernel, transposed_in=transposed_in),
        grid=(B, S // ts),
        in_specs=[y_spec,
                  pl.BlockSpec((Di, D), lambda b, i: (0, 0)),
                  pl.BlockSpec((1, D), lambda b, i: (0, 0)),
                  pl.BlockSpec((1, ts, D), lambda b, i: (b, i, 0))],
        out_specs=pl.BlockSpec((1, ts, D), lambda b, i: (b, i, 0)),
        out_shape=jax.ShapeDtypeStruct((B, S, D), F32),
        compiler_params=_compiler_params(("parallel", "parallel")),
        name="out_proj_t" if transposed_in else "out_proj",
    )(y, w, g.reshape(1, D), x)


def _retention_kernel(q_ref, k_ref, v_ref, g_ref, dmat_ref, xi_ref, zeta_ref,
                      cdec_ref, o_ref, state_ref):
    C = RET_CHUNK

    @pl.when(pl.program_id(2) == 0)
    def _():
        state_ref[...] = jnp.zeros_like(state_ref)

    dmat = dmat_ref[0]
    xi = xi_ref[0]
    zeta = zeta_ref[0]
    cdec = cdec_ref[0]

    for ci in range(RET_CHUNKS_PER_STEP):
        rows = pl.ds(ci * C, C)
        q, k, v = q_ref[0, rows, :], k_ref[0, rows, :], v_ref[0, rows, :]
        state = state_ref[...]
        scores = lax.dot_general(q, k, (((1,), (1,)), ((), ())),
                                 preferred_element_type=F32) * dmat
        inner = jnp.dot(scores.astype(BF16), v, preferred_element_type=F32)
        cross = jnp.dot(q, state.astype(BF16), preferred_element_type=F32) * xi
        kz = (k.astype(F32) * zeta).astype(BF16)
        state_ref[...] = state * cdec + lax.dot_general(
            kz, v, (((0,), (0,)), ((), ())), preferred_element_type=F32)
        ret = inner + cross
        mu = jnp.mean(ret, axis=-1, keepdims=True)
        cen = ret - mu
        var = jnp.mean(cen * cen, axis=-1, keepdims=True)
        y = cen * lax.rsqrt(var + GN_EPS)
        o_ref[0, rows, :] = (g_ref[0, rows, :].astype(F32) * y).astype(o_ref.dtype)


def _retention(proj, d_model, d_inner):
    B, S, _ = proj.shape
    H, C = RET_HEADS, RET_CHUNK
    dk, dv = d_model // H, d_inner // H
    rows = C * RET_CHUNKS_PER_STEP
    k_off = d_model // dk
    v_off = 2 * d_model // dv
    g_off = (2 * d_model + d_inner) // dv

    log_gamma = jnp.log1p(-jnp.exp2(-5.0 - jnp.arange(H, dtype=F32)))
    idx = jnp.arange(C, dtype=F32)
    rel = idx[:, None] - idx[None, :]
    dmat = jnp.exp(log_gamma[:, None, None] * jnp.maximum(rel, 0.0)) * (rel >= 0)
    xi = jnp.exp(log_gamma[:, None] * (idx + 1.0))
    zeta = jnp.exp(log_gamma[:, None] * (C - 1.0 - idx))
    cdec = jnp.exp(log_gamma * C)
    xi = jnp.broadcast_to(xi[:, :, None], (H, C, dv))
    zeta = jnp.broadcast_to(zeta[:, :, None], (H, C, dk))
    cdec = jnp.broadcast_to(cdec[:, None, None], (H, 1, dv))

    return pl.pallas_call(
        _retention_kernel,
        grid=(B, H, S // rows),
        in_specs=[
            pl.BlockSpec((1, rows, dk), lambda b, h, n: (b, n, h)),
            pl.BlockSpec((1, rows, dk), lambda b, h, n: (b, n, k_off + h)),
            pl.BlockSpec((1, rows, dv), lambda b, h, n: (b, n, v_off + h)),
            pl.BlockSpec((1, rows, dv), lambda b, h, n: (b, n, g_off + h)),
            pl.BlockSpec((1, C, C), lambda b, h, n: (h, 0, 0)),
            pl.BlockSpec((1, C, dv), lambda b, h, n: (h, 0, 0)),
            pl.BlockSpec((1, C, dk), lambda b, h, n: (h, 0, 0)),
            pl.BlockSpec((1, 1, dv), lambda b, h, n: (h, 0, 0)),
        ],
        out_specs=pl.BlockSpec((1, rows, dv), lambda b, h, n: (b, n, h)),
        out_shape=jax.ShapeDtypeStruct((B, S, d_inner), BF16),
        scratch_shapes=[pltpu.VMEM((dk, dv), F32)],
        compiler_params=_compiler_params(("parallel", "parallel", "arbitrary")),
        name="retention",
    )(proj, proj, proj, proj, dmat, xi, zeta, cdec)


def _diff_attn_kernel(tab_ref, bfar_ref, q_ref, k_ref, vaug_ref, gt_ref, bk0_ref,
                      bk1_ref, lq1_ref, lk1_ref, lq2_ref, lk2_ref, sg_ref, o_ref,
                      qbd_ref, bias_ref, m_ref, acc_ref, s0_ref, s1_ref, s2_ref, s3_ref,
                      *, d, lambda_init):
    T, G, H = ATT_TILE, ATT_HEADS_PER_STEP, DIFF_HEADS
    dv = 2 * d
    hp = pl.program_id(1)
    i = pl.program_id(2)

    @pl.when(i == 0)
    def _():
        rows = lax.broadcasted_iota(jnp.int32, (T, T), 0)
        cols = lax.broadcasted_iota(jnp.int32, (T, T), 1)
        for u in range(G):
            h = hp * G + u
            cfar = tab_ref[bfar_ref[0] * H + h]
            for t, bk_ref in enumerate((bk0_ref, bk1_ref)):
                bk = bk_ref[...]
                b = jnp.zeros((T, T), F32)
                for bucket in range(REL_BUCKETS):
                    b = jnp.where(bk == bucket, tab_ref[bucket * H + h], b)
                b = (b - cfar) * LOG2E
                if t == 0:
                    b = jnp.where(rows <= cols, b, -jnp.inf)
                bias_ref[u, t] = b

    def logits_into(dst_ref, w, j, bias_index):
        u = w % G
        start = pl.multiple_of(j * T, T)
        kt = k_ref[0, pl.ds(start, T), u * dv:(u + 1) * dv]
        s = jnp.dot(kt, qbd_ref[w], preferred_element_type=F32)
        if bias_index is None:
            dst_ref[w] = s
        else:
            bias = bias_ref[u, bias_index]
            dst_ref[w, :, 0:T] = s[:, 0:T] + bias
            dst_ref[w, :, T:2 * T] = s[:, T:2 * T] + bias

    def consume(src_ref, w, j):
        u = w % G
        start = pl.multiple_of(j * T, T)
        s = src_ref[w]
        m_old = m_ref[w]
        m_new = jnp.maximum(m_old, jnp.max(s, axis=0, keepdims=True))
        p = jnp.exp2(s - m_new).astype(BF16)
        alpha = jnp.exp2(m_old - m_new)
        pv = jnp.dot(vaug_ref[0, u, :, pl.ds(start, T)], p,
                     preferred_element_type=F32)
        acc_ref[w] = acc_ref[w] * alpha + pv
        m_ref[w] = m_new

    first, second, both = range(G), range(G, 2 * G), range(2 * G)

    def step(dst_ref, j_next, bias_of, src_ref, j_cur, cur_units=both):
        for w in both:
            logits_into(dst_ref, w, j_next, bias_of(w))
            if w in cur_units:
                consume(src_ref, w, j_cur)

    no_bias = lambda w: None

    def start():
        for w in both:
            sub, u = divmod(w, G)
            q = q_ref[0, sub * T:(sub + 1) * T, u * dv:(u + 1) * dv].astype(F32)
            qt = (q * (d ** -0.5 * LOG2E)).T
            row = lax.broadcasted_iota(jnp.int32, qt.shape, 0)
            qbd_ref[w, :, 0:T] = jnp.where(row < d, qt, 0.0).astype(BF16)
            qbd_ref[w, :, T:2 * T] = jnp.where(row >= d, qt, 0.0).astype(BF16)
        m_ref[...] = jnp.full_like(m_ref, -jnp.inf)
        acc_ref[...] = jnp.zeros_like(acc_ref)
        for w in second:
            logits_into(s0_ref, w, 2 * i + 1, 0)
        step(s1_ref, 2 * i, lambda w: 0 if w in first else 1,
             s0_ref, 2 * i + 1, cur_units=second)

    def finish(src_ref, j_cur):
        lam = (jnp.exp(jnp.sum(lq1_ref[...] * lk1_ref[...], keepdims=True))
               - jnp.exp(jnp.sum(lq2_ref[...] * lk2_ref[...], keepdims=True))
               + lambda_init)
        for w in both:
            sub, u = divmod(w, G)
            consume(src_ref, w, j_cur)
            acc = acc_ref[w]
            l = acc[dv:dv + 1, :]
            o = acc[:dv, :T] / l[:, :T] - lam * (acc[:dv, T:] / l[:, T:])
            ms = jnp.mean(o * o, axis=0, keepdims=True)
            on = o * lax.rsqrt(ms + GN_EPS) * sg_ref[...] * (1.0 - lambda_init)
            head_rows = slice(u * dv, (u + 1) * dv)
            sub_cols = slice(sub * T, (sub + 1) * T)
            gate = gt_ref[0, head_rows, sub_cols].astype(F32)
            o_ref[0, head_rows, sub_cols] = (gate * on).astype(o_ref.dtype)

    @pl.when(i == 0)
    def _():
        start()
        finish(s1_ref, 2 * i)

    @pl.when(i >= 1)
    def _():
        start()
        step(s2_ref, 2 * i - 1, lambda w: 1 if w in first else None,
             s1_ref, 2 * i)
        step(s3_ref, 0, no_bias, s2_ref, 2 * i - 1)

        def pair(r, bufs):
            prev, a, b = bufs
            step(a, 2 * r - 1, no_bias, prev, 2 * r - 2)
            step(b, 2 * r, no_bias, a, 2 * r - 1)

        def quad_body(t, carry):
            pair(2 * t + 1, (s3_ref, s0_ref, s1_ref))
            pair(2 * t + 2, (s1_ref, s2_ref, s3_ref))
            return carry

        lax.fori_loop(0, (i - 1) // 2, quad_body, 0)

        @pl.when((i - 1) % 2 == 1)
        def _():
            pair(i - 1, (s3_ref, s0_ref, s1_ref))
            finish(s1_ref, 2 * i - 2)

        @pl.when((i - 1) % 2 == 0)
        def _():
            finish(s3_ref, 2 * i - 2)


def _t5_bucket(n):
    max_exact = REL_BUCKETS // 2
    nf = jnp.maximum(n, max_exact).astype(F32)
    large = max_exact + (jnp.log(nf / max_exact) / math.log(REL_MAX_DIST / max_exact)
                         * (REL_BUCKETS - max_exact)).astype(jnp.int32)
    large = jnp.minimum(large, REL_BUCKETS - 1)
    return jnp.where(n < max_exact, n, large)


def _diff_attn(qk, vaug, gt, lq1, lk1, lq2, lk2, subln_g, rel_bias, lambda_init):
    B, S, _ = qk.shape
    H, T, G = DIFF_HEADS, ATT_TILE, ATT_HEADS_PER_STEP
    d_inner = gt.shape[1]
    dv = d_inner // H
    d = dv // 2
    gw = G * dv
    n_groups = H // G

    kk = jnp.arange(T, dtype=jnp.int32)[:, None]
    qq = jnp.arange(T, dtype=jnp.int32)[None, :]
    bk0 = _t5_bucket(jnp.maximum(qq - kk, 0))
    bk1 = _t5_bucket(qq - kk + T)
    bfar = _t5_bucket(jnp.full((1,), T + 1, jnp.int32))

    smem = pl.BlockSpec(memory_space=pltpu.SMEM)
    vec = pl.BlockSpec((1, d), lambda b, h, i: (0, 0))
    tile_spec = pl.BlockSpec((T, T), lambda b, h, i: (0, 0))
    return pl.pallas_call(
        functools.partial(_diff_attn_kernel, d=d, lambda_init=lambda_init),
        grid=(B, n_groups, S // (2 * T)),
        in_specs=[
            smem, smem,
            pl.BlockSpec((1, 2 * T, gw), lambda b, h, i: (b, i, h)),
            pl.BlockSpec((1, S, gw), lambda b, h, i: (b, 0, n_groups + h),
                         pipeline_mode=pl.Buffered(1)),
            pl.BlockSpec((1, G, dv + ONES_ROWS, S), lambda b, h, i: (b, h, 0, 0),
                         pipeline_mode=pl.Buffered(1)),
            pl.BlockSpec((1, gw, 2 * T), lambda b, h, i: (b, h, i)),
            tile_spec, tile_spec,
            vec, vec, vec, vec,
            pl.BlockSpec((dv, 1), lambda b, h, i: (0, 0)),
        ],
        out_specs=pl.BlockSpec((1, gw, 2 * T), lambda b, h, i: (b, h, i)),
        out_shape=jax.ShapeDtypeStruct((B, d_inner, S), BF16),
        scratch_shapes=[pltpu.VMEM((2 * G, dv, 2 * T), BF16),
                        pltpu.VMEM((G, 2, T, T), F32),
                        pltpu.VMEM((2 * G, 1, 2 * T), F32),
                        pltpu.VMEM((2 * G, dv + ONES_ROWS, 2 * T), F32),
                        *[pltpu.VMEM((2 * G, T, 2 * T), F32)] * 4],
        compiler_params=_compiler_params(("parallel", "arbitrary", "arbitrary")),
        name="diff_attn",
    )(rel_bias.astype(F32).reshape(-1), bfar, qk, qk, vaug, gt, bk0, bk1,
      lq1.reshape(1, d), lk1.reshape(1, d), lq2.reshape(1, d), lk2.reshape(1, d),
      subln_g.reshape(dv, 1))


def kernel(x, pre_norm_g, post_norm_g, ret_w_in, ret_w_out, diff_w_in, diff_w_out,
           diff_lambda_q1, diff_lambda_k1, diff_lambda_q2, diff_lambda_k2,
           diff_subln_g, rel_bias):
    d_model = x.shape[-1]
    d_inner = ret_w_out.shape[1]
    for i in range(DEPTH):
        j = i // N_MIXERS
        if i % N_MIXERS == 0:
            proj = _norm_matmul_ret(x, pre_norm_g[i], ret_w_in[j].astype(BF16),
                                    d_inner, RET_HEADS)
            y = _retention(proj, d_model, d_inner)
            x = _out_proj(y, ret_w_out[j].astype(BF16), post_norm_g[i], x,
                          transposed_in=False)
        else:
            lambda_init = 0.8 - 0.6 * math.exp(-0.3 * i)
            qk, vaug, gt = _norm_matmul_diff(
                x, pre_norm_g[i], diff_w_in[j].astype(BF16), DIFF_HEADS)
            yt = _diff_attn(qk, vaug, gt, diff_lambda_q1[j], diff_lambda_k1[j],
                            diff_lambda_q2[j], diff_lambda_k2[j], diff_subln_g[j],
                            rel_bias, lambda_init)
            x = _out_proj(yt, diff_w_out[j].astype(BF16), post_norm_g[i], x,
                          transposed_in=True)
    return x
```

```python
import functools
import math

import jax
import jax.numpy as jnp
import numpy as np
from jax import lax
from jax.experimental import pallas as pl
from jax.experimental.pallas import tpu as pltpu

F32 = jnp.float32
BF16 = jnp.bfloat16

DEPTH = 2
N_MIXERS = 2
RET_HEADS = 4
RET_CHUNK = 128
DIFF_HEADS = 16
REL_BUCKETS = 32
REL_MAX_DIST = 128
ROPE_BASE = 10000.0
RMS_EPS = 1e-6
GN_EPS = 1e-5

V7X_VMEM_BYTES = 64 * 1024 * 1024
VMEM_LIMIT_BYTES = V7X_VMEM_BYTES * 7 // 8

PROJ_ROWS = 1024
PROJ_COLS = 2048
OUT_ROWS = 1024
RET_CHUNKS_PER_STEP = 8
ATT_TILE = 256
ATT_HEADS_PER_STEP = 4
ONES_ROWS = 16
LOG2E = math.log2(math.e)


def _compiler_params(semantics):
    return pltpu.CompilerParams(dimension_semantics=semantics,
                                vmem_limit_bytes=VMEM_LIMIT_BYTES)


def _store_normed(x_ref, g_ref, h_ref):
    x = x_ref[0]
    ms = jnp.mean(x * x, axis=-1, keepdims=True)
    h_ref[...] = (x * lax.rsqrt(ms + RMS_EPS) * g_ref[...]).astype(BF16)


def _silu(x):
    return x * jax.nn.sigmoid(x)


def _norm_matmul_ret_kernel(x_ref, g_ref, w_ref, cos_ref, sin_ref, o_ref, h_ref,
                            *, dk):
    j = pl.program_id(2)

    def project(cols=slice(None)):
        return jnp.dot(h_ref[...], w_ref[:, cols], preferred_element_type=F32)

    @pl.when(j == 0)
    def _():
        _store_normed(x_ref, g_ref, h_ref)
        half = dk // 2
        n_blocks = w_ref.shape[1] // dk
        c, s = cos_ref[...], sin_ref[...]
        for blk in range(n_blocks):
            o = project(slice(blk * dk, (blk + 1) * dk))
            scale = 1.0 if blk < n_blocks // 2 else dk ** -0.5
            x1, x2 = o[:, :half], o[:, half:]
            o_ref[0, :, blk * dk:blk * dk + half] = (
                (x1 * c - x2 * s) * scale).astype(BF16)
            o_ref[0, :, blk * dk + half:(blk + 1) * dk] = (
                (x1 * s + x2 * c) * scale).astype(BF16)

    @pl.when(j == 1)
    def _():
        o_ref[0] = project().astype(BF16)

    @pl.when(j == 2)
    def _():
        o_ref[0] = _silu(project()).astype(BF16)


def _norm_matmul_ret(x, g, w, d_inner, n_heads):
    B, S, D = x.shape
    N = w.shape[1]
    ts, tn = PROJ_ROWS, d_inner
    dk = (N - 2 * d_inner) // 2 // n_heads
    half = dk // 2
    assert N == 3 * tn, "q | k must fill exactly one column tile"

    pos = np.arange(S, dtype=np.float64)
    inv = ROPE_BASE ** (-np.arange(half, dtype=np.float64) / half)
    ang = pos[:, None] * inv[None, :]
    cos = jnp.asarray(np.cos(ang), dtype=F32)
    sin = jnp.asarray(np.sin(ang), dtype=F32)

    return pl.pallas_call(
        functools.partial(_norm_matmul_ret_kernel, dk=dk),
        grid=(B, S // ts, N // tn),
        in_specs=[pl.BlockSpec((1, ts, D), lambda b, i, j: (b, i, 0)),
                  pl.BlockSpec((1, D), lambda b, i, j: (0, 0)),
                  pl.BlockSpec((D, tn), lambda b, i, j: (0, j)),
                  pl.BlockSpec((ts, half), lambda b, i, j: (i, 0)),
                  pl.BlockSpec((ts, half), lambda b, i, j: (i, 0))],
        out_specs=pl.BlockSpec((1, ts, tn), lambda b, i, j: (b, i, j)),
        out_shape=jax.ShapeDtypeStruct((B, S, N), BF16),
        scratch_shapes=[pltpu.VMEM((ts, D), BF16)],
        compiler_params=_compiler_params(("parallel", "parallel", "arbitrary")),
        name="norm_matmul_ret",
    )(x, g.reshape(1, D), w, cos, sin)


def _norm_matmul_diff_kernel(x_ref, g_ref, w_ref, qk_ref, vaug_ref, gt_ref, h_ref):
    j = pl.program_id(2)

    @pl.when(j == 0)
    def _():
        _store_normed(x_ref, g_ref, h_ref)

    def project():
        return jnp.dot(h_ref[...], w_ref[...], preferred_element_type=F32)

    @pl.when(j < 2)
    def _():
        qk_ref[0] = project().astype(BF16)

    @pl.when(j == 2)
    def _():
        n_heads, rows_aug, ts = vaug_ref.shape[1:]
        dv = w_ref.shape[1] // n_heads
        for h in range(0, n_heads, 2):
            ot = jnp.dot(h_ref[...], w_ref[:, h * dv:(h + 2) * dv],
                         preferred_element_type=F32).T.astype(BF16)
            for e in range(2):
                vaug_ref[0, h + e, 0:dv, :] = ot[e * dv:(e + 1) * dv, :]
                vaug_ref[0, h + e, dv:rows_aug, :] = jnp.ones(
                    (rows_aug - dv, ts), BF16)

    @pl.when(j == 3)
    def _():
        gt_ref[0] = _silu(project()).T.astype(BF16)


def _norm_matmul_diff(x, g, w, n_heads):
    B, S, D = x.shape
    Di = w.shape[1] // 4
    dv = Di // n_heads
    ts = PROJ_ROWS
    return pl.pallas_call(
        _norm_matmul_diff_kernel,
        grid=(B, S // ts, 4),
        in_specs=[pl.BlockSpec((1, ts, D), lambda b, i, j: (b, i, 0)),
                  pl.BlockSpec((1, D), lambda b, i, j: (0, 0)),
                  pl.BlockSpec((D, Di), lambda b, i, j: (0, j))],
        out_specs=[
            pl.BlockSpec((1, ts, Di), lambda b, i, j: (b, i, jnp.minimum(j, 1))),
            pl.BlockSpec((1, n_heads, dv + ONES_ROWS, ts),
                         lambda b, i, j: (b, 0, 0, i)),
            pl.BlockSpec((1, Di, ts), lambda b, i, j: (b, 0, i)),
        ],
        out_shape=[jax.ShapeDtypeStruct((B, S, 2 * Di), BF16),
                   jax.ShapeDtypeStruct((B, n_heads, dv + ONES_ROWS, S), BF16),
                   jax.ShapeDtypeStruct((B, Di, S), BF16)],
        scratch_shapes=[pltpu.VMEM((ts, D), BF16)],
        compiler_params=_compiler_params(("parallel", "parallel", "arbitrary")),
        name="norm_matmul_diff",
    )(x, g.reshape(1, D), w)


def _out_proj_kernel(y_ref, w_ref, g_ref, x_ref, o_ref, *, transposed_in):
    y = y_ref[0]
    if transposed_in:
        o = lax.dot_general(y, w_ref[...], (((0,), (0,)), ((), ())),
                            preferred_element_type=F32)
    else:
        o = jnp.dot(y, w_ref[...], preferred_element_type=F32)
    ms = jnp.mean(o * o, axis=-1, keepdims=True)
    o_ref[0] = x_ref[0] + o * lax.rsqrt(ms + RMS_EPS) * g_ref[...]


def _out_proj(y, w, g, x, *, transposed_in):
    B, S, D = x.shape
    Di = w.shape[0]
    ts = OUT_ROWS
    if transposed_in:
        y_spec = pl.BlockSpec((1, Di, ts), lambda b, i: (b, 0, i))
    else:
        y_spec = pl.BlockSpec((1, ts, Di), lambda b, i: (b, i, 0))
    return pl.pallas_call(
        functools.partial(_out_proj_kernel, transposed_in=transposed_in),
        grid=(B, S // ts),
        in_specs=[y_spec,
                  pl.BlockSpec((Di, D), lambda b, i: (0, 0)),
                  pl.BlockSpec((1, D), lambda b, i: (0, 0)),
                  pl.BlockSpec((1, ts, D), lambda b, i: (b, i, 0))],
        out_specs=pl.BlockSpec((1, ts, D), lambda b, i: (b, i, 0)),
        out_shape=jax.ShapeDtypeStruct((B, S, D), F32),
        compiler_params=_compiler_params(("parallel", "parallel")),
        name="out_proj_t" if transposed_in else "out_proj",
    )(y, w, g.reshape(1, D), x)


def _retention_kernel(q_ref, k_ref, v_ref, g_ref, dmat_ref, xi_ref, zeta_ref,
                      cdec_ref, o_ref, state_ref):
    C = RET_CHUNK

    @pl.when(pl.program_id(2) == 0)
    def _():
        state_ref[...] = jnp.zeros_like(state_ref)

    dmat = dmat_ref[0]
    xi = xi_ref[0]
    zeta = zeta_ref[0]
    cdec = cdec_ref[0]

    for ci in range(RET_CHUNKS_PER_STEP):
        rows = pl.ds(ci * C, C)
        q, k, v = q_ref[0, rows, :], k_ref[0, rows, :], v_ref[0, rows, :]
        state = state_ref[...]
        scores = lax.dot_general(q, k, (((1,), (1,)), ((), ())),
                                 preferred_element_type=F32) * dmat
        inner = jnp.dot(scores.astype(BF16), v, preferred_element_type=F32)
        cross = jnp.dot(q, state.astype(BF16), preferred_element_type=F32) * xi
        kz = (k.astype(F32) * zeta).astype(BF16)
        state_ref[...] = state * cdec + lax.dot_general(
            kz, v, (((0,), (0,)), ((), ())), preferred_element_type=F32)
        ret = inner + cross
        mu = jnp.mean(ret, axis=-1, keepdims=True)
        cen = ret - mu
        var = jnp.mean(cen * cen, axis=-1, keepdims=True)
        y = cen * lax.rsqrt(var + GN_EPS)
        o_ref[0, rows, :] = (g_ref[0, rows, :].astype(F32) * y).astype(o_ref.dtype)


def _retention(proj, d_model, d_inner):
    B, S, _ = proj.shape
    H, C = RET_HEADS, RET_CHUNK
    dk, dv = d_model // H, d_inner // H
    rows = C * RET_CHUNKS_PER_STEP
    k_off = d_model // dk
    v_off = 2 * d_model // dv
    g_off = (2 * d_model + d_inner) // dv

    log_gamma = jnp.log1p(-jnp.exp2(-5.0 - jnp.arange(H, dtype=F32)))
    idx = jnp.arange(C, dtype=F32)
    rel = idx[:, None] - idx[None, :]
    dmat = jnp.exp(log_gamma[:, None, None] * jnp.maximum(rel, 0.0)) * (rel >= 0)
    xi = jnp.exp(log_gamma[:, None] * (idx + 1.0))
    zeta = jnp.exp(log_gamma[:, None] * (C - 1.0 - idx))
    cdec = jnp.exp(log_gamma * C)
    xi = jnp.broadcast_to(xi[:, :, None], (H, C, dv))
    zeta = jnp.broadcast_to(zeta[:, :, None], (H, C, dk))
    cdec = jnp.broadcast_to(cdec[:, None, None], (H, 1, dv))

    return pl.pallas_call(
        _retention_kernel,
        grid=(B, H, S // rows),
        in_specs=[
            pl.BlockSpec((1, rows, dk), lambda b, h, n: (b, n, h)),
            pl.BlockSpec((1, rows, dk), lambda b, h, n: (b, n, k_off + h)),
            pl.BlockSpec((1, rows, dv), lambda b, h, n: (b, n, v_off + h)),
            pl.BlockSpec((1, rows, dv), lambda b, h, n: (b, n, g_off + h)),
            pl.BlockSpec((1, C, C), lambda b, h, n: (h, 0, 0)),
            pl.BlockSpec((1, C, dv), lambda b, h, n: (h, 0, 0)),
            pl.BlockSpec((1, C, dk), lambda b, h, n: (h, 0, 0)),
            pl.BlockSpec((1, 1, dv), lambda b, h, n: (h, 0, 0)),
        ],
        out_specs=pl.BlockSpec((1, rows, dv), lambda b, h, n: (b, n, h)),
        out_shape=jax.ShapeDtypeStruct((B, S, d_inner), BF16),
        scratch_shapes=[pltpu.VMEM((dk, dv), F32)],
        compiler_params=_compiler_params(("parallel", "parallel", "arbitrary")),
        name="retention",
    )(proj, proj, proj, proj, dmat, xi, zeta, cdec)


def _diff_attn_kernel(tab_ref, bfar_ref, q_ref, k_ref, vaug_ref, gt_ref, bk0_ref,
                      bk1_ref, lq1_ref, lk1_ref, lq2_ref, lk2_ref, sg_ref, o_ref,
                      qbd_ref, bias_ref, m_ref, acc_ref, sa_ref, sb_ref,
                      *, d, lambda_init):
    T, G, H = ATT_TILE, ATT_HEADS_PER_STEP, DIFF_HEADS
    dv = 2 * d
    hp = pl.program_id(1)
    i = pl.program_id(2)

    @pl.when(i == 0)
    def _():
        rows = lax.broadcasted_iota(jnp.int32, (T, T), 0)
        cols = lax.broadcasted_iota(jnp.int32, (T, T), 1)
        for u in range(G):
            h = hp * G + u
            cfar = tab_ref[bfar_ref[0] * H + h]
            for t, bk_ref in enumerate((bk0_ref, bk1_ref)):
                bk = bk_ref[...]
                b = jnp.zeros((T, T), F32)
                for bucket in range(REL_BUCKETS):
                    b = jnp.where(bk == bucket, tab_ref[bucket * H + h], b)
                b = (b - cfar) * LOG2E
                if t == 0:
                    b = jnp.where(rows <= cols, b, -jnp.inf)
                bias_ref[u, t] = b

    def logits_into(dst_ref, w, j, bias_index):
        u = w % G
        start = pl.multiple_of(j * T, T)
        kt = k_ref[0, pl.ds(start, T), u * dv:(u + 1) * dv]
        s = jnp.dot(kt, qbd_ref[w], preferred_element_type=F32)
        if bias_index is None:
            dst_ref[w] = s
        else:
            bias = bias_ref[u, bias_index]
            dst_ref[w, :, 0:T] = s[:, 0:T] + bias
            dst_ref[w, :, T:2 * T] = s[:, T:2 * T] + bias

    def consume(src_ref, w, j):
        u = w % G
        start = pl.multiple_of(j * T, T)
        s = src_ref[w]
        m_old = m_ref[w]
        m_new = jnp.maximum(m_old, jnp.max(s, axis=0, keepdims=True))
        p = jnp.exp2(s - m_new).astype(BF16)
        alpha = jnp.exp2(m_old - m_new)
        pv = jnp.dot(vaug_ref[0, u, :, pl.ds(start, T)], p,
                     preferred_element_type=F32)
        acc_ref[w] = acc_ref[w] * alpha + pv
        m_ref[w] = m_new

    first, second, both = range(G), range(G, 2 * G), range(2 * G)

    def step(dst_ref, j_next, bias_of, src_ref, j_cur, cur_units=both):
        for w in both:
            logits_into(dst_ref, w, j_next, bias_of(w))
            if w in cur_units:
                consume(src_ref, w, j_cur)

    no_bias = lambda w: None

    def start():
        for w in both:
            sub, u = divmod(w, G)
            q = q_ref[0, sub * T:(sub + 1) * T, u * dv:(u + 1) * dv].astype(F32)
            qt = (q * (d ** -0.5 * LOG2E)).T
            row = lax.broadcasted_iota(jnp.int32, qt.shape, 0)
            qbd_ref[w, :, 0:T] = jnp.where(row < d, qt, 0.0).astype(BF16)
            qbd_ref[w, :, T:2 * T] = jnp.where(row >= d, qt, 0.0).astype(BF16)
        m_ref[...] = jnp.full_like(m_ref, -jnp.inf)
        acc_ref[...] = jnp.zeros_like(acc_ref)
        for w in second:
            logits_into(sa_ref, w, 2 * i + 1, 0)
        step(sb_ref, 2 * i, lambda w: 0 if w in first else 1,
             sa_ref, 2 * i + 1, cur_units=second)

    def finish(src_ref, j_cur):
        lam = (jnp.exp(jnp.sum(lq1_ref[...] * lk1_ref[...], keepdims=True))
               - jnp.exp(jnp.sum(lq2_ref[...] * lk2_ref[...], keepdims=True))
               + lambda_init)
        out_scale = sg_ref[...] * (1.0 - lambda_init)
        for w in both:
            sub, u = divmod(w, G)
            consume(src_ref, w, j_cur)
            acc = acc_ref[w]
            inv_l = 1.0 / acc[dv:dv + 1, :]
            o = (acc[:dv, :T] * inv_l[:, :T]
                 - acc[:dv, T:] * (lam * inv_l[:, T:]))
            ms = jnp.mean(o * o, axis=0, keepdims=True)
            on = (o * lax.rsqrt(ms + GN_EPS)) * out_scale
            head_rows = slice(u * dv, (u + 1) * dv)
            sub_cols = slice(sub * T, (sub + 1) * T)
            gate = gt_ref[0, head_rows, sub_cols].astype(F32)
            o_ref[0, head_rows, sub_cols] = (gate * on).astype(o_ref.dtype)

    @pl.when(i == 0)
    def _():
        start()
        finish(sb_ref, 2 * i)

    @pl.when(i >= 1)
    def _():
        start()
        step(sa_ref, 2 * i - 1, lambda w: 1 if w in first else None,
             sb_ref, 2 * i)
        step(sb_ref, 0, no_bias, sa_ref, 2 * i - 1)

        def pair(r):
            step(sa_ref, 2 * r - 1, no_bias, sb_ref, 2 * r - 2)
            step(sb_ref, 2 * r, no_bias, sa_ref, 2 * r - 1)

        def quad_body(t, carry):
            pair(2 * t + 1)
            pair(2 * t + 2)
            return carry

        lax.fori_loop(0, (i - 1) // 2, quad_body, 0)

        @pl.when((i - 1) % 2 == 1)
        def _():
            pair(i - 1)
            finish(sb_ref, 2 * i - 2)

        @pl.when((i - 1) % 2 == 0)
        def _():
            finish(sb_ref, 2 * i - 2)


def _t5_bucket(n):
    max_exact = REL_BUCKETS // 2
    nf = jnp.maximum(n, max_exact).astype(F32)
    large = max_exact + (jnp.log(nf / max_exact) / math.log(REL_MAX_DIST / max_exact)
                         * (REL_BUCKETS - max_exact)).astype(jnp.int32)
    large = jnp.minimum(large, REL_BUCKETS - 1)
    return jnp.where(n < max_exact, n, large)


def _diff_attn(qk, vaug, gt, lq1, lk1, lq2, lk2, subln_g, rel_bias, lambda_init):
    B, S, _ = qk.shape
    H, T, G = DIFF_HEADS, ATT_TILE, ATT_HEADS_PER_STEP
    d_inner = gt.shape[1]
    dv = d_inner // H
    d = dv // 2
    gw = G * dv
    n_groups = H // G

    kk = jnp.arange(T, dtype=jnp.int32)[:, None]
    qq = jnp.arange(T, dtype=jnp.int32)[None, :]
    bk0 = _t5_bucket(jnp.maximum(qq - kk, 0))
    bk1 = _t5_bucket(qq - kk + T)
    bfar = _t5_bucket(jnp.full((1,), T + 1, jnp.int32))

    smem = pl.BlockSpec(memory_space=pltpu.SMEM)
    vec = pl.BlockSpec((1, d), lambda b, h, i: (0, 0))
    tile_spec = pl.BlockSpec((T, T), lambda b, h, i: (0, 0))
    return pl.pallas_call(
        functools.partial(_diff_attn_kernel, d=d, lambda_init=lambda_init),
        grid=(B, n_groups, S // (2 * T)),
        in_specs=[
            smem, smem,
            pl.BlockSpec((1, 2 * T, gw), lambda b, h, i: (b, i, h)),
            pl.BlockSpec((1, S, gw), lambda b, h, i: (b, 0, n_groups + h)),
            pl.BlockSpec((1, G, dv + ONES_ROWS, S), lambda b, h, i: (b, h, 0, 0),
                         pipeline_mode=pl.Buffered(1)),
            pl.BlockSpec((1, gw, 2 * T), lambda b, h, i: (b, h, i)),
            tile_spec, tile_spec,
            vec, vec, vec, vec,
            pl.BlockSpec((dv, 1), lambda b, h, i: (0, 0)),
        ],
        out_specs=pl.BlockSpec((1, gw, 2 * T), lambda b, h, i: (b, h, i)),
        out_shape=jax.ShapeDtypeStruct((B, d_inner, S), BF16),
        scratch_shapes=[pltpu.VMEM((2 * G, dv, 2 * T), BF16),
                        pltpu.VMEM((G, 2, T, T), F32),
                        pltpu.VMEM((2 * G, 1, 2 * T), F32),
                        pltpu.VMEM((2 * G, dv + ONES_ROWS, 2 * T), F32),
                        pltpu.VMEM((2 * G, T, 2 * T), F32),
                        pltpu.VMEM((2 * G, T, 2 * T), F32)],
        compiler_params=_compiler_params(("parallel", "arbitrary", "arbitrary")),
        name="diff_attn",
    )(rel_bias.astype(F32).reshape(-1), bfar, qk, qk, vaug, gt, bk0, bk1,
      lq1.reshape(1, d), lk1.reshape(1, d), lq2.reshape(1, d), lk2.reshape(1, d),
      subln_g.reshape(dv, 1))


def kernel(x, pre_norm_g, post_norm_g, ret_w_in, ret_w_out, diff_w_in, diff_w_out,
           diff_lambda_q1, diff_lambda_k1, diff_lambda_q2, diff_lambda_k2,
           diff_subln_g, rel_bias):
    d_model = x.shape[-1]
    d_inner = ret_w_out.shape[1]
    for i in range(DEPTH):
        j = i // N_MIXERS
        if i % N_MIXERS == 0:
            proj = _norm_matmul_ret(x, pre_norm_g[i], ret_w_in[j].astype(BF16),
                                    d_inner, RET_HEADS)
            y = _retention(proj, d_model, d_inner)
            x = _out_proj(y, ret_w_out[j].astype(BF16), post_norm_g[i], x,
                          transposed_in=False)
        else:
            lambda_init = 0.8 - 0.6 * math.exp(-0.3 * i)
            qk, vaug, gt = _norm_matmul_diff(
                x, pre_norm_g[i], diff_w_in[j].astype(BF16), DIFF_HEADS)
            yt = _diff_attn(qk, vaug, gt, diff_lambda_q1[j], diff_lambda_k1[j],
                            diff_lambda_q2[j], diff_lambda_k2[j], diff_subln_g[j],
                            rel_bias, lambda_init)
            x = _out_proj(yt, diff_w_out[j].astype(BF16), post_norm_g[i], x,
                          transposed_in=True)
    return x
```

```python
import functools
import math

import jax
import jax.numpy as jnp
import numpy as np
from jax import lax
from jax.experimental import pallas as pl
from jax.experimental.pallas import tpu as pltpu

F32 = jnp.float32
BF16 = jnp.bfloat16

DEPTH = 2
N_MIXERS = 2
RET_HEADS = 4
RET_CHUNK = 128
DIFF_HEADS = 16
REL_BUCKETS = 32
REL_MAX_DIST = 128
ROPE_BASE = 10000.0
RMS_EPS = 1e-6
GN_EPS = 1e-5

V7X_VMEM_BYTES = 64 * 1024 * 1024
VMEM_LIMIT_BYTES = V7X_VMEM_BYTES * 7 // 8

PROJ_ROWS = 1024
PROJ_COLS = 2048
OUT_ROWS = 1024
RET_CHUNKS_PER_STEP = 8
ATT_TILE = 256
ATT_HEADS_PER_STEP = 4
ONES_ROWS = 16
LOG2E = math.log2(math.e)


def _compiler_params(semantics):
    return pltpu.CompilerParams(dimension_semantics=semantics,
                                vmem_limit_bytes=VMEM_LIMIT_BYTES)


def _store_normed(x_ref, g_ref, h_ref):
    x = x_ref[0]
    ms = jnp.mean(x * x, axis=-1, keepdims=True)
    h_ref[...] = (x * lax.rsqrt(ms + RMS_EPS) * g_ref[...]).astype(BF16)


def _silu(x):
    return x * jax.nn.sigmoid(x)


def _norm_matmul_ret_kernel(x_ref, g_ref, w_ref, cos_ref, sin_ref, o_ref, h_ref,
                            *, dk):
    j = pl.program_id(2)

    def project(cols=slice(None)):
        return jnp.dot(h_ref[...], w_ref[:, cols], preferred_element_type=F32)

    @pl.when(j == 0)
    def _():
        _store_normed(x_ref, g_ref, h_ref)
        half = dk // 2
        n_blocks = w_ref.shape[1] // dk
        c, s = cos_ref[...], sin_ref[...]
        for blk in range(n_blocks):
            o = project(slice(blk * dk, (blk + 1) * dk))
            scale = 1.0 if blk < n_blocks // 2 else dk ** -0.5
            x1, x2 = o[:, :half], o[:, half:]
            o_ref[0, :, blk * dk:blk * dk + half] = (
                (x1 * c - x2 * s) * scale).astype(BF16)
            o_ref[0, :, blk * dk + half:(blk + 1) * dk] = (
                (x1 * s + x2 * c) * scale).astype(BF16)

    @pl.when(j == 1)
    def _():
        o_ref[0] = project().astype(BF16)

    @pl.when(j == 2)
    def _():
        o_ref[0] = _silu(project()).astype(BF16)


def _norm_matmul_ret(x, g, w, d_inner, n_heads):
    B, S, D = x.shape
    N = w.shape[1]
    ts, tn = PROJ_ROWS, d_inner
    dk = (N - 2 * d_inner) // 2 // n_heads
    half = dk // 2
    assert N == 3 * tn, "q | k must fill exactly one column tile"

    pos = np.arange(S, dtype=np.float64)
    inv = ROPE_BASE ** (-np.arange(half, dtype=np.float64) / half)
    ang = pos[:, None] * inv[None, :]
    cos = jnp.asarray(np.cos(ang), dtype=F32)
    sin = jnp.asarray(np.sin(ang), dtype=F32)

    return pl.pallas_call(
        functools.partial(_norm_matmul_ret_kernel, dk=dk),
        grid=(B, S // ts, N // tn),
        in_specs=[pl.BlockSpec((1, ts, D), lambda b, i, j: (b, i, 0)),
                  pl.BlockSpec((1, D), lambda b, i, j: (0, 0)),
                  pl.BlockSpec((D, tn), lambda b, i, j: (0, j)),
                  pl.BlockSpec((ts, half), lambda b, i, j: (i, 0)),
                  pl.BlockSpec((ts, half), lambda b, i, j: (i, 0))],
        out_specs=pl.BlockSpec((1, ts, tn), lambda b, i, j: (b, i, j)),
        out_shape=jax.ShapeDtypeStruct((B, S, N), BF16),
        scratch_shapes=[pltpu.VMEM((ts, D), BF16)],
        compiler_params=_compiler_params(("parallel", "parallel", "arbitrary")),
        name="norm_matmul_ret",
    )(x, g.reshape(1, D), w, cos, sin)


def _norm_matmul_diff_kernel(x_ref, g_ref, w_ref, qk_ref, vaug_ref, gt_ref, h_ref):
    j = pl.program_id(2)

    @pl.when(j == 0)
    def _():
        _store_normed(x_ref, g_ref, h_ref)

    def project():
        return jnp.dot(h_ref[...], w_ref[...], preferred_element_type=F32)

    @pl.when(j < 2)
    def _():
        qk_ref[0] = project().astype(BF16)

    @pl.when(j == 2)
    def _():
        n_heads, rows_aug, ts = vaug_ref.shape[1:]
        dv = w_ref.shape[1] // n_heads
        for h in range(0, n_heads, 2):
            ot = jnp.dot(h_ref[...], w_ref[:, h * dv:(h + 2) * dv],
                         preferred_element_type=F32).T.astype(BF16)
            for e in range(2):
                vaug_ref[0, h + e, 0:dv, :] = ot[e * dv:(e + 1) * dv, :]
                vaug_ref[0, h + e, dv:rows_aug, :] = jnp.ones(
                    (rows_aug - dv, ts), BF16)

    @pl.when(j == 3)
    def _():
        gt_ref[0] = _silu(project()).T.astype(BF16)


def _norm_matmul_diff(x, g, w, n_heads):
    B, S, D = x.shape
    Di = w.shape[1] // 4
    dv = Di // n_heads
    ts = PROJ_ROWS
    return pl.pallas_call(
        _norm_matmul_diff_kernel,
        grid=(B, S // ts, 4),
        in_specs=[pl.BlockSpec((1, ts, D), lambda b, i, j: (b, i, 0)),
                  pl.BlockSpec((1, D), lambda b, i, j: (0, 0)),
                  pl.BlockSpec((D, Di), lambda b, i, j: (0, j))],
        out_specs=[
            pl.BlockSpec((1, ts, Di), lambda b, i, j: (b, i, jnp.minimum(j, 1))),
            pl.BlockSpec((1, n_heads, dv + ONES_ROWS, ts),
                         lambda b, i, j: (b, 0, 0, i)),
            pl.BlockSpec((1, Di, ts), lambda b, i, j: (b, 0, i)),
        ],
        out_shape=[jax.ShapeDtypeStruct((B, S, 2 * Di), BF16),
                   jax.ShapeDtypeStruct((B, n_heads, dv + ONES_ROWS, S), BF16),
                   jax.ShapeDtypeStruct((B, Di, S), BF16)],
        scratch_shapes=[pltpu.VMEM((ts, D), BF16)],
        compiler_params=_compiler_params(("parallel", "parallel", "arbitrary")),
        name="norm_matmul_diff",
    )(x, g.reshape(1, D), w)


def _out_proj_kernel(y_ref, w_ref, g_ref, x_ref, o_ref, *, transposed_in):
    y = y_ref[0]
    if transposed_in:
        o = lax.dot_general(y, w_ref[...], (((0,), (0,)), ((), ())),
                            preferred_element_type=F32)
    else:
        o = jnp.dot(y, w_ref[...], preferred_element_type=F32)
    ms = jnp.mean(o * o, axis=-1, keepdims=True)
    o_ref[0] = x_ref[0] + o * lax.rsqrt(ms + RMS_EPS) * g_ref[...]


def _out_proj(y, w, g, x, *, transposed_in):
    B, S, D = x.shape
    Di = w.shape[0]
    ts = OUT_ROWS
    if transposed_in:
        y_spec = pl.BlockSpec((1, Di, ts), lambda b, i: (b, 0, i))
    else:
        y_spec = pl.BlockSpec((1, ts, Di), lambda b, i: (b, i, 0))
    return pl.pallas_call(
        functools.partial(_out_proj_kernel, transposed_in=transposed_in),
        grid=(B, S // ts),
        in_specs=[y_spec,
                  pl.BlockSpec((Di, D), lambda b, i: (0, 0)),
                  pl.BlockSpec((1, D), lambda b, i: (0, 0)),
                  pl.BlockSpec((1, ts, D), lambda b, i: (b, i, 0))],
        out_specs=pl.BlockSpec((1, ts, D), lambda b, i: (b, i, 0)),
        out_shape=jax.ShapeDtypeStruct((B, S, D), F32),
        compiler_params=_compiler_params(("parallel", "parallel")),
        name="out_proj_t" if transposed_in else "out_proj",
    )(y, w, g.reshape(1, D), x)


def _retention_kernel(q_ref, k_ref, v_ref, g_ref, dmat_ref, xi_ref, zeta_ref,
                      cdec_ref, o_ref, state_ref):
    C = RET_CHUNK

    @pl.when(pl.program_id(2) == 0)
    def _():
        state_ref[...] = jnp.zeros_like(state_ref)

    dmat = dmat_ref[0]
    xi = xi_ref[0]
    zeta = zeta_ref[0]
    cdec = cdec_ref[0]

    for ci in range(RET_CHUNKS_PER_STEP):
        rows = pl.ds(ci * C, C)
        q, k, v = q_ref[0, rows, :], k_ref[0, rows, :], v_ref[0, rows, :]
        state = state_ref[...]
        scores = lax.dot_general(q, k, (((1,), (1,)), ((), ())),
                                 preferred_element_type=F32) * dmat
        inner = jnp.dot(scores.astype(BF16), v, preferred_element_type=F32)
        cross = jnp.dot(q, state.astype(BF16), preferred_element_type=F32) * xi
        kz = (k.astype(F32) * zeta).astype(BF16)
        state_ref[...] = state * cdec + lax.dot_general(
            kz, v, (((0,), (0,)), ((), ())), preferred_element_type=F32)
        ret = inner + cross
        mu = jnp.mean(ret, axis=-1, keepdims=True)
        cen = ret - mu
        var = jnp.mean(cen * cen, axis=-1, keepdims=True)
        y = cen * lax.rsqrt(var + GN_EPS)
        o_ref[0, rows, :] = (g_ref[0, rows, :].astype(F32) * y).astype(o_ref.dtype)


def _retention(proj, d_model, d_inner):
    B, S, _ = proj.shape
    H, C = RET_HEADS, RET_CHUNK
    dk, dv = d_model // H, d_inner // H
    rows = C * RET_CHUNKS_PER_STEP
    k_off = d_model // dk
    v_off = 2 * d_model // dv
    g_off = (2 * d_model + d_inner) // dv

    log_gamma = jnp.log1p(-jnp.exp2(-5.0 - jnp.arange(H, dtype=F32)))
    idx = jnp.arange(C, dtype=F32)
    rel = idx[:, None] - idx[None, :]
    dmat = jnp.exp(log_gamma[:, None, None] * jnp.maximum(rel, 0.0)) * (rel >= 0)
    xi = jnp.exp(log_gamma[:, None] * (idx + 1.0))
    zeta = jnp.exp(log_gamma[:, None] * (C - 1.0 - idx))
    cdec = jnp.exp(log_gamma * C)
    xi = jnp.broadcast_to(xi[:, :, None], (H, C, dv))
    zeta = jnp.broadcast_to(zeta[:, :, None], (H, C, dk))
    cdec = jnp.broadcast_to(cdec[:, None, None], (H, 1, dv))

    return pl.pallas_call(
        _retention_kernel,
        grid=(B, H, S // rows),
        in_specs=[
            pl.BlockSpec((1, rows, dk), lambda b, h, n: (b, n, h)),
            pl.BlockSpec((1, rows, dk), lambda b, h, n: (b, n, k_off + h)),
            pl.BlockSpec((1, rows, dv), lambda b, h, n: (b, n, v_off + h)),
            pl.BlockSpec((1, rows, dv), lambda b, h, n: (b, n, g_off + h)),
            pl.BlockSpec((1, C, C), lambda b, h, n: (h, 0, 0)),
            pl.BlockSpec((1, C, dv), lambda b, h, n: (h, 0, 0)),
            pl.BlockSpec((1, C, dk), lambda b, h, n: (h, 0, 0)),
            pl.BlockSpec((1, 1, dv), lambda b, h, n: (h, 0, 0)),
        ],
        out_specs=pl.BlockSpec((1, rows, dv), lambda b, h, n: (b, n, h)),
        out_shape=jax.ShapeDtypeStruct((B, S, d_inner), BF16),
        scratch_shapes=[pltpu.VMEM((dk, dv), F32)],
        compiler_params=_compiler_params(("parallel", "parallel", "arbitrary")),
        name="retention",
    )(proj, proj, proj, proj, dmat, xi, zeta, cdec)


def _diff_attn_kernel(tab_ref, bfar_ref, q_ref, k_ref, vaug_ref, gt_ref, bkd_ref,
                      lq1_ref, lk1_ref, lq2_ref, lk2_ref, sg_ref, o_ref,
                      qbd_ref, bias_ref, m_ref, acc_ref, sa_ref, sb_ref,
                      *, d, lambda_init):
    T, G, H = ATT_TILE, ATT_HEADS_PER_STEP, DIFF_HEADS
    dv = 2 * d
    hp = pl.program_id(1)
    i = pl.program_id(2)

    @pl.when(i == 0)
    def _():
        rows = lax.broadcasted_iota(jnp.int32, (T, T), 0)
        cols = lax.broadcasted_iota(jnp.int32, (T, T), 1)
        bucket_of_distance = bkd_ref[...]
        for u in range(G):
            h = hp * G + u
            cfar = tab_ref[bfar_ref[0] * H + h]
            by_distance = jnp.zeros((1, 2 * T), F32)
            for bucket in range(REL_BUCKETS):
                by_distance = jnp.where(bucket_of_distance == bucket,
                                        tab_ref[bucket * H + h], by_distance)
            by_distance = (by_distance - cfar) * LOG2E
            toeplitz = pltpu.roll(jnp.broadcast_to(by_distance, (T, 2 * T)),
                                  0, 1, stride=1, stride_axis=0)
            bias_ref[u, 0] = jnp.where(rows <= cols, toeplitz[:, 0:T], -jnp.inf)
            bias_ref[u, 1] = toeplitz[:, T:2 * T]

    def logits_into(dst_ref, w, j, bias_index):
        u = w % G
        start = pl.multiple_of(j * T, T)
        kt = k_ref[0, pl.ds(start, T), u * dv:(u + 1) * dv]
        s = jnp.dot(kt, qbd_ref[w], preferred_element_type=F32)
        if bias_index is None:
            dst_ref[w] = s
        else:
            bias = bias_ref[u, bias_index]
            dst_ref[w, :, 0:T] = s[:, 0:T] + bias
            dst_ref[w, :, T:2 * T] = s[:, T:2 * T] + bias

    def consume(src_ref, w, j):
        u = w % G
        start = pl.multiple_of(j * T, T)
        s = src_ref[w]
        m_old = m_ref[w]
        m_new = jnp.maximum(m_old, jnp.max(s, axis=0, keepdims=True))
        p = jnp.exp2(s - m_new).astype(BF16)
        alpha = jnp.exp2(m_old - m_new)
        pv = jnp.dot(vaug_ref[0, u, :, pl.ds(start, T)], p,
                     preferred_element_type=F32)
        acc_ref[w] = acc_ref[w] * alpha + pv
        m_ref[w] = m_new

    first, second, both = range(G), range(G, 2 * G), range(2 * G)

    def step(dst_ref, j_next, bias_of, src_ref, j_cur, cur_units=both):
        for w in both:
            logits_into(dst_ref, w, j_next, bias_of(w))
            if w in cur_units:
                consume(src_ref, w, j_cur)

    no_bias = lambda w: None

    def start():
        for w in both:
            sub, u = divmod(w, G)
            q = q_ref[0, sub * T:(sub + 1) * T, u * dv:(u + 1) * dv].astype(F32)
            qt = (q * (d ** -0.5 * LOG2E)).T
            row = lax.broadcasted_iota(jnp.int32, qt.shape, 0)
            qbd_ref[w, :, 0:T] = jnp.where(row < d, qt, 0.0).astype(BF16)
            qbd_ref[w, :, T:2 * T] = jnp.where(row >= d, qt, 0.0).astype(BF16)
        m_ref[...] = jnp.full_like(m_ref, -jnp.inf)
        acc_ref[...] = jnp.zeros_like(acc_ref)
        for w in second:
            logits_into(sa_ref, w, 2 * i + 1, 0)
        step(sb_ref, 2 * i, lambda w: 0 if w in first else 1,
             sa_ref, 2 * i + 1, cur_units=second)

    def finish(src_ref, j_cur):
        lam = (jnp.exp(jnp.sum(lq1_ref[...] * lk1_ref[...], keepdims=True))
               - jnp.exp(jnp.sum(lq2_ref[...] * lk2_ref[...], keepdims=True))
               + lambda_init)
        out_scale = sg_ref[...] * (1.0 - lambda_init)
        for w in both:
            sub, u = divmod(w, G)
            consume(src_ref, w, j_cur)
            acc = acc_ref[w]
            inv_l = 1.0 / acc[dv:dv + 1, :]
            o = (acc[:dv, :T] * inv_l[:, :T]
                 - acc[:dv, T:] * (lam * inv_l[:, T:]))
            ms = jnp.mean(o * o, axis=0, keepdims=True)
            on = (o * lax.rsqrt(ms + GN_EPS)) * out_scale
            head_rows = slice(u * dv, (u + 1) * dv)
            sub_cols = slice(sub * T, (sub + 1) * T)
            gate = gt_ref[0, head_rows, sub_cols].astype(F32)
            o_ref[0, head_rows, sub_cols] = (gate * on).astype(o_ref.dtype)

    @pl.when(i == 0)
    def _():
        start()
        finish(sb_ref, 2 * i)

    @pl.when(i >= 1)
    def _():
        start()
        step(sa_ref, 2 * i - 1, lambda w: 1 if w in first else None,
             sb_ref, 2 * i)
        step(sb_ref, 0, no_bias, sa_ref, 2 * i - 1)

        def pair(r):
            step(sa_ref, 2 * r - 1, no_bias, sb_ref, 2 * r - 2)
            step(sb_ref, 2 * r, no_bias, sa_ref, 2 * r - 1)

        def quad_body(t, carry):
            pair(2 * t + 1)
            pair(2 * t + 2)
            return carry

        lax.fori_loop(0, (i - 1) // 2, quad_body, 0)

        @pl.when((i - 1) % 2 == 1)
        def _():
            pair(i - 1)
            finish(sb_ref, 2 * i - 2)

        @pl.when((i - 1) % 2 == 0)
        def _():
            finish(sb_ref, 2 * i - 2)


def _t5_bucket(n):
    max_exact = REL_BUCKETS // 2
    nf = jnp.maximum(n, max_exact).astype(F32)
    large = max_exact + (jnp.log(nf / max_exact) / math.log(REL_MAX_DIST / max_exact)
                         * (REL_BUCKETS - max_exact)).astype(jnp.int32)
    large = jnp.minimum(large, REL_BUCKETS - 1)
    return jnp.where(n < max_exact, n, large)


def _diff_attn(qk, vaug, gt, lq1, lk1, lq2, lk2, subln_g, rel_bias, lambda_init):
    B, S, _ = qk.shape
    H, T, G = DIFF_HEADS, ATT_TILE, ATT_HEADS_PER_STEP
    d_inner = gt.shape[1]
    dv = d_inner // H
    d = dv // 2
    gw = G * dv
    n_groups = H // G

    bkd = _t5_bucket(jnp.arange(2 * T, dtype=jnp.int32)[None, :])
    bfar = _t5_bucket(jnp.full((1,), T + 1, jnp.int32))

    smem = pl.BlockSpec(memory_space=pltpu.SMEM)
    vec = pl.BlockSpec((1, d), lambda b, h, i: (0, 0))
    return pl.pallas_call(
        functools.partial(_diff_attn_kernel, d=d, lambda_init=lambda_init),
        grid=(B, n_groups, S // (2 * T)),
        in_specs=[
            smem, smem,
            pl.BlockSpec((1, 2 * T, gw), lambda b, h, i: (b, i, h)),
            pl.BlockSpec((1, S, gw), lambda b, h, i: (b, 0, n_groups + h)),
            pl.BlockSpec((1, G, dv + ONES_ROWS, S), lambda b, h, i: (b, h, 0, 0),
                         pipeline_mode=pl.Buffered(1)),
            pl.BlockSpec((1, gw, 2 * T), lambda b, h, i: (b, h, i)),
            pl.BlockSpec((1, 2 * T), lambda b, h, i: (0, 0)),
            vec, vec, vec, vec,
            pl.BlockSpec((dv, 1), lambda b, h, i: (0, 0)),
        ],
        out_specs=pl.BlockSpec((1, gw, 2 * T), lambda b, h, i: (b, h, i)),
        out_shape=jax.ShapeDtypeStruct((B, d_inner, S), BF16),
        scratch_shapes=[pltpu.VMEM((2 * G, dv, 2 * T), BF16),
                        pltpu.VMEM((G, 2, T, T), F32),
                        pltpu.VMEM((2 * G, 1, 2 * T), F32),
                        pltpu.VMEM((2 * G, dv + ONES_ROWS, 2 * T), F32),
                        pltpu.VMEM((2 * G, T, 2 * T), F32),
                        pltpu.VMEM((2 * G, T, 2 * T), F32)],
        compiler_params=_compiler_params(("parallel", "arbitrary", "arbitrary")),
        name="diff_attn",
    )(rel_bias.astype(F32).reshape(-1), bfar, qk, qk, vaug, gt, bkd,
      lq1.reshape(1, d), lk1.reshape(1, d), lq2.reshape(1, d), lk2.reshape(1, d),
      subln_g.reshape(dv, 1))


def kernel(x, pre_norm_g, post_norm_g, ret_w_in, ret_w_out, diff_w_in, diff_w_out,
           diff_lambda_q1, diff_lambda_k1, diff_lambda_q2, diff_lambda_k2,
           diff_subln_g, rel_bias):
    d_model = x.shape[-1]
    d_inner = ret_w_out.shape[1]
    for i in range(DEPTH):
        j = i // N_MIXERS
        if i % N_MIXERS == 0:
            proj = _norm_matmul_ret(x, pre_norm_g[i], ret_w_in[j].astype(BF16),
                                    d_inner, RET_HEADS)
            y = _retention(proj, d_model, d_inner)
            x = _out_proj(y, ret_w_out[j].astype(BF16), post_norm_g[i], x,
                          transposed_in=False)
        else:
            lambda_init = 0.8 - 0.6 * math.exp(-0.3 * i)
            qk, vaug, gt = _norm_matmul_diff(
                x, pre_norm_g[i], diff_w_in[j].astype(BF16), DIFF_HEADS)
            yt = _diff_attn(qk, vaug, gt, diff_lambda_q1[j], diff_lambda_k1[j],
                            diff_lambda_q2[j], diff_lambda_k2[j], diff_subln_g[j],
                            rel_bias, lambda_init)
            x = _out_proj(yt, diff_w_out[j].astype(BF16), post_norm_g[i], x,
                          transposed_in=True)
    return x
```

```python
import functools
import math

import jax
import jax.numpy as jnp
import numpy as np
from jax import lax
from jax.experimental import pallas as pl
from jax.experimental.pallas import tpu as pltpu

F32 = jnp.float32
BF16 = jnp.bfloat16

DEPTH = 2
N_MIXERS = 2
RET_HEADS = 4
RET_CHUNK = 128
DIFF_HEADS = 16
REL_BUCKETS = 32
REL_MAX_DIST = 128
ROPE_BASE = 10000.0
RMS_EPS = 1e-6
GN_EPS = 1e-5

V7X_VMEM_BYTES = 64 * 1024 * 1024
VMEM_LIMIT_BYTES = V7X_VMEM_BYTES * 7 // 8

PROJ_ROWS = 1024
PROJ_COLS = 2048
OUT_ROWS = 1024
RET_CHUNKS_PER_STEP = 16
RET_NORM_ROWS = 32
ATT_TILE = 256
ATT_HEADS_PER_STEP = 4
ONES_ROWS = 16
LOG2E = math.log2(math.e)


def _compiler_params(semantics):
    return pltpu.CompilerParams(dimension_semantics=semantics,
                                vmem_limit_bytes=VMEM_LIMIT_BYTES)


def _store_normed(x_ref, g_ref, h_ref):
    x = x_ref[0]
    ms = jnp.mean(x * x, axis=-1, keepdims=True)
    h_ref[...] = (x * lax.rsqrt(ms + RMS_EPS) * g_ref[...]).astype(BF16)


def _silu(x):
    return x * jax.nn.sigmoid(x)


def _norm_matmul_ret_kernel(x_ref, g_ref, w_ref, cos_ref, sin_ref, o_ref, h_ref,
                            *, dk):
    j = pl.program_id(2)

    def project(cols=slice(None)):
        return jnp.dot(h_ref[...], w_ref[:, cols], preferred_element_type=F32)

    @pl.when(j == 0)
    def _():
        _store_normed(x_ref, g_ref, h_ref)
        half = dk // 2
        n_blocks = w_ref.shape[1] // dk
        c, s = cos_ref[...], sin_ref[...]
        for blk in range(n_blocks):
            o = project(slice(blk * dk, (blk + 1) * dk))
            scale = 1.0 if blk < n_blocks // 2 else dk ** -0.5
            x1, x2 = o[:, :half], o[:, half:]
            o_ref[0, :, blk * dk:blk * dk + half] = (
                (x1 * c - x2 * s) * scale).astype(BF16)
            o_ref[0, :, blk * dk + half:(blk + 1) * dk] = (
                (x1 * s + x2 * c) * scale).astype(BF16)

    @pl.when(j == 1)
    def _():
        o_ref[0] = project().astype(BF16)

    @pl.when(j == 2)
    def _():
        o_ref[0] = _silu(project()).astype(BF16)


def _norm_matmul_ret(x, g, w, d_inner, n_heads):
    B, S, D = x.shape
    N = w.shape[1]
    ts, tn = PROJ_ROWS, d_inner
    dk = (N - 2 * d_inner) // 2 // n_heads
    half = dk // 2
    assert N == 3 * tn, "q | k must fill exactly one column tile"

    pos = np.arange(S, dtype=np.float64)
    inv = ROPE_BASE ** (-np.arange(half, dtype=np.float64) / half)
    ang = pos[:, None] * inv[None, :]
    cos = jnp.asarray(np.cos(ang), dtype=F32)
    sin = jnp.asarray(np.sin(ang), dtype=F32)

    return pl.pallas_call(
        functools.partial(_norm_matmul_ret_kernel, dk=dk),
        grid=(B, S // ts, N // tn),
        in_specs=[pl.BlockSpec((1, ts, D), lambda b, i, j: (b, i, 0)),
                  pl.BlockSpec((1, D), lambda b, i, j: (0, 0)),
                  pl.BlockSpec((D, tn), lambda b, i, j: (0, j)),
                  pl.BlockSpec((ts, half), lambda b, i, j: (i, 0)),
                  pl.BlockSpec((ts, half), lambda b, i, j: (i, 0))],
        out_specs=pl.BlockSpec((1, ts, tn), lambda b, i, j: (b, i, j)),
        out_shape=jax.ShapeDtypeStruct((B, S, N), BF16),
        scratch_shapes=[pltpu.VMEM((ts, D), BF16)],
        compiler_params=_compiler_params(("parallel", "parallel", "arbitrary")),
        name="norm_matmul_ret",
    )(x, g.reshape(1, D), w, cos, sin)


def _norm_matmul_diff_kernel(x_ref, g_ref, w_ref, qk_ref, vaug_ref, gt_ref, h_ref):
    j = pl.program_id(2)

    @pl.when(j == 0)
    def _():
        _store_normed(x_ref, g_ref, h_ref)

    def project():
        return jnp.dot(h_ref[...], w_ref[...], preferred_element_type=F32)

    @pl.when(j < 2)
    def _():
        qk_ref[0] = project().astype(BF16)

    @pl.when(j == 2)
    def _():
        n_heads, rows_aug, ts = vaug_ref.shape[1:]
        dv = w_ref.shape[1] // n_heads
        for h in range(0, n_heads, 2):
            ot = jnp.dot(h_ref[...], w_ref[:, h * dv:(h + 2) * dv],
                         preferred_element_type=F32).T.astype(BF16)
            for e in range(2):
                vaug_ref[0, h + e, 0:dv, :] = ot[e * dv:(e + 1) * dv, :]
                vaug_ref[0, h + e, dv:rows_aug, :] = jnp.ones(
                    (rows_aug - dv, ts), BF16)

    @pl.when(j == 3)
    def _():
        gt_ref[0] = _silu(project()).T.astype(BF16)


def _norm_matmul_diff(x, g, w, n_heads):
    B, S, D = x.shape
    Di = w.shape[1] // 4
    dv = Di // n_heads
    ts = PROJ_ROWS
    return pl.pallas_call(
        _norm_matmul_diff_kernel,
        grid=(B, S // ts, 4),
        in_specs=[pl.BlockSpec((1, ts, D), lambda b, i, j: (b, i, 0)),
                  pl.BlockSpec((1, D), lambda b, i, j: (0, 0)),
                  pl.BlockSpec((D, Di), lambda b, i, j: (0, j))],
        out_specs=[
            pl.BlockSpec((1, ts, Di), lambda b, i, j: (b, i, jnp.minimum(j, 1))),
            pl.BlockSpec((1, n_heads, dv + ONES_ROWS, ts),
                         lambda b, i, j: (b, 0, 0, i)),
            pl.BlockSpec((1, Di, ts), lambda b, i, j: (b, 0, i)),
        ],
        out_shape=[jax.ShapeDtypeStruct((B, S, 2 * Di), BF16),
                   jax.ShapeDtypeStruct((B, n_heads, dv + ONES_ROWS, S), BF16),
                   jax.ShapeDtypeStruct((B, Di, S), BF16)],
        scratch_shapes=[pltpu.VMEM((ts, D), BF16)],
        compiler_params=_compiler_params(("parallel", "parallel", "arbitrary")),
        name="norm_matmul_diff",
    )(x, g.reshape(1, D), w)


def _out_proj_kernel(y_ref, w_ref, g_ref, x_ref, o_ref, *, transposed_in):
    y = y_ref[0]
    if transposed_in:
        o = lax.dot_general(y, w_ref[...], (((0,), (0,)), ((), ())),
                            preferred_element_type=F32)
    else:
        o = jnp.dot(y, w_ref[...], preferred_element_type=F32)
    ms = jnp.mean(o * o, axis=-1, keepdims=True)
    o_ref[0] = x_ref[0] + o * lax.rsqrt(ms + RMS_EPS) * g_ref[...]


def _out_proj(y, w, g, x, *, transposed_in):
    B, S, D = x.shape
    Di = w.shape[0]
    ts = OUT_ROWS
    if transposed_in:
        y_spec = pl.BlockSpec((1, Di, ts), lambda b, i: (b, 0, i))
    else:
        y_spec = pl.BlockSpec((1, ts, Di), lambda b, i: (b, i, 0))
    return pl.pallas_call(
        functools.partial(_out_proj_kernel, transposed_in=transposed_in),
        grid=(B, S // ts),
        in_specs=[y_spec,
                  pl.BlockSpec((Di, D), lambda b, i: (0, 0)),
                  pl.BlockSpec((1, D), lambda b, i: (0, 0)),
                  pl.BlockSpec((1, ts, D), lambda b, i: (b, i, 0))],
        out_specs=pl.BlockSpec((1, ts, D), lambda b, i: (b, i, 0)),
        out_shape=jax.ShapeDtypeStruct((B, S, D), F32),
        compiler_params=_compiler_params(("parallel", "parallel")),
        name="out_proj_t" if transposed_in else "out_proj",
    )(y, w, g.reshape(1, D), x)


def _retention_kernel(q_ref, k_ref, v_ref, g_ref, dmat_ref, xi_ref, zeta_ref,
                      cdec_ref, o_ref, state_ref, state_bf_ref):
    C = RET_CHUNK

    @pl.when(pl.program_id(2) == 0)
    def _():
        state_ref[...] = jnp.zeros_like(state_ref)
        state_bf_ref[...] = jnp.zeros_like(state_bf_ref)

    dmat = dmat_ref[0]
    xi = xi_ref[0]
    zeta = zeta_ref[0]
    cdec = cdec_ref[0]

    for ci in range(RET_CHUNKS_PER_STEP):
        rows = pl.ds(ci * C, C)
        q, k, v = q_ref[0, rows, :], k_ref[0, rows, :], v_ref[0, rows, :]
        scores = lax.dot_general(q, k, (((1,), (1,)), ((), ())),
                                 preferred_element_type=F32) * dmat
        cross = jnp.dot(q, state_bf_ref[...], preferred_element_type=F32) * xi
        kz = (k.astype(F32) * zeta).astype(BF16)
        state = state_ref[...] * cdec + lax.dot_general(
            kz, v, (((0,), (0,)), ((), ())), preferred_element_type=F32)
        state_ref[...] = state
        state_bf_ref[...] = state.astype(BF16)
        inner = jnp.dot(scores.astype(BF16), v, preferred_element_type=F32)
        for r0 in range(0, C, RET_NORM_ROWS):
            blk = slice(r0, r0 + RET_NORM_ROWS)
            ret = inner[blk] + cross[blk]
            mu = jnp.mean(ret, axis=-1, keepdims=True)
            cen = ret - mu
            var = jnp.mean(cen * cen, axis=-1, keepdims=True)
            y = cen * lax.rsqrt(var + GN_EPS)
            out_rows = pl.ds(ci * C + r0, RET_NORM_ROWS)
            o_ref[0, out_rows, :] = (
                g_ref[0, out_rows, :].astype(F32) * y).astype(o_ref.dtype)


def _retention(proj, d_model, d_inner):
    B, S, _ = proj.shape
    H, C = RET_HEADS, RET_CHUNK
    dk, dv = d_model // H, d_inner // H
    rows = C * RET_CHUNKS_PER_STEP
    k_off = d_model // dk
    v_off = 2 * d_model // dv
    g_off = (2 * d_model + d_inner) // dv

    log_gamma = jnp.log1p(-jnp.exp2(-5.0 - jnp.arange(H, dtype=F32)))
    idx = jnp.arange(C, dtype=F32)
    rel = idx[:, None] - idx[None, :]
    dmat = jnp.exp(log_gamma[:, None, None] * jnp.maximum(rel, 0.0)) * (rel >= 0)
    xi = jnp.exp(log_gamma[:, None] * (idx + 1.0))
    zeta = jnp.exp(log_gamma[:, None] * (C - 1.0 - idx))
    cdec = jnp.exp(log_gamma * C)
    xi = jnp.broadcast_to(xi[:, :, None], (H, C, dv))
    zeta = jnp.broadcast_to(zeta[:, :, None], (H, C, dk))
    cdec = jnp.broadcast_to(cdec[:, None, None], (H, 1, dv))

    return pl.pallas_call(
        _retention_kernel,
        grid=(B, H, S // rows),
        in_specs=[
            pl.BlockSpec((1, rows, dk), lambda b, h, n: (b, n, h)),
            pl.BlockSpec((1, rows, dk), lambda b, h, n: (b, n, k_off + h)),
            pl.BlockSpec((1, rows, dv), lambda b, h, n: (b, n, v_off + h)),
            pl.BlockSpec((1, rows, dv), lambda b, h, n: (b, n, g_off + h)),
            pl.BlockSpec((1, C, C), lambda b, h, n: (h, 0, 0)),
            pl.BlockSpec((1, C, dv), lambda b, h, n: (h, 0, 0)),
            pl.BlockSpec((1, C, dk), lambda b, h, n: (h, 0, 0)),
            pl.BlockSpec((1, 1, dv), lambda b, h, n: (h, 0, 0)),
        ],
        out_specs=pl.BlockSpec((1, rows, dv), lambda b, h, n: (b, n, h)),
        out_shape=jax.ShapeDtypeStruct((B, S, d_inner), BF16),
        scratch_shapes=[pltpu.VMEM((dk, dv), F32), pltpu.VMEM((dk, dv), BF16)],
        compiler_params=_compiler_params(("parallel", "parallel", "arbitrary")),
        name="retention",
    )(proj, proj, proj, proj, dmat, xi, zeta, cdec)


def _diff_attn_kernel(tab_ref, bfar_ref, q_ref, k_ref, vaug_ref, gt_ref, bkd_ref,
                      lq1_ref, lk1_ref, lq2_ref, lk2_ref, sg_ref, o_ref,
                      qbd_ref, bias_ref, m_ref, acc_ref, sa_ref, sb_ref,
                      *, d, lambda_init):
    T, G, H = ATT_TILE, ATT_HEADS_PER_STEP, DIFF_HEADS
    dv = 2 * d
    hp = pl.program_id(1)
    i = pl.program_id(2)

    @pl.when(i == 0)
    def _():
        rows = lax.broadcasted_iota(jnp.int32, (T, T), 0)
        cols = lax.broadcasted_iota(jnp.int32, (T, T), 1)
        bucket_of_distance = bkd_ref[...]
        for u in range(G):
            h = hp * G + u
            cfar = tab_ref[bfar_ref[0] * H + h]
            by_distance = jnp.zeros((1, 2 * T), F32)
            for bucket in range(REL_BUCKETS):
                by_distance = jnp.where(bucket_of_distance == bucket,
                                        tab_ref[bucket * H + h], by_distance)
            by_distance = (by_distance - cfar) * LOG2E
            toeplitz = pltpu.roll(jnp.broadcast_to(by_distance, (T, 2 * T)),
                                  0, 1, stride=1, stride_axis=0)
            bias_ref[u, 0] = jnp.where(rows <= cols, toeplitz[:, 0:T], -jnp.inf)
            bias_ref[u, 1] = toeplitz[:, T:2 * T]

    def logits_into(dst_ref, w, j, bias_index):
        u = w % G
        start = pl.multiple_of(j * T, T)
        kt = k_ref[0, pl.ds(start, T), u * dv:(u + 1) * dv]
        s = jnp.dot(kt, qbd_ref[w], preferred_element_type=F32)
        if bias_index is None:
            dst_ref[w] = s
        else:
            bias = bias_ref[u, bias_index]
            dst_ref[w, :, 0:T] = s[:, 0:T] + bias
            dst_ref[w, :, T:2 * T] = s[:, T:2 * T] + bias

    def consume(src_ref, w, j):
        u = w % G
        start = pl.multiple_of(j * T, T)
        s = src_ref[w]
        m_old = m_ref[w]
        m_new = jnp.maximum(m_old, jnp.max(s, axis=0, keepdims=True))
        p = jnp.exp2(s - m_new).astype(BF16)
        alpha = jnp.exp2(m_old - m_new)
        pv = jnp.dot(vaug_ref[0, u, :, pl.ds(start, T)], p,
                     preferred_element_type=F32)
        acc_ref[w] = acc_ref[w] * alpha + pv
        m_ref[w] = m_new

    first, second, both = range(G), range(G, 2 * G), range(2 * G)

    def step(dst_ref, j_next, bias_of, src_ref, j_cur, cur_units=both):
        for w in both:
            logits_into(dst_ref, w, j_next, bias_of(w))
            if w in cur_units:
                consume(src_ref, w, j_cur)

    no_bias = lambda w: None

    def start():
        for w in both:
            sub, u = divmod(w, G)
            q = q_ref[0, sub * T:(sub + 1) * T, u * dv:(u + 1) * dv].astype(F32)
            qt = (q * (d ** -0.5 * LOG2E)).T
            row = lax.broadcasted_iota(jnp.int32, qt.shape, 0)
            qbd_ref[w, :, 0:T] = jnp.where(row < d, qt, 0.0).astype(BF16)
            qbd_ref[w, :, T:2 * T] = jnp.where(row >= d, qt, 0.0).astype(BF16)
        m_ref[...] = jnp.full_like(m_ref, -jnp.inf)
        acc_ref[...] = jnp.zeros_like(acc_ref)
        for w in second:
            logits_into(sa_ref, w, 2 * i + 1, 0)
        step(sb_ref, 2 * i, lambda w: 0 if w in first else 1,
             sa_ref, 2 * i + 1, cur_units=second)

    def finish(src_ref, j_cur):
        lam = (jnp.exp(jnp.sum(lq1_ref[...] * lk1_ref[...], keepdims=True))
               - jnp.exp(jnp.sum(lq2_ref[...] * lk2_ref[...], keepdims=True))
               + lambda_init)
        out_scale = sg_ref[...] * (1.0 - lambda_init)
        for w in both:
            sub, u = divmod(w, G)
            consume(src_ref, w, j_cur)
            acc = acc_ref[w]
            inv_l = 1.0 / acc[dv:dv + 1, :]
            o = (acc[:dv, :T] * inv_l[:, :T]
                 - acc[:dv, T:] * (lam * inv_l[:, T:]))
            ms = jnp.mean(o * o, axis=0, keepdims=True)
            on = (o * lax.rsqrt(ms + GN_EPS)) * out_scale
            head_rows = slice(u * dv, (u + 1) * dv)
            sub_cols = slice(sub * T, (sub + 1) * T)
            gate = gt_ref[0, head_rows, sub_cols].astype(F32)
            o_ref[0, head_rows, sub_cols] = (gate * on).astype(o_ref.dtype)

    @pl.when(i == 0)
    def _():
        start()
        finish(sb_ref, 2 * i)

    @pl.when(i >= 1)
    def _():
        start()
        step(sa_ref, 2 * i - 1, lambda w: 1 if w in first else None,
             sb_ref, 2 * i)
        step(sb_ref, 0, no_bias, sa_ref, 2 * i - 1)

        def pair(r):
            step(sa_ref, 2 * r - 1, no_bias, sb_ref, 2 * r - 2)
            step(sb_ref, 2 * r, no_bias, sa_ref, 2 * r - 1)

        def quad_body(t, carry):
            pair(2 * t + 1)
            pair(2 * t + 2)
            return carry

        lax.fori_loop(0, (i - 1) // 2, quad_body, 0)

        @pl.when((i - 1) % 2 == 1)
        def _():
            pair(i - 1)
            finish(sb_ref, 2 * i - 2)

        @pl.when((i - 1) % 2 == 0)
        def _():
            finish(sb_ref, 2 * i - 2)


def _t5_bucket(n):
    max_exact = REL_BUCKETS // 2
    nf = jnp.maximum(n, max_exact).astype(F32)
    large = max_exact + (jnp.log(nf / max_exact) / math.log(REL_MAX_DIST / max_exact)
                         * (REL_BUCKETS - max_exact)).astype(jnp.int32)
    large = jnp.minimum(large, REL_BUCKETS - 1)
    return jnp.where(n < max_exact, n, large)


def _diff_attn(qk, vaug, gt, lq1, lk1, lq2, lk2, subln_g, rel_bias, lambda_init):
    B, S, _ = qk.shape
    H, T, G = DIFF_HEADS, ATT_TILE, ATT_HEADS_PER_STEP
    d_inner = gt.shape[1]
    dv = d_inner // H
    d = dv // 2
    gw = G * dv
    n_groups = H // G

    bkd = _t5_bucket(jnp.arange(2 * T, dtype=jnp.int32)[None, :])
    bfar = _t5_bucket(jnp.full((1,), T + 1, jnp.int32))

    smem = pl.BlockSpec(memory_space=pltpu.SMEM)
    vec = pl.BlockSpec((1, d), lambda b, h, i: (0, 0))
    return pl.pallas_call(
        functools.partial(_diff_attn_kernel, d=d, lambda_init=lambda_init),
        grid=(B, n_groups, S // (2 * T)),
        in_specs=[
            smem, smem,
            pl.BlockSpec((1, 2 * T, gw), lambda b, h, i: (b, i, h)),
            pl.BlockSpec((1, S, gw), lambda b, h, i: (b, 0, n_groups + h)),
            pl.BlockSpec((1, G, dv + ONES_ROWS, S), lambda b, h, i: (b, h, 0, 0),
                         pipeline_mode=pl.Buffered(1)),
            pl.BlockSpec((1, gw, 2 * T), lambda b, h, i: (b, h, i)),
            pl.BlockSpec((1, 2 * T), lambda b, h, i: (0, 0)),
            vec, vec, vec, vec,
            pl.BlockSpec((dv, 1), lambda b, h, i: (0, 0)),
        ],
        out_specs=pl.BlockSpec((1, gw, 2 * T), lambda b, h, i: (b, h, i)),
        out_shape=jax.ShapeDtypeStruct((B, d_inner, S), BF16),
        scratch_shapes=[pltpu.VMEM((2 * G, dv, 2 * T), BF16),
                        pltpu.VMEM((G, 2, T, T), F32),
                        pltpu.VMEM((2 * G, 1, 2 * T), F32),
                        pltpu.VMEM((2 * G, dv + ONES_ROWS, 2 * T), F32),
                        pltpu.VMEM((2 * G, T, 2 * T), F32),
                        pltpu.VMEM((2 * G, T, 2 * T), F32)],
        compiler_params=_compiler_params(("parallel", "arbitrary", "arbitrary")),
        name="diff_attn",
    )(rel_bias.astype(F32).reshape(-1), bfar, qk, qk, vaug, gt, bkd,
      lq1.reshape(1, d), lk1.reshape(1, d), lq2.reshape(1, d), lk2.reshape(1, d),
      subln_g.reshape(dv, 1))


def kernel(x, pre_norm_g, post_norm_g, ret_w_in, ret_w_out, diff_w_in, diff_w_out,
           diff_lambda_q1, diff_lambda_k1, diff_lambda_q2, diff_lambda_k2,
           diff_subln_g, rel_bias):
    d_model = x.shape[-1]
    d_inner = ret_w_out.shape[1]
    for i in range(DEPTH):
        j = i // N_MIXERS
        if i % N_MIXERS == 0:
            proj = _norm_matmul_ret(x, pre_norm_g[i], ret_w_in[j].astype(BF16),
                                    d_inner, RET_HEADS)
            y = _retention(proj, d_model, d_inner)
            x = _out_proj(y, ret_w_out[j].astype(BF16), post_norm_g[i], x,
                          transposed_in=False)
        else:
            lambda_init = 0.8 - 0.6 * math.exp(-0.3 * i)
            qk, vaug, gt = _norm_matmul_diff(
                x, pre_norm_g[i], diff_w_in[j].astype(BF16), DIFF_HEADS)
            yt = _diff_attn(qk, vaug, gt, diff_lambda_q1[j], diff_lambda_k1[j],
                            diff_lambda_q2[j], diff_lambda_k2[j], diff_subln_g[j],
                            rel_bias, lambda_init)
            x = _out_proj(yt, diff_w_out[j].astype(BF16), post_norm_g[i], x,
                          transposed_in=True)
    return x
```

```python
import functools
import math

import jax
import jax.numpy as jnp
import numpy as np
from jax import lax
from jax.experimental import pallas as pl
from jax.experimental.pallas import tpu as pltpu

F32 = jnp.float32
BF16 = jnp.bfloat16

DEPTH = 2
N_MIXERS = 2
RET_HEADS = 4
RET_CHUNK = 128
DIFF_HEADS = 16
REL_BUCKETS = 32
REL_MAX_DIST = 128
ROPE_BASE = 10000.0
RMS_EPS = 1e-6
GN_EPS = 1e-5

V7X_VMEM_BYTES = 64 * 1024 * 1024
VMEM_LIMIT_BYTES = V7X_VMEM_BYTES * 7 // 8

PROJ_ROWS = 1024
PROJ_COLS = 2048
OUT_ROWS = 1024
RET_CHUNKS_PER_STEP = 16
RET_NORM_ROWS = 32
ATT_TILE = 256
ATT_HEADS_PER_STEP = 4
ONES_ROWS = 16
LOG2E = math.log2(math.e)


def _compiler_params(semantics):
    return pltpu.CompilerParams(dimension_semantics=semantics,
                                vmem_limit_bytes=VMEM_LIMIT_BYTES)


def _store_normed(x_ref, g_ref, h_ref):
    x = x_ref[0]
    ms = jnp.mean(x * x, axis=-1, keepdims=True)
    h_ref[...] = (x * lax.rsqrt(ms + RMS_EPS) * g_ref[...]).astype(BF16)


def _silu(x):
    return x * jax.nn.sigmoid(x)


def _norm_matmul_ret_kernel(x_ref, g_ref, w_ref, cos_ref, sin_ref, o_ref, h_ref,
                            *, dk):
    j = pl.program_id(2)

    def project(cols=slice(None)):
        return jnp.dot(h_ref[...], w_ref[:, cols], preferred_element_type=F32)

    @pl.when(j == 0)
    def _():
        _store_normed(x_ref, g_ref, h_ref)
        half = dk // 2
        n_blocks = w_ref.shape[1] // dk
        c, s = cos_ref[...], sin_ref[...]
        for blk in range(n_blocks):
            o = project(slice(blk * dk, (blk + 1) * dk))
            scale = 1.0 if blk < n_blocks // 2 else dk ** -0.5
            x1, x2 = o[:, :half], o[:, half:]
            o_ref[0, :, blk * dk:blk * dk + half] = (
                (x1 * c - x2 * s) * scale).astype(BF16)
            o_ref[0, :, blk * dk + half:(blk + 1) * dk] = (
                (x1 * s + x2 * c) * scale).astype(BF16)

    @pl.when(j == 1)
    def _():
        o_ref[0] = project().astype(BF16)

    @pl.when(j == 2)
    def _():
        o_ref[0] = _silu(project()).astype(BF16)


def _norm_matmul_ret(x, g, w, d_inner, n_heads):
    B, S, D = x.shape
    N = w.shape[1]
    ts, tn = PROJ_ROWS, d_inner
    dk = (N - 2 * d_inner) // 2 // n_heads
    half = dk // 2
    assert N == 3 * tn, "q | k must fill exactly one column tile"

    pos = np.arange(S, dtype=np.float64)
    inv = ROPE_BASE ** (-np.arange(half, dtype=np.float64) / half)
    ang = pos[:, None] * inv[None, :]
    cos = jnp.asarray(np.cos(ang), dtype=F32)
    sin = jnp.asarray(np.sin(ang), dtype=F32)

    return pl.pallas_call(
        functools.partial(_norm_matmul_ret_kernel, dk=dk),
        grid=(B, S // ts, N // tn),
        in_specs=[pl.BlockSpec((1, ts, D), lambda b, i, j: (b, i, 0)),
                  pl.BlockSpec((1, D), lambda b, i, j: (0, 0)),
                  pl.BlockSpec((D, tn), lambda b, i, j: (0, j)),
                  pl.BlockSpec((ts, half), lambda b, i, j: (i, 0)),
                  pl.BlockSpec((ts, half), lambda b, i, j: (i, 0))],
        out_specs=pl.BlockSpec((1, ts, tn), lambda b, i, j: (b, i, j)),
        out_shape=jax.ShapeDtypeStruct((B, S, N), BF16),
        scratch_shapes=[pltpu.VMEM((ts, D), BF16)],
        compiler_params=_compiler_params(("parallel", "parallel", "arbitrary")),
        name="norm_matmul_ret",
    )(x, g.reshape(1, D), w, cos, sin)


def _norm_matmul_diff_kernel(x_ref, g_ref, w_ref, qk_ref, vaug_ref, gt_ref, h_ref):
    j = pl.program_id(2)

    @pl.when(j == 0)
    def _():
        _store_normed(x_ref, g_ref, h_ref)

    def project():
        return jnp.dot(h_ref[...], w_ref[...], preferred_element_type=F32)

    @pl.when(j < 2)
    def _():
        qk_ref[0] = project().astype(BF16)

    @pl.when(j == 2)
    def _():
        n_heads, rows_aug, ts = vaug_ref.shape[1:]
        dv = w_ref.shape[1] // n_heads
        for h in range(0, n_heads, 2):
            ot = jnp.dot(h_ref[...], w_ref[:, h * dv:(h + 2) * dv],
                         preferred_element_type=F32).T.astype(BF16)
            for e in range(2):
                vaug_ref[0, h + e, 0:dv, :] = ot[e * dv:(e + 1) * dv, :]
                vaug_ref[0, h + e, dv:rows_aug, :] = jnp.ones(
                    (rows_aug - dv, ts), BF16)

    @pl.when(j == 3)
    def _():
        gt_ref[0] = _silu(project()).T.astype(BF16)


def _norm_matmul_diff(x, g, w, n_heads):
    B, S, D = x.shape
    Di = w.shape[1] // 4
    dv = Di // n_heads
    ts = PROJ_ROWS
    return pl.pallas_call(
        _norm_matmul_diff_kernel,
        grid=(B, S // ts, 4),
        in_specs=[pl.BlockSpec((1, ts, D), lambda b, i, j: (b, i, 0)),
                  pl.BlockSpec((1, D), lambda b, i, j: (0, 0)),
                  pl.BlockSpec((D, Di), lambda b, i, j: (0, j))],
        out_specs=[
            pl.BlockSpec((1, ts, Di), lambda b, i, j: (b, i, jnp.minimum(j, 1))),
            pl.BlockSpec((1, n_heads, dv + ONES_ROWS, ts),
                         lambda b, i, j: (b, 0, 0, i)),
            pl.BlockSpec((1, Di, ts), lambda b, i, j: (b, 0, i)),
        ],
        out_shape=[jax.ShapeDtypeStruct((B, S, 2 * Di), BF16),
                   jax.ShapeDtypeStruct((B, n_heads, dv + ONES_ROWS, S), BF16),
                   jax.ShapeDtypeStruct((B, Di, S), BF16)],
        scratch_shapes=[pltpu.VMEM((ts, D), BF16)],
        compiler_params=_compiler_params(("parallel", "parallel", "arbitrary")),
        name="norm_matmul_diff",
    )(x, g.reshape(1, D), w)


def _out_proj_kernel(y_ref, w_ref, g_ref, x_ref, o_ref, *, transposed_in):
    y = y_ref[0]
    if transposed_in:
        o = lax.dot_general(y, w_ref[...], (((0,), (0,)), ((), ())),
                            preferred_element_type=F32)
    else:
        o = jnp.dot(y, w_ref[...], preferred_element_type=F32)
    ms = jnp.mean(o * o, axis=-1, keepdims=True)
    o_ref[0] = x_ref[0] + o * lax.rsqrt(ms + RMS_EPS) * g_ref[...]


def _out_proj(y, w, g, x, *, transposed_in):
    B, S, D = x.shape
    Di = w.shape[0]
    ts = OUT_ROWS
    if transposed_in:
        y_spec = pl.BlockSpec((1, Di, ts), lambda b, i: (b, 0, i))
    else:
        y_spec = pl.BlockSpec((1, ts, Di), lambda b, i: (b, i, 0))
    return pl.pallas_call(
        functools.partial(_out_proj_kernel, transposed_in=transposed_in),
        grid=(B, S // ts),
        in_specs=[y_spec,
                  pl.BlockSpec((Di, D), lambda b, i: (0, 0)),
                  pl.BlockSpec((1, D), lambda b, i: (0, 0)),
                  pl.BlockSpec((1, ts, D), lambda b, i: (b, i, 0))],
        out_specs=pl.BlockSpec((1, ts, D), lambda b, i: (b, i, 0)),
        out_shape=jax.ShapeDtypeStruct((B, S, D), F32),
        compiler_params=_compiler_params(("parallel", "parallel")),
        name="out_proj_t" if transposed_in else "out_proj",
    )(y, w, g.reshape(1, D), x)


def _retention_kernel(q_ref, k_ref, v_ref, g_ref, dmat_ref, xi_ref, zeta_ref,
                      cdec_ref, o_ref, state_ref, state_bf_ref):
    C = RET_CHUNK

    @pl.when(pl.program_id(2) == 0)
    def _():
        state_ref[...] = jnp.zeros_like(state_ref)
        state_bf_ref[...] = jnp.zeros_like(state_bf_ref)

    dmat = dmat_ref[0]
    xi = xi_ref[0]
    zeta = zeta_ref[0]
    cdec = cdec_ref[0]

    for ci in range(RET_CHUNKS_PER_STEP):
        rows = pl.ds(ci * C, C)
        q, k, v = q_ref[0, rows, :], k_ref[0, rows, :], v_ref[0, rows, :]
        scores = lax.dot_general(q, k, (((1,), (1,)), ((), ())),
                                 preferred_element_type=F32) * dmat
        cross = jnp.dot(q, state_bf_ref[...], preferred_element_type=F32) * xi
        kz = (k.astype(F32) * zeta).astype(BF16)
        state = state_ref[...] * cdec + lax.dot_general(
            kz, v, (((0,), (0,)), ((), ())), preferred_element_type=F32)
        state_ref[...] = state
        state_bf_ref[...] = state.astype(BF16)
        inner = jnp.dot(scores.astype(BF16), v, preferred_element_type=F32)
        for r0 in range(0, C, RET_NORM_ROWS):
            blk = slice(r0, r0 + RET_NORM_ROWS)
            ret = inner[blk] + cross[blk]
            mu = jnp.mean(ret, axis=-1, keepdims=True)
            cen = ret - mu
            var = jnp.mean(cen * cen, axis=-1, keepdims=True)
            y = cen * lax.rsqrt(var + GN_EPS)
            out_rows = pl.ds(ci * C + r0, RET_NORM_ROWS)
            o_ref[0, out_rows, :] = (
                g_ref[0, out_rows, :].astype(F32) * y).astype(o_ref.dtype)


def _retention(proj, d_model, d_inner):
    B, S, _ = proj.shape
    H, C = RET_HEADS, RET_CHUNK
    dk, dv = d_model // H, d_inner // H
    rows = C * RET_CHUNKS_PER_STEP
    k_off = d_model // dk
    v_off = 2 * d_model // dv
    g_off = (2 * d_model + d_inner) // dv

    log_gamma = jnp.log1p(-jnp.exp2(-5.0 - jnp.arange(H, dtype=F32)))
    idx = jnp.arange(C, dtype=F32)
    rel = idx[:, None] - idx[None, :]
    dmat = jnp.exp(log_gamma[:, None, None] * jnp.maximum(rel, 0.0)) * (rel >= 0)
    xi = jnp.exp(log_gamma[:, None] * (idx + 1.0))
    zeta = jnp.exp(log_gamma[:, None] * (C - 1.0 - idx))
    cdec = jnp.exp(log_gamma * C)
    xi = jnp.broadcast_to(xi[:, :, None], (H, C, dv))
    zeta = jnp.broadcast_to(zeta[:, :, None], (H, C, dk))
    cdec = jnp.broadcast_to(cdec[:, None, None], (H, 1, dv))

    return pl.pallas_call(
        _retention_kernel,
        grid=(B, H, S // rows),
        in_specs=[
            pl.BlockSpec((1, rows, dk), lambda b, h, n: (b, n, h)),
            pl.BlockSpec((1, rows, dk), lambda b, h, n: (b, n, k_off + h)),
            pl.BlockSpec((1, rows, dv), lambda b, h, n: (b, n, v_off + h)),
            pl.BlockSpec((1, rows, dv), lambda b, h, n: (b, n, g_off + h)),
            pl.BlockSpec((1, C, C), lambda b, h, n: (h, 0, 0)),
            pl.BlockSpec((1, C, dv), lambda b, h, n: (h, 0, 0)),
            pl.BlockSpec((1, C, dk), lambda b, h, n: (h, 0, 0)),
            pl.BlockSpec((1, 1, dv), lambda b, h, n: (h, 0, 0)),
        ],
        out_specs=pl.BlockSpec((1, rows, dv), lambda b, h, n: (b, n, h)),
        out_shape=jax.ShapeDtypeStruct((B, S, d_inner), BF16),
        scratch_shapes=[pltpu.VMEM((dk, dv), F32), pltpu.VMEM((dk, dv), BF16)],
        compiler_params=_compiler_params(("parallel", "parallel", "arbitrary")),
        name="retention",
    )(proj, proj, proj, proj, dmat, xi, zeta, cdec)


def _diff_attn_kernel(tab_ref, bfar_ref, q_ref, k_ref, vaug_ref, gt_ref, bkd_ref,
                      lq1_ref, lk1_ref, lq2_ref, lk2_ref, sg_ref, o_ref,
                      qbd_ref, bias_ref, m_ref, acc_ref, sa_ref, sb_ref,
                      *, d, lambda_init):
    T, G, H = ATT_TILE, ATT_HEADS_PER_STEP, DIFF_HEADS
    dv = 2 * d
    hp = pl.program_id(1)
    i = pl.program_id(2)

    @pl.when(i == 0)
    def _():
        rows = lax.broadcasted_iota(jnp.int32, (T, T), 0)
        cols = lax.broadcasted_iota(jnp.int32, (T, T), 1)
        bucket_of_distance = bkd_ref[...]
        for u in range(G):
            h = hp * G + u
            cfar = tab_ref[bfar_ref[0] * H + h]
            by_distance = jnp.zeros((1, 2 * T), F32)
            for bucket in range(REL_BUCKETS):
                by_distance = jnp.where(bucket_of_distance == bucket,
                                        tab_ref[bucket * H + h], by_distance)
            by_distance = (by_distance - cfar) * LOG2E
            toeplitz = pltpu.roll(jnp.broadcast_to(by_distance, (T, 2 * T)),
                                  0, 1, stride=1, stride_axis=0)
            bias_ref[u, 0] = jnp.where(rows <= cols, toeplitz[:, 0:T], -jnp.inf)
            bias_ref[u, 1] = toeplitz[:, T:2 * T]

    def logits_into(dst_ref, w, j, bias_index):
        u = w % G
        start = pl.multiple_of(j * T, T)
        kt = k_ref[0, pl.ds(start, T), u * dv:(u + 1) * dv]
        s = jnp.dot(kt, qbd_ref[w], preferred_element_type=F32)
        if bias_index is None:
            dst_ref[w] = s
        else:
            bias = bias_ref[u, bias_index]
            dst_ref[w, :, 0:T] = s[:, 0:T] + bias
            dst_ref[w, :, T:2 * T] = s[:, T:2 * T] + bias

    def consume(src_ref, w, j):
        u = w % G
        start = pl.multiple_of(j * T, T)
        s = src_ref[w]
        m_old = m_ref[w]
        m_new = jnp.maximum(m_old, jnp.max(s, axis=0, keepdims=True))
        p = jnp.exp2(s - m_new).astype(BF16)
        alpha = jnp.exp2(m_old - m_new)
        pv = jnp.dot(vaug_ref[0, u, :, pl.ds(start, T)], p,
                     preferred_element_type=F32)
        acc_ref[w] = acc_ref[w] * alpha + pv
        m_ref[w] = m_new

    first, second, both = range(G), range(G, 2 * G), range(2 * G)

    def step(dst_ref, j_next, bias_of, src_ref, j_cur, cur_units=both):
        for w in both:
            logits_into(dst_ref, w, j_next, bias_of(w))
            if w in cur_units:
                consume(src_ref, w, j_cur)

    no_bias = lambda w: None

    def start():
        for w in both:
            sub, u = divmod(w, G)
            q = q_ref[0, sub * T:(sub + 1) * T, u * dv:(u + 1) * dv].astype(F32)
            qt = (q * (d ** -0.5 * LOG2E)).T
            row = lax.broadcasted_iota(jnp.int32, qt.shape, 0)
            qbd_ref[w, :, 0:T] = jnp.where(row < d, qt, 0.0).astype(BF16)
            qbd_ref[w, :, T:2 * T] = jnp.where(row >= d, qt, 0.0).astype(BF16)
        m_ref[...] = jnp.full_like(m_ref, -jnp.inf)
        acc_ref[...] = jnp.zeros_like(acc_ref)
        for w in second:
            logits_into(sa_ref, w, 2 * i + 1, 0)
        step(sb_ref, 2 * i, lambda w: 0 if w in first else 1,
             sa_ref, 2 * i + 1, cur_units=second)

    def finish(src_ref, j_cur):
        lam = (jnp.exp(jnp.sum(lq1_ref[...] * lk1_ref[...], keepdims=True))
               - jnp.exp(jnp.sum(lq2_ref[...] * lk2_ref[...], keepdims=True))
               + lambda_init)
        out_scale = sg_ref[...] * (1.0 - lambda_init)
        for w in both:
            sub, u = divmod(w, G)
            consume(src_ref, w, j_cur)
            acc = acc_ref[w]
            inv_l = 1.0 / acc[dv:dv + 1, :]
            o = (acc[:dv, :T] * inv_l[:, :T]
                 - acc[:dv, T:] * (lam * inv_l[:, T:]))
            ms = jnp.mean(o * o, axis=0, keepdims=True)
            on = (o * lax.rsqrt(ms + GN_EPS)) * out_scale
            head_rows = slice(u * dv, (u + 1) * dv)
            sub_cols = slice(sub * T, (sub + 1) * T)
            gate = gt_ref[0, head_rows, sub_cols].astype(F32)
            o_ref[0, head_rows, sub_cols] = (gate * on).astype(o_ref.dtype)

    @pl.when(i == 0)
    def _():
        start()
        finish(sb_ref, 2 * i)

    @pl.when(i >= 1)
    def _():
        start()
        step(sa_ref, 2 * i - 1, lambda w: 1 if w in first else None,
             sb_ref, 2 * i)
        step(sb_ref, 0, no_bias, sa_ref, 2 * i - 1)

        def pair(r):
            step(sa_ref, 2 * r - 1, no_bias, sb_ref, 2 * r - 2)
            step(sb_ref, 2 * r, no_bias, sa_ref, 2 * r - 1)

        n_pairs = i - 1

        def oct_body(t, carry):
            for e in range(1, 5):
                pair(4 * t + e)
            return carry

        lax.fori_loop(0, n_pairs // 4, oct_body, 0)

        @pl.when(n_pairs % 4 >= 2)
        def _():
            first_left = (n_pairs // 4) * 4 + 1
            pair(first_left)
            pair(first_left + 1)

        @pl.when((i - 1) % 2 == 1)
        def _():
            pair(i - 1)
            finish(sb_ref, 2 * i - 2)

        @pl.when((i - 1) % 2 == 0)
        def _():
            finish(sb_ref, 2 * i - 2)


def _t5_bucket(n):
    max_exact = REL_BUCKETS // 2
    nf = jnp.maximum(n, max_exact).astype(F32)
    large = max_exact + (jnp.log(nf / max_exact) / math.log(REL_MAX_DIST / max_exact)
                         * (REL_BUCKETS - max_exact)).astype(jnp.int32)
    large = jnp.minimum(large, REL_BUCKETS - 1)
    return jnp.where(n < max_exact, n, large)


def _diff_attn(qk, vaug, gt, lq1, lk1, lq2, lk2, subln_g, rel_bias, lambda_init):
    B, S, _ = qk.shape
    H, T, G = DIFF_HEADS, ATT_TILE, ATT_HEADS_PER_STEP
    d_inner = gt.shape[1]
    dv = d_inner // H
    d = dv // 2
    gw = G * dv
    n_groups = H // G

    bkd = _t5_bucket(jnp.arange(2 * T, dtype=jnp.int32)[None, :])
    bfar = _t5_bucket(jnp.full((1,), T + 1, jnp.int32))

    smem = pl.BlockSpec(memory_space=pltpu.SMEM)
    vec = pl.BlockSpec((1, d), lambda b, h, i: (0, 0))
    return pl.pallas_call(
        functools.partial(_diff_attn_kernel, d=d, lambda_init=lambda_init),
        grid=(B, n_groups, S // (2 * T)),
        in_specs=[
            smem, smem,
            pl.BlockSpec((1, 2 * T, gw), lambda b, h, i: (b, i, h)),
            pl.BlockSpec((1, S, gw), lambda b, h, i: (b, 0, n_groups + h)),
            pl.BlockSpec((1, G, dv + ONES_ROWS, S), lambda b, h, i: (b, h, 0, 0),
                         pipeline_mode=pl.Buffered(1)),
            pl.BlockSpec((1, gw, 2 * T), lambda b, h, i: (b, h, i)),
            pl.BlockSpec((1, 2 * T), lambda b, h, i: (0, 0)),
            vec, vec, vec, vec,
            pl.BlockSpec((dv, 1), lambda b, h, i: (0, 0)),
        ],
        out_specs=pl.BlockSpec((1, gw, 2 * T), lambda b, h, i: (b, h, i)),
        out_shape=jax.ShapeDtypeStruct((B, d_inner, S), BF16),
        scratch_shapes=[pltpu.VMEM((2 * G, dv, 2 * T), BF16),
                        pltpu.VMEM((G, 2, T, T), F32),
                        pltpu.VMEM((2 * G, 1, 2 * T), F32),
                        pltpu.VMEM((2 * G, dv + ONES_ROWS, 2 * T), F32),
                        pltpu.VMEM((2 * G, T, 2 * T), F32),
                        pltpu.VMEM((2 * G, T, 2 * T), F32)],
        compiler_params=_compiler_params(("parallel", "arbitrary", "arbitrary")),
        name="diff_attn",
    )(rel_bias.astype(F32).reshape(-1), bfar, qk, qk, vaug, gt, bkd,
      lq1.reshape(1, d), lk1.reshape(1, d), lq2.reshape(1, d), lk2.reshape(1, d),
      subln_g.reshape(dv, 1))


def kernel(x, pre_norm_g, post_norm_g, ret_w_in, ret_w_out, diff_w_in, diff_w_out,
           diff_lambda_q1, diff_lambda_k1, diff_lambda_q2, diff_lambda_k2,
           diff_subln_g, rel_bias):
    d_model = x.shape[-1]
    d_inner = ret_w_out.shape[1]
    for i in range(DEPTH):
        j = i // N_MIXERS
        if i % N_MIXERS == 0:
            proj = _norm_matmul_ret(x, pre_norm_g[i], ret_w_in[j].astype(BF16),
                                    d_inner, RET_HEADS)
            y = _retention(proj, d_model, d_inner)
            x = _out_proj(y, ret_w_out[j].astype(BF16), post_norm_g[i], x,
                          transposed_in=False)
        else:
            lambda_init = 0.8 - 0.6 * math.exp(-0.3 * i)
            qk, vaug, gt = _norm_matmul_diff(
                x, pre_norm_g[i], diff_w_in[j].astype(BF16), DIFF_HEADS)
            yt = _diff_attn(qk, vaug, gt, diff_lambda_q1[j], diff_lambda_k1[j],
                            diff_lambda_q2[j], diff_lambda_k2[j], diff_subln_g[j],
                            rel_bias, lambda_init)
            x = _out_proj(yt, diff_w_out[j].astype(BF16), post_norm_g[i], x,
                          transposed_in=True)
    return x
```

```python
import functools
import math

import jax
import jax.numpy as jnp
import numpy as np
from jax import lax
from jax.experimental import pallas as pl
from jax.experimental.pallas import tpu as pltpu

F32 = jnp.float32
BF16 = jnp.bfloat16

DEPTH = 2
N_MIXERS = 2
RET_HEADS = 4
RET_CHUNK = 128
DIFF_HEADS = 16
REL_BUCKETS = 32
REL_MAX_DIST = 128
ROPE_BASE = 10000.0
RMS_EPS = 1e-6
GN_EPS = 1e-5

V7X_VMEM_BYTES = 64 * 1024 * 1024
VMEM_LIMIT_BYTES = V7X_VMEM_BYTES * 7 // 8

PROJ_ROWS = 1024
PROJ_COLS = 2048
OUT_ROWS = 1024
RET_CHUNKS_PER_STEP = 16
RET_NORM_ROWS = 32
ATT_TILE = 256
ATT_HEADS_PER_STEP = 4
ONES_ROWS = 16
LOG2E = math.log2(math.e)


def _compiler_params(semantics):
    return pltpu.CompilerParams(dimension_semantics=semantics,
                                vmem_limit_bytes=VMEM_LIMIT_BYTES)


def _store_normed(x_ref, g_ref, h_ref):
    x = x_ref[0]
    ms = jnp.mean(x * x, axis=-1, keepdims=True)
    h_ref[...] = (x * lax.rsqrt(ms + RMS_EPS) * g_ref[...]).astype(BF16)


def _silu(x):
    return x * jax.nn.sigmoid(x)


def _norm_matmul_ret_kernel(x_ref, g_ref, w_ref, cos_ref, sin_ref, o_ref, h_ref,
                            *, dk):
    j = pl.program_id(2)

    def project(cols=slice(None)):
        return jnp.dot(h_ref[...], w_ref[:, cols], preferred_element_type=F32)

    @pl.when(j == 0)
    def _():
        _store_normed(x_ref, g_ref, h_ref)
        half = dk // 2
        n_blocks = w_ref.shape[1] // dk
        c, s = cos_ref[...], sin_ref[...]
        for blk in range(n_blocks):
            o = project(slice(blk * dk, (blk + 1) * dk))
            scale = 1.0 if blk < n_blocks // 2 else dk ** -0.5
            x1, x2 = o[:, :half], o[:, half:]
            o_ref[0, :, blk * dk:blk * dk + half] = (
                (x1 * c - x2 * s) * scale).astype(BF16)
            o_ref[0, :, blk * dk + half:(blk + 1) * dk] = (
                (x1 * s + x2 * c) * scale).astype(BF16)

    @pl.when(j == 1)
    def _():
        o_ref[0] = project().astype(BF16)

    @pl.when(j == 2)
    def _():
        o_ref[0] = _silu(project()).astype(BF16)


def _norm_matmul_ret(x, g, w, d_inner, n_heads):
    B, S, D = x.shape
    N = w.shape[1]
    ts, tn = PROJ_ROWS, d_inner
    dk = (N - 2 * d_inner) // 2 // n_heads
    half = dk // 2
    assert N == 3 * tn, "q | k must fill exactly one column tile"

    pos = np.arange(S, dtype=np.float64)
    inv = ROPE_BASE ** (-np.arange(half, dtype=np.float64) / half)
    ang = pos[:, None] * inv[None, :]
    cos = jnp.asarray(np.cos(ang), dtype=F32)
    sin = jnp.asarray(np.sin(ang), dtype=F32)

    return pl.pallas_call(
        functools.partial(_norm_matmul_ret_kernel, dk=dk),
        grid=(B, S // ts, N // tn),
        in_specs=[pl.BlockSpec((1, ts, D), lambda b, i, j: (b, i, 0)),
                  pl.BlockSpec((1, D), lambda b, i, j: (0, 0)),
                  pl.BlockSpec((D, tn), lambda b, i, j: (0, j)),
                  pl.BlockSpec((ts, half), lambda b, i, j: (i, 0)),
                  pl.BlockSpec((ts, half), lambda b, i, j: (i, 0))],
        out_specs=pl.BlockSpec((1, ts, tn), lambda b, i, j: (b, i, j)),
        out_shape=jax.ShapeDtypeStruct((B, S, N), BF16),
        scratch_shapes=[pltpu.VMEM((ts, D), BF16)],
        compiler_params=_compiler_params(("parallel", "parallel", "arbitrary")),
        name="norm_matmul_ret",
    )(x, g.reshape(1, D), w, cos, sin)


def _norm_matmul_diff_kernel(x_ref, g_ref, w_ref, qk_ref, vaug_ref, gt_ref, h_ref):
    j = pl.program_id(2)

    @pl.when(j == 0)
    def _():
        _store_normed(x_ref, g_ref, h_ref)

    def project():
        return jnp.dot(h_ref[...], w_ref[...], preferred_element_type=F32)

    @pl.when(j < 2)
    def _():
        qk_ref[0] = project().astype(BF16)

    @pl.when(j == 2)
    def _():
        n_heads, rows_aug, ts = vaug_ref.shape[1:]
        dv = w_ref.shape[1] // n_heads
        for h in range(0, n_heads, 2):
            ot = jnp.dot(h_ref[...], w_ref[:, h * dv:(h + 2) * dv],
                         preferred_element_type=F32).T.astype(BF16)
            for e in range(2):
                vaug_ref[0, h + e, 0:dv, :] = ot[e * dv:(e + 1) * dv, :]
                vaug_ref[0, h + e, dv:rows_aug, :] = jnp.ones(
                    (rows_aug - dv, ts), BF16)

    @pl.when(j == 3)
    def _():
        gt_ref[0] = _silu(project()).T.astype(BF16)


def _norm_matmul_diff(x, g, w, n_heads):
    B, S, D = x.shape
    Di = w.shape[1] // 4
    dv = Di // n_heads
    ts = PROJ_ROWS
    return pl.pallas_call(
        _norm_matmul_diff_kernel,
        grid=(B, S // ts, 4),
        in_specs=[pl.BlockSpec((1, ts, D), lambda b, i, j: (b, i, 0)),
                  pl.BlockSpec((1, D), lambda b, i, j: (0, 0)),
                  pl.BlockSpec((D, Di), lambda b, i, j: (0, j))],
        out_specs=[
            pl.BlockSpec((1, ts, Di), lambda b, i, j: (b, i, jnp.minimum(j, 1))),
            pl.BlockSpec((1, n_heads, dv + ONES_ROWS, ts),
                         lambda b, i, j: (b, 0, 0, i)),
            pl.BlockSpec((1, Di, ts), lambda b, i, j: (b, 0, i)),
        ],
        out_shape=[jax.ShapeDtypeStruct((B, S, 2 * Di), BF16),
                   jax.ShapeDtypeStruct((B, n_heads, dv + ONES_ROWS, S), BF16),
                   jax.ShapeDtypeStruct((B, Di, S), BF16)],
        scratch_shapes=[pltpu.VMEM((ts, D), BF16)],
        compiler_params=_compiler_params(("parallel", "parallel", "arbitrary")),
        name="norm_matmul_diff",
    )(x, g.reshape(1, D), w)


def _out_proj_kernel(y_ref, w_ref, g_ref, x_ref, o_ref, *, transposed_in):
    y = y_ref[0]
    if transposed_in:
        o = lax.dot_general(y, w_ref[...], (((0,), (0,)), ((), ())),
                            preferred_element_type=F32)
    else:
        o = jnp.dot(y, w_ref[...], preferred_element_type=F32)
    ms = jnp.mean(o * o, axis=-1, keepdims=True)
    o_ref[0] = x_ref[0] + o * lax.rsqrt(ms + RMS_EPS) * g_ref[...]


def _out_proj(y, w, g, x, *, transposed_in):
    B, S, D = x.shape
    Di = w.shape[0]
    ts = OUT_ROWS
    if transposed_in:
        y_spec = pl.BlockSpec((1, Di, ts), lambda b, i: (b, 0, i))
    else:
        y_spec = pl.BlockSpec((1, ts, Di), lambda b, i: (b, i, 0))
    return pl.pallas_call(
        functools.partial(_out_proj_kernel, transposed_in=transposed_in),
        grid=(B, S // ts),
        in_specs=[y_spec,
                  pl.BlockSpec((Di, D), lambda b, i: (0, 0)),
                  pl.BlockSpec((1, D), lambda b, i: (0, 0)),
                  pl.BlockSpec((1, ts, D), lambda b, i: (b, i, 0))],
        out_specs=pl.BlockSpec((1, ts, D), lambda b, i: (b, i, 0)),
        out_shape=jax.ShapeDtypeStruct((B, S, D), F32),
        compiler_params=_compiler_params(("parallel", "parallel")),
        name="out_proj_t" if transposed_in else "out_proj",
    )(y, w, g.reshape(1, D), x)


def _retention_kernel(q_ref, k_ref, v_ref, g_ref, dmat_ref, xi_ref, zeta_ref,
                      cdec_ref, o_ref, state_ref, state_bf_ref):
    C = RET_CHUNK

    @pl.when(pl.program_id(2) == 0)
    def _():
        state_ref[...] = jnp.zeros_like(state_ref)
        state_bf_ref[...] = jnp.zeros_like(state_bf_ref)

    dmat = dmat_ref[0]
    xi = xi_ref[0]
    zeta = zeta_ref[0]
    cdec = cdec_ref[0]

    for ci in range(RET_CHUNKS_PER_STEP):
        rows = pl.ds(ci * C, C)
        q, k, v = q_ref[0, rows, :], k_ref[0, rows, :], v_ref[0, rows, :]
        scores = lax.dot_general(q, k, (((1,), (1,)), ((), ())),
                                 preferred_element_type=F32) * dmat
        cross = jnp.dot(q, state_bf_ref[...], preferred_element_type=F32) * xi
        kz = (k.astype(F32) * zeta).astype(BF16)
        state = state_ref[...] * cdec + lax.dot_general(
            kz, v, (((0,), (0,)), ((), ())), preferred_element_type=F32)
        state_ref[...] = state
        state_bf_ref[...] = state.astype(BF16)
        inner = jnp.dot(scores.astype(BF16), v, preferred_element_type=F32)
        for r0 in range(0, C, RET_NORM_ROWS):
            blk = slice(r0, r0 + RET_NORM_ROWS)
            ret = inner[blk] + cross[blk]
            mu = jnp.mean(ret, axis=-1, keepdims=True)
            cen = ret - mu
            var = jnp.mean(cen * cen, axis=-1, keepdims=True)
            y = cen * lax.rsqrt(var + GN_EPS)
            out_rows = pl.ds(ci * C + r0, RET_NORM_ROWS)
            o_ref[0, out_rows, :] = (
                g_ref[0, out_rows, :].astype(F32) * y).astype(o_ref.dtype)


def _retention(proj, d_model, d_inner):
    B, S, _ = proj.shape
    H, C = RET_HEADS, RET_CHUNK
    dk, dv = d_model // H, d_inner // H
    rows = C * RET_CHUNKS_PER_STEP
    k_off = d_model // dk
    v_off = 2 * d_model // dv
    g_off = (2 * d_model + d_inner) // dv

    log_gamma = jnp.log1p(-jnp.exp2(-5.0 - jnp.arange(H, dtype=F32)))
    idx = jnp.arange(C, dtype=F32)
    rel = idx[:, None] - idx[None, :]
    dmat = jnp.exp(log_gamma[:, None, None] * jnp.maximum(rel, 0.0)) * (rel >= 0)
    xi = jnp.exp(log_gamma[:, None] * (idx + 1.0))
    zeta = jnp.exp(log_gamma[:, None] * (C - 1.0 - idx))
    cdec = jnp.exp(log_gamma * C)
    xi = jnp.broadcast_to(xi[:, :, None], (H, C, dv))
    zeta = jnp.broadcast_to(zeta[:, :, None], (H, C, dk))
    cdec = jnp.broadcast_to(cdec[:, None, None], (H, 1, dv))

    return pl.pallas_call(
        _retention_kernel,
        grid=(B, H, S // rows),
        in_specs=[
            pl.BlockSpec((1, rows, dk), lambda b, h, n: (b, n, h)),
            pl.BlockSpec((1, rows, dk), lambda b, h, n: (b, n, k_off + h)),
            pl.BlockSpec((1, rows, dv), lambda b, h, n: (b, n, v_off + h)),
            pl.BlockSpec((1, rows, dv), lambda b, h, n: (b, n, g_off + h)),
            pl.BlockSpec((1, C, C), lambda b, h, n: (h, 0, 0)),
            pl.BlockSpec((1, C, dv), lambda b, h, n: (h, 0, 0)),
            pl.BlockSpec((1, C, dk), lambda b, h, n: (h, 0, 0)),
            pl.BlockSpec((1, 1, dv), lambda b, h, n: (h, 0, 0)),
        ],
        out_specs=pl.BlockSpec((1, rows, dv), lambda b, h, n: (b, n, h)),
        out_shape=jax.ShapeDtypeStruct((B, S, d_inner), BF16),
        scratch_shapes=[pltpu.VMEM((dk, dv), F32), pltpu.VMEM((dk, dv), BF16)],
        compiler_params=_compiler_params(("parallel", "parallel", "arbitrary")),
        name="retention",
    )(proj, proj, proj, proj, dmat, xi, zeta, cdec)


def _diff_attn_kernel(tab_ref, bfar_ref, q_ref, k_ref, vaug_ref, gt_ref, bkd_ref,
                      lq1_ref, lk1_ref, lq2_ref, lk2_ref, sg_ref, o_ref,
                      qbd_ref, bias_ref, m_ref, acc_ref, sa_ref, sb_ref,
                      *, d, lambda_init):
    T, G, H = ATT_TILE, ATT_HEADS_PER_STEP, DIFF_HEADS
    dv = 2 * d
    hp = pl.program_id(1)
    i = pl.program_id(2)

    @pl.when(i == 0)
    def _():
        rows = lax.broadcasted_iota(jnp.int32, (T, T), 0)
        cols = lax.broadcasted_iota(jnp.int32, (T, T), 1)
        bucket_of_distance = bkd_ref[...]
        for u in range(G):
            h = hp * G + u
            cfar = tab_ref[bfar_ref[0] * H + h]
            by_distance = jnp.zeros((1, 2 * T), F32)
            for bucket in range(REL_BUCKETS):
                by_distance = jnp.where(bucket_of_distance == bucket,
                                        tab_ref[bucket * H + h], by_distance)
            by_distance = (by_distance - cfar) * LOG2E
            toeplitz = pltpu.roll(jnp.broadcast_to(by_distance, (T, 2 * T)),
                                  0, 1, stride=1, stride_axis=0)
            bias_ref[u, 0] = jnp.where(rows <= cols, toeplitz[:, 0:T], -jnp.inf)
            bias_ref[u, 1] = toeplitz[:, T:2 * T]

    def logits_into(dst_ref, w, j, bias_index):
        u = w % G
        start = pl.multiple_of(j * T, T)
        kt = k_ref[0, pl.ds(start, T), u * dv:(u + 1) * dv]
        s = jnp.dot(kt, qbd_ref[w], preferred_element_type=F32)
        if bias_index is None:
            dst_ref[w] = s
        else:
            bias = bias_ref[u, bias_index]
            dst_ref[w, :, 0:T] = s[:, 0:T] + bias
            dst_ref[w, :, T:2 * T] = s[:, T:2 * T] + bias

    def consume(src_ref, w, j):
        u = w % G
        start = pl.multiple_of(j * T, T)
        s = src_ref[w]
        m_old = m_ref[w]
        m_new = jnp.maximum(m_old, jnp.max(s, axis=0, keepdims=True))
        p = jnp.exp2(s - m_new).astype(BF16)
        alpha = jnp.exp2(m_old - m_new)
        pv = jnp.dot(vaug_ref[0, u, :, pl.ds(start, T)], p,
                     preferred_element_type=F32)
        acc_ref[w] = acc_ref[w] * alpha + pv
        m_ref[w] = m_new

    first, second, both = range(G), range(G, 2 * G), range(2 * G)

    def step(dst_ref, j_next, bias_of, src_ref, j_cur, cur_units=both):
        for w in both:
            logits_into(dst_ref, w, j_next, bias_of(w))
            if w in cur_units:
                consume(src_ref, w, j_cur)

    no_bias = lambda w: None

    def start():
        for w in both:
            sub, u = divmod(w, G)
            q = q_ref[0, sub * T:(sub + 1) * T, u * dv:(u + 1) * dv].astype(F32)
            qt = (q * (d ** -0.5 * LOG2E)).T
            row = lax.broadcasted_iota(jnp.int32, qt.shape, 0)
            qbd_ref[w, :, 0:T] = jnp.where(row < d, qt, 0.0).astype(BF16)
            qbd_ref[w, :, T:2 * T] = jnp.where(row >= d, qt, 0.0).astype(BF16)
        m_ref[...] = jnp.full_like(m_ref, -jnp.inf)
        acc_ref[...] = jnp.zeros_like(acc_ref)
        for w in second:
            logits_into(sa_ref, w, 2 * i + 1, 0)
        step(sb_ref, 2 * i, lambda w: 0 if w in first else 1,
             sa_ref, 2 * i + 1, cur_units=second)

    def finish(src_ref, j_cur):
        lam = (jnp.exp(jnp.sum(lq1_ref[...] * lk1_ref[...], keepdims=True))
               - jnp.exp(jnp.sum(lq2_ref[...] * lk2_ref[...], keepdims=True))
               + lambda_init)
        out_scale = sg_ref[...] * (1.0 - lambda_init)
        for w in both:
            sub, u = divmod(w, G)
            consume(src_ref, w, j_cur)
            acc = acc_ref[w]
            inv_l = 1.0 / acc[dv:dv + 1, :]
            o = (acc[:dv, :T] * inv_l[:, :T]
                 - acc[:dv, T:] * (lam * inv_l[:, T:]))
            ms = jnp.mean(o * o, axis=0, keepdims=True)
            on = (o * lax.rsqrt(ms + GN_EPS)) * out_scale
            head_rows = slice(u * dv, (u + 1) * dv)
            sub_cols = slice(sub * T, (sub + 1) * T)
            gate = gt_ref[0, head_rows, sub_cols].astype(F32)
            o_ref[0, head_rows, sub_cols] = (gate * on).astype(o_ref.dtype)

    @pl.when(i == 0)
    def _():
        start()
        finish(sb_ref, 2 * i)

    @pl.when(i >= 1)
    def _():
        def prologue():
            start()
            step(sa_ref, 2 * i - 1, lambda w: 1 if w in first else None,
                 sb_ref, 2 * i)
            step(sb_ref, 0, no_bias, sa_ref, 2 * i - 1)

        def pair(r):
            step(sa_ref, 2 * r - 1, no_bias, sb_ref, 2 * r - 2)
            step(sb_ref, 2 * r, no_bias, sa_ref, 2 * r - 1)

        n_pairs = i - 1
        n_trips = n_pairs // 4

        def trip(t):
            for e in range(1, 5):
                pair(4 * t + e)

        @pl.when(n_trips == 0)
        def _():
            prologue()

        @pl.when(n_trips >= 1)
        def _():
            prologue()
            trip(0)

        def trip_body(t, carry):
            trip(t)
            return carry

        lax.fori_loop(1, n_trips, trip_body, 0)

        for left in range(4):
            @pl.when(n_pairs % 4 == left)
            def _(left=left):
                for e in range(1, left + 1):
                    pair(4 * n_trips + e)
                finish(sb_ref, 2 * i - 2)


def _t5_bucket(n):
    max_exact = REL_BUCKETS // 2
    nf = jnp.maximum(n, max_exact).astype(F32)
    large = max_exact + (jnp.log(nf / max_exact) / math.log(REL_MAX_DIST / max_exact)
                         * (REL_BUCKETS - max_exact)).astype(jnp.int32)
    large = jnp.minimum(large, REL_BUCKETS - 1)
    return jnp.where(n < max_exact, n, large)


def _diff_attn(qk, vaug, gt, lq1, lk1, lq2, lk2, subln_g, rel_bias, lambda_init):
    B, S, _ = qk.shape
    H, T, G = DIFF_HEADS, ATT_TILE, ATT_HEADS_PER_STEP
    d_inner = gt.shape[1]
    dv = d_inner // H
    d = dv // 2
    gw = G * dv
    n_groups = H // G

    bkd = _t5_bucket(jnp.arange(2 * T, dtype=jnp.int32)[None, :])
    bfar = _t5_bucket(jnp.full((1,), T + 1, jnp.int32))

    smem = pl.BlockSpec(memory_space=pltpu.SMEM)
    vec = pl.BlockSpec((1, d), lambda b, h, i: (0, 0))
    return pl.pallas_call(
        functools.partial(_diff_attn_kernel, d=d, lambda_init=lambda_init),
        grid=(B, n_groups, S // (2 * T)),
        in_specs=[
            smem, smem,
            pl.BlockSpec((1, 2 * T, gw), lambda b, h, i: (b, i, h)),
            pl.BlockSpec((1, S, gw), lambda b, h, i: (b, 0, n_groups + h)),
            pl.BlockSpec((1, G, dv + ONES_ROWS, S), lambda b, h, i: (b, h, 0, 0),
                         pipeline_mode=pl.Buffered(1)),
            pl.BlockSpec((1, gw, 2 * T), lambda b, h, i: (b, h, i)),
            pl.BlockSpec((1, 2 * T), lambda b, h, i: (0, 0)),
            vec, vec, vec, vec,
            pl.BlockSpec((dv, 1), lambda b, h, i: (0, 0)),
        ],
        out_specs=pl.BlockSpec((1, gw, 2 * T), lambda b, h, i: (b, h, i)),
        out_shape=jax.ShapeDtypeStruct((B, d_inner, S), BF16),
        scratch_shapes=[pltpu.VMEM((2 * G, dv, 2 * T), BF16),
                        pltpu.VMEM((G, 2, T, T), F32),
                        pltpu.VMEM((2 * G, 1, 2 * T), F32),
                        pltpu.VMEM((2 * G, dv + ONES_ROWS, 2 * T), F32),
                        pltpu.VMEM((2 * G, T, 2 * T), F32),
                        pltpu.VMEM((2 * G, T, 2 * T), F32)],
        compiler_params=_compiler_params(("parallel", "arbitrary", "arbitrary")),
        name="diff_attn",
    )(rel_bias.astype(F32).reshape(-1), bfar, qk, qk, vaug, gt, bkd,
      lq1.reshape(1, d), lk1.reshape(1, d), lq2.reshape(1, d), lk2.reshape(1, d),
      subln_g.reshape(dv, 1))


def kernel(x, pre_norm_g, post_norm_g, ret_w_in, ret_w_out, diff_w_in, diff_w_out,
           diff_lambda_q1, diff_lambda_k1, diff_lambda_q2, diff_lambda_k2,
           diff_subln_g, rel_bias):
    d_model = x.shape[-1]
    d_inner = ret_w_out.shape[1]
    for i in range(DEPTH):
        j = i // N_MIXERS
        if i % N_MIXERS == 0:
            proj = _norm_matmul_ret(x, pre_norm_g[i], ret_w_in[j].astype(BF16),
                                    d_inner, RET_HEADS)
            y = _retention(proj, d_model, d_inner)
            x = _out_proj(y, ret_w_out[j].astype(BF16), post_norm_g[i], x,
                          transposed_in=False)
        else:
            lambda_init = 0.8 - 0.6 * math.exp(-0.3 * i)
            qk, vaug, gt = _norm_matmul_diff(
                x, pre_norm_g[i], diff_w_in[j].astype(BF16), DIFF_HEADS)
            yt = _diff_attn(qk, vaug, gt, diff_lambda_q1[j], diff_lambda_k1[j],
                            diff_lambda_q2[j], diff_lambda_k2[j], diff_subln_g[j],
                            rel_bias, lambda_init)
            x = _out_proj(yt, diff_w_out[j].astype(BF16), post_norm_g[i], x,
                          transposed_in=True)
    return x
```

```python
import functools
import math

import jax
import jax.numpy as jnp
import numpy as np
from jax import lax
from jax.experimental import pallas as pl
from jax.experimental.pallas import tpu as pltpu

F32 = jnp.float32
BF16 = jnp.bfloat16

DEPTH = 2
N_MIXERS = 2
RET_HEADS = 4
RET_CHUNK = 128
DIFF_HEADS = 16
REL_BUCKETS = 32
REL_MAX_DIST = 128
ROPE_BASE = 10000.0
RMS_EPS = 1e-6
GN_EPS = 1e-5

V7X_VMEM_BYTES = 64 * 1024 * 1024
VMEM_LIMIT_BYTES = V7X_VMEM_BYTES * 7 // 8

PROJ_ROWS = 1024
PROJ_COLS = 2048
OUT_ROWS = 1024
RET_CHUNKS_PER_STEP = 16
RET_NORM_ROWS = 32
ATT_TILE = 256
ATT_HEADS_PER_STEP = 4
ONES_ROWS = 16
LOG2E = math.log2(math.e)


def _compiler_params(semantics):
    return pltpu.CompilerParams(dimension_semantics=semantics,
                                vmem_limit_bytes=VMEM_LIMIT_BYTES)


def _store_normed(x_ref, g_ref, h_ref):
    x = x_ref[0]
    ms = jnp.mean(x * x, axis=-1, keepdims=True)
    h_ref[...] = (x * lax.rsqrt(ms + RMS_EPS) * g_ref[...]).astype(BF16)


def _silu(x):
    return x * jax.nn.sigmoid(x)


def _norm_matmul_ret_kernel(x_ref, g_ref, w_ref, cos_ref, sin_ref, o_ref, h_ref,
                            *, dk):
    j = pl.program_id(2)

    def project(cols=slice(None)):
        return jnp.dot(h_ref[...], w_ref[:, cols], preferred_element_type=F32)

    @pl.when(j == 0)
    def _():
        _store_normed(x_ref, g_ref, h_ref)
        half = dk // 2
        n_blocks = w_ref.shape[1] // dk
        c, s = cos_ref[...], sin_ref[...]
        for blk in range(n_blocks):
            o = project(slice(blk * dk, (blk + 1) * dk))
            scale = 1.0 if blk < n_blocks // 2 else dk ** -0.5
            x1, x2 = o[:, :half], o[:, half:]
            o_ref[0, :, blk * dk:blk * dk + half] = (
                (x1 * c - x2 * s) * scale).astype(BF16)
            o_ref[0, :, blk * dk + half:(blk + 1) * dk] = (
                (x1 * s + x2 * c) * scale).astype(BF16)

    @pl.when(j == 1)
    def _():
        o_ref[0] = project().astype(BF16)

    @pl.when(j == 2)
    def _():
        o_ref[0] = _silu(project()).astype(BF16)


def _norm_matmul_ret(x, g, w, d_inner, n_heads):
    B, S, D = x.shape
    N = w.shape[1]
    ts, tn = PROJ_ROWS, d_inner
    dk = (N - 2 * d_inner) // 2 // n_heads
    half = dk // 2
    assert N == 3 * tn, "q | k must fill exactly one column tile"

    pos = np.arange(S, dtype=np.float64)
    inv = ROPE_BASE ** (-np.arange(half, dtype=np.float64) / half)
    ang = pos[:, None] * inv[None, :]
    cos = jnp.asarray(np.cos(ang), dtype=F32)
    sin = jnp.asarray(np.sin(ang), dtype=F32)

    return pl.pallas_call(
        functools.partial(_norm_matmul_ret_kernel, dk=dk),
        grid=(B, S // ts, N // tn),
        in_specs=[pl.BlockSpec((1, ts, D), lambda b, i, j: (b, i, 0)),
                  pl.BlockSpec((1, D), lambda b, i, j: (0, 0)),
                  pl.BlockSpec((D, tn), lambda b, i, j: (0, j)),
                  pl.BlockSpec((ts, half), lambda b, i, j: (i, 0)),
                  pl.BlockSpec((ts, half), lambda b, i, j: (i, 0))],
        out_specs=pl.BlockSpec((1, ts, tn), lambda b, i, j: (b, i, j)),
        out_shape=jax.ShapeDtypeStruct((B, S, N), BF16),
        scratch_shapes=[pltpu.VMEM((ts, D), BF16)],
        compiler_params=_compiler_params(("parallel", "parallel", "arbitrary")),
        name="norm_matmul_ret",
    )(x, g.reshape(1, D), w, cos, sin)


def _norm_matmul_diff_kernel(x_ref, g_ref, w_ref, qk_ref, vaug_ref, gt_ref, h_ref):
    j = pl.program_id(2)

    @pl.when(j == 0)
    def _():
        _store_normed(x_ref, g_ref, h_ref)

    def project():
        return jnp.dot(h_ref[...], w_ref[...], preferred_element_type=F32)

    @pl.when(j < 2)
    def _():
        qk_ref[0] = project().astype(BF16)

    @pl.when(j == 2)
    def _():
        n_heads, rows_aug, ts = vaug_ref.shape[1:]
        dv = w_ref.shape[1] // n_heads
        for h in range(0, n_heads, 2):
            ot = jnp.dot(h_ref[...], w_ref[:, h * dv:(h + 2) * dv],
                         preferred_element_type=F32).T.astype(BF16)
            for e in range(2):
                vaug_ref[0, h + e, 0:dv, :] = ot[e * dv:(e + 1) * dv, :]
                vaug_ref[0, h + e, dv:rows_aug, :] = jnp.ones(
                    (rows_aug - dv, ts), BF16)

    @pl.when(j == 3)
    def _():
        gt_ref[0] = _silu(project()).T.astype(BF16)


def _norm_matmul_diff(x, g, w, n_heads):
    B, S, D = x.shape
    Di = w.shape[1] // 4
    dv = Di // n_heads
    ts = PROJ_ROWS
    return pl.pallas_call(
        _norm_matmul_diff_kernel,
        grid=(B, S // ts, 4),
        in_specs=[pl.BlockSpec((1, ts, D), lambda b, i, j: (b, i, 0)),
                  pl.BlockSpec((1, D), lambda b, i, j: (0, 0)),
                  pl.BlockSpec((D, Di), lambda b, i, j: (0, j))],
        out_specs=[
            pl.BlockSpec((1, ts, Di), lambda b, i, j: (b, i, jnp.minimum(j, 1))),
            pl.BlockSpec((1, n_heads, dv + ONES_ROWS, ts),
                         lambda b, i, j: (b, 0, 0, i)),
            pl.BlockSpec((1, Di, ts), lambda b, i, j: (b, 0, i)),
        ],
        out_shape=[jax.ShapeDtypeStruct((B, S, 2 * Di), BF16),
                   jax.ShapeDtypeStruct((B, n_heads, dv + ONES_ROWS, S), BF16),
                   jax.ShapeDtypeStruct((B, Di, S), BF16)],
        scratch_shapes=[pltpu.VMEM((ts, D), BF16)],
        compiler_params=_compiler_params(("parallel", "parallel", "arbitrary")),
        name="norm_matmul_diff",
    )(x, g.reshape(1, D), w)


def _out_proj_kernel(y_ref, w_ref, g_ref, x_ref, o_ref, *, transposed_in):
    y = y_ref[0]
    if transposed_in:
        o = lax.dot_general(y, w_ref[...], (((0,), (0,)), ((), ())),
                            preferred_element_type=F32)
    else:
        o = jnp.dot(y, w_ref[...], preferred_element_type=F32)
    ms = jnp.mean(o * o, axis=-1, keepdims=True)
    o_ref[0] = x_ref[0] + o * lax.rsqrt(ms + RMS_EPS) * g_ref[...]


def _out_proj(y, w, g, x, *, transposed_in):
    B, S, D = x.shape
    Di = w.shape[0]
    ts = OUT_ROWS
    if transposed_in:
        y_spec = pl.BlockSpec((1, Di, ts), lambda b, i: (b, 0, i))
    else:
        y_spec = pl.BlockSpec((1, ts, Di), lambda b, i: (b, i, 0))
    return pl.pallas_call(
        functools.partial(_out_proj_kernel, transposed_in=transposed_in),
        grid=(B, S // ts),
        in_specs=[y_spec,
                  pl.BlockSpec((Di, D), lambda b, i: (0, 0)),
                  pl.BlockSpec((1, D), lambda b, i: (0, 0)),
                  pl.BlockSpec((1, ts, D), lambda b, i: (b, i, 0))],
        out_specs=pl.BlockSpec((1, ts, D), lambda b, i: (b, i, 0)),
        out_shape=jax.ShapeDtypeStruct((B, S, D), F32),
        compiler_params=_compiler_params(("parallel", "parallel")),
        name="out_proj_t" if transposed_in else "out_proj",
    )(y, w, g.reshape(1, D), x)


def _retention_kernel(q_ref, k_ref, v_ref, g_ref, dmat_ref, xi_ref, zeta_ref,
                      cdec_ref, o_ref, state_ref, state_bf_ref):
    C = RET_CHUNK

    @pl.when(pl.program_id(2) == 0)
    def _():
        state_ref[...] = jnp.zeros_like(state_ref)
        state_bf_ref[...] = jnp.zeros_like(state_bf_ref)

    dmat = dmat_ref[0]
    xi = xi_ref[0]
    zeta = zeta_ref[0]
    cdec = cdec_ref[0]

    for ci in range(RET_CHUNKS_PER_STEP):
        rows = pl.ds(ci * C, C)
        q, k, v = q_ref[0, rows, :], k_ref[0, rows, :], v_ref[0, rows, :]
        scores = lax.dot_general(q, k, (((1,), (1,)), ((), ())),
                                 preferred_element_type=F32) * dmat
        cross = jnp.dot(q, state_bf_ref[...], preferred_element_type=F32) * xi
        kz = (k.astype(F32) * zeta).astype(BF16)
        state = state_ref[...] * cdec + lax.dot_general(
            kz, v, (((0,), (0,)), ((), ())), preferred_element_type=F32)
        state_ref[...] = state
        state_bf_ref[...] = state.astype(BF16)
        inner = jnp.dot(scores.astype(BF16), v, preferred_element_type=F32)
        for r0 in range(0, C, RET_NORM_ROWS):
            blk = slice(r0, r0 + RET_NORM_ROWS)
            ret = inner[blk] + cross[blk]
            mu = jnp.mean(ret, axis=-1, keepdims=True)
            cen = ret - mu
            var = jnp.mean(cen * cen, axis=-1, keepdims=True)
            y = cen * lax.rsqrt(var + GN_EPS)
            out_rows = pl.ds(ci * C + r0, RET_NORM_ROWS)
            o_ref[0, out_rows, :] = (
                g_ref[0, out_rows, :].astype(F32) * y).astype(o_ref.dtype)


def _retention(proj, d_model, d_inner):
    B, S, _ = proj.shape
    H, C = RET_HEADS, RET_CHUNK
    dk, dv = d_model // H, d_inner // H
    rows = C * RET_CHUNKS_PER_STEP
    k_off = d_model // dk
    v_off = 2 * d_model // dv
    g_off = (2 * d_model + d_inner) // dv

    log_gamma = jnp.log1p(-jnp.exp2(-5.0 - jnp.arange(H, dtype=F32)))
    idx = jnp.arange(C, dtype=F32)
    rel = idx[:, None] - idx[None, :]
    dmat = jnp.exp(log_gamma[:, None, None] * jnp.maximum(rel, 0.0)) * (rel >= 0)
    xi = jnp.exp(log_gamma[:, None] * (idx + 1.0))
    zeta = jnp.exp(log_gamma[:, None] * (C - 1.0 - idx))
    cdec = jnp.exp(log_gamma * C)
    xi = jnp.broadcast_to(xi[:, :, None], (H, C, dv))
    zeta = jnp.broadcast_to(zeta[:, :, None], (H, C, dk))
    cdec = jnp.broadcast_to(cdec[:, None, None], (H, 1, dv))

    return pl.pallas_call(
        _retention_kernel,
        grid=(B, H, S // rows),
        in_specs=[
            pl.BlockSpec((1, rows, dk), lambda b, h, n: (b, n, h)),
            pl.BlockSpec((1, rows, dk), lambda b, h, n: (b, n, k_off + h)),
            pl.BlockSpec((1, rows, dv), lambda b, h, n: (b, n, v_off + h)),
            pl.BlockSpec((1, rows, dv), lambda b, h, n: (b, n, g_off + h)),
            pl.BlockSpec((1, C, C), lambda b, h, n: (h, 0, 0)),
            pl.BlockSpec((1, C, dv), lambda b, h, n: (h, 0, 0)),
            pl.BlockSpec((1, C, dk), lambda b, h, n: (h, 0, 0)),
            pl.BlockSpec((1, 1, dv), lambda b, h, n: (h, 0, 0)),
        ],
        out_specs=pl.BlockSpec((1, rows, dv), lambda b, h, n: (b, n, h)),
        out_shape=jax.ShapeDtypeStruct((B, S, d_inner), BF16),
        scratch_shapes=[pltpu.VMEM((dk, dv), F32), pltpu.VMEM((dk, dv), BF16)],
        compiler_params=_compiler_params(("parallel", "parallel", "arbitrary")),
        name="retention",
    )(proj, proj, proj, proj, dmat, xi, zeta, cdec)


def _diff_attn_kernel(tab_ref, bfar_ref, q_ref, k_ref, vaug_ref, gt_ref, bkd_ref,
                      lq1_ref, lk1_ref, lq2_ref, lk2_ref, sg_ref, o_ref,
                      qbd_ref, bias_ref, m_ref, acc_ref, sa_ref, sb_ref,
                      *, d, lambda_init):
    T, G, H = ATT_TILE, ATT_HEADS_PER_STEP, DIFF_HEADS
    dv = 2 * d
    hp = pl.program_id(1)
    i = pl.program_id(2)

    @pl.when(i == 0)
    def _():
        rows = lax.broadcasted_iota(jnp.int32, (T, T), 0)
        cols = lax.broadcasted_iota(jnp.int32, (T, T), 1)
        bucket_of_distance = bkd_ref[...]
        for u in range(G):
            h = hp * G + u
            cfar = tab_ref[bfar_ref[0] * H + h]
            by_distance = jnp.full((1, 2 * T), tab_ref[h], F32)
            for bucket in range(1, REL_BUCKETS):
                by_distance = jnp.where(bucket_of_distance >= bucket,
                                        tab_ref[bucket * H + h], by_distance)
            by_distance = (by_distance - cfar) * LOG2E
            toeplitz = pltpu.roll(jnp.broadcast_to(by_distance, (T, 2 * T)),
                                  0, 1, stride=1, stride_axis=0)
            bias_ref[u, 0] = jnp.where(rows <= cols, toeplitz[:, 0:T], -jnp.inf)
            bias_ref[u, 1] = toeplitz[:, T:2 * T]

    def logits_into(dst_ref, w, j, bias_index):
        u = w % G
        start = pl.multiple_of(j * T, T)
        kt = k_ref[0, pl.ds(start, T), u * dv:(u + 1) * dv]
        s = jnp.dot(kt, qbd_ref[w], preferred_element_type=F32)
        if bias_index is None:
            dst_ref[w] = s
        else:
            bias = bias_ref[u, bias_index]
            dst_ref[w, :, 0:T] = s[:, 0:T] + bias
            dst_ref[w, :, T:2 * T] = s[:, T:2 * T] + bias

    def consume(src_ref, w, j):
        u = w % G
        start = pl.multiple_of(j * T, T)
        s = src_ref[w]
        m_old = m_ref[w]
        m_new = jnp.maximum(m_old, jnp.max(s, axis=0, keepdims=True))
        p = jnp.exp2(s - m_new).astype(BF16)
        alpha = jnp.exp2(m_old - m_new)
        pv = jnp.dot(vaug_ref[0, u, :, pl.ds(start, T)], p,
                     preferred_element_type=F32)
        acc_ref[w] = acc_ref[w] * alpha + pv
        m_ref[w] = m_new

    first, second, both = range(G), range(G, 2 * G), range(2 * G)

    def step(dst_ref, j_next, bias_of, src_ref, j_cur, cur_units=both):
        for w in both:
            logits_into(dst_ref, w, j_next, bias_of(w))
            if w in cur_units:
                consume(src_ref, w, j_cur)

    no_bias = lambda w: None

    def start():
        for w in both:
            sub, u = divmod(w, G)
            q = q_ref[0, sub * T:(sub + 1) * T, u * dv:(u + 1) * dv].astype(F32)
            qt = (q * (d ** -0.5 * LOG2E)).T
            row = lax.broadcasted_iota(jnp.int32, qt.shape, 0)
            qbd_ref[w, :, 0:T] = jnp.where(row < d, qt, 0.0).astype(BF16)
            qbd_ref[w, :, T:2 * T] = jnp.where(row >= d, qt, 0.0).astype(BF16)
        m_ref[...] = jnp.full_like(m_ref, -jnp.inf)
        acc_ref[...] = jnp.zeros_like(acc_ref)
        for w in second:
            logits_into(sa_ref, w, 2 * i + 1, 0)
        step(sb_ref, 2 * i, lambda w: 0 if w in first else 1,
             sa_ref, 2 * i + 1, cur_units=second)

    def finish(src_ref, j_cur):
        lam = (jnp.exp(jnp.sum(lq1_ref[...] * lk1_ref[...], keepdims=True))
               - jnp.exp(jnp.sum(lq2_ref[...] * lk2_ref[...], keepdims=True))
               + lambda_init)
        out_scale = sg_ref[...] * (1.0 - lambda_init)
        for w in both:
            sub, u = divmod(w, G)
            consume(src_ref, w, j_cur)
            acc = acc_ref[w]
            inv_l = 1.0 / acc[dv:dv + 1, :]
            o = (acc[:dv, :T] * inv_l[:, :T]
                 - acc[:dv, T:] * (lam * inv_l[:, T:]))
            ms = jnp.mean(o * o, axis=0, keepdims=True)
            on = (o * lax.rsqrt(ms + GN_EPS)) * out_scale
            head_rows = slice(u * dv, (u + 1) * dv)
            sub_cols = slice(sub * T, (sub + 1) * T)
            gate = gt_ref[0, head_rows, sub_cols].astype(F32)
            o_ref[0, head_rows, sub_cols] = (gate * on).astype(o_ref.dtype)

    @pl.when(i == 0)
    def _():
        start()
        finish(sb_ref, 2 * i)

    @pl.when(i >= 1)
    def _():
        def prologue():
            start()
            step(sa_ref, 2 * i - 1, lambda w: 1 if w in first else None,
                 sb_ref, 2 * i)
            step(sb_ref, 0, no_bias, sa_ref, 2 * i - 1)

        def pair(r):
            step(sa_ref, 2 * r - 1, no_bias, sb_ref, 2 * r - 2)
            step(sb_ref, 2 * r, no_bias, sa_ref, 2 * r - 1)

        n_pairs = i - 1
        n_trips = n_pairs // 4

        def trip(t):
            for e in range(1, 5):
                pair(4 * t + e)

        @pl.when(n_trips == 0)
        def _():
            prologue()

        @pl.when(n_trips >= 1)
        def _():
            prologue()
            trip(0)

        def trip_body(t, carry):
            trip(t)
            return carry

        lax.fori_loop(1, n_trips, trip_body, 0)

        @pl.when(n_pairs % 4 >= 2)
        def _():
            first_left = (n_pairs // 4) * 4 + 1
            pair(first_left)
            pair(first_left + 1)

        @pl.when((i - 1) % 2 == 1)
        def _():
            pair(i - 1)
            finish(sb_ref, 2 * i - 2)

        @pl.when((i - 1) % 2 == 0)
        def _():
            finish(sb_ref, 2 * i - 2)


def _t5_bucket(n):
    max_exact = REL_BUCKETS // 2
    nf = jnp.maximum(n, max_exact).astype(F32)
    large = max_exact + (jnp.log(nf / max_exact) / math.log(REL_MAX_DIST / max_exact)
                         * (REL_BUCKETS - max_exact)).astype(jnp.int32)
    large = jnp.minimum(large, REL_BUCKETS - 1)
    return jnp.where(n < max_exact, n, large)


def _diff_attn(qk, vaug, gt, lq1, lk1, lq2, lk2, subln_g, rel_bias, lambda_init):
    B, S, _ = qk.shape
    H, T, G = DIFF_HEADS, ATT_TILE, ATT_HEADS_PER_STEP
    d_inner = gt.shape[1]
    dv = d_inner // H
    d = dv // 2
    gw = G * dv
    n_groups = H // G

    bkd = _t5_bucket(jnp.arange(2 * T, dtype=jnp.int32)[None, :])
    bfar = _t5_bucket(jnp.full((1,), T + 1, jnp.int32))

    smem = pl.BlockSpec(memory_space=pltpu.SMEM)
    vec = pl.BlockSpec((1, d), lambda b, h, i: (0, 0))
    return pl.pallas_call(
        functools.partial(_diff_attn_kernel, d=d, lambda_init=lambda_init),
        grid=(B, n_groups, S // (2 * T)),
        in_specs=[
            smem, smem,
            pl.BlockSpec((1, 2 * T, gw), lambda b, h, i: (b, i, h)),
            pl.BlockSpec((1, S, gw), lambda b, h, i: (b, 0, n_groups + h)),
            pl.BlockSpec((1, G, dv + ONES_ROWS, S), lambda b, h, i: (b, h, 0, 0),
                         pipeline_mode=pl.Buffered(1)),
            pl.BlockSpec((1, gw, 2 * T), lambda b, h, i: (b, h, i)),
            pl.BlockSpec((1, 2 * T), lambda b, h, i: (0, 0)),
            vec, vec, vec, vec,
            pl.BlockSpec((dv, 1), lambda b, h, i: (0, 0)),
        ],
        out_specs=pl.BlockSpec((1, gw, 2 * T), lambda b, h, i: (b, h, i)),
        out_shape=jax.ShapeDtypeStruct((B, d_inner, S), BF16),
        scratch_shapes=[pltpu.VMEM((2 * G, dv, 2 * T), BF16),
                        pltpu.VMEM((G, 2, T, T), F32),
                        pltpu.VMEM((2 * G, 1, 2 * T), F32),
                        pltpu.VMEM((2 * G, dv + ONES_ROWS, 2 * T), F32),
                        pltpu.VMEM((2 * G, T, 2 * T), F32),
                        pltpu.VMEM((2 * G, T, 2 * T), F32)],
        compiler_params=_compiler_params(("parallel", "arbitrary", "arbitrary")),
        name="diff_attn",
    )(rel_bias.astype(F32).reshape(-1), bfar, qk, qk, vaug, gt, bkd,
      lq1.reshape(1, d), lk1.reshape(1, d), lq2.reshape(1, d), lk2.reshape(1, d),
      subln_g.reshape(dv, 1))


def kernel(x, pre_norm_g, post_norm_g, ret_w_in, ret_w_out, diff_w_in, diff_w_out,
           diff_lambda_q1, diff_lambda_k1, diff_lambda_q2, diff_lambda_k2,
           diff_subln_g, rel_bias):
    d_model = x.shape[-1]
    d_inner = ret_w_out.shape[1]
    for i in range(DEPTH):
        j = i // N_MIXERS
        if i % N_MIXERS == 0:
            proj = _norm_matmul_ret(x, pre_norm_g[i], ret_w_in[j].astype(BF16),
                                    d_inner, RET_HEADS)
            y = _retention(proj, d_model, d_inner)
            x = _out_proj(y, ret_w_out[j].astype(BF16), post_norm_g[i], x,
                          transposed_in=False)
        else:
            lambda_init = 0.8 - 0.6 * math.exp(-0.3 * i)
            qk, vaug, gt = _norm_matmul_diff(
                x, pre_norm_g[i], diff_w_in[j].astype(BF16), DIFF_HEADS)
            yt = _diff_attn(qk, vaug, gt, diff_lambda_q1[j], diff_lambda_k1[j],
                            diff_lambda_q2[j], diff_lambda_k2[j], diff_subln_g[j],
                            rel_bias, lambda_init)
            x = _out_proj(yt, diff_w_out[j].astype(BF16), post_norm_g[i], x,
                          transposed_in=True)
    return x
```

```python
import functools
import math

import jax
import jax.numpy as jnp
import numpy as np
from jax import lax
from jax.experimental import pallas as pl
from jax.experimental.pallas import tpu as pltpu

F32 = jnp.float32
BF16 = jnp.bfloat16

DEPTH = 2
N_MIXERS = 2
RET_HEADS = 4
RET_CHUNK = 128
DIFF_HEADS = 16
REL_BUCKETS = 32
REL_MAX_DIST = 128
ROPE_BASE = 10000.0
RMS_EPS = 1e-6
GN_EPS = 1e-5

V7X_VMEM_BYTES = 64 * 1024 * 1024
VMEM_LIMIT_BYTES = V7X_VMEM_BYTES * 15 // 16

PROJ_ROWS = 1024
PROJ_COLS = 2048
OUT_ROWS = 1024
RET_CHUNKS_PER_STEP = 16
RET_NORM_ROWS = 32
ATT_TILE = 256
ATT_HEADS_PER_STEP = 4
ONES_ROWS = 16
LOG2E = math.log2(math.e)


def _compiler_params(semantics):
    return pltpu.CompilerParams(dimension_semantics=semantics,
                                vmem_limit_bytes=VMEM_LIMIT_BYTES)


def _store_normed(x_ref, g_ref, h_ref):
    x = x_ref[0]
    ms = jnp.mean(x * x, axis=-1, keepdims=True)
    h_ref[...] = (x * lax.rsqrt(ms + RMS_EPS) * g_ref[...]).astype(BF16)


def _silu(x):
    return x * jax.nn.sigmoid(x)


def _norm_matmul_ret_kernel(x_ref, g_ref, w_ref, cos_ref, sin_ref, o_ref, h_ref,
                            *, dk):
    j = pl.program_id(2)

    def project(cols=slice(None)):
        return jnp.dot(h_ref[...], w_ref[:, cols], preferred_element_type=F32)

    @pl.when(j == 0)
    def _():
        _store_normed(x_ref, g_ref, h_ref)
        half = dk // 2
        n_blocks = w_ref.shape[1] // dk
        c, s = cos_ref[...], sin_ref[...]
        for blk in range(n_blocks):
            o = project(slice(blk * dk, (blk + 1) * dk))
            scale = 1.0 if blk < n_blocks // 2 else dk ** -0.5
            x1, x2 = o[:, :half], o[:, half:]
            o_ref[0, :, blk * dk:blk * dk + half] = (
                (x1 * c - x2 * s) * scale).astype(BF16)
            o_ref[0, :, blk * dk + half:(blk + 1) * dk] = (
                (x1 * s + x2 * c) * scale).astype(BF16)

    @pl.when(j == 1)
    def _():
        o_ref[0] = project().astype(BF16)

    @pl.when(j == 2)
    def _():
        o_ref[0] = _silu(project()).astype(BF16)


def _norm_matmul_ret(x, g, w, d_inner, n_heads):
    B, S, D = x.shape
    N = w.shape[1]
    ts, tn = PROJ_ROWS, d_inner
    dk = (N - 2 * d_inner) // 2 // n_heads
    half = dk // 2
    assert N == 3 * tn, "q | k must fill exactly one column tile"

    pos = np.arange(S, dtype=np.float64)
    inv = ROPE_BASE ** (-np.arange(half, dtype=np.float64) / half)
    ang = pos[:, None] * inv[None, :]
    cos = jnp.asarray(np.cos(ang), dtype=F32)
    sin = jnp.asarray(np.sin(ang), dtype=F32)

    return pl.pallas_call(
        functools.partial(_norm_matmul_ret_kernel, dk=dk),
        grid=(B, S // ts, N // tn),
        in_specs=[pl.BlockSpec((1, ts, D), lambda b, i, j: (b, i, 0)),
                  pl.BlockSpec((1, D), lambda b, i, j: (0, 0)),
                  pl.BlockSpec((D, tn), lambda b, i, j: (0, j)),
                  pl.BlockSpec((ts, half), lambda b, i, j: (i, 0)),
                  pl.BlockSpec((ts, half), lambda b, i, j: (i, 0))],
        out_specs=pl.BlockSpec((1, ts, tn), lambda b, i, j: (b, i, j)),
        out_shape=jax.ShapeDtypeStruct((B, S, N), BF16),
        scratch_shapes=[pltpu.VMEM((ts, D), BF16)],
        compiler_params=_compiler_params(("parallel", "parallel", "arbitrary")),
        name="norm_matmul_ret",
    )(x, g.reshape(1, D), w, cos, sin)


def _norm_matmul_diff_kernel(x_ref, g_ref, w_ref, qk_ref, vaug_ref, gt_ref, h_ref):
    j = pl.program_id(2)

    @pl.when(j == 0)
    def _():
        _store_normed(x_ref, g_ref, h_ref)

    def project():
        return jnp.dot(h_ref[...], w_ref[...], preferred_element_type=F32)

    @pl.when(j < 2)
    def _():
        qk_ref[0] = project().astype(BF16)

    @pl.when(j == 2)
    def _():
        n_heads, rows_aug, ts = vaug_ref.shape[1:]
        dv = w_ref.shape[1] // n_heads
        for h in range(0, n_heads, 2):
            ot = jnp.dot(h_ref[...], w_ref[:, h * dv:(h + 2) * dv],
                         preferred_element_type=F32).T.astype(BF16)
            for e in range(2):
                vaug_ref[0, h + e, 0:dv, :] = ot[e * dv:(e + 1) * dv, :]
                vaug_ref[0, h + e, dv:rows_aug, :] = jnp.ones(
                    (rows_aug - dv, ts), BF16)

    @pl.when(j == 3)
    def _():
        gt_ref[0] = _silu(project()).T.astype(BF16)


def _norm_matmul_diff(x, g, w, n_heads):
    B, S, D = x.shape
    Di = w.shape[1] // 4
    dv = Di // n_heads
    ts = PROJ_ROWS
    return pl.pallas_call(
        _norm_matmul_diff_kernel,
        grid=(B, S // ts, 4),
        in_specs=[pl.BlockSpec((1, ts, D), lambda b, i, j: (b, i, 0)),
                  pl.BlockSpec((1, D), lambda b, i, j: (0, 0)),
                  pl.BlockSpec((D, Di), lambda b, i, j: (0, j))],
        out_specs=[
            pl.BlockSpec((1, ts, Di), lambda b, i, j: (b, i, jnp.minimum(j, 1))),
            pl.BlockSpec((1, n_heads, dv + ONES_ROWS, ts),
                         lambda b, i, j: (b, 0, 0, i)),
            pl.BlockSpec((1, Di, ts), lambda b, i, j: (b, 0, i)),
        ],
        out_shape=[jax.ShapeDtypeStruct((B, S, 2 * Di), BF16),
                   jax.ShapeDtypeStruct((B, n_heads, dv + ONES_ROWS, S), BF16),
                   jax.ShapeDtypeStruct((B, Di, S), BF16)],
        scratch_shapes=[pltpu.VMEM((ts, D), BF16)],
        compiler_params=_compiler_params(("parallel", "parallel", "arbitrary")),
        name="norm_matmul_diff",
    )(x, g.reshape(1, D), w)


def _out_proj_kernel(y_ref, w_ref, g_ref, x_ref, o_ref, *, transposed_in):
    y = y_ref[0]
    if transposed_in:
        o = lax.dot_general(y, w_ref[...], (((0,), (0,)), ((), ())),
                            preferred_element_type=F32)
    else:
        o = jnp.dot(y, w_ref[...], preferred_element_type=F32)
    ms = jnp.mean(o * o, axis=-1, keepdims=True)
    o_ref[0] = x_ref[0] + o * lax.rsqrt(ms + RMS_EPS) * g_ref[...]


def _out_proj(y, w, g, x, *, transposed_in):
    B, S, D = x.shape
    Di = w.shape[0]
    ts = OUT_ROWS
    if transposed_in:
        y_spec = pl.BlockSpec((1, Di, ts), lambda b, i: (b, 0, i))
    else:
        y_spec = pl.BlockSpec((1, ts, Di), lambda b, i: (b, i, 0))
    return pl.pallas_call(
        functools.partial(_out_proj_kernel, transposed_in=transposed_in),
        grid=(B, S // ts),
        in_specs=[y_spec,
                  pl.BlockSpec((Di, D), lambda b, i: (0, 0)),
                  pl.BlockSpec((1, D), lambda b, i: (0, 0)),
                  pl.BlockSpec((1, ts, D), lambda b, i: (b, i, 0))],
        out_specs=pl.BlockSpec((1, ts, D), lambda b, i: (b, i, 0)),
        out_shape=jax.ShapeDtypeStruct((B, S, D), F32),
        compiler_params=_compiler_params(("parallel", "parallel")),
        name="out_proj_t" if transposed_in else "out_proj",
    )(y, w, g.reshape(1, D), x)


def _retention_kernel(q_ref, k_ref, v_ref, g_ref, dmat_ref, xi_ref, zeta_ref,
                      cdec_ref, o_ref, state_ref, state_bf_ref):
    C = RET_CHUNK

    @pl.when(pl.program_id(2) == 0)
    def _():
        state_ref[...] = jnp.zeros_like(state_ref)
        state_bf_ref[...] = jnp.zeros_like(state_bf_ref)

    dmat = dmat_ref[0]
    xi = xi_ref[0]
    zeta = zeta_ref[0]
    cdec = cdec_ref[0]

    for ci in range(RET_CHUNKS_PER_STEP):
        rows = pl.ds(ci * C, C)
        q, k, v = q_ref[0, rows, :], k_ref[0, rows, :], v_ref[0, rows, :]
        scores = lax.dot_general(q, k, (((1,), (1,)), ((), ())),
                                 preferred_element_type=F32) * dmat
        cross = jnp.dot(q, state_bf_ref[...], preferred_element_type=F32) * xi
        kz = (k.astype(F32) * zeta).astype(BF16)
        state = state_ref[...] * cdec + lax.dot_general(
            kz, v, (((0,), (0,)), ((), ())), preferred_element_type=F32)
        state_ref[...] = state
        state_bf_ref[...] = state.astype(BF16)
        inner = jnp.dot(scores.astype(BF16), v, preferred_element_type=F32)
        for r0 in range(0, C, RET_NORM_ROWS):
            blk = slice(r0, r0 + RET_NORM_ROWS)
            ret = inner[blk] + cross[blk]
            mu = jnp.mean(ret, axis=-1, keepdims=True)
            cen = ret - mu
            var = jnp.mean(cen * cen, axis=-1, keepdims=True)
            y = cen * lax.rsqrt(var + GN_EPS)
            out_rows = pl.ds(ci * C + r0, RET_NORM_ROWS)
            o_ref[0, out_rows, :] = (
                g_ref[0, out_rows, :].astype(F32) * y).astype(o_ref.dtype)


def _retention(proj, d_model, d_inner):
    B, S, _ = proj.shape
    H, C = RET_HEADS, RET_CHUNK
    dk, dv = d_model // H, d_inner // H
    rows = C * RET_CHUNKS_PER_STEP
    k_off = d_model // dk
    v_off = 2 * d_model // dv
    g_off = (2 * d_model + d_inner) // dv

    log_gamma = jnp.log1p(-jnp.exp2(-5.0 - jnp.arange(H, dtype=F32)))
    idx = jnp.arange(C, dtype=F32)
    rel = idx[:, None] - idx[None, :]
    dmat = jnp.exp(log_gamma[:, None, None] * jnp.maximum(rel, 0.0)) * (rel >= 0)
    xi = jnp.exp(log_gamma[:, None] * (idx + 1.0))
    zeta = jnp.exp(log_gamma[:, None] * (C - 1.0 - idx))
    cdec = jnp.exp(log_gamma * C)
    xi = jnp.broadcast_to(xi[:, :, None], (H, C, dv))
    zeta = jnp.broadcast_to(zeta[:, :, None], (H, C, dk))
    cdec = jnp.broadcast_to(cdec[:, None, None], (H, 1, dv))

    return pl.pallas_call(
        _retention_kernel,
        grid=(B, H, S // rows),
        in_specs=[
            pl.BlockSpec((1, rows, dk), lambda b, h, n: (b, n, h)),
            pl.BlockSpec((1, rows, dk), lambda b, h, n: (b, n, k_off + h)),
            pl.BlockSpec((1, rows, dv), lambda b, h, n: (b, n, v_off + h)),
            pl.BlockSpec((1, rows, dv), lambda b, h, n: (b, n, g_off + h)),
            pl.BlockSpec((1, C, C), lambda b, h, n: (h, 0, 0)),
            pl.BlockSpec((1, C, dv), lambda b, h, n: (h, 0, 0)),
            pl.BlockSpec((1, C, dk), lambda b, h, n: (h, 0, 0)),
            pl.BlockSpec((1, 1, dv), lambda b, h, n: (h, 0, 0)),
        ],
        out_specs=pl.BlockSpec((1, rows, dv), lambda b, h, n: (b, n, h)),
        out_shape=jax.ShapeDtypeStruct((B, S, d_inner), BF16),
        scratch_shapes=[pltpu.VMEM((dk, dv), F32), pltpu.VMEM((dk, dv), BF16)],
        compiler_params=_compiler_params(("parallel", "parallel", "arbitrary")),
        name="retention",
    )(proj, proj, proj, proj, dmat, xi, zeta, cdec)


def _diff_attn_kernel(tab_ref, bfar_ref, q_ref, k_ref, vaug_ref, gt_ref, bkd_ref,
                      lq1_ref, lk1_ref, lq2_ref, lk2_ref, sg_ref, o_ref,
                      qbd_ref, bias_ref, m_ref, acc_ref, sa_ref, sb_ref,
                      *, d, lambda_init):
    T, G, H = ATT_TILE, ATT_HEADS_PER_STEP, DIFF_HEADS
    dv = 2 * d
    hp = pl.program_id(1)
    i = pl.program_id(2)

    @pl.when(i == 0)
    def _():
        rows = lax.broadcasted_iota(jnp.int32, (T, T), 0)
        cols = lax.broadcasted_iota(jnp.int32, (T, T), 1)
        bucket_of_distance = bkd_ref[...]
        for u in range(G):
            h = hp * G + u
            cfar = tab_ref[bfar_ref[0] * H + h]
            by_distance = jnp.full((1, 2 * T), tab_ref[h], F32)
            for bucket in range(1, REL_BUCKETS):
                by_distance = jnp.where(bucket_of_distance >= bucket,
                                        tab_ref[bucket * H + h], by_distance)
            by_distance = (by_distance - cfar) * LOG2E
            toeplitz = pltpu.roll(jnp.broadcast_to(by_distance, (T, 2 * T)),
                                  0, 1, stride=1, stride_axis=0)
            bias_ref[u, 0] = jnp.where(rows <= cols, toeplitz[:, 0:T], -jnp.inf)
            bias_ref[u, 1] = toeplitz[:, T:2 * T]

    def logits_into(dst_ref, w, j, bias_index):
        u = w % G
        start = pl.multiple_of(j * T, T)
        kt = k_ref[0, pl.ds(start, T), u * dv:(u + 1) * dv]
        s = jnp.dot(kt, qbd_ref[w], preferred_element_type=F32)
        if bias_index is None:
            dst_ref[w] = s
        else:
            bias = bias_ref[u, bias_index]
            dst_ref[w, :, 0:T] = s[:, 0:T] + bias
            dst_ref[w, :, T:2 * T] = s[:, T:2 * T] + bias

    def consume(src_ref, w, j):
        u = w % G
        start = pl.multiple_of(j * T, T)
        s = src_ref[w]
        m_old = m_ref[w]
        m_new = jnp.maximum(m_old, jnp.max(s, axis=0, keepdims=True))
        p = jnp.exp2(s - m_new).astype(BF16)
        alpha = jnp.exp2(m_old - m_new)
        pv = jnp.dot(vaug_ref[0, u, :, pl.ds(start, T)], p,
                     preferred_element_type=F32)
        acc_ref[w] = acc_ref[w] * alpha + pv
        m_ref[w] = m_new

    first, second, both = range(G), range(G, 2 * G), range(2 * G)

    def step(dst_ref, j_next, bias_of, src_ref, j_cur, cur_units=both):
        for w in both:
            logits_into(dst_ref, w, j_next, bias_of(w))
            if w in cur_units:
                consume(src_ref, w, j_cur)

    no_bias = lambda w: None

    def start():
        for w in both:
            sub, u = divmod(w, G)
            q = q_ref[0, sub * T:(sub + 1) * T, u * dv:(u + 1) * dv].astype(F32)
            qt = (q * (d ** -0.5 * LOG2E)).T
            row = lax.broadcasted_iota(jnp.int32, qt.shape, 0)
            qbd_ref[w, :, 0:T] = jnp.where(row < d, qt, 0.0).astype(BF16)
            qbd_ref[w, :, T:2 * T] = jnp.where(row >= d, qt, 0.0).astype(BF16)
        m_ref[...] = jnp.full_like(m_ref, -jnp.inf)
        acc_ref[...] = jnp.zeros_like(acc_ref)
        for w in second:
            logits_into(sa_ref, w, 2 * i + 1, 0)
        step(sb_ref, 2 * i, lambda w: 0 if w in first else 1,
             sa_ref, 2 * i + 1, cur_units=second)

    def finish(src_ref, j_cur):
        lam = (jnp.exp(jnp.sum(lq1_ref[...] * lk1_ref[...], keepdims=True))
               - jnp.exp(jnp.sum(lq2_ref[...] * lk2_ref[...], keepdims=True))
               + lambda_init)
        out_scale = sg_ref[...] * (1.0 - lambda_init)
        for w in both:
            sub, u = divmod(w, G)
            consume(src_ref, w, j_cur)
            acc = acc_ref[w]
            inv_l = 1.0 / acc[dv:dv + 1, :]
            o = (acc[:dv, :T] * inv_l[:, :T]
                 - acc[:dv, T:] * (lam * inv_l[:, T:]))
            ms = jnp.mean(o * o, axis=0, keepdims=True)
            on = (o * lax.rsqrt(ms + GN_EPS)) * out_scale
            head_rows = slice(u * dv, (u + 1) * dv)
            sub_cols = slice(sub * T, (sub + 1) * T)
            gate = gt_ref[0, head_rows, sub_cols].astype(F32)
            o_ref[0, head_rows, sub_cols] = (gate * on).astype(o_ref.dtype)

    @pl.when(i == 0)
    def _():
        start()
        finish(sb_ref, 2 * i)

    @pl.when(i >= 1)
    def _():
        def prologue():
            start()
            step(sa_ref, 2 * i - 1, lambda w: 1 if w in first else None,
                 sb_ref, 2 * i)
            step(sb_ref, 0, no_bias, sa_ref, 2 * i - 1)

        def pair(r):
            step(sa_ref, 2 * r - 1, no_bias, sb_ref, 2 * r - 2)
            step(sb_ref, 2 * r, no_bias, sa_ref, 2 * r - 1)

        n_pairs = i - 1
        n_trips = n_pairs // 4

        def trip(t):
            for e in range(1, 5):
                pair(4 * t + e)

        @pl.when(n_trips == 0)
        def _():
            prologue()

        @pl.when(n_trips >= 1)
        def _():
            prologue()
            trip(0)

        def trip_body(t, carry):
            trip(t)
            return carry

        lax.fori_loop(1, n_trips, trip_body, 0)

        @pl.when(n_pairs % 4 >= 2)
        def _():
            first_left = (n_pairs // 4) * 4 + 1
            pair(first_left)
            pair(first_left + 1)

        @pl.when((i - 1) % 2 == 1)
        def _():
            pair(i - 1)
            finish(sb_ref, 2 * i - 2)

        @pl.when((i - 1) % 2 == 0)
        def _():
            finish(sb_ref, 2 * i - 2)


def _t5_bucket(n):
    max_exact = REL_BUCKETS // 2
    nf = jnp.maximum(n, max_exact).astype(F32)
    large = max_exact + (jnp.log(nf / max_exact) / math.log(REL_MAX_DIST / max_exact)
                         * (REL_BUCKETS - max_exact)).astype(jnp.int32)
    large = jnp.minimum(large, REL_BUCKETS - 1)
    return jnp.where(n < max_exact, n, large)


def _diff_attn(qk, vaug, gt, lq1, lk1, lq2, lk2, subln_g, rel_bias, lambda_init):
    B, S, _ = qk.shape
    H, T, G = DIFF_HEADS, ATT_TILE, ATT_HEADS_PER_STEP
    d_inner = gt.shape[1]
    dv = d_inner // H
    d = dv // 2
    gw = G * dv
    n_groups = H // G

    bkd = _t5_bucket(jnp.arange(2 * T, dtype=jnp.int32)[None, :])
    bfar = _t5_bucket(jnp.full((1,), T + 1, jnp.int32))

    smem = pl.BlockSpec(memory_space=pltpu.SMEM)
    vec = pl.BlockSpec((1, d), lambda b, h, i: (0, 0))
    return pl.pallas_call(
        functools.partial(_diff_attn_kernel, d=d, lambda_init=lambda_init),
        grid=(B, n_groups, S // (2 * T)),
        in_specs=[
            smem, smem,
            pl.BlockSpec((1, 2 * T, gw), lambda b, h, i: (b, i, h)),
            pl.BlockSpec((1, S, gw), lambda b, h, i: (b, 0, n_groups + h)),
            pl.BlockSpec((1, G, dv + ONES_ROWS, S), lambda b, h, i: (b, h, 0, 0)),
            pl.BlockSpec((1, gw, 2 * T), lambda b, h, i: (b, h, i)),
            pl.BlockSpec((1, 2 * T), lambda b, h, i: (0, 0)),
            vec, vec, vec, vec,
            pl.BlockSpec((dv, 1), lambda b, h, i: (0, 0)),
        ],
        out_specs=pl.BlockSpec((1, gw, 2 * T), lambda b, h, i: (b, h, i)),
        out_shape=jax.ShapeDtypeStruct((B, d_inner, S), BF16),
        scratch_shapes=[pltpu.VMEM((2 * G, dv, 2 * T), BF16),
                        pltpu.VMEM((G, 2, T, T), F32),
                        pltpu.VMEM((2 * G, 1, 2 * T), F32),
                        pltpu.VMEM((2 * G, dv + ONES_ROWS, 2 * T), F32),
                        pltpu.VMEM((2 * G, T, 2 * T), F32),
                        pltpu.VMEM((2 * G, T, 2 * T), F32)],
        compiler_params=_compiler_params(("parallel", "arbitrary", "arbitrary")),
        name="diff_attn",
    )(rel_bias.astype(F32).reshape(-1), bfar, qk, qk, vaug, gt, bkd,
      lq1.reshape(1, d), lk1.reshape(1, d), lq2.reshape(1, d), lk2.reshape(1, d),
      subln_g.reshape(dv, 1))


def kernel(x, pre_norm_g, post_norm_g, ret_w_in, ret_w_out, diff_w_in, diff_w_out,
           diff_lambda_q1, diff_lambda_k1, diff_lambda_q2, diff_lambda_k2,
           diff_subln_g, rel_bias):
    d_model = x.shape[-1]
    d_inner = ret_w_out.shape[1]
    for i in range(DEPTH):
        j = i // N_MIXERS
        if i % N_MIXERS == 0:
            proj = _norm_matmul_ret(x, pre_norm_g[i], ret_w_in[j].astype(BF16),
                                    d_inner, RET_HEADS)
            y = _retention(proj, d_model, d_inner)
            x = _out_proj(y, ret_w_out[j].astype(BF16), post_norm_g[i], x,
                          transposed_in=False)
        else:
            lambda_init = 0.8 - 0.6 * math.exp(-0.3 * i)
            qk, vaug, gt = _norm_matmul_diff(
                x, pre_norm_g[i], diff_w_in[j].astype(BF16), DIFF_HEADS)
            yt = _diff_attn(qk, vaug, gt, diff_lambda_q1[j], diff_lambda_k1[j],
                            diff_lambda_q2[j], diff_lambda_k2[j], diff_subln_g[j],
                            rel_bias, lambda_init)
            x = _out_proj(yt, diff_w_out[j].astype(BF16), post_norm_g[i], x,
                          transposed_in=True)
    return x
```

```python
import functools
import math

import jax
import jax.numpy as jnp
import numpy as np
from jax import lax
from jax.experimental import pallas as pl
from jax.experimental.pallas import tpu as pltpu

F32 = jnp.float32
BF16 = jnp.bfloat16

DEPTH = 2
N_MIXERS = 2
RET_HEADS = 4
RET_CHUNK = 128
DIFF_HEADS = 16
REL_BUCKETS = 32
REL_MAX_DIST = 128
ROPE_BASE = 10000.0
RMS_EPS = 1e-6
GN_EPS = 1e-5

V7X_VMEM_BYTES = 64 * 1024 * 1024
VMEM_LIMIT_BYTES = V7X_VMEM_BYTES * 15 // 16

PROJ_ROWS = 1024
PROJ_COLS = 2048
OUT_ROWS = 1024
OUT_CHUNK_ROWS = 256
RET_CHUNKS_PER_STEP = 16
RET_NORM_ROWS = 32
ATT_TILE = 256
ATT_HEADS_PER_STEP = 4
ONES_ROWS = 16
LOG2E = math.log2(math.e)


def _compiler_params(semantics):
    return pltpu.CompilerParams(dimension_semantics=semantics,
                                vmem_limit_bytes=VMEM_LIMIT_BYTES)


def _store_normed(x_ref, g_ref, h_ref):
    x = x_ref[0]
    ms = jnp.mean(x * x, axis=-1, keepdims=True)
    h_ref[...] = (x * lax.rsqrt(ms + RMS_EPS) * g_ref[...]).astype(BF16)


def _silu(x):
    return x * jax.nn.sigmoid(x)


def _norm_matmul_ret_kernel(x_ref, g_ref, w_ref, cos_ref, sin_ref, o_ref, h_ref,
                            *, dk):
    j = pl.program_id(2)

    def project(cols=slice(None)):
        return jnp.dot(h_ref[...], w_ref[:, cols], preferred_element_type=F32)

    @pl.when(j == 0)
    def _():
        _store_normed(x_ref, g_ref, h_ref)
        half = dk // 2
        n_blocks = w_ref.shape[1] // dk
        c, s = cos_ref[...], sin_ref[...]
        for blk in range(n_blocks):
            o = project(slice(blk * dk, (blk + 1) * dk))
            scale = 1.0 if blk < n_blocks // 2 else dk ** -0.5
            x1, x2 = o[:, :half], o[:, half:]
            o_ref[0, :, blk * dk:blk * dk + half] = (
                (x1 * c - x2 * s) * scale).astype(BF16)
            o_ref[0, :, blk * dk + half:(blk + 1) * dk] = (
                (x1 * s + x2 * c) * scale).astype(BF16)

    @pl.when(j == 1)
    def _():
        o_ref[0] = project().astype(BF16)

    @pl.when(j == 2)
    def _():
        o_ref[0] = _silu(project()).astype(BF16)


def _norm_matmul_ret(x, g, w, d_inner, n_heads):
    B, S, D = x.shape
    N = w.shape[1]
    ts, tn = PROJ_ROWS, d_inner
    dk = (N - 2 * d_inner) // 2 // n_heads
    half = dk // 2
    assert N == 3 * tn, "q | k must fill exactly one column tile"

    pos = np.arange(S, dtype=np.float64)
    inv = ROPE_BASE ** (-np.arange(half, dtype=np.float64) / half)
    ang = pos[:, None] * inv[None, :]
    cos = jnp.asarray(np.cos(ang), dtype=F32)
    sin = jnp.asarray(np.sin(ang), dtype=F32)

    return pl.pallas_call(
        functools.partial(_norm_matmul_ret_kernel, dk=dk),
        grid=(B, S // ts, N // tn),
        in_specs=[pl.BlockSpec((1, ts, D), lambda b, i, j: (b, i, 0)),
                  pl.BlockSpec((1, D), lambda b, i, j: (0, 0)),
                  pl.BlockSpec((D, tn), lambda b, i, j: (0, j)),
                  pl.BlockSpec((ts, half), lambda b, i, j: (i, 0)),
                  pl.BlockSpec((ts, half), lambda b, i, j: (i, 0))],
        out_specs=pl.BlockSpec((1, ts, tn), lambda b, i, j: (b, i, j)),
        out_shape=jax.ShapeDtypeStruct((B, S, N), BF16),
        scratch_shapes=[pltpu.VMEM((ts, D), BF16)],
        compiler_params=_compiler_params(("parallel", "parallel", "arbitrary")),
        name="norm_matmul_ret",
    )(x, g.reshape(1, D), w, cos, sin)


def _norm_matmul_diff_kernel(x_ref, g_ref, w_ref, qk_ref, vaug_ref, gt_ref, h_ref):
    j = pl.program_id(2)

    @pl.when(j == 0)
    def _():
        _store_normed(x_ref, g_ref, h_ref)

    def project():
        return jnp.dot(h_ref[...], w_ref[...], preferred_element_type=F32)

    @pl.when(j < 2)
    def _():
        qk_ref[0] = project().astype(BF16)

    @pl.when(j == 2)
    def _():
        n_heads, rows_aug, ts = vaug_ref.shape[1:]
        dv = w_ref.shape[1] // n_heads
        for h in range(0, n_heads, 2):
            ot = jnp.dot(h_ref[...], w_ref[:, h * dv:(h + 2) * dv],
                         preferred_element_type=F32).T.astype(BF16)
            for e in range(2):
                vaug_ref[0, h + e, 0:dv, :] = ot[e * dv:(e + 1) * dv, :]
                vaug_ref[0, h + e, dv:rows_aug, :] = jnp.ones(
                    (rows_aug - dv, ts), BF16)

    @pl.when(j == 3)
    def _():
        gt_ref[0] = _silu(project()).T.astype(BF16)


def _norm_matmul_diff(x, g, w, n_heads):
    B, S, D = x.shape
    Di = w.shape[1] // 4
    dv = Di // n_heads
    ts = PROJ_ROWS
    return pl.pallas_call(
        _norm_matmul_diff_kernel,
        grid=(B, S // ts, 4),
        in_specs=[pl.BlockSpec((1, ts, D), lambda b, i, j: (b, i, 0)),
                  pl.BlockSpec((1, D), lambda b, i, j: (0, 0)),
                  pl.BlockSpec((D, Di), lambda b, i, j: (0, j))],
        out_specs=[
            pl.BlockSpec((1, ts, Di), lambda b, i, j: (b, i, jnp.minimum(j, 1))),
            pl.BlockSpec((1, n_heads, dv + ONES_ROWS, ts),
                         lambda b, i, j: (b, 0, 0, i)),
            pl.BlockSpec((1, Di, ts), lambda b, i, j: (b, 0, i)),
        ],
        out_shape=[jax.ShapeDtypeStruct((B, S, 2 * Di), BF16),
                   jax.ShapeDtypeStruct((B, n_heads, dv + ONES_ROWS, S), BF16),
                   jax.ShapeDtypeStruct((B, Di, S), BF16)],
        scratch_shapes=[pltpu.VMEM((ts, D), BF16)],
        compiler_params=_compiler_params(("parallel", "parallel", "arbitrary")),
        name="norm_matmul_diff",
    )(x, g.reshape(1, D), w)


def _out_proj_kernel(y_ref, w_ref, g_ref, x_ref, o_ref, *, transposed_in):
    rows = x_ref.shape[1]
    for r0 in range(0, rows, OUT_CHUNK_ROWS):
        blk = slice(r0, r0 + OUT_CHUNK_ROWS)
        if transposed_in:
            o = lax.dot_general(y_ref[0, :, blk], w_ref[...],
                                (((0,), (0,)), ((), ())),
                                preferred_element_type=F32)
        else:
            o = jnp.dot(y_ref[0, blk, :], w_ref[...], preferred_element_type=F32)
        ms = jnp.mean(o * o, axis=-1, keepdims=True)
        o_ref[0, blk, :] = x_ref[0, blk, :] + o * lax.rsqrt(ms + RMS_EPS) * g_ref[...]


def _out_proj(y, w, g, x, *, transposed_in):
    B, S, D = x.shape
    Di = w.shape[0]
    ts = OUT_ROWS
    if transposed_in:
        y_spec = pl.BlockSpec((1, Di, ts), lambda b, i: (b, 0, i))
    else:
        y_spec = pl.BlockSpec((1, ts, Di), lambda b, i: (b, i, 0))
    return pl.pallas_call(
        functools.partial(_out_proj_kernel, transposed_in=transposed_in),
        grid=(B, S // ts),
        in_specs=[y_spec,
                  pl.BlockSpec((Di, D), lambda b, i: (0, 0)),
                  pl.BlockSpec((1, D), lambda b, i: (0, 0)),
                  pl.BlockSpec((1, ts, D), lambda b, i: (b, i, 0))],
        out_specs=pl.BlockSpec((1, ts, D), lambda b, i: (b, i, 0)),
        out_shape=jax.ShapeDtypeStruct((B, S, D), F32),
        compiler_params=_compiler_params(("parallel", "parallel")),
        name="out_proj_t" if transposed_in else "out_proj",
    )(y, w, g.reshape(1, D), x)


def _retention_kernel(q_ref, k_ref, v_ref, g_ref, dmat_ref, xi_ref, zeta_ref,
                      cdec_ref, o_ref, state_ref, state_bf_ref):
    C = RET_CHUNK

    @pl.when(pl.program_id(2) == 0)
    def _():
        state_ref[...] = jnp.zeros_like(state_ref)
        state_bf_ref[...] = jnp.zeros_like(state_bf_ref)

    dmat = dmat_ref[0]
    xi = xi_ref[0]
    zeta = zeta_ref[0]
    cdec = cdec_ref[0]

    for ci in range(RET_CHUNKS_PER_STEP):
        rows = pl.ds(ci * C, C)
        q, k, v = q_ref[0, rows, :], k_ref[0, rows, :], v_ref[0, rows, :]
        scores = lax.dot_general(q, k, (((1,), (1,)), ((), ())),
                                 preferred_element_type=F32) * dmat
        cross = jnp.dot(q, state_bf_ref[...], preferred_element_type=F32) * xi
        kz = (k.astype(F32) * zeta).astype(BF16)
        state = state_ref[...] * cdec + lax.dot_general(
            kz, v, (((0,), (0,)), ((), ())), preferred_element_type=F32)
        state_ref[...] = state
        state_bf_ref[...] = state.astype(BF16)
        inner = jnp.dot(scores.astype(BF16), v, preferred_element_type=F32)
        for r0 in range(0, C, RET_NORM_ROWS):
            blk = slice(r0, r0 + RET_NORM_ROWS)
            ret = inner[blk] + cross[blk]
            mu = jnp.mean(ret, axis=-1, keepdims=True)
            cen = ret - mu
            var = jnp.mean(cen * cen, axis=-1, keepdims=True)
            y = cen * lax.rsqrt(var + GN_EPS)
            out_rows = pl.ds(ci * C + r0, RET_NORM_ROWS)
            o_ref[0, out_rows, :] = (
                g_ref[0, out_rows, :].astype(F32) * y).astype(o_ref.dtype)


def _retention(proj, d_model, d_inner):
    B, S, _ = proj.shape
    H, C = RET_HEADS, RET_CHUNK
    dk, dv = d_model // H, d_inner // H
    rows = C * RET_CHUNKS_PER_STEP
    k_off = d_model // dk
    v_off = 2 * d_model // dv
    g_off = (2 * d_model + d_inner) // dv

    log_gamma = jnp.log1p(-jnp.exp2(-5.0 - jnp.arange(H, dtype=F32)))
    idx = jnp.arange(C, dtype=F32)
    rel = idx[:, None] - idx[None, :]
    dmat = jnp.exp(log_gamma[:, None, None] * jnp.maximum(rel, 0.0)) * (rel >= 0)
    xi = jnp.exp(log_gamma[:, None] * (idx + 1.0))
    zeta = jnp.exp(log_gamma[:, None] * (C - 1.0 - idx))
    cdec = jnp.exp(log_gamma * C)
    xi = jnp.broadcast_to(xi[:, :, None], (H, C, dv))
    zeta = jnp.broadcast_to(zeta[:, :, None], (H, C, dk))
    cdec = jnp.broadcast_to(cdec[:, None, None], (H, 1, dv))

    return pl.pallas_call(
        _retention_kernel,
        grid=(B, H, S // rows),
        in_specs=[
            pl.BlockSpec((1, rows, dk), lambda b, h, n: (b, n, h)),
            pl.BlockSpec((1, rows, dk), lambda b, h, n: (b, n, k_off + h)),
            pl.BlockSpec((1, rows, dv), lambda b, h, n: (b, n, v_off + h)),
            pl.BlockSpec((1, rows, dv), lambda b, h, n: (b, n, g_off + h)),
            pl.BlockSpec((1, C, C), lambda b, h, n: (h, 0, 0)),
            pl.BlockSpec((1, C, dv), lambda b, h, n: (h, 0, 0)),
            pl.BlockSpec((1, C, dk), lambda b, h, n: (h, 0, 0)),
            pl.BlockSpec((1, 1, dv), lambda b, h, n: (h, 0, 0)),
        ],
        out_specs=pl.BlockSpec((1, rows, dv), lambda b, h, n: (b, n, h)),
        out_shape=jax.ShapeDtypeStruct((B, S, d_inner), BF16),
        scratch_shapes=[pltpu.VMEM((dk, dv), F32), pltpu.VMEM((dk, dv), BF16)],
        compiler_params=_compiler_params(("parallel", "parallel", "arbitrary")),
        name="retention",
    )(proj, proj, proj, proj, dmat, xi, zeta, cdec)


def _diff_attn_kernel(tab_ref, bfar_ref, q_ref, k_ref, vaug_ref, gt_ref, bkd_ref,
                      lq1_ref, lk1_ref, lq2_ref, lk2_ref, sg_ref, o_ref,
                      qbd_ref, bias_ref, m_ref, acc_ref, sa_ref, sb_ref,
                      *, d, lambda_init):
    T, G, H = ATT_TILE, ATT_HEADS_PER_STEP, DIFF_HEADS
    dv = 2 * d
    hp = pl.program_id(1)
    i = pl.program_id(2)

    @pl.when(i == 0)
    def _():
        rows = lax.broadcasted_iota(jnp.int32, (T, T), 0)
        cols = lax.broadcasted_iota(jnp.int32, (T, T), 1)
        bucket_of_distance = bkd_ref[...]
        for u in range(G):
            h = hp * G + u
            cfar = tab_ref[bfar_ref[0] * H + h]
            by_distance = jnp.full((1, 2 * T), tab_ref[h], F32)
            for bucket in range(1, REL_BUCKETS):
                by_distance = jnp.where(bucket_of_distance >= bucket,
                                        tab_ref[bucket * H + h], by_distance)
            by_distance = (by_distance - cfar) * LOG2E
            toeplitz = pltpu.roll(jnp.broadcast_to(by_distance, (T, 2 * T)),
                                  0, 1, stride=1, stride_axis=0)
            bias_ref[u, 0] = jnp.where(rows <= cols, toeplitz[:, 0:T], -jnp.inf)
            bias_ref[u, 1] = toeplitz[:, T:2 * T]

    def logits_into(dst_ref, w, j, bias_index):
        u = w % G
        start = pl.multiple_of(j * T, T)
        kt = k_ref[0, pl.ds(start, T), u * dv:(u + 1) * dv]
        s = jnp.dot(kt, qbd_ref[w], preferred_element_type=F32)
        if bias_index is None:
            dst_ref[w] = s
        else:
            bias = bias_ref[u, bias_index]
            dst_ref[w, :, 0:T] = s[:, 0:T] + bias
            dst_ref[w, :, T:2 * T] = s[:, T:2 * T] + bias

    def consume(src_ref, w, j):
        u = w % G
        start = pl.multiple_of(j * T, T)
        s = src_ref[w]
        m_old = m_ref[w]
        m_new = jnp.maximum(m_old, jnp.max(s, axis=0, keepdims=True))
        p = jnp.exp2(s - m_new).astype(BF16)
        alpha = jnp.exp2(m_old - m_new)
        pv = jnp.dot(vaug_ref[0, u, :, pl.ds(start, T)], p,
                     preferred_element_type=F32)
        acc_ref[w] = acc_ref[w] * alpha + pv
        m_ref[w] = m_new

    first, second, both = range(G), range(G, 2 * G), range(2 * G)

    def step(dst_ref, j_next, bias_of, src_ref, j_cur, cur_units=both):
        for w in both:
            logits_into(dst_ref, w, j_next, bias_of(w))
            if w in cur_units:
                consume(src_ref, w, j_cur)

    no_bias = lambda w: None

    def start():
        for w in both:
            sub, u = divmod(w, G)
            q = q_ref[0, sub * T:(sub + 1) * T, u * dv:(u + 1) * dv].astype(F32)
            qt = (q * (d ** -0.5 * LOG2E)).T
            row = lax.broadcasted_iota(jnp.int32, qt.shape, 0)
            qbd_ref[w, :, 0:T] = jnp.where(row < d, qt, 0.0).astype(BF16)
            qbd_ref[w, :, T:2 * T] = jnp.where(row >= d, qt, 0.0).astype(BF16)
        m_ref[...] = jnp.full_like(m_ref, -jnp.inf)
        acc_ref[...] = jnp.zeros_like(acc_ref)
        for w in second:
            logits_into(sa_ref, w, 2 * i + 1, 0)
        step(sb_ref, 2 * i, lambda w: 0 if w in first else 1,
             sa_ref, 2 * i + 1, cur_units=second)

    def finish(src_ref, j_cur):
        lam = (jnp.exp(jnp.sum(lq1_ref[...] * lk1_ref[...], keepdims=True))
               - jnp.exp(jnp.sum(lq2_ref[...] * lk2_ref[...], keepdims=True))
               + lambda_init)
        out_scale = sg_ref[...] * (1.0 - lambda_init)
        for w in both:
            sub, u = divmod(w, G)
            consume(src_ref, w, j_cur)
            acc = acc_ref[w]
            inv_l = 1.0 / acc[dv:dv + 1, :]
            o = (acc[:dv, :T] * inv_l[:, :T]
                 - acc[:dv, T:] * (lam * inv_l[:, T:]))
            ms = jnp.mean(o * o, axis=0, keepdims=True)
            on = (o * lax.rsqrt(ms + GN_EPS)) * out_scale
            head_rows = slice(u * dv, (u + 1) * dv)
            sub_cols = slice(sub * T, (sub + 1) * T)
            gate = gt_ref[0, head_rows, sub_cols].astype(F32)
            o_ref[0, head_rows, sub_cols] = (gate * on).astype(o_ref.dtype)

    @pl.when(i == 0)
    def _():
        start()
        finish(sb_ref, 2 * i)

    @pl.when(i >= 1)
    def _():
        def prologue():
            start()
            step(sa_ref, 2 * i - 1, lambda w: 1 if w in first else None,
                 sb_ref, 2 * i)
            step(sb_ref, 0, no_bias, sa_ref, 2 * i - 1)

        def pair(r):
            step(sa_ref, 2 * r - 1, no_bias, sb_ref, 2 * r - 2)
            step(sb_ref, 2 * r, no_bias, sa_ref, 2 * r - 1)

        n_pairs = i - 1
        n_trips = n_pairs // 4

        def trip(t):
            for e in range(1, 5):
                pair(4 * t + e)

        @pl.when(n_trips == 0)
        def _():
            prologue()

        @pl.when(n_trips >= 1)
        def _():
            prologue()
            trip(0)

        def trip_body(t, carry):
            trip(t)
            return carry

        lax.fori_loop(1, n_trips, trip_body, 0)

        @pl.when(n_pairs % 4 >= 2)
        def _():
            first_left = (n_pairs // 4) * 4 + 1
            pair(first_left)
            pair(first_left + 1)

        @pl.when((i - 1) % 2 == 1)
        def _():
            pair(i - 1)
            finish(sb_ref, 2 * i - 2)

        @pl.when((i - 1) % 2 == 0)
        def _():
            finish(sb_ref, 2 * i - 2)


def _t5_bucket(n):
    max_exact = REL_BUCKETS // 2
    nf = jnp.maximum(n, max_exact).astype(F32)
    large = max_exact + (jnp.log(nf / max_exact) / math.log(REL_MAX_DIST / max_exact)
                         * (REL_BUCKETS - max_exact)).astype(jnp.int32)
    large = jnp.minimum(large, REL_BUCKETS - 1)
    return jnp.where(n < max_exact, n, large)


def _diff_attn(qk, vaug, gt, lq1, lk1, lq2, lk2, subln_g, rel_bias, lambda_init):
    B, S, _ = qk.shape
    H, T, G = DIFF_HEADS, ATT_TILE, ATT_HEADS_PER_STEP
    d_inner = gt.shape[1]
    dv = d_inner // H
    d = dv // 2
    gw = G * dv
    n_groups = H // G

    bkd = _t5_bucket(jnp.arange(2 * T, dtype=jnp.int32)[None, :])
    bfar = _t5_bucket(jnp.full((1,), T + 1, jnp.int32))

    smem = pl.BlockSpec(memory_space=pltpu.SMEM)
    vec = pl.BlockSpec((1, d), lambda b, h, i: (0, 0))
    return pl.pallas_call(
        functools.partial(_diff_attn_kernel, d=d, lambda_init=lambda_init),
        grid=(B, n_groups, S // (2 * T)),
        in_specs=[
            smem, smem,
            pl.BlockSpec((1, 2 * T, gw), lambda b, h, i: (b, i, h)),
            pl.BlockSpec((1, S, gw), lambda b, h, i: (b, 0, n_groups + h)),
            pl.BlockSpec((1, G, dv + ONES_ROWS, S), lambda b, h, i: (b, h, 0, 0)),
            pl.BlockSpec((1, gw, 2 * T), lambda b, h, i: (b, h, i)),
            pl.BlockSpec((1, 2 * T), lambda b, h, i: (0, 0)),
            vec, vec, vec, vec,
            pl.BlockSpec((dv, 1), lambda b, h, i: (0, 0)),
        ],
        out_specs=pl.BlockSpec((1, gw, 2 * T), lambda b, h, i: (b, h, i)),
        out_shape=jax.ShapeDtypeStruct((B, d_inner, S), BF16),
        scratch_shapes=[pltpu.VMEM((2 * G, dv, 2 * T), BF16),
                        pltpu.VMEM((G, 2, T, T), F32),
                        pltpu.VMEM((2 * G, 1, 2 * T), F32),
                        pltpu.VMEM((2 * G, dv + ONES_ROWS, 2 * T), F32),
                        pltpu.VMEM((2 * G, T, 2 * T), F32),
                        pltpu.VMEM((2 * G, T, 2 * T), F32)],
        compiler_params=_compiler_params(("parallel", "arbitrary", "arbitrary")),
        name="diff_attn",
    )(rel_bias.astype(F32).reshape(-1), bfar, qk, qk, vaug, gt, bkd,
      lq1.reshape(1, d), lk1.reshape(1, d), lq2.reshape(1, d), lk2.reshape(1, d),
      subln_g.reshape(dv, 1))


def kernel(x, pre_norm_g, post_norm_g, ret_w_in, ret_w_out, diff_w_in, diff_w_out,
           diff_lambda_q1, diff_lambda_k1, diff_lambda_q2, diff_lambda_k2,
           diff_subln_g, rel_bias):
    d_model = x.shape[-1]
    d_inner = ret_w_out.shape[1]
    for i in range(DEPTH):
        j = i // N_MIXERS
        if i % N_MIXERS == 0:
            proj = _norm_matmul_ret(x, pre_norm_g[i], ret_w_in[j].astype(BF16),
                                    d_inner, RET_HEADS)
            y = _retention(proj, d_model, d_inner)
            x = _out_proj(y, ret_w_out[j].astype(BF16), post_norm_g[i], x,
                          transposed_in=False)
        else:
            lambda_init = 0.8 - 0.6 * math.exp(-0.3 * i)
            qk, vaug, gt = _norm_matmul_diff(
                x, pre_norm_g[i], diff_w_in[j].astype(BF16), DIFF_HEADS)
            yt = _diff_attn(qk, vaug, gt, diff_lambda_q1[j], diff_lambda_k1[j],
                            diff_lambda_q2[j], diff_lambda_k2[j], diff_subln_g[j],
                            rel_bias, lambda_init)
            x = _out_proj(yt, diff_w_out[j].astype(BF16), post_norm_g[i], x,
                          transposed_in=True)
    return x
```

```python
import functools
import math

import jax
import jax.numpy as jnp
import numpy as np
from jax import lax
from jax.experimental import pallas as pl
from jax.experimental.pallas import tpu as pltpu

F32 = jnp.float32
BF16 = jnp.bfloat16

DEPTH = 2
N_MIXERS = 2
RET_HEADS = 4
DIFF_HEADS = 16
REL_BUCKETS = 32
REL_MAX_DIST = 128
ROPE_BASE = 10000.0
RMS_EPS = 1e-6
GN_EPS = 1e-5

V7X_VMEM_BYTES = 64 * 1024 * 1024
VMEM_LIMIT_BYTES = V7X_VMEM_BYTES * 15 // 16

PROJ_ROWS = 1024
PROJ_COLS = 2048
OUT_ROWS = 1024
RET_CHUNK = 256
RET_CHUNKS_PER_STEP = 8
RET_NORM_ROWS = 32
ATT_TILE = 256
ATT_HEADS_PER_STEP = 4
ONES_ROWS = 16
LOG2E = math.log2(math.e)


def _compiler_params(semantics):
    return pltpu.CompilerParams(dimension_semantics=semantics,
                                vmem_limit_bytes=VMEM_LIMIT_BYTES)


def _store_normed(x_ref, g_ref, h_ref):
    x = x_ref[0]
    ms = jnp.mean(x * x, axis=-1, keepdims=True)
    h_ref[...] = (x * lax.rsqrt(ms + RMS_EPS) * g_ref[...]).astype(BF16)


def _silu(x):
    return x * jax.nn.sigmoid(x)


def _norm_matmul_ret_kernel(x_ref, g_ref, w_ref, cos_ref, sin_ref, o_ref, h_ref,
                            *, dk):
    j = pl.program_id(2)

    def project(cols=slice(None)):
        return jnp.dot(h_ref[...], w_ref[:, cols], preferred_element_type=F32)

    @pl.when(j == 0)
    def _():
        _store_normed(x_ref, g_ref, h_ref)
        half = dk // 2
        n_blocks = w_ref.shape[1] // dk
        c, s = cos_ref[...], sin_ref[...]
        for blk in range(n_blocks):
            o = project(slice(blk * dk, (blk + 1) * dk))
            scale = 1.0 if blk < n_blocks // 2 else dk ** -0.5
            x1, x2 = o[:, :half], o[:, half:]
            o_ref[0, :, blk * dk:blk * dk + half] = (
                (x1 * c - x2 * s) * scale).astype(BF16)
            o_ref[0, :, blk * dk + half:(blk + 1) * dk] = (
                (x1 * s + x2 * c) * scale).astype(BF16)

    @pl.when(j == 1)
    def _():
        o_ref[0] = project().astype(BF16)

    @pl.when(j == 2)
    def _():
        o_ref[0] = _silu(project()).astype(BF16)


def _norm_matmul_ret(x, g, w, d_inner, n_heads):
    B, S, D = x.shape
    N = w.shape[1]
    ts, tn = PROJ_ROWS, d_inner
    dk = (N - 2 * d_inner) // 2 // n_heads
    half = dk // 2
    assert N == 3 * tn, "q | k must fill exactly one column tile"

    pos = np.arange(S, dtype=np.float64)
    inv = ROPE_BASE ** (-np.arange(half, dtype=np.float64) / half)
    ang = pos[:, None] * inv[None, :]
    cos = jnp.asarray(np.cos(ang), dtype=F32)
    sin = jnp.asarray(np.sin(ang), dtype=F32)

    return pl.pallas_call(
        functools.partial(_norm_matmul_ret_kernel, dk=dk),
        grid=(B, S // ts, N // tn),
        in_specs=[pl.BlockSpec((1, ts, D), lambda b, i, j: (b, i, 0)),
                  pl.BlockSpec((1, D), lambda b, i, j: (0, 0)),
                  pl.BlockSpec((D, tn), lambda b, i, j: (0, j)),
                  pl.BlockSpec((ts, half), lambda b, i, j: (i, 0)),
                  pl.BlockSpec((ts, half), lambda b, i, j: (i, 0))],
        out_specs=pl.BlockSpec((1, ts, tn), lambda b, i, j: (b, i, j)),
        out_shape=jax.ShapeDtypeStruct((B, S, N), BF16),
        scratch_shapes=[pltpu.VMEM((ts, D), BF16)],
        compiler_params=_compiler_params(("parallel", "parallel", "arbitrary")),
        name="norm_matmul_ret",
    )(x, g.reshape(1, D), w, cos, sin)


def _norm_matmul_diff_kernel(x_ref, g_ref, w_ref, qk_ref, vaug_ref, gt_ref, h_ref):
    j = pl.program_id(2)

    @pl.when(j == 0)
    def _():
        _store_normed(x_ref, g_ref, h_ref)

    def project():
        return jnp.dot(h_ref[...], w_ref[...], preferred_element_type=F32)

    @pl.when(j < 2)
    def _():
        qk_ref[0] = project().astype(BF16)

    @pl.when(j == 2)
    def _():
        n_heads, rows_aug, ts = vaug_ref.shape[1:]
        dv = w_ref.shape[1] // n_heads
        for h in range(0, n_heads, 2):
            ot = jnp.dot(h_ref[...], w_ref[:, h * dv:(h + 2) * dv],
                         preferred_element_type=F32).T.astype(BF16)
            for e in range(2):
                vaug_ref[0, h + e, 0:dv, :] = ot[e * dv:(e + 1) * dv, :]
                vaug_ref[0, h + e, dv:rows_aug, :] = jnp.ones(
                    (rows_aug - dv, ts), BF16)

    @pl.when(j == 3)
    def _():
        gt_ref[0] = _silu(project()).T.astype(BF16)


def _norm_matmul_diff(x, g, w, n_heads):
    B, S, D = x.shape
    Di = w.shape[1] // 4
    dv = Di // n_heads
    ts = PROJ_ROWS
    return pl.pallas_call(
        _norm_matmul_diff_kernel,
        grid=(B, S // ts, 4),
        in_specs=[pl.BlockSpec((1, ts, D), lambda b, i, j: (b, i, 0)),
                  pl.BlockSpec((1, D), lambda b, i, j: (0, 0)),
                  pl.BlockSpec((D, Di), lambda b, i, j: (0, j))],
        out_specs=[
            pl.BlockSpec((1, ts, Di), lambda b, i, j: (b, i, jnp.minimum(j, 1))),
            pl.BlockSpec((1, n_heads, dv + ONES_ROWS, ts),
                         lambda b, i, j: (b, 0, 0, i)),
            pl.BlockSpec((1, Di, ts), lambda b, i, j: (b, 0, i)),
        ],
        out_shape=[jax.ShapeDtypeStruct((B, S, 2 * Di), BF16),
                   jax.ShapeDtypeStruct((B, n_heads, dv + ONES_ROWS, S), BF16),
                   jax.ShapeDtypeStruct((B, Di, S), BF16)],
        scratch_shapes=[pltpu.VMEM((ts, D), BF16)],
        compiler_params=_compiler_params(("parallel", "parallel", "arbitrary")),
        name="norm_matmul_diff",
    )(x, g.reshape(1, D), w)


def _out_proj_kernel(y_ref, w_ref, g_ref, x_ref, o_ref, *, transposed_in):
    y = y_ref[0]
    if transposed_in:
        o = lax.dot_general(y, w_ref[...], (((0,), (0,)), ((), ())),
                            preferred_element_type=F32)
    else:
        o = jnp.dot(y, w_ref[...], preferred_element_type=F32)
    ms = jnp.mean(o * o, axis=-1, keepdims=True)
    o_ref[0] = x_ref[0] + o * lax.rsqrt(ms + RMS_EPS) * g_ref[...]


def _out_proj(y, w, g, x, *, transposed_in):
    B, S, D = x.shape
    Di = w.shape[0]
    ts = OUT_ROWS
    if transposed_in:
        y_spec = pl.BlockSpec((1, Di, ts), lambda b, i: (b, 0, i))
    else:
        y_spec = pl.BlockSpec((1, ts, Di), lambda b, i: (b, i, 0))
    return pl.pallas_call(
        functools.partial(_out_proj_kernel, transposed_in=transposed_in),
        grid=(B, S // ts),
        in_specs=[y_spec,
                  pl.BlockSpec((Di, D), lambda b, i: (0, 0)),
                  pl.BlockSpec((1, D), lambda b, i: (0, 0)),
                  pl.BlockSpec((1, ts, D), lambda b, i: (b, i, 0))],
        out_specs=pl.BlockSpec((1, ts, D), lambda b, i: (b, i, 0)),
        out_shape=jax.ShapeDtypeStruct((B, S, D), F32),
        compiler_params=_compiler_params(("parallel", "parallel")),
        name="out_proj_t" if transposed_in else "out_proj",
    )(y, w, g.reshape(1, D), x)


def _retention_kernel(q_ref, k_ref, v_ref, g_ref, dmat_ref, xi_ref, zeta_ref,
                      cdec_ref, o_ref, state_ref, state_bf_ref):
    C = RET_CHUNK

    @pl.when(pl.program_id(2) == 0)
    def _():
        state_ref[...] = jnp.zeros_like(state_ref)
        state_bf_ref[...] = jnp.zeros_like(state_bf_ref)

    dmat = dmat_ref[0]
    xi = xi_ref[0]
    zeta = zeta_ref[0]
    cdec = cdec_ref[0]

    for ci in range(RET_CHUNKS_PER_STEP):
        rows = pl.ds(ci * C, C)
        q, k, v = q_ref[0, rows, :], k_ref[0, rows, :], v_ref[0, rows, :]
        scores = lax.dot_general(q, k, (((1,), (1,)), ((), ())),
                                 preferred_element_type=F32) * dmat
        cross = jnp.dot(q, state_bf_ref[...], preferred_element_type=F32) * xi
        kz = (k.astype(F32) * zeta).astype(BF16)
        state = state_ref[...] * cdec + lax.dot_general(
            kz, v, (((0,), (0,)), ((), ())), preferred_element_type=F32)
        state_ref[...] = state
        state_bf_ref[...] = state.astype(BF16)
        inner = jnp.dot(scores.astype(BF16), v, preferred_element_type=F32)
        for r0 in range(0, C, RET_NORM_ROWS):
            blk = slice(r0, r0 + RET_NORM_ROWS)
            ret = inner[blk] + cross[blk]
            mu = jnp.mean(ret, axis=-1, keepdims=True)
            cen = ret - mu
            var = jnp.mean(cen * cen, axis=-1, keepdims=True)
            y = cen * lax.rsqrt(var + GN_EPS)
            out_rows = pl.ds(ci * C + r0, RET_NORM_ROWS)
            o_ref[0, out_rows, :] = (
                g_ref[0, out_rows, :].astype(F32) * y).astype(o_ref.dtype)


def _retention(proj, d_model, d_inner):
    B, S, _ = proj.shape
    H, C = RET_HEADS, RET_CHUNK
    dk, dv = d_model // H, d_inner // H
    rows = C * RET_CHUNKS_PER_STEP
    k_off = d_model // dk
    v_off = 2 * d_model // dv
    g_off = (2 * d_model + d_inner) // dv

    log_gamma = jnp.log1p(-jnp.exp2(-5.0 - jnp.arange(H, dtype=F32)))
    idx = jnp.arange(C, dtype=F32)
    rel = idx[:, None] - idx[None, :]
    dmat = jnp.exp(log_gamma[:, None, None] * jnp.maximum(rel, 0.0)) * (rel >= 0)
    xi = jnp.exp(log_gamma[:, None] * (idx + 1.0))
    zeta = jnp.exp(log_gamma[:, None] * (C - 1.0 - idx))
    cdec = jnp.exp(log_gamma * C)
    xi = jnp.broadcast_to(xi[:, :, None], (H, C, dv))
    zeta = jnp.broadcast_to(zeta[:, :, None], (H, C, dk))
    cdec = jnp.broadcast_to(cdec[:, None, None], (H, 1, dv))

    return pl.pallas_call(
        _retention_kernel,
        grid=(B, H, S // rows),
        in_specs=[
            pl.BlockSpec((1, rows, dk), lambda b, h, n: (b, n, h)),
            pl.BlockSpec((1, rows, dk), lambda b, h, n: (b, n, k_off + h)),
            pl.BlockSpec((1, rows, dv), lambda b, h, n: (b, n, v_off + h)),
            pl.BlockSpec((1, rows, dv), lambda b, h, n: (b, n, g_off + h)),
            pl.BlockSpec((1, C, C), lambda b, h, n: (h, 0, 0)),
            pl.BlockSpec((1, C, dv), lambda b, h, n: (h, 0, 0)),
            pl.BlockSpec((1, C, dk), lambda b, h, n: (h, 0, 0)),
            pl.BlockSpec((1, 1, dv), lambda b, h, n: (h, 0, 0)),
        ],
        out_specs=pl.BlockSpec((1, rows, dv), lambda b, h, n: (b, n, h)),
        out_shape=jax.ShapeDtypeStruct((B, S, d_inner), BF16),
        scratch_shapes=[pltpu.VMEM((dk, dv), F32), pltpu.VMEM((dk, dv), BF16)],
        compiler_params=_compiler_params(("parallel", "parallel", "arbitrary")),
        name="retention",
    )(proj, proj, proj, proj, dmat, xi, zeta, cdec)


def _diff_attn_kernel(tab_ref, bfar_ref, q_ref, k_ref, vaug_ref, gt_ref, bkd_ref,
                      lq1_ref, lk1_ref, lq2_ref, lk2_ref, sg_ref, o_ref,
                      qbd_ref, bias_ref, m_ref, acc_ref, sa_ref, sb_ref,
                      *, d, lambda_init):
    T, G, H = ATT_TILE, ATT_HEADS_PER_STEP, DIFF_HEADS
    dv = 2 * d
    hp = pl.program_id(1)
    i = pl.program_id(2)

    @pl.when(i == 0)
    def _():
        rows = lax.broadcasted_iota(jnp.int32, (T, T), 0)
        cols = lax.broadcasted_iota(jnp.int32, (T, T), 1)
        bucket_of_distance = bkd_ref[...]
        for u in range(G):
            h = hp * G + u
            cfar = tab_ref[bfar_ref[0] * H + h]
            by_distance = jnp.full((1, 2 * T), tab_ref[h], F32)
            for bucket in range(1, REL_BUCKETS):
                by_distance = jnp.where(bucket_of_distance >= bucket,
                                        tab_ref[bucket * H + h], by_distance)
            by_distance = (by_distance - cfar) * LOG2E
            toeplitz = pltpu.roll(jnp.broadcast_to(by_distance, (T, 2 * T)),
                                  0, 1, stride=1, stride_axis=0)
            bias_ref[u, 0] = jnp.where(rows <= cols, toeplitz[:, 0:T], -jnp.inf)
            bias_ref[u, 1] = toeplitz[:, T:2 * T]

    def logits_into(dst_ref, w, j, bias_index):
        u = w % G
        start = pl.multiple_of(j * T, T)
        kt = k_ref[0, pl.ds(start, T), u * dv:(u + 1) * dv]
        s = jnp.dot(kt, qbd_ref[w], preferred_element_type=F32)
        if bias_index is None:
            dst_ref[w] = s
        else:
            bias = bias_ref[u, bias_index]
            dst_ref[w, :, 0:T] = s[:, 0:T] + bias
            dst_ref[w, :, T:2 * T] = s[:, T:2 * T] + bias

    def consume(src_ref, w, j):
        u = w % G
        start = pl.multiple_of(j * T, T)
        s = src_ref[w]
        m_old = m_ref[w]
        m_new = jnp.maximum(m_old, jnp.max(s, axis=0, keepdims=True))
        p = jnp.exp2(s - m_new).astype(BF16)
        alpha = jnp.exp2(m_old - m_new)
        pv = jnp.dot(vaug_ref[0, u, :, pl.ds(start, T)], p,
                     preferred_element_type=F32)
        acc_ref[w] = acc_ref[w] * alpha + pv
        m_ref[w] = m_new

    first, second, both = range(G), range(G, 2 * G), range(2 * G)

    def step(dst_ref, j_next, bias_of, src_ref, j_cur, cur_units=both):
        for w in both:
            logits_into(dst_ref, w, j_next, bias_of(w))
            if w in cur_units:
                consume(src_ref, w, j_cur)

    no_bias = lambda w: None

    def start():
        for w in both:
            sub, u = divmod(w, G)
            q = q_ref[0, sub * T:(sub + 1) * T, u * dv:(u + 1) * dv].astype(F32)
            qt = (q * (d ** -0.5 * LOG2E)).T
            row = lax.broadcasted_iota(jnp.int32, qt.shape, 0)
            qbd_ref[w, :, 0:T] = jnp.where(row < d, qt, 0.0).astype(BF16)
            qbd_ref[w, :, T:2 * T] = jnp.where(row >= d, qt, 0.0).astype(BF16)
        m_ref[...] = jnp.full_like(m_ref, -jnp.inf)
        acc_ref[...] = jnp.zeros_like(acc_ref)
        for w in second:
            logits_into(sa_ref, w, 2 * i + 1, 0)
        step(sb_ref, 2 * i, lambda w: 0 if w in first else 1,
             sa_ref, 2 * i + 1, cur_units=second)

    def finish(src_ref, j_cur):
        lam = (jnp.exp(jnp.sum(lq1_ref[...] * lk1_ref[...], keepdims=True))
               - jnp.exp(jnp.sum(lq2_ref[...] * lk2_ref[...], keepdims=True))
               + lambda_init)
        out_scale = sg_ref[...] * (1.0 - lambda_init)
        for w in both:
            sub, u = divmod(w, G)
            consume(src_ref, w, j_cur)
            acc = acc_ref[w]
            inv_l = 1.0 / acc[dv:dv + 1, :]
            o = (acc[:dv, :T] * inv_l[:, :T]
                 - acc[:dv, T:] * (lam * inv_l[:, T:]))
            ms = jnp.mean(o * o, axis=0, keepdims=True)
            on = (o * lax.rsqrt(ms + GN_EPS)) * out_scale
            head_rows = slice(u * dv, (u + 1) * dv)
            sub_cols = slice(sub * T, (sub + 1) * T)
            gate = gt_ref[0, head_rows, sub_cols].astype(F32)
            o_ref[0, head_rows, sub_cols] = (gate * on).astype(o_ref.dtype)

    @pl.when(i == 0)
    def _():
        start()
        finish(sb_ref, 2 * i)

    @pl.when(i >= 1)
    def _():
        def prologue():
            start()
            step(sa_ref, 2 * i - 1, lambda w: 1 if w in first else None,
                 sb_ref, 2 * i)
            step(sb_ref, 0, no_bias, sa_ref, 2 * i - 1)

        def pair(r):
            step(sa_ref, 2 * r - 1, no_bias, sb_ref, 2 * r - 2)
            step(sb_ref, 2 * r, no_bias, sa_ref, 2 * r - 1)

        n_pairs = i - 1
        n_trips = n_pairs // 4

        def trip(t):
            for e in range(1, 5):
                pair(4 * t + e)

        @pl.when(n_trips == 0)
        def _():
            prologue()

        @pl.when(n_trips >= 1)
        def _():
            prologue()
            trip(0)

        def trip_body(t, carry):
            trip(t)
            return carry

        lax.fori_loop(1, n_trips, trip_body, 0)

        @pl.when(n_pairs % 4 >= 2)
        def _():
            first_left = (n_pairs // 4) * 4 + 1
            pair(first_left)
            pair(first_left + 1)

        @pl.when((i - 1) % 2 == 1)
        def _():
            pair(i - 1)
            finish(sb_ref, 2 * i - 2)

        @pl.when((i - 1) % 2 == 0)
        def _():
            finish(sb_ref, 2 * i - 2)


def _t5_bucket(n):
    max_exact = REL_BUCKETS // 2
    nf = jnp.maximum(n, max_exact).astype(F32)
    large = max_exact + (jnp.log(nf / max_exact) / math.log(REL_MAX_DIST / max_exact)
                         * (REL_BUCKETS - max_exact)).astype(jnp.int32)
    large = jnp.minimum(large, REL_BUCKETS - 1)
    return jnp.where(n < max_exact, n, large)


def _diff_attn(qk, vaug, gt, lq1, lk1, lq2, lk2, subln_g, rel_bias, lambda_init):
    B, S, _ = qk.shape
    H, T, G = DIFF_HEADS, ATT_TILE, ATT_HEADS_PER_STEP
    d_inner = gt.shape[1]
    dv = d_inner // H
    d = dv // 2
    gw = G * dv
    n_groups = H // G

    bkd = _t5_bucket(jnp.arange(2 * T, dtype=jnp.int32)[None, :])
    bfar = _t5_bucket(jnp.full((1,), T + 1, jnp.int32))

    smem = pl.BlockSpec(memory_space=pltpu.SMEM)
    vec = pl.BlockSpec((1, d), lambda b, h, i: (0, 0))
    return pl.pallas_call(
        functools.partial(_diff_attn_kernel, d=d, lambda_init=lambda_init),
        grid=(B, n_groups, S // (2 * T)),
        in_specs=[
            smem, smem,
            pl.BlockSpec((1, 2 * T, gw), lambda b, h, i: (b, i, h)),
            pl.BlockSpec((1, S, gw), lambda b, h, i: (b, 0, n_groups + h)),
            pl.BlockSpec((1, G, dv + ONES_ROWS, S), lambda b, h, i: (b, h, 0, 0)),
            pl.BlockSpec((1, gw, 2 * T), lambda b, h, i: (b, h, i)),
            pl.BlockSpec((1, 2 * T), lambda b, h, i: (0, 0)),
            vec, vec, vec, vec,
            pl.BlockSpec((dv, 1), lambda b, h, i: (0, 0)),
        ],
        out_specs=pl.BlockSpec((1, gw, 2 * T), lambda b, h, i: (b, h, i)),
        out_shape=jax.ShapeDtypeStruct((B, d_inner, S), BF16),
        scratch_shapes=[pltpu.VMEM((2 * G, dv, 2 * T), BF16),
                        pltpu.VMEM((G, 2, T, T), F32),
                        pltpu.VMEM((2 * G, 1, 2 * T), F32),
                        pltpu.VMEM((2 * G, dv + ONES_ROWS, 2 * T), F32),
                        pltpu.VMEM((2 * G, T, 2 * T), F32),
                        pltpu.VMEM((2 * G, T, 2 * T), F32)],
        compiler_params=_compiler_params(("parallel", "arbitrary", "arbitrary")),
        name="diff_attn",
    )(rel_bias.astype(F32).reshape(-1), bfar, qk, qk, vaug, gt, bkd,
      lq1.reshape(1, d), lk1.reshape(1, d), lq2.reshape(1, d), lk2.reshape(1, d),
      subln_g.reshape(dv, 1))


def kernel(x, pre_norm_g, post_norm_g, ret_w_in, ret_w_out, diff_w_in, diff_w_out,
           diff_lambda_q1, diff_lambda_k1, diff_lambda_q2, diff_lambda_k2,
           diff_subln_g, rel_bias):
    d_model = x.shape[-1]
    d_inner = ret_w_out.shape[1]
    for i in range(DEPTH):
        j = i // N_MIXERS
        if i % N_MIXERS == 0:
            proj = _norm_matmul_ret(x, pre_norm_g[i], ret_w_in[j].astype(BF16),
                                    d_inner, RET_HEADS)
            y = _retention(proj, d_model, d_inner)
            x = _out_proj(y, ret_w_out[j].astype(BF16), post_norm_g[i], x,
                          transposed_in=False)
        else:
            lambda_init = 0.8 - 0.6 * math.exp(-0.3 * i)
            qk, vaug, gt = _norm_matmul_diff(
                x, pre_norm_g[i], diff_w_in[j].astype(BF16), DIFF_HEADS)
            yt = _diff_attn(qk, vaug, gt, diff_lambda_q1[j], diff_lambda_k1[j],
                            diff_lambda_q2[j], diff_lambda_k2[j], diff_subln_g[j],
                            rel_bias, lambda_init)
            x = _out_proj(yt, diff_w_out[j].astype(BF16), post_norm_g[i], x,
                          transposed_in=True)
    return x
```

```python
import functools
import math

import jax
import jax.numpy as jnp
import numpy as np
from jax import lax
from jax.experimental import pallas as pl
from jax.experimental.pallas import tpu as pltpu

F32 = jnp.float32
BF16 = jnp.bfloat16

DEPTH = 2
N_MIXERS = 2
RET_HEADS = 4
DIFF_HEADS = 16
REL_BUCKETS = 32
REL_MAX_DIST = 128
ROPE_BASE = 10000.0
RMS_EPS = 1e-6
GN_EPS = 1e-5

V7X_VMEM_BYTES = 64 * 1024 * 1024
VMEM_LIMIT_BYTES = V7X_VMEM_BYTES * 15 // 16

PROJ_ROWS = 1024
PROJ_COLS = 2048
OUT_ROWS = 1024
RET_CHUNK = 256
RET_CHUNKS_PER_STEP = 16
RET_NORM_ROWS = 32
ATT_TILE = 256
ATT_HEADS_PER_STEP = 4
ONES_ROWS = 16
LOG2E = math.log2(math.e)


def _compiler_params(semantics):
    return pltpu.CompilerParams(dimension_semantics=semantics,
                                vmem_limit_bytes=VMEM_LIMIT_BYTES)


def _store_normed(x_ref, g_ref, h_ref):
    x = x_ref[0]
    ms = jnp.mean(x * x, axis=-1, keepdims=True)
    h_ref[...] = (x * lax.rsqrt(ms + RMS_EPS) * g_ref[...]).astype(BF16)


def _silu(x):
    return x * jax.nn.sigmoid(x)


def _norm_matmul_ret_kernel(x_ref, g_ref, w_ref, cos_ref, sin_ref, o_ref, h_ref,
                            *, dk):
    j = pl.program_id(2)

    def project(cols=slice(None)):
        return jnp.dot(h_ref[...], w_ref[:, cols], preferred_element_type=F32)

    @pl.when(j == 0)
    def _():
        _store_normed(x_ref, g_ref, h_ref)
        half = dk // 2
        n_blocks = w_ref.shape[1] // dk
        c, s = cos_ref[...], sin_ref[...]
        for blk in range(n_blocks):
            o = project(slice(blk * dk, (blk + 1) * dk))
            scale = 1.0 if blk < n_blocks // 2 else dk ** -0.5
            x1, x2 = o[:, :half], o[:, half:]
            o_ref[0, :, blk * dk:blk * dk + half] = (
                (x1 * c - x2 * s) * scale).astype(BF16)
            o_ref[0, :, blk * dk + half:(blk + 1) * dk] = (
                (x1 * s + x2 * c) * scale).astype(BF16)

    @pl.when(j == 1)
    def _():
        o_ref[0] = project().astype(BF16)

    @pl.when(j == 2)
    def _():
        o_ref[0] = _silu(project()).astype(BF16)


def _norm_matmul_ret(x, g, w, d_inner, n_heads):
    B, S, D = x.shape
    N = w.shape[1]
    ts, tn = PROJ_ROWS, d_inner
    dk = (N - 2 * d_inner) // 2 // n_heads
    half = dk // 2
    assert N == 3 * tn, "q | k must fill exactly one column tile"

    pos = np.arange(S, dtype=np.float64)
    inv = ROPE_BASE ** (-np.arange(half, dtype=np.float64) / half)
    ang = pos[:, None] * inv[None, :]
    cos = jnp.asarray(np.cos(ang), dtype=F32)
    sin = jnp.asarray(np.sin(ang), dtype=F32)

    return pl.pallas_call(
        functools.partial(_norm_matmul_ret_kernel, dk=dk),
        grid=(B, S // ts, N // tn),
        in_specs=[pl.BlockSpec((1, ts, D), lambda b, i, j: (b, i, 0)),
                  pl.BlockSpec((1, D), lambda b, i, j: (0, 0)),
                  pl.BlockSpec((D, tn), lambda b, i, j: (0, j)),
                  pl.BlockSpec((ts, half), lambda b, i, j: (i, 0)),
                  pl.BlockSpec((ts, half), lambda b, i, j: (i, 0))],
        out_specs=pl.BlockSpec((1, ts, tn), lambda b, i, j: (b, i, j)),
        out_shape=jax.ShapeDtypeStruct((B, S, N), BF16),
        scratch_shapes=[pltpu.VMEM((ts, D), BF16)],
        compiler_params=_compiler_params(("parallel", "parallel", "arbitrary")),
        name="norm_matmul_ret",
    )(x, g.reshape(1, D), w, cos, sin)


def _norm_matmul_diff_kernel(x_ref, g_ref, w_ref, qk_ref, vaug_ref, gt_ref, h_ref):
    j = pl.program_id(2)

    @pl.when(j == 0)
    def _():
        _store_normed(x_ref, g_ref, h_ref)

    def project():
        return jnp.dot(h_ref[...], w_ref[...], preferred_element_type=F32)

    @pl.when(j < 2)
    def _():
        qk_ref[0] = project().astype(BF16)

    @pl.when(j == 2)
    def _():
        n_heads, rows_aug, ts = vaug_ref.shape[1:]
        dv = w_ref.shape[1] // n_heads
        for h in range(0, n_heads, 2):
            ot = jnp.dot(h_ref[...], w_ref[:, h * dv:(h + 2) * dv],
                         preferred_element_type=F32).T.astype(BF16)
            for e in range(2):
                vaug_ref[0, h + e, 0:dv, :] = ot[e * dv:(e + 1) * dv, :]
                vaug_ref[0, h + e, dv:rows_aug, :] = jnp.ones(
                    (rows_aug - dv, ts), BF16)

    @pl.when(j == 3)
    def _():
        gt_ref[0] = _silu(project()).T.astype(BF16)


def _norm_matmul_diff(x, g, w, n_heads):
    B, S, D = x.shape
    Di = w.shape[1] // 4
    dv = Di // n_heads
    ts = PROJ_ROWS
    return pl.pallas_call(
        _norm_matmul_diff_kernel,
        grid=(B, S // ts, 4),
        in_specs=[pl.BlockSpec((1, ts, D), lambda b, i, j: (b, i, 0)),
                  pl.BlockSpec((1, D), lambda b, i, j: (0, 0)),
                  pl.BlockSpec((D, Di), lambda b, i, j: (0, j))],
        out_specs=[
            pl.BlockSpec((1, ts, Di), lambda b, i, j: (b, i, jnp.minimum(j, 1))),
            pl.BlockSpec((1, n_heads, dv + ONES_ROWS, ts),
                         lambda b, i, j: (b, 0, 0, i)),
            pl.BlockSpec((1, Di, ts), lambda b, i, j: (b, 0, i)),
        ],
        out_shape=[jax.ShapeDtypeStruct((B, S, 2 * Di), BF16),
                   jax.ShapeDtypeStruct((B, n_heads, dv + ONES_ROWS, S), BF16),
                   jax.ShapeDtypeStruct((B, Di, S), BF16)],
        scratch_shapes=[pltpu.VMEM((ts, D), BF16)],
        compiler_params=_compiler_params(("parallel", "parallel", "arbitrary")),
        name="norm_matmul_diff",
    )(x, g.reshape(1, D), w)


def _out_proj_kernel(y_ref, w_ref, g_ref, x_ref, o_ref, *, transposed_in):
    y = y_ref[0]
    if transposed_in:
        o = lax.dot_general(y, w_ref[...], (((0,), (0,)), ((), ())),
                            preferred_element_type=F32)
    else:
        o = jnp.dot(y, w_ref[...], preferred_element_type=F32)
    ms = jnp.mean(o * o, axis=-1, keepdims=True)
    o_ref[0] = x_ref[0] + o * lax.rsqrt(ms + RMS_EPS) * g_ref[...]


def _out_proj(y, w, g, x, *, transposed_in):
    B, S, D = x.shape
    Di = w.shape[0]
    ts = OUT_ROWS
    if transposed_in:
        y_spec = pl.BlockSpec((1, Di, ts), lambda b, i: (b, 0, i))
    else:
        y_spec = pl.BlockSpec((1, ts, Di), lambda b, i: (b, i, 0))
    return pl.pallas_call(
        functools.partial(_out_proj_kernel, transposed_in=transposed_in),
        grid=(B, S // ts),
        in_specs=[y_spec,
                  pl.BlockSpec((Di, D), lambda b, i: (0, 0)),
                  pl.BlockSpec((1, D), lambda b, i: (0, 0)),
                  pl.BlockSpec((1, ts, D), lambda b, i: (b, i, 0))],
        out_specs=pl.BlockSpec((1, ts, D), lambda b, i: (b, i, 0)),
        out_shape=jax.ShapeDtypeStruct((B, S, D), F32),
        compiler_params=_compiler_params(("parallel", "parallel")),
        name="out_proj_t" if transposed_in else "out_proj",
    )(y, w, g.reshape(1, D), x)


def _retention_kernel(q_ref, k_ref, v_ref, g_ref, dmat_ref, xi_ref, zeta_ref,
                      cdec_ref, o_ref, state_ref, state_bf_ref):
    C = RET_CHUNK

    @pl.when(pl.program_id(2) == 0)
    def _():
        state_ref[...] = jnp.zeros_like(state_ref)
        state_bf_ref[...] = jnp.zeros_like(state_bf_ref)

    dmat = dmat_ref[0]
    xi = xi_ref[0]
    zeta = zeta_ref[0]
    cdec = cdec_ref[0]

    for ci in range(RET_CHUNKS_PER_STEP):
        rows = pl.ds(ci * C, C)
        q, k, v = q_ref[0, rows, :], k_ref[0, rows, :], v_ref[0, rows, :]
        scores = lax.dot_general(q, k, (((1,), (1,)), ((), ())),
                                 preferred_element_type=F32) * dmat
        cross = jnp.dot(q, state_bf_ref[...], preferred_element_type=F32) * xi
        kz = (k.astype(F32) * zeta).astype(BF16)
        state = state_ref[...] * cdec + lax.dot_general(
            kz, v, (((0,), (0,)), ((), ())), preferred_element_type=F32)
        state_ref[...] = state
        state_bf_ref[...] = state.astype(BF16)
        inner = jnp.dot(scores.astype(BF16), v, preferred_element_type=F32)
        for r0 in range(0, C, RET_NORM_ROWS):
            blk = slice(r0, r0 + RET_NORM_ROWS)
            ret = inner[blk] + cross[blk]
            mu = jnp.mean(ret, axis=-1, keepdims=True)
            cen = ret - mu
            var = jnp.mean(cen * cen, axis=-1, keepdims=True)
            y = cen * lax.rsqrt(var + GN_EPS)
            out_rows = pl.ds(ci * C + r0, RET_NORM_ROWS)
            o_ref[0, out_rows, :] = (
                g_ref[0, out_rows, :].astype(F32) * y).astype(o_ref.dtype)


def _retention(proj, d_model, d_inner):
    B, S, _ = proj.shape
    H, C = RET_HEADS, RET_CHUNK
    dk, dv = d_model // H, d_inner // H
    rows = C * RET_CHUNKS_PER_STEP
    k_off = d_model // dk
    v_off = 2 * d_model // dv
    g_off = (2 * d_model + d_inner) // dv

    log_gamma = jnp.log1p(-jnp.exp2(-5.0 - jnp.arange(H, dtype=F32)))
    idx = jnp.arange(C, dtype=F32)
    rel = idx[:, None] - idx[None, :]
    dmat = jnp.exp(log_gamma[:, None, None] * jnp.maximum(rel, 0.0)) * (rel >= 0)
    xi = jnp.exp(log_gamma[:, None] * (idx + 1.0))
    zeta = jnp.exp(log_gamma[:, None] * (C - 1.0 - idx))
    cdec = jnp.exp(log_gamma * C)
    xi = jnp.broadcast_to(xi[:, :, None], (H, C, dv))
    zeta = jnp.broadcast_to(zeta[:, :, None], (H, C, dk))
    cdec = jnp.broadcast_to(cdec[:, None, None], (H, 1, dv))

    return pl.pallas_call(
        _retention_kernel,
        grid=(B, H, S // rows),
        in_specs=[
            pl.BlockSpec((1, rows, dk), lambda b, h, n: (b, n, h)),
            pl.BlockSpec((1, rows, dk), lambda b, h, n: (b, n, k_off + h)),
            pl.BlockSpec((1, rows, dv), lambda b, h, n: (b, n, v_off + h)),
            pl.BlockSpec((1, rows, dv), lambda b, h, n: (b, n, g_off + h)),
            pl.BlockSpec((1, C, C), lambda b, h, n: (h, 0, 0)),
            pl.BlockSpec((1, C, dv), lambda b, h, n: (h, 0, 0)),
            pl.BlockSpec((1, C, dk), lambda b, h, n: (h, 0, 0)),
            pl.BlockSpec((1, 1, dv), lambda b, h, n: (h, 0, 0)),
        ],
        out_specs=pl.BlockSpec((1, rows, dv), lambda b, h, n: (b, n, h)),
        out_shape=jax.ShapeDtypeStruct((B, S, d_inner), BF16),
        scratch_shapes=[pltpu.VMEM((dk, dv), F32), pltpu.VMEM((dk, dv), BF16)],
        compiler_params=_compiler_params(("parallel", "parallel", "arbitrary")),
        name="retention",
    )(proj, proj, proj, proj, dmat, xi, zeta, cdec)


def _diff_attn_kernel(tab_ref, bfar_ref, q_ref, k_ref, vaug_ref, gt_ref, bkd_ref,
                      lq1_ref, lk1_ref, lq2_ref, lk2_ref, sg_ref, o_ref,
                      qbd_ref, bias_ref, m_ref, acc_ref, sa_ref, sb_ref,
                      *, d, lambda_init):
    T, G, H = ATT_TILE, ATT_HEADS_PER_STEP, DIFF_HEADS
    dv = 2 * d
    hp = pl.program_id(1)
    i = pl.program_id(2)

    @pl.when(i == 0)
    def _():
        rows = lax.broadcasted_iota(jnp.int32, (T, T), 0)
        cols = lax.broadcasted_iota(jnp.int32, (T, T), 1)
        bucket_of_distance = bkd_ref[...]
        for u in range(G):
            h = hp * G + u
            cfar = tab_ref[bfar_ref[0] * H + h]
            by_distance = jnp.full((1, 2 * T), tab_ref[h], F32)
            for bucket in range(1, REL_BUCKETS):
                by_distance = jnp.where(bucket_of_distance >= bucket,
                                        tab_ref[bucket * H + h], by_distance)
            by_distance = (by_distance - cfar) * LOG2E
            toeplitz = pltpu.roll(jnp.broadcast_to(by_distance, (T, 2 * T)),
                                  0, 1, stride=1, stride_axis=0)
            bias_ref[u, 0] = jnp.where(rows <= cols, toeplitz[:, 0:T], -jnp.inf)
            bias_ref[u, 1] = toeplitz[:, T:2 * T]

    def logits_into(dst_ref, w, j, bias_index):
        u = w % G
        start = pl.multiple_of(j * T, T)
        kt = k_ref[0, pl.ds(start, T), u * dv:(u + 1) * dv]
        s = jnp.dot(kt, qbd_ref[w], preferred_element_type=F32)
        if bias_index is None:
            dst_ref[w] = s
        else:
            bias = bias_ref[u, bias_index]
            dst_ref[w, :, 0:T] = s[:, 0:T] + bias
            dst_ref[w, :, T:2 * T] = s[:, T:2 * T] + bias

    def consume(src_ref, w, j):
        u = w % G
        start = pl.multiple_of(j * T, T)
        s = src_ref[w]
        m_old = m_ref[w]
        m_new = jnp.maximum(m_old, jnp.max(s, axis=0, keepdims=True))
        p = jnp.exp2(s - m_new).astype(BF16)
        alpha = jnp.exp2(m_old - m_new)
        pv = jnp.dot(vaug_ref[0, u, :, pl.ds(start, T)], p,
                     preferred_element_type=F32)
        acc_ref[w] = acc_ref[w] * alpha + pv
        m_ref[w] = m_new

    first, second, both = range(G), range(G, 2 * G), range(2 * G)

    def step(dst_ref, j_next, bias_of, src_ref, j_cur, cur_units=both):
        for w in both:
            logits_into(dst_ref, w, j_next, bias_of(w))
            if w in cur_units:
                consume(src_ref, w, j_cur)

    no_bias = lambda w: None

    def start():
        for w in both:
            sub, u = divmod(w, G)
            q = q_ref[0, sub * T:(sub + 1) * T, u * dv:(u + 1) * dv].astype(F32)
            qt = (q * (d ** -0.5 * LOG2E)).T
            row = lax.broadcasted_iota(jnp.int32, qt.shape, 0)
            qbd_ref[w, :, 0:T] = jnp.where(row < d, qt, 0.0).astype(BF16)
            qbd_ref[w, :, T:2 * T] = jnp.where(row >= d, qt, 0.0).astype(BF16)
        m_ref[...] = jnp.full_like(m_ref, -jnp.inf)
        acc_ref[...] = jnp.zeros_like(acc_ref)
        for w in second:
            logits_into(sa_ref, w, 2 * i + 1, 0)
        step(sb_ref, 2 * i, lambda w: 0 if w in first else 1,
             sa_ref, 2 * i + 1, cur_units=second)

    def finish(src_ref, j_cur):
        lam = (jnp.exp(jnp.sum(lq1_ref[...] * lk1_ref[...], keepdims=True))
               - jnp.exp(jnp.sum(lq2_ref[...] * lk2_ref[...], keepdims=True))
               + lambda_init)
        out_scale = sg_ref[...] * (1.0 - lambda_init)
        for w in both:
            sub, u = divmod(w, G)
            consume(src_ref, w, j_cur)
            acc = acc_ref[w]
            inv_l = 1.0 / acc[dv:dv + 1, :]
            o = (acc[:dv, :T] * inv_l[:, :T]
                 - acc[:dv, T:] * (lam * inv_l[:, T:]))
            ms = jnp.mean(o * o, axis=0, keepdims=True)
            on = (o * lax.rsqrt(ms + GN_EPS)) * out_scale
            head_rows = slice(u * dv, (u + 1) * dv)
            sub_cols = slice(sub * T, (sub + 1) * T)
            gate = gt_ref[0, head_rows, sub_cols].astype(F32)
            o_ref[0, head_rows, sub_cols] = (gate * on).astype(o_ref.dtype)

    @pl.when(i == 0)
    def _():
        start()
        finish(sb_ref, 2 * i)

    @pl.when(i >= 1)
    def _():
        def prologue():
            start()
            step(sa_ref, 2 * i - 1, lambda w: 1 if w in first else None,
                 sb_ref, 2 * i)
            step(sb_ref, 0, no_bias, sa_ref, 2 * i - 1)

        def pair(r):
            step(sa_ref, 2 * r - 1, no_bias, sb_ref, 2 * r - 2)
            step(sb_ref, 2 * r, no_bias, sa_ref, 2 * r - 1)

        n_pairs = i - 1
        n_trips = n_pairs // 4

        def trip(t):
            for e in range(1, 5):
                pair(4 * t + e)

        @pl.when(n_trips == 0)
        def _():
            prologue()

        @pl.when(n_trips >= 1)
        def _():
            prologue()
            trip(0)

        def trip_body(t, carry):
            trip(t)
            return carry

        lax.fori_loop(1, n_trips, trip_body, 0)

        @pl.when(n_pairs % 4 >= 2)
        def _():
            first_left = (n_pairs // 4) * 4 + 1
            pair(first_left)
            pair(first_left + 1)

        @pl.when((i - 1) % 2 == 1)
        def _():
            pair(i - 1)
            finish(sb_ref, 2 * i - 2)

        @pl.when((i - 1) % 2 == 0)
        def _():
            finish(sb_ref, 2 * i - 2)


def _t5_bucket(n):
    max_exact = REL_BUCKETS // 2
    nf = jnp.maximum(n, max_exact).astype(F32)
    large = max_exact + (jnp.log(nf / max_exact) / math.log(REL_MAX_DIST / max_exact)
                         * (REL_BUCKETS - max_exact)).astype(jnp.int32)
    large = jnp.minimum(large, REL_BUCKETS - 1)
    return jnp.where(n < max_exact, n, large)


def _diff_attn(qk, vaug, gt, lq1, lk1, lq2, lk2, subln_g, rel_bias, lambda_init):
    B, S, _ = qk.shape
    H, T, G = DIFF_HEADS, ATT_TILE, ATT_HEADS_PER_STEP
    d_inner = gt.shape[1]
    dv = d_inner // H
    d = dv // 2
    gw = G * dv
    n_groups = H // G

    bkd = _t5_bucket(jnp.arange(2 * T, dtype=jnp.int32)[None, :])
    bfar = _t5_bucket(jnp.full((1,), T + 1, jnp.int32))

    smem = pl.BlockSpec(memory_space=pltpu.SMEM)
    vec = pl.BlockSpec((1, d), lambda b, h, i: (0, 0))
    return pl.pallas_call(
        functools.partial(_diff_attn_kernel, d=d, lambda_init=lambda_init),
        grid=(B, n_groups, S // (2 * T)),
        in_specs=[
            smem, smem,
            pl.BlockSpec((1, 2 * T, gw), lambda b, h, i: (b, i, h)),
            pl.BlockSpec((1, S, gw), lambda b, h, i: (b, 0, n_groups + h)),
            pl.BlockSpec((1, G, dv + ONES_ROWS, S), lambda b, h, i: (b, h, 0, 0)),
            pl.BlockSpec((1, gw, 2 * T), lambda b, h, i: (b, h, i)),
            pl.BlockSpec((1, 2 * T), lambda b, h, i: (0, 0)),
            vec, vec, vec, vec,
            pl.BlockSpec((dv, 1), lambda b, h, i: (0, 0)),
        ],
        out_specs=pl.BlockSpec((1, gw, 2 * T), lambda b, h, i: (b, h, i)),
        out_shape=jax.ShapeDtypeStruct((B, d_inner, S), BF16),
        scratch_shapes=[pltpu.VMEM((2 * G, dv, 2 * T), BF16),
                        pltpu.VMEM((G, 2, T, T), F32),
                        pltpu.VMEM((2 * G, 1, 2 * T), F32),
                        pltpu.VMEM((2 * G, dv + ONES_ROWS, 2 * T), F32),
                        pltpu.VMEM((2 * G, T, 2 * T), F32),
                        pltpu.VMEM((2 * G, T, 2 * T), F32)],
        compiler_params=_compiler_params(("parallel", "arbitrary", "arbitrary")),
        name="diff_attn",
    )(rel_bias.astype(F32).reshape(-1), bfar, qk, qk, vaug, gt, bkd,
      lq1.reshape(1, d), lk1.reshape(1, d), lq2.reshape(1, d), lk2.reshape(1, d),
      subln_g.reshape(dv, 1))


def kernel(x, pre_norm_g, post_norm_g, ret_w_in, ret_w_out, diff_w_in, diff_w_out,
           diff_lambda_q1, diff_lambda_k1, diff_lambda_q2, diff_lambda_k2,
           diff_subln_g, rel_bias):
    d_model = x.shape[-1]
    d_inner = ret_w_out.shape[1]
    for i in range(DEPTH):
        j = i // N_MIXERS
        if i % N_MIXERS == 0:
            proj = _norm_matmul_ret(x, pre_norm_g[i], ret_w_in[j].astype(BF16),
                                    d_inner, RET_HEADS)
            y = _retention(proj, d_model, d_inner)
            x = _out_proj(y, ret_w_out[j].astype(BF16), post_norm_g[i], x,
                          transposed_in=False)
        else:
            lambda_init = 0.8 - 0.6 * math.exp(-0.3 * i)
            qk, vaug, gt = _norm_matmul_diff(
                x, pre_norm_g[i], diff_w_in[j].astype(BF16), DIFF_HEADS)
            yt = _diff_attn(qk, vaug, gt, diff_lambda_q1[j], diff_lambda_k1[j],
                            diff_lambda_q2[j], diff_lambda_k2[j], diff_subln_g[j],
                            rel_bias, lambda_init)
            x = _out_proj(yt, diff_w_out[j].astype(BF16), post_norm_g[i], x,
                          transposed_in=True)
    return x
```

```python
import functools
import math

import jax
import jax.numpy as jnp
import numpy as np
from jax import lax
from jax.experimental import pallas as pl
from jax.experimental.pallas import tpu as pltpu

F32 = jnp.float32
BF16 = jnp.bfloat16

DEPTH = 2
N_MIXERS = 2
RET_HEADS = 4
DIFF_HEADS = 16
REL_BUCKETS = 32
REL_MAX_DIST = 128
ROPE_BASE = 10000.0
RMS_EPS = 1e-6
GN_EPS = 1e-5

V7X_VMEM_BYTES = 64 * 1024 * 1024
VMEM_LIMIT_BYTES = V7X_VMEM_BYTES * 15 // 16

PROJ_ROWS = 1024
PROJ_COLS = 2048
OUT_ROWS = 1024
RET_CHUNK = 256
RET_CHUNKS_PER_STEP = 8
RET_NORM_ROWS = 32
ATT_TILE = 256
ATT_HEADS_PER_STEP = 4
ONES_ROWS = 16
LOG2E = math.log2(math.e)


def _compiler_params(semantics):
    return pltpu.CompilerParams(dimension_semantics=semantics,
                                vmem_limit_bytes=VMEM_LIMIT_BYTES)


def _store_normed(x_ref, g_ref, h_ref):
    x = x_ref[0]
    ms = jnp.mean(x * x, axis=-1, keepdims=True)
    h_ref[...] = (x * lax.rsqrt(ms + RMS_EPS) * g_ref[...]).astype(BF16)


def _silu(x):
    return x * jax.nn.sigmoid(x)


def _norm_matmul_ret_kernel(x_ref, g_ref, w_ref, cos_ref, sin_ref, o_ref, h_ref,
                            *, dk):
    j = pl.program_id(2)

    def project(cols=slice(None)):
        return jnp.dot(h_ref[...], w_ref[:, cols], preferred_element_type=F32)

    @pl.when(j == 0)
    def _():
        _store_normed(x_ref, g_ref, h_ref)
        half = dk // 2
        n_blocks = w_ref.shape[1] // dk
        c, s = cos_ref[...], sin_ref[...]
        for blk in range(n_blocks):
            o = project(slice(blk * dk, (blk + 1) * dk))
            scale = 1.0 if blk < n_blocks // 2 else dk ** -0.5
            x1, x2 = o[:, :half], o[:, half:]
            o_ref[0, :, blk * dk:blk * dk + half] = (
                (x1 * c - x2 * s) * scale).astype(BF16)
            o_ref[0, :, blk * dk + half:(blk + 1) * dk] = (
                (x1 * s + x2 * c) * scale).astype(BF16)

    @pl.when(j == 1)
    def _():
        o_ref[0] = project().astype(BF16)

    @pl.when(j == 2)
    def _():
        o_ref[0] = _silu(project()).astype(BF16)


def _norm_matmul_ret(x, g, w, d_inner, n_heads):
    B, S, D = x.shape
    N = w.shape[1]
    ts, tn = PROJ_ROWS, d_inner
    dk = (N - 2 * d_inner) // 2 // n_heads
    half = dk // 2
    assert N == 3 * tn, "q | k must fill exactly one column tile"

    pos = np.arange(S, dtype=np.float64)
    inv = ROPE_BASE ** (-np.arange(half, dtype=np.float64) / half)
    ang = pos[:, None] * inv[None, :]
    cos = jnp.asarray(np.cos(ang), dtype=F32)
    sin = jnp.asarray(np.sin(ang), dtype=F32)

    return pl.pallas_call(
        functools.partial(_norm_matmul_ret_kernel, dk=dk),
        grid=(B, S // ts, N // tn),
        in_specs=[pl.BlockSpec((1, ts, D), lambda b, i, j: (b, i, 0)),
                  pl.BlockSpec((1, D), lambda b, i, j: (0, 0)),
                  pl.BlockSpec((D, tn), lambda b, i, j: (0, j)),
                  pl.BlockSpec((ts, half), lambda b, i, j: (i, 0)),
                  pl.BlockSpec((ts, half), lambda b, i, j: (i, 0))],
        out_specs=pl.BlockSpec((1, ts, tn), lambda b, i, j: (b, i, j)),
        out_shape=jax.ShapeDtypeStruct((B, S, N), BF16),
        scratch_shapes=[pltpu.VMEM((ts, D), BF16)],
        compiler_params=_compiler_params(("parallel", "parallel", "arbitrary")),
        name="norm_matmul_ret",
    )(x, g.reshape(1, D), w, cos, sin)


def _norm_matmul_diff_kernel(x_ref, g_ref, w_ref, qk_ref, vaug_ref, gt_ref, h_ref):
    j = pl.program_id(2)

    @pl.when(j == 0)
    def _():
        _store_normed(x_ref, g_ref, h_ref)

    def project():
        return jnp.dot(h_ref[...], w_ref[...], preferred_element_type=F32)

    @pl.when(j < 2)
    def _():
        qk_ref[0] = project().astype(BF16)

    @pl.when(j == 2)
    def _():
        n_heads, rows_aug, ts = vaug_ref.shape[1:]
        dv = w_ref.shape[1] // n_heads
        for h in range(0, n_heads, 2):
            ot = jnp.dot(h_ref[...], w_ref[:, h * dv:(h + 2) * dv],
                         preferred_element_type=F32).T.astype(BF16)
            for e in range(2):
                vaug_ref[0, h + e, 0:dv, :] = ot[e * dv:(e + 1) * dv, :]
                vaug_ref[0, h + e, dv:rows_aug, :] = jnp.ones(
                    (rows_aug - dv, ts), BF16)

    @pl.when(j == 3)
    def _():
        gt_ref[0] = _silu(project()).T.astype(BF16)


def _norm_matmul_diff(x, g, w, n_heads):
    B, S, D = x.shape
    Di = w.shape[1] // 4
    dv = Di // n_heads
    ts = PROJ_ROWS
    return pl.pallas_call(
        _norm_matmul_diff_kernel,
        grid=(B, S // ts, 4),
        in_specs=[pl.BlockSpec((1, ts, D), lambda b, i, j: (b, i, 0)),
                  pl.BlockSpec((1, D), lambda b, i, j: (0, 0)),
                  pl.BlockSpec((D, Di), lambda b, i, j: (0, j))],
        out_specs=[
            pl.BlockSpec((1, ts, Di), lambda b, i, j: (b, i, jnp.minimum(j, 1))),
            pl.BlockSpec((1, n_heads, dv + ONES_ROWS, ts),
                         lambda b, i, j: (b, 0, 0, i)),
            pl.BlockSpec((1, Di, ts), lambda b, i, j: (b, 0, i)),
        ],
        out_shape=[jax.ShapeDtypeStruct((B, S, 2 * Di), BF16),
                   jax.ShapeDtypeStruct((B, n_heads, dv + ONES_ROWS, S), BF16),
                   jax.ShapeDtypeStruct((B, Di, S), BF16)],
        scratch_shapes=[pltpu.VMEM((ts, D), BF16)],
        compiler_params=_compiler_params(("parallel", "parallel", "arbitrary")),
        name="norm_matmul_diff",
    )(x, g.reshape(1, D), w)


def _out_proj_kernel(y_ref, w_ref, g_ref, x_ref, o_ref, *, transposed_in):
    y = y_ref[0]
    if transposed_in:
        o = lax.dot_general(y, w_ref[...], (((0,), (0,)), ((), ())),
                            preferred_element_type=F32)
    else:
        o = jnp.dot(y, w_ref[...], preferred_element_type=F32)
    ms = jnp.mean(o * o, axis=-1, keepdims=True)
    o_ref[0] = x_ref[0] + o * lax.rsqrt(ms + RMS_EPS) * g_ref[...]


def _out_proj(y, w, g, x, *, transposed_in):
    B, S, D = x.shape
    Di = w.shape[0]
    ts = OUT_ROWS
    if transposed_in:
        y_spec = pl.BlockSpec((1, Di, ts), lambda b, i: (b, 0, i))
    else:
        y_spec = pl.BlockSpec((1, ts, Di), lambda b, i: (b, i, 0))
    return pl.pallas_call(
        functools.partial(_out_proj_kernel, transposed_in=transposed_in),
        grid=(B, S // ts),
        in_specs=[y_spec,
                  pl.BlockSpec((Di, D), lambda b, i: (0, 0)),
                  pl.BlockSpec((1, D), lambda b, i: (0, 0)),
                  pl.BlockSpec((1, ts, D), lambda b, i: (b, i, 0))],
        out_specs=pl.BlockSpec((1, ts, D), lambda b, i: (b, i, 0)),
        out_shape=jax.ShapeDtypeStruct((B, S, D), F32),
        compiler_params=_compiler_params(("parallel", "parallel")),
        name="out_proj_t" if transposed_in else "out_proj",
    )(y, w, g.reshape(1, D), x)


def _retention_kernel(q_ref, k_ref, v_ref, g_ref, dmat_ref, xi_ref, zeta_ref,
                      cdec_ref, o_ref, state_ref, state_bf_ref):
    C = RET_CHUNK

    @pl.when(pl.program_id(1) == 0)
    def _():
        state_ref[...] = jnp.zeros_like(state_ref)
        state_bf_ref[...] = jnp.zeros_like(state_bf_ref)

    dmat = dmat_ref[0]
    xi = xi_ref[0]
    zeta = zeta_ref[0]
    cdec = cdec_ref[0]

    n_batch = q_ref.shape[0]
    for ci, b in [(ci, b) for ci in range(RET_CHUNKS_PER_STEP)
                  for b in range(n_batch)]:
        rows = pl.ds(ci * C, C)
        q, k, v = q_ref[b, rows, :], k_ref[b, rows, :], v_ref[b, rows, :]
        scores = lax.dot_general(q, k, (((1,), (1,)), ((), ())),
                                 preferred_element_type=F32) * dmat
        cross = jnp.dot(q, state_bf_ref[b], preferred_element_type=F32) * xi
        kz = (k.astype(F32) * zeta).astype(BF16)
        state = state_ref[b] * cdec + lax.dot_general(
            kz, v, (((0,), (0,)), ((), ())), preferred_element_type=F32)
        state_ref[b] = state
        state_bf_ref[b] = state.astype(BF16)
        inner = jnp.dot(scores.astype(BF16), v, preferred_element_type=F32)
        for r0 in range(0, C, RET_NORM_ROWS):
            blk = slice(r0, r0 + RET_NORM_ROWS)
            ret = inner[blk] + cross[blk]
            mu = jnp.mean(ret, axis=-1, keepdims=True)
            cen = ret - mu
            var = jnp.mean(cen * cen, axis=-1, keepdims=True)
            y = cen * lax.rsqrt(var + GN_EPS)
            out_rows = pl.ds(ci * C + r0, RET_NORM_ROWS)
            o_ref[b, out_rows, :] = (
                g_ref[b, out_rows, :].astype(F32) * y).astype(o_ref.dtype)


def _retention(proj, d_model, d_inner):
    B, S, _ = proj.shape
    H, C = RET_HEADS, RET_CHUNK
    dk, dv = d_model // H, d_inner // H
    rows = C * RET_CHUNKS_PER_STEP
    k_off = d_model // dk
    v_off = 2 * d_model // dv
    g_off = (2 * d_model + d_inner) // dv

    log_gamma = jnp.log1p(-jnp.exp2(-5.0 - jnp.arange(H, dtype=F32)))
    idx = jnp.arange(C, dtype=F32)
    rel = idx[:, None] - idx[None, :]
    dmat = jnp.exp(log_gamma[:, None, None] * jnp.maximum(rel, 0.0)) * (rel >= 0)
    xi = jnp.exp(log_gamma[:, None] * (idx + 1.0))
    zeta = jnp.exp(log_gamma[:, None] * (C - 1.0 - idx))
    cdec = jnp.exp(log_gamma * C)
    xi = jnp.broadcast_to(xi[:, :, None], (H, C, dv))
    zeta = jnp.broadcast_to(zeta[:, :, None], (H, C, dk))
    cdec = jnp.broadcast_to(cdec[:, None, None], (H, 1, dv))

    return pl.pallas_call(
        _retention_kernel,
        grid=(H, S // rows),
        in_specs=[
            pl.BlockSpec((B, rows, dk), lambda h, n: (0, n, h)),
            pl.BlockSpec((B, rows, dk), lambda h, n: (0, n, k_off + h)),
            pl.BlockSpec((B, rows, dv), lambda h, n: (0, n, v_off + h)),
            pl.BlockSpec((B, rows, dv), lambda h, n: (0, n, g_off + h)),
            pl.BlockSpec((1, C, C), lambda h, n: (h, 0, 0)),
            pl.BlockSpec((1, C, dv), lambda h, n: (h, 0, 0)),
            pl.BlockSpec((1, C, dk), lambda h, n: (h, 0, 0)),
            pl.BlockSpec((1, 1, dv), lambda h, n: (h, 0, 0)),
        ],
        out_specs=pl.BlockSpec((B, rows, dv), lambda h, n: (0, n, h)),
        out_shape=jax.ShapeDtypeStruct((B, S, d_inner), BF16),
        scratch_shapes=[pltpu.VMEM((B, dk, dv), F32), pltpu.VMEM((B, dk, dv), BF16)],
        compiler_params=_compiler_params(("parallel", "arbitrary")),
        name="retention",
    )(proj, proj, proj, proj, dmat, xi, zeta, cdec)


def _diff_attn_kernel(tab_ref, bfar_ref, q_ref, k_ref, vaug_ref, gt_ref, bkd_ref,
                      lq1_ref, lk1_ref, lq2_ref, lk2_ref, sg_ref, o_ref,
                      qbd_ref, bias_ref, m_ref, acc_ref, sa_ref, sb_ref,
                      *, d, lambda_init):
    T, G, H = ATT_TILE, ATT_HEADS_PER_STEP, DIFF_HEADS
    dv = 2 * d
    hp = pl.program_id(1)
    i = pl.program_id(2)

    @pl.when(i == 0)
    def _():
        rows = lax.broadcasted_iota(jnp.int32, (T, T), 0)
        cols = lax.broadcasted_iota(jnp.int32, (T, T), 1)
        bucket_of_distance = bkd_ref[...]
        for u in range(G):
            h = hp * G + u
            cfar = tab_ref[bfar_ref[0] * H + h]
            by_distance = jnp.full((1, 2 * T), tab_ref[h], F32)
            for bucket in range(1, REL_BUCKETS):
                by_distance = jnp.where(bucket_of_distance >= bucket,
                                        tab_ref[bucket * H + h], by_distance)
            by_distance = (by_distance - cfar) * LOG2E
            toeplitz = pltpu.roll(jnp.broadcast_to(by_distance, (T, 2 * T)),
                                  0, 1, stride=1, stride_axis=0)
            bias_ref[u, 0] = jnp.where(rows <= cols, toeplitz[:, 0:T], -jnp.inf)
            bias_ref[u, 1] = toeplitz[:, T:2 * T]

    def logits_into(dst_ref, w, j, bias_index):
        u = w % G
        start = pl.multiple_of(j * T, T)
        kt = k_ref[0, pl.ds(start, T), u * dv:(u + 1) * dv]
        s = jnp.dot(kt, qbd_ref[w], preferred_element_type=F32)
        if bias_index is None:
            dst_ref[w] = s
        else:
            bias = bias_ref[u, bias_index]
            dst_ref[w, :, 0:T] = s[:, 0:T] + bias
            dst_ref[w, :, T:2 * T] = s[:, T:2 * T] + bias

    def consume(src_ref, w, j):
        u = w % G
        start = pl.multiple_of(j * T, T)
        s = src_ref[w]
        m_old = m_ref[w]
        m_new = jnp.maximum(m_old, jnp.max(s, axis=0, keepdims=True))
        p = jnp.exp2(s - m_new).astype(BF16)
        alpha = jnp.exp2(m_old - m_new)
        pv = jnp.dot(vaug_ref[0, u, :, pl.ds(start, T)], p,
                     preferred_element_type=F32)
        acc_ref[w] = acc_ref[w] * alpha + pv
        m_ref[w] = m_new

    first, second, both = range(G), range(G, 2 * G), range(2 * G)

    def step(dst_ref, j_next, bias_of, src_ref, j_cur, cur_units=both):
        for w in both:
            logits_into(dst_ref, w, j_next, bias_of(w))
            if w in cur_units:
                consume(src_ref, w, j_cur)

    no_bias = lambda w: None

    def start():
        for w in both:
            sub, u = divmod(w, G)
            q = q_ref[0, sub * T:(sub + 1) * T, u * dv:(u + 1) * dv].astype(F32)
            qt = (q * (d ** -0.5 * LOG2E)).T
            row = lax.broadcasted_iota(jnp.int32, qt.shape, 0)
            qbd_ref[w, :, 0:T] = jnp.where(row < d, qt, 0.0).astype(BF16)
            qbd_ref[w, :, T:2 * T] = jnp.where(row >= d, qt, 0.0).astype(BF16)
        m_ref[...] = jnp.full_like(m_ref, -jnp.inf)
        acc_ref[...] = jnp.zeros_like(acc_ref)
        for w in second:
            logits_into(sa_ref, w, 2 * i + 1, 0)
        step(sb_ref, 2 * i, lambda w: 0 if w in first else 1,
             sa_ref, 2 * i + 1, cur_units=second)

    def finish(src_ref, j_cur):
        lam = (jnp.exp(jnp.sum(lq1_ref[...] * lk1_ref[...], keepdims=True))
               - jnp.exp(jnp.sum(lq2_ref[...] * lk2_ref[...], keepdims=True))
               + lambda_init)
        out_scale = sg_ref[...] * (1.0 - lambda_init)
        for w in both:
            sub, u = divmod(w, G)
            consume(src_ref, w, j_cur)
            acc = acc_ref[w]
            inv_l = 1.0 / acc[dv:dv + 1, :]
            o = (acc[:dv, :T] * inv_l[:, :T]
                 - acc[:dv, T:] * (lam * inv_l[:, T:]))
            ms = jnp.mean(o * o, axis=0, keepdims=True)
            on = (o * lax.rsqrt(ms + GN_EPS)) * out_scale
            head_rows = slice(u * dv, (u + 1) * dv)
            sub_cols = slice(sub * T, (sub + 1) * T)
            gate = gt_ref[0, head_rows, sub_cols].astype(F32)
            o_ref[0, head_rows, sub_cols] = (gate * on).astype(o_ref.dtype)

    @pl.when(i == 0)
    def _():
        start()
        finish(sb_ref, 2 * i)

    @pl.when(i >= 1)
    def _():
        def prologue():
            start()
            step(sa_ref, 2 * i - 1, lambda w: 1 if w in first else None,
                 sb_ref, 2 * i)
            step(sb_ref, 0, no_bias, sa_ref, 2 * i - 1)

        def pair(r):
            step(sa_ref, 2 * r - 1, no_bias, sb_ref, 2 * r - 2)
            step(sb_ref, 2 * r, no_bias, sa_ref, 2 * r - 1)

        n_pairs = i - 1
        n_trips = n_pairs // 4

        def trip(t):
            for e in range(1, 5):
                pair(4 * t + e)

        @pl.when(n_trips == 0)
        def _():
            prologue()

        @pl.when(n_trips >= 1)
        def _():
            prologue()
            trip(0)

        def trip_body(t, carry):
            trip(t)
            return carry

        lax.fori_loop(1, n_trips, trip_body, 0)

        @pl.when(n_pairs % 4 >= 2)
        def _():
            first_left = (n_pairs // 4) * 4 + 1
            pair(first_left)
            pair(first_left + 1)

        @pl.when((i - 1) % 2 == 1)
        def _():
            pair(i - 1)
            finish(sb_ref, 2 * i - 2)

        @pl.when((i - 1) % 2 == 0)
        def _():
            finish(sb_ref, 2 * i - 2)


def _t5_bucket(n):
    max_exact = REL_BUCKETS // 2
    nf = jnp.maximum(n, max_exact).astype(F32)
    large = max_exact + (jnp.log(nf / max_exact) / math.log(REL_MAX_DIST / max_exact)
                         * (REL_BUCKETS - max_exact)).astype(jnp.int32)
    large = jnp.minimum(large, REL_BUCKETS - 1)
    return jnp.where(n < max_exact, n, large)


def _diff_attn(qk, vaug, gt, lq1, lk1, lq2, lk2, subln_g, rel_bias, lambda_init):
    B, S, _ = qk.shape
    H, T, G = DIFF_HEADS, ATT_TILE, ATT_HEADS_PER_STEP
    d_inner = gt.shape[1]
    dv = d_inner // H
    d = dv // 2
    gw = G * dv
    n_groups = H // G

    bkd = _t5_bucket(jnp.arange(2 * T, dtype=jnp.int32)[None, :])
    bfar = _t5_bucket(jnp.full((1,), T + 1, jnp.int32))

    smem = pl.BlockSpec(memory_space=pltpu.SMEM)
    vec = pl.BlockSpec((1, d), lambda b, h, i: (0, 0))
    return pl.pallas_call(
        functools.partial(_diff_attn_kernel, d=d, lambda_init=lambda_init),
        grid=(B, n_groups, S // (2 * T)),
        in_specs=[
            smem, smem,
            pl.BlockSpec((1, 2 * T, gw), lambda b, h, i: (b, i, h)),
            pl.BlockSpec((1, S, gw), lambda b, h, i: (b, 0, n_groups + h)),
            pl.BlockSpec((1, G, dv + ONES_ROWS, S), lambda b, h, i: (b, h, 0, 0)),
            pl.BlockSpec((1, gw, 2 * T), lambda b, h, i: (b, h, i)),
            pl.BlockSpec((1, 2 * T), lambda b, h, i: (0, 0)),
            vec, vec, vec, vec,
            pl.BlockSpec((dv, 1), lambda b, h, i: (0, 0)),
        ],
        out_specs=pl.BlockSpec((1, gw, 2 * T), lambda b, h, i: (b, h, i)),
        out_shape=jax.ShapeDtypeStruct((B, d_inner, S), BF16),
        scratch_shapes=[pltpu.VMEM((2 * G, dv, 2 * T), BF16),
                        pltpu.VMEM((G, 2, T, T), F32),
                        pltpu.VMEM((2 * G, 1, 2 * T), F32),
                        pltpu.VMEM((2 * G, dv + ONES_ROWS, 2 * T), F32),
                        pltpu.VMEM((2 * G, T, 2 * T), F32),
                        pltpu.VMEM((2 * G, T, 2 * T), F32)],
        compiler_params=_compiler_params(("parallel", "arbitrary", "arbitrary")),
        name="diff_attn",
    )(rel_bias.astype(F32).reshape(-1), bfar, qk, qk, vaug, gt, bkd,
      lq1.reshape(1, d), lk1.reshape(1, d), lq2.reshape(1, d), lk2.reshape(1, d),
      subln_g.reshape(dv, 1))


def kernel(x, pre_norm_g, post_norm_g, ret_w_in, ret_w_out, diff_w_in, diff_w_out,
           diff_lambda_q1, diff_lambda_k1, diff_lambda_q2, diff_lambda_k2,
           diff_subln_g, rel_bias):
    d_model = x.shape[-1]
    d_inner = ret_w_out.shape[1]
    for i in range(DEPTH):
        j = i // N_MIXERS
        if i % N_MIXERS == 0:
            proj = _norm_matmul_ret(x, pre_norm_g[i], ret_w_in[j].astype(BF16),
                                    d_inner, RET_HEADS)
            y = _retention(proj, d_model, d_inner)
            x = _out_proj(y, ret_w_out[j].astype(BF16), post_norm_g[i], x,
                          transposed_in=False)
        else:
            lambda_init = 0.8 - 0.6 * math.exp(-0.3 * i)
            qk, vaug, gt = _norm_matmul_diff(
                x, pre_norm_g[i], diff_w_in[j].astype(BF16), DIFF_HEADS)
            yt = _diff_attn(qk, vaug, gt, diff_lambda_q1[j], diff_lambda_k1[j],
                            diff_lambda_q2[j], diff_lambda_k2[j], diff_subln_g[j],
                            rel_bias, lambda_init)
            x = _out_proj(yt, diff_w_out[j].astype(BF16), post_norm_g[i], x,
                          transposed_in=True)
    return x
```

```python
import functools
import math

import jax
import jax.numpy as jnp
import numpy as np
from jax import lax
from jax.experimental import pallas as pl
from jax.experimental.pallas import tpu as pltpu

F32 = jnp.float32
BF16 = jnp.bfloat16

DEPTH = 2
N_MIXERS = 2
RET_HEADS = 4
DIFF_HEADS = 16
REL_BUCKETS = 32
REL_MAX_DIST = 128
ROPE_BASE = 10000.0
RMS_EPS = 1e-6
GN_EPS = 1e-5

V7X_VMEM_BYTES = 64 * 1024 * 1024
VMEM_LIMIT_BYTES = V7X_VMEM_BYTES * 15 // 16

PROJ_ROWS = 1024
PROJ_COLS = 2048
OUT_ROWS = 1024
RET_CHUNK = 256
RET_CHUNKS_PER_STEP = 8
RET_NORM_ROWS = 32
ATT_TILE = 256
ATT_HEADS_PER_STEP = 4
ONES_ROWS = 16
LOG2E = math.log2(math.e)


def _compiler_params(semantics):
    return pltpu.CompilerParams(dimension_semantics=semantics,
                                vmem_limit_bytes=VMEM_LIMIT_BYTES)


def _store_normed(x_ref, g_ref, h_ref):
    x = x_ref[0]
    ms = jnp.mean(x * x, axis=-1, keepdims=True)
    h_ref[...] = (x * lax.rsqrt(ms + RMS_EPS) * g_ref[...]).astype(BF16)


def _silu(x):
    return x * jax.nn.sigmoid(x)


def _norm_matmul_ret_kernel(x_ref, g_ref, w_ref, cos_ref, sin_ref, o_ref, h_ref,
                            *, dk):
    j = pl.program_id(2)

    def project(cols=slice(None)):
        return jnp.dot(h_ref[...], w_ref[:, cols], preferred_element_type=F32)

    @pl.when(j == 0)
    def _():
        _store_normed(x_ref, g_ref, h_ref)
        half = dk // 2
        n_blocks = w_ref.shape[1] // dk
        c, s = cos_ref[...], sin_ref[...]
        for blk in range(n_blocks):
            o = project(slice(blk * dk, (blk + 1) * dk))
            scale = 1.0 if blk < n_blocks // 2 else dk ** -0.5
            x1, x2 = o[:, :half], o[:, half:]
            o_ref[0, :, blk * dk:blk * dk + half] = (
                (x1 * c - x2 * s) * scale).astype(BF16)
            o_ref[0, :, blk * dk + half:(blk + 1) * dk] = (
                (x1 * s + x2 * c) * scale).astype(BF16)

    @pl.when(j == 1)
    def _():
        o_ref[0] = project().astype(BF16)

    @pl.when(j == 2)
    def _():
        o_ref[0] = _silu(project()).astype(BF16)


def _norm_matmul_ret(x, g, w, d_inner, n_heads):
    B, S, D = x.shape
    N = w.shape[1]
    ts, tn = PROJ_ROWS, d_inner
    dk = (N - 2 * d_inner) // 2 // n_heads
    half = dk // 2
    assert N == 3 * tn, "q | k must fill exactly one column tile"

    pos = np.arange(S, dtype=np.float64)
    inv = ROPE_BASE ** (-np.arange(half, dtype=np.float64) / half)
    ang = pos[:, None] * inv[None, :]
    cos = jnp.asarray(np.cos(ang), dtype=F32)
    sin = jnp.asarray(np.sin(ang), dtype=F32)

    return pl.pallas_call(
        functools.partial(_norm_matmul_ret_kernel, dk=dk),
        grid=(B, S // ts, N // tn),
        in_specs=[pl.BlockSpec((1, ts, D), lambda b, i, j: (b, i, 0)),
                  pl.BlockSpec((1, D), lambda b, i, j: (0, 0)),
                  pl.BlockSpec((D, tn), lambda b, i, j: (0, j)),
                  pl.BlockSpec((ts, half), lambda b, i, j: (i, 0)),
                  pl.BlockSpec((ts, half), lambda b, i, j: (i, 0))],
        out_specs=pl.BlockSpec((1, ts, tn), lambda b, i, j: (b, i, j)),
        out_shape=jax.ShapeDtypeStruct((B, S, N), BF16),
        scratch_shapes=[pltpu.VMEM((ts, D), BF16)],
        compiler_params=_compiler_params(("parallel", "parallel", "arbitrary")),
        name="norm_matmul_ret",
    )(x, g.reshape(1, D), w, cos, sin)


def _norm_matmul_diff_kernel(x_ref, g_ref, w_ref, qt_ref, k_ref, vaug_ref, gt_ref,
                             h_ref, *, q_scale):
    j = pl.program_id(2)

    @pl.when(j == 0)
    def _():
        _store_normed(x_ref, g_ref, h_ref)

    def project():
        return jnp.dot(h_ref[...], w_ref[...], preferred_element_type=F32)

    @pl.when(j == 0)
    def _():
        qt_ref[0] = (project() * q_scale).T.astype(BF16)

    @pl.when(j == 1)
    def _():
        k_ref[0] = project().astype(BF16)

    @pl.when(j == 2)
    def _():
        n_heads, rows_aug, ts = vaug_ref.shape[1:]
        dv = w_ref.shape[1] // n_heads
        for h in range(0, n_heads, 2):
            ot = jnp.dot(h_ref[...], w_ref[:, h * dv:(h + 2) * dv],
                         preferred_element_type=F32).T.astype(BF16)
            for e in range(2):
                vaug_ref[0, h + e, 0:dv, :] = ot[e * dv:(e + 1) * dv, :]
                vaug_ref[0, h + e, dv:rows_aug, :] = jnp.ones(
                    (rows_aug - dv, ts), BF16)

    @pl.when(j == 3)
    def _():
        gt_ref[0] = _silu(project()).T.astype(BF16)


def _norm_matmul_diff(x, g, w, n_heads, q_scale):
    B, S, D = x.shape
    Di = w.shape[1] // 4
    dv = Di // n_heads
    ts = PROJ_ROWS
    return pl.pallas_call(
        functools.partial(_norm_matmul_diff_kernel, q_scale=q_scale),
        grid=(B, S // ts, 4),
        in_specs=[pl.BlockSpec((1, ts, D), lambda b, i, j: (b, i, 0)),
                  pl.BlockSpec((1, D), lambda b, i, j: (0, 0)),
                  pl.BlockSpec((D, Di), lambda b, i, j: (0, j))],
        out_specs=[
            pl.BlockSpec((1, Di, ts), lambda b, i, j: (b, 0, i)),
            pl.BlockSpec((1, ts, Di), lambda b, i, j: (b, i, 0)),
            pl.BlockSpec((1, n_heads, dv + ONES_ROWS, ts),
                         lambda b, i, j: (b, 0, 0, i)),
            pl.BlockSpec((1, Di, ts), lambda b, i, j: (b, 0, i)),
        ],
        out_shape=[jax.ShapeDtypeStruct((B, Di, S), BF16),
                   jax.ShapeDtypeStruct((B, S, Di), BF16),
                   jax.ShapeDtypeStruct((B, n_heads, dv + ONES_ROWS, S), BF16),
                   jax.ShapeDtypeStruct((B, Di, S), BF16)],
        scratch_shapes=[pltpu.VMEM((ts, D), BF16)],
        compiler_params=_compiler_params(("parallel", "parallel", "arbitrary")),
        name="norm_matmul_diff",
    )(x, g.reshape(1, D), w)


def _out_proj_kernel(y_ref, w_ref, g_ref, x_ref, o_ref, *, transposed_in):
    y = y_ref[0]
    if transposed_in:
        o = lax.dot_general(y, w_ref[...], (((0,), (0,)), ((), ())),
                            preferred_element_type=F32)
    else:
        o = jnp.dot(y, w_ref[...], preferred_element_type=F32)
    ms = jnp.mean(o * o, axis=-1, keepdims=True)
    o_ref[0] = x_ref[0] + o * lax.rsqrt(ms + RMS_EPS) * g_ref[...]


def _out_proj(y, w, g, x, *, transposed_in):
    B, S, D = x.shape
    Di = w.shape[0]
    ts = OUT_ROWS
    if transposed_in:
        y_spec = pl.BlockSpec((1, Di, ts), lambda b, i: (b, 0, i))
    else:
        y_spec = pl.BlockSpec((1, ts, Di), lambda b, i: (b, i, 0))
    return pl.pallas_call(
        functools.partial(_out_proj_kernel, transposed_in=transposed_in),
        grid=(B, S // ts),
        in_specs=[y_spec,
                  pl.BlockSpec((Di, D), lambda b, i: (0, 0)),
                  pl.BlockSpec((1, D), lambda b, i: (0, 0)),
                  pl.BlockSpec((1, ts, D), lambda b, i: (b, i, 0))],
        out_specs=pl.BlockSpec((1, ts, D), lambda b, i: (b, i, 0)),
        out_shape=jax.ShapeDtypeStruct((B, S, D), F32),
        compiler_params=_compiler_params(("parallel", "parallel")),
        name="out_proj_t" if transposed_in else "out_proj",
    )(y, w, g.reshape(1, D), x)


def _retention_kernel(q_ref, k_ref, v_ref, g_ref, dmat_ref, xi_ref, zeta_ref,
                      cdec_ref, o_ref, state_ref, state_bf_ref):
    C = RET_CHUNK

    @pl.when(pl.program_id(2) == 0)
    def _():
        state_ref[...] = jnp.zeros_like(state_ref)
        state_bf_ref[...] = jnp.zeros_like(state_bf_ref)

    dmat = dmat_ref[0]
    xi = xi_ref[0]
    zeta = zeta_ref[0]
    cdec = cdec_ref[0]

    for ci in range(RET_CHUNKS_PER_STEP):
        rows = pl.ds(ci * C, C)
        q, k, v = q_ref[0, rows, :], k_ref[0, rows, :], v_ref[0, rows, :]
        scores = lax.dot_general(q, k, (((1,), (1,)), ((), ())),
                                 preferred_element_type=F32) * dmat
        cross = jnp.dot(q, state_bf_ref[...], preferred_element_type=F32) * xi
        kz = (k.astype(F32) * zeta).astype(BF16)
        state = state_ref[...] * cdec + lax.dot_general(
            kz, v, (((0,), (0,)), ((), ())), preferred_element_type=F32)
        state_ref[...] = state
        state_bf_ref[...] = state.astype(BF16)
        inner = jnp.dot(scores.astype(BF16), v, preferred_element_type=F32)
        for r0 in range(0, C, RET_NORM_ROWS):
            blk = slice(r0, r0 + RET_NORM_ROWS)
            ret = inner[blk] + cross[blk]
            mu = jnp.mean(ret, axis=-1, keepdims=True)
            cen = ret - mu
            var = jnp.mean(cen * cen, axis=-1, keepdims=True)
            y = cen * lax.rsqrt(var + GN_EPS)
            out_rows = pl.ds(ci * C + r0, RET_NORM_ROWS)
            o_ref[0, out_rows, :] = (
                g_ref[0, out_rows, :].astype(F32) * y).astype(o_ref.dtype)


def _retention(proj, d_model, d_inner):
    B, S, _ = proj.shape
    H, C = RET_HEADS, RET_CHUNK
    dk, dv = d_model // H, d_inner // H
    rows = C * RET_CHUNKS_PER_STEP
    k_off = d_model // dk
    v_off = 2 * d_model // dv
    g_off = (2 * d_model + d_inner) // dv

    log_gamma = jnp.log1p(-jnp.exp2(-5.0 - jnp.arange(H, dtype=F32)))
    idx = jnp.arange(C, dtype=F32)
    rel = idx[:, None] - idx[None, :]
    dmat = jnp.exp(log_gamma[:, None, None] * jnp.maximum(rel, 0.0)) * (rel >= 0)
    xi = jnp.exp(log_gamma[:, None] * (idx + 1.0))
    zeta = jnp.exp(log_gamma[:, None] * (C - 1.0 - idx))
    cdec = jnp.exp(log_gamma * C)
    xi = jnp.broadcast_to(xi[:, :, None], (H, C, dv))
    zeta = jnp.broadcast_to(zeta[:, :, None], (H, C, dk))
    cdec = jnp.broadcast_to(cdec[:, None, None], (H, 1, dv))

    return pl.pallas_call(
        _retention_kernel,
        grid=(B, H, S // rows),
        in_specs=[
            pl.BlockSpec((1, rows, dk), lambda b, h, n: (b, n, h)),
            pl.BlockSpec((1, rows, dk), lambda b, h, n: (b, n, k_off + h)),
            pl.BlockSpec((1, rows, dv), lambda b, h, n: (b, n, v_off + h)),
            pl.BlockSpec((1, rows, dv), lambda b, h, n: (b, n, g_off + h)),
            pl.BlockSpec((1, C, C), lambda b, h, n: (h, 0, 0)),
            pl.BlockSpec((1, C, dv), lambda b, h, n: (h, 0, 0)),
            pl.BlockSpec((1, C, dk), lambda b, h, n: (h, 0, 0)),
            pl.BlockSpec((1, 1, dv), lambda b, h, n: (h, 0, 0)),
        ],
        out_specs=pl.BlockSpec((1, rows, dv), lambda b, h, n: (b, n, h)),
        out_shape=jax.ShapeDtypeStruct((B, S, d_inner), BF16),
        scratch_shapes=[pltpu.VMEM((dk, dv), F32), pltpu.VMEM((dk, dv), BF16)],
        compiler_params=_compiler_params(("parallel", "parallel", "arbitrary")),
        name="retention",
    )(proj, proj, proj, proj, dmat, xi, zeta, cdec)


def _diff_attn_kernel(tab_ref, bfar_ref, qt_ref, k_ref, vaug_ref, gt_ref, bkd_ref,
                      lq1_ref, lk1_ref, lq2_ref, lk2_ref, sg_ref, o_ref,
                      qbd_ref, bias_ref, m_ref, acc_ref, sa_ref, sb_ref,
                      *, d, lambda_init):
    T, G, H = ATT_TILE, ATT_HEADS_PER_STEP, DIFF_HEADS
    dv = 2 * d
    hp = pl.program_id(1)
    i = pl.program_id(2)

    @pl.when(i == 0)
    def _():
        rows = lax.broadcasted_iota(jnp.int32, (T, T), 0)
        cols = lax.broadcasted_iota(jnp.int32, (T, T), 1)
        bucket_of_distance = bkd_ref[...]
        for u in range(G):
            h = hp * G + u
            cfar = tab_ref[bfar_ref[0] * H + h]
            by_distance = jnp.full((1, 2 * T), tab_ref[h], F32)
            for bucket in range(1, REL_BUCKETS):
                by_distance = jnp.where(bucket_of_distance >= bucket,
                                        tab_ref[bucket * H + h], by_distance)
            by_distance = (by_distance - cfar) * LOG2E
            toeplitz = pltpu.roll(jnp.broadcast_to(by_distance, (T, 2 * T)),
                                  0, 1, stride=1, stride_axis=0)
            bias_ref[u, 0] = jnp.where(rows <= cols, toeplitz[:, 0:T], -jnp.inf)
            bias_ref[u, 1] = toeplitz[:, T:2 * T]

    def logits_into(dst_ref, w, j, bias_index):
        u = w % G
        start = pl.multiple_of(j * T, T)
        kt = k_ref[0, pl.ds(start, T), u * dv:(u + 1) * dv]
        s = jnp.dot(kt, qbd_ref[w], preferred_element_type=F32)
        if bias_index is None:
            dst_ref[w] = s
        else:
            bias = bias_ref[u, bias_index]
            dst_ref[w, :, 0:T] = s[:, 0:T] + bias
            dst_ref[w, :, T:2 * T] = s[:, T:2 * T] + bias

    def consume(src_ref, w, j):
        u = w % G
        start = pl.multiple_of(j * T, T)
        s = src_ref[w]
        m_old = m_ref[w]
        m_new = jnp.maximum(m_old, jnp.max(s, axis=0, keepdims=True))
        p = jnp.exp2(s - m_new).astype(BF16)
        alpha = jnp.exp2(m_old - m_new)
        pv = jnp.dot(vaug_ref[0, u, :, pl.ds(start, T)], p,
                     preferred_element_type=F32)
        acc_ref[w] = acc_ref[w] * alpha + pv
        m_ref[w] = m_new

    first, second, both = range(G), range(G, 2 * G), range(2 * G)

    def step(dst_ref, j_next, bias_of, src_ref, j_cur, cur_units=both):
        for w in both:
            logits_into(dst_ref, w, j_next, bias_of(w))
            if w in cur_units:
                consume(src_ref, w, j_cur)

    no_bias = lambda w: None

    def start():
        for w in both:
            sub, u = divmod(w, G)
            qt = qt_ref[0, u * dv:(u + 1) * dv, sub * T:(sub + 1) * T]
            row = lax.broadcasted_iota(jnp.int32, qt.shape, 0)
            zero = jnp.zeros_like(qt)
            qbd_ref[w, :, 0:T] = jnp.where(row < d, qt, zero)
            qbd_ref[w, :, T:2 * T] = jnp.where(row >= d, qt, zero)
        m_ref[...] = jnp.full_like(m_ref, -jnp.inf)
        acc_ref[...] = jnp.zeros_like(acc_ref)
        for w in second:
            logits_into(sa_ref, w, 2 * i + 1, 0)
        step(sb_ref, 2 * i, lambda w: 0 if w in first else 1,
             sa_ref, 2 * i + 1, cur_units=second)

    def finish(src_ref, j_cur):
        lam = (jnp.exp(jnp.sum(lq1_ref[...] * lk1_ref[...], keepdims=True))
               - jnp.exp(jnp.sum(lq2_ref[...] * lk2_ref[...], keepdims=True))
               + lambda_init)
        out_scale = sg_ref[...] * (1.0 - lambda_init)
        for w in both:
            sub, u = divmod(w, G)
            consume(src_ref, w, j_cur)
            acc = acc_ref[w]
            inv_l = 1.0 / acc[dv:dv + 1, :]
            o = (acc[:dv, :T] * inv_l[:, :T]
                 - acc[:dv, T:] * (lam * inv_l[:, T:]))
            ms = jnp.mean(o * o, axis=0, keepdims=True)
            on = (o * lax.rsqrt(ms + GN_EPS)) * out_scale
            head_rows = slice(u * dv, (u + 1) * dv)
            sub_cols = slice(sub * T, (sub + 1) * T)
            gate = gt_ref[0, head_rows, sub_cols].astype(F32)
            o_ref[0, head_rows, sub_cols] = (gate * on).astype(o_ref.dtype)

    @pl.when(i == 0)
    def _():
        start()
        finish(sb_ref, 2 * i)

    @pl.when(i >= 1)
    def _():
        def prologue():
            start()
            step(sa_ref, 2 * i - 1, lambda w: 1 if w in first else None,
                 sb_ref, 2 * i)
            step(sb_ref, 0, no_bias, sa_ref, 2 * i - 1)

        def pair(r):
            step(sa_ref, 2 * r - 1, no_bias, sb_ref, 2 * r - 2)
            step(sb_ref, 2 * r, no_bias, sa_ref, 2 * r - 1)

        n_pairs = i - 1
        n_trips = n_pairs // 4

        def trip(t):
            for e in range(1, 5):
                pair(4 * t + e)

        @pl.when(n_trips == 0)
        def _():
            prologue()

        @pl.when(n_trips >= 1)
        def _():
            prologue()
            trip(0)

        def trip_body(t, carry):
            trip(t)
            return carry

        lax.fori_loop(1, n_trips, trip_body, 0)

        @pl.when(n_pairs % 4 >= 2)
        def _():
            first_left = (n_pairs // 4) * 4 + 1
            pair(first_left)
            pair(first_left + 1)

        @pl.when((i - 1) % 2 == 1)
        def _():
            pair(i - 1)
            finish(sb_ref, 2 * i - 2)

        @pl.when((i - 1) % 2 == 0)
        def _():
            finish(sb_ref, 2 * i - 2)


def _t5_bucket(n):
    max_exact = REL_BUCKETS // 2
    nf = jnp.maximum(n, max_exact).astype(F32)
    large = max_exact + (jnp.log(nf / max_exact) / math.log(REL_MAX_DIST / max_exact)
                         * (REL_BUCKETS - max_exact)).astype(jnp.int32)
    large = jnp.minimum(large, REL_BUCKETS - 1)
    return jnp.where(n < max_exact, n, large)


def _diff_attn(qt, k, vaug, gt, lq1, lk1, lq2, lk2, subln_g, rel_bias, lambda_init):
    B, S, _ = k.shape
    H, T, G = DIFF_HEADS, ATT_TILE, ATT_HEADS_PER_STEP
    d_inner = gt.shape[1]
    dv = d_inner // H
    d = dv // 2
    gw = G * dv
    n_groups = H // G

    bkd = _t5_bucket(jnp.arange(2 * T, dtype=jnp.int32)[None, :])
    bfar = _t5_bucket(jnp.full((1,), T + 1, jnp.int32))

    smem = pl.BlockSpec(memory_space=pltpu.SMEM)
    vec = pl.BlockSpec((1, d), lambda b, h, i: (0, 0))
    return pl.pallas_call(
        functools.partial(_diff_attn_kernel, d=d, lambda_init=lambda_init),
        grid=(B, n_groups, S // (2 * T)),
        in_specs=[
            smem, smem,
            pl.BlockSpec((1, gw, 2 * T), lambda b, h, i: (b, h, i)),
            pl.BlockSpec((1, S, gw), lambda b, h, i: (b, 0, h)),
            pl.BlockSpec((1, G, dv + ONES_ROWS, S), lambda b, h, i: (b, h, 0, 0)),
            pl.BlockSpec((1, gw, 2 * T), lambda b, h, i: (b, h, i)),
            pl.BlockSpec((1, 2 * T), lambda b, h, i: (0, 0)),
            vec, vec, vec, vec,
            pl.BlockSpec((dv, 1), lambda b, h, i: (0, 0)),
        ],
        out_specs=pl.BlockSpec((1, gw, 2 * T), lambda b, h, i: (b, h, i)),
        out_shape=jax.ShapeDtypeStruct((B, d_inner, S), BF16),
        scratch_shapes=[pltpu.VMEM((2 * G, dv, 2 * T), BF16),
                        pltpu.VMEM((G, 2, T, T), F32),
                        pltpu.VMEM((2 * G, 1, 2 * T), F32),
                        pltpu.VMEM((2 * G, dv + ONES_ROWS, 2 * T), F32),
                        pltpu.VMEM((2 * G, T, 2 * T), F32),
                        pltpu.VMEM((2 * G, T, 2 * T), F32)],
        compiler_params=_compiler_params(("parallel", "arbitrary", "arbitrary")),
        name="diff_attn",
    )(rel_bias.astype(F32).reshape(-1), bfar, qt, k, vaug, gt, bkd,
      lq1.reshape(1, d), lk1.reshape(1, d), lq2.reshape(1, d), lk2.reshape(1, d),
      subln_g.reshape(dv, 1))


def kernel(x, pre_norm_g, post_norm_g, ret_w_in, ret_w_out, diff_w_in, diff_w_out,
           diff_lambda_q1, diff_lambda_k1, diff_lambda_q2, diff_lambda_k2,
           diff_subln_g, rel_bias):
    d_model = x.shape[-1]
    d_inner = ret_w_out.shape[1]
    for i in range(DEPTH):
        j = i // N_MIXERS
        if i % N_MIXERS == 0:
            proj = _norm_matmul_ret(x, pre_norm_g[i], ret_w_in[j].astype(BF16),
                                    d_inner, RET_HEADS)
            y = _retention(proj, d_model, d_inner)
            x = _out_proj(y, ret_w_out[j].astype(BF16), post_norm_g[i], x,
                          transposed_in=False)
        else:
            lambda_init = 0.8 - 0.6 * math.exp(-0.3 * i)
            head_dim = d_inner // DIFF_HEADS // 2
            qt, k, vaug, gt = _norm_matmul_diff(
                x, pre_norm_g[i], diff_w_in[j].astype(BF16), DIFF_HEADS,
                q_scale=head_dim ** -0.5 * LOG2E)
            yt = _diff_attn(qt, k, vaug, gt, diff_lambda_q1[j], diff_lambda_k1[j],
                            diff_lambda_q2[j], diff_lambda_k2[j], diff_subln_g[j],
                            rel_bias, lambda_init)
            x = _out_proj(yt, diff_w_out[j].astype(BF16), post_norm_g[i], x,
                          transposed_in=True)
    return x
```

```python
import functools
import math

import jax
import jax.numpy as jnp
import numpy as np
from jax import lax
from jax.experimental import pallas as pl
from jax.experimental.pallas import tpu as pltpu

F32 = jnp.float32
BF16 = jnp.bfloat16

DEPTH = 2
N_MIXERS = 2
RET_HEADS = 4
DIFF_HEADS = 16
REL_BUCKETS = 32
REL_MAX_DIST = 128
ROPE_BASE = 10000.0
RMS_EPS = 1e-6
GN_EPS = 1e-5

V7X_VMEM_BYTES = 64 * 1024 * 1024
VMEM_LIMIT_BYTES = V7X_VMEM_BYTES * 15 // 16

PROJ_ROWS = 1024
PROJ_COLS = 2048
OUT_ROWS = 1024
RET_CHUNK = 256
RET_CHUNKS_PER_STEP = 8
RET_NORM_ROWS = 32
ATT_TILE = 256
ATT_HEADS_PER_STEP = 4
ONES_ROWS = 16
LOG2E = math.log2(math.e)


def _compiler_params(semantics):
    return pltpu.CompilerParams(dimension_semantics=semantics,
                                vmem_limit_bytes=VMEM_LIMIT_BYTES)


def _store_normed(x_ref, g_ref, h_ref):
    x = x_ref[0]
    ms = jnp.mean(x * x, axis=-1, keepdims=True)
    h_ref[...] = (x * lax.rsqrt(ms + RMS_EPS) * g_ref[...]).astype(BF16)


def _silu(x):
    return x * jax.nn.sigmoid(x)


def _norm_matmul_ret_kernel(x_ref, g_ref, w_ref, cos_ref, sin_ref, o_ref, h_ref,
                            *, dk):
    j = pl.program_id(2)

    def project(cols=slice(None)):
        return jnp.dot(h_ref[...], w_ref[:, cols], preferred_element_type=F32)

    @pl.when(j == 0)
    def _():
        _store_normed(x_ref, g_ref, h_ref)
        half = dk // 2
        n_blocks = w_ref.shape[1] // dk
        c, s = cos_ref[...], sin_ref[...]
        for blk in range(n_blocks):
            o = project(slice(blk * dk, (blk + 1) * dk))
            scale = 1.0 if blk < n_blocks // 2 else dk ** -0.5
            x1, x2 = o[:, :half], o[:, half:]
            o_ref[0, :, blk * dk:blk * dk + half] = (
                (x1 * c - x2 * s) * scale).astype(BF16)
            o_ref[0, :, blk * dk + half:(blk + 1) * dk] = (
                (x1 * s + x2 * c) * scale).astype(BF16)

    @pl.when(j == 1)
    def _():
        o_ref[0] = project().astype(BF16)

    @pl.when(j == 2)
    def _():
        o_ref[0] = _silu(project()).astype(BF16)


def _norm_matmul_ret(x, g, w, d_inner, n_heads):
    B, S, D = x.shape
    N = w.shape[1]
    ts, tn = PROJ_ROWS, d_inner
    dk = (N - 2 * d_inner) // 2 // n_heads
    half = dk // 2
    assert N == 3 * tn, "q | k must fill exactly one column tile"

    pos = np.arange(S, dtype=np.float64)
    inv = ROPE_BASE ** (-np.arange(half, dtype=np.float64) / half)
    ang = pos[:, None] * inv[None, :]
    cos = jnp.asarray(np.cos(ang), dtype=F32)
    sin = jnp.asarray(np.sin(ang), dtype=F32)

    return pl.pallas_call(
        functools.partial(_norm_matmul_ret_kernel, dk=dk),
        grid=(B, S // ts, N // tn),
        in_specs=[pl.BlockSpec((1, ts, D), lambda b, i, j: (b, i, 0)),
                  pl.BlockSpec((1, D), lambda b, i, j: (0, 0)),
                  pl.BlockSpec((D, tn), lambda b, i, j: (0, j)),
                  pl.BlockSpec((ts, half), lambda b, i, j: (i, 0)),
                  pl.BlockSpec((ts, half), lambda b, i, j: (i, 0))],
        out_specs=pl.BlockSpec((1, ts, tn), lambda b, i, j: (b, i, j)),
        out_shape=jax.ShapeDtypeStruct((B, S, N), BF16),
        scratch_shapes=[pltpu.VMEM((ts, D), BF16)],
        compiler_params=_compiler_params(("parallel", "parallel", "arbitrary")),
        name="norm_matmul_ret",
    )(x, g.reshape(1, D), w, cos, sin)


def _norm_matmul_diff_kernel(x_ref, g_ref, w_ref, qk_ref, vaug_ref, gt_ref, h_ref):
    j = pl.program_id(2)

    @pl.when(j == 0)
    def _():
        _store_normed(x_ref, g_ref, h_ref)

    def project():
        return jnp.dot(h_ref[...], w_ref[...], preferred_element_type=F32)

    @pl.when(j < 2)
    def _():
        qk_ref[0] = project().astype(BF16)

    @pl.when(j == 2)
    def _():
        n_heads, rows_aug, ts = vaug_ref.shape[1:]
        dv = w_ref.shape[1] // n_heads
        for h in range(0, n_heads, 2):
            ot = jnp.dot(h_ref[...], w_ref[:, h * dv:(h + 2) * dv],
                         preferred_element_type=F32).T.astype(BF16)
            for e in range(2):
                vaug_ref[0, h + e, 0:dv, :] = ot[e * dv:(e + 1) * dv, :]
                vaug_ref[0, h + e, dv:rows_aug, :] = jnp.ones(
                    (rows_aug - dv, ts), BF16)

    @pl.when(j == 3)
    def _():
        gt_ref[0] = _silu(project()).T.astype(BF16)


def _norm_matmul_diff(x, g, w, n_heads):
    B, S, D = x.shape
    Di = w.shape[1] // 4
    dv = Di // n_heads
    ts = PROJ_ROWS
    return pl.pallas_call(
        _norm_matmul_diff_kernel,
        grid=(B, S // ts, 4),
        in_specs=[pl.BlockSpec((1, ts, D), lambda b, i, j: (b, i, 0)),
                  pl.BlockSpec((1, D), lambda b, i, j: (0, 0)),
                  pl.BlockSpec((D, Di), lambda b, i, j: (0, j))],
        out_specs=[
            pl.BlockSpec((1, ts, Di), lambda b, i, j: (b, i, jnp.minimum(j, 1))),
            pl.BlockSpec((1, n_heads, dv + ONES_ROWS, ts),
                         lambda b, i, j: (b, 0, 0, i)),
            pl.BlockSpec((1, Di, ts), lambda b, i, j: (b, 0, i)),
        ],
        out_shape=[jax.ShapeDtypeStruct((B, S, 2 * Di), BF16),
                   jax.ShapeDtypeStruct((B, n_heads, dv + ONES_ROWS, S), BF16),
                   jax.ShapeDtypeStruct((B, Di, S), BF16)],
        scratch_shapes=[pltpu.VMEM((ts, D), BF16)],
        compiler_params=_compiler_params(("parallel", "parallel", "arbitrary")),
        name="norm_matmul_diff",
    )(x, g.reshape(1, D), w)


def _out_proj_kernel(y_ref, w_ref, g_ref, x_ref, o_ref, *, transposed_in):
    y = y_ref[0]
    if transposed_in:
        o = lax.dot_general(y, w_ref[...], (((0,), (0,)), ((), ())),
                            preferred_element_type=F32)
    else:
        o = jnp.dot(y, w_ref[...], preferred_element_type=F32)
    ms = jnp.mean(o * o, axis=-1, keepdims=True)
    o_ref[0] = x_ref[0] + o * lax.rsqrt(ms + RMS_EPS) * g_ref[...]


def _out_proj(y, w, g, x, *, transposed_in):
    B, S, D = x.shape
    Di = w.shape[0]
    ts = OUT_ROWS
    if transposed_in:
        y_spec = pl.BlockSpec((1, Di, ts), lambda b, i: (b, 0, i))
    else:
        y_spec = pl.BlockSpec((1, ts, Di), lambda b, i: (b, i, 0))
    return pl.pallas_call(
        functools.partial(_out_proj_kernel, transposed_in=transposed_in),
        grid=(B, S // ts),
        in_specs=[y_spec,
                  pl.BlockSpec((Di, D), lambda b, i: (0, 0)),
                  pl.BlockSpec((1, D), lambda b, i: (0, 0)),
                  pl.BlockSpec((1, ts, D), lambda b, i: (b, i, 0))],
        out_specs=pl.BlockSpec((1, ts, D), lambda b, i: (b, i, 0)),
        out_shape=jax.ShapeDtypeStruct((B, S, D), F32),
        compiler_params=_compiler_params(("parallel", "parallel")),
        name="out_proj_t" if transposed_in else "out_proj",
    )(y, w, g.reshape(1, D), x)


def _retention_kernel(q_ref, k_ref, v_ref, g_ref, dmat_ref, xi_ref, zeta_ref,
                      cdec_ref, o_ref, state_ref, state_bf_ref):
    C = RET_CHUNK

    @pl.when(pl.program_id(2) == 0)
    def _():
        state_ref[...] = jnp.zeros_like(state_ref)
        state_bf_ref[...] = jnp.zeros_like(state_bf_ref)

    dmat = dmat_ref[0]
    xi = xi_ref[0]
    zeta = zeta_ref[0]
    cdec = cdec_ref[0]

    for ci in range(RET_CHUNKS_PER_STEP):
        rows = pl.ds(ci * C, C)
        q, k, v = q_ref[0, rows, :], k_ref[0, rows, :], v_ref[0, rows, :]
        scores = lax.dot_general(q, k, (((1,), (1,)), ((), ())),
                                 preferred_element_type=F32) * dmat
        cross = jnp.dot(q, state_bf_ref[...], preferred_element_type=F32) * xi
        kz = (k.astype(F32) * zeta).astype(BF16)
        state = state_ref[...] * cdec + lax.dot_general(
            kz, v, (((0,), (0,)), ((), ())), preferred_element_type=F32)
        state_ref[...] = state
        state_bf_ref[...] = state.astype(BF16)
        inner = jnp.dot(scores.astype(BF16), v, preferred_element_type=F32)
        for r0 in range(0, C, RET_NORM_ROWS):
            blk = slice(r0, r0 + RET_NORM_ROWS)
            ret = inner[blk] + cross[blk]
            mu = jnp.mean(ret, axis=-1, keepdims=True)
            cen = ret - mu
            var = jnp.mean(cen * cen, axis=-1, keepdims=True)
            y = cen * lax.rsqrt(var + GN_EPS)
            out_rows = pl.ds(ci * C + r0, RET_NORM_ROWS)
            o_ref[0, out_rows, :] = (
                g_ref[0, out_rows, :].astype(F32) * y).astype(o_ref.dtype)


def _retention(proj, d_model, d_inner):
    B, S, _ = proj.shape
    H, C = RET_HEADS, RET_CHUNK
    dk, dv = d_model // H, d_inner // H
    rows = C * RET_CHUNKS_PER_STEP
    k_off = d_model // dk
    v_off = 2 * d_model // dv
    g_off = (2 * d_model + d_inner) // dv

    log_gamma = jnp.log1p(-jnp.exp2(-5.0 - jnp.arange(H, dtype=F32)))
    idx = jnp.arange(C, dtype=F32)
    rel = idx[:, None] - idx[None, :]
    dmat = jnp.exp(log_gamma[:, None, None] * jnp.maximum(rel, 0.0)) * (rel >= 0)
    xi = jnp.exp(log_gamma[:, None] * (idx + 1.0))
    zeta = jnp.exp(log_gamma[:, None] * (C - 1.0 - idx))
    cdec = jnp.exp(log_gamma * C)
    xi = jnp.broadcast_to(xi[:, :, None], (H, C, dv))
    zeta = jnp.broadcast_to(zeta[:, :, None], (H, C, dk))
    cdec = jnp.broadcast_to(cdec[:, None, None], (H, 1, dv))

    return pl.pallas_call(
        _retention_kernel,
        grid=(B, H, S // rows),
        in_specs=[
            pl.BlockSpec((1, rows, dk), lambda b, h, n: (b, n, h)),
            pl.BlockSpec((1, rows, dk), lambda b, h, n: (b, n, k_off + h)),
            pl.BlockSpec((1, rows, dv), lambda b, h, n: (b, n, v_off + h)),
            pl.BlockSpec((1, rows, dv), lambda b, h, n: (b, n, g_off + h)),
            pl.BlockSpec((1, C, C), lambda b, h, n: (h, 0, 0)),
            pl.BlockSpec((1, C, dv), lambda b, h, n: (h, 0, 0)),
            pl.BlockSpec((1, C, dk), lambda b, h, n: (h, 0, 0)),
            pl.BlockSpec((1, 1, dv), lambda b, h, n: (h, 0, 0)),
        ],
        out_specs=pl.BlockSpec((1, rows, dv), lambda b, h, n: (b, n, h)),
        out_shape=jax.ShapeDtypeStruct((B, S, d_inner), BF16),
        scratch_shapes=[pltpu.VMEM((dk, dv), F32), pltpu.VMEM((dk, dv), BF16)],
        compiler_params=_compiler_params(("parallel", "parallel", "arbitrary")),
        name="retention",
    )(proj, proj, proj, proj, dmat, xi, zeta, cdec)


def _diff_attn_kernel(tab_ref, bfar_ref, q_ref, k_ref, vaug_ref, gt_ref, bkd_ref,
                      lq1_ref, lk1_ref, lq2_ref, lk2_ref, sg_ref, o_ref,
                      qbd_ref, bias_ref, m_ref, l_ref, acc_ref, sa_ref, sb_ref,
                      *, d, lambda_init):
    T, G, H = ATT_TILE, ATT_HEADS_PER_STEP, DIFF_HEADS
    dv = 2 * d
    hp = pl.program_id(1)
    i = pl.program_id(2)

    @pl.when(i == 0)
    def _():
        rows = lax.broadcasted_iota(jnp.int32, (T, T), 0)
        cols = lax.broadcasted_iota(jnp.int32, (T, T), 1)
        bucket_of_distance = bkd_ref[...]
        for u in range(G):
            h = hp * G + u
            cfar = tab_ref[bfar_ref[0] * H + h]
            by_distance = jnp.full((1, 2 * T), tab_ref[h], F32)
            for bucket in range(1, REL_BUCKETS):
                by_distance = jnp.where(bucket_of_distance >= bucket,
                                        tab_ref[bucket * H + h], by_distance)
            by_distance = (by_distance - cfar) * LOG2E
            toeplitz = pltpu.roll(jnp.broadcast_to(by_distance, (T, 2 * T)),
                                  0, 1, stride=1, stride_axis=0)
            bias_ref[u, 0] = jnp.where(rows <= cols, toeplitz[:, 0:T], -jnp.inf)
            bias_ref[u, 1] = toeplitz[:, T:2 * T]

    def logits_into(dst_ref, w, j, bias_index):
        u = w % G
        start = pl.multiple_of(j * T, T)
        kt = k_ref[0, pl.ds(start, T), u * dv:(u + 1) * dv]
        s = jnp.dot(kt, qbd_ref[w], preferred_element_type=F32)
        if bias_index is None:
            dst_ref[w] = s
        else:
            bias = bias_ref[u, bias_index]
            dst_ref[w, :, 0:T] = s[:, 0:T] + bias
            dst_ref[w, :, T:2 * T] = s[:, T:2 * T] + bias

    def consume(src_ref, w, j):
        u = w % G
        start = pl.multiple_of(j * T, T)
        s = src_ref[w]
        m_old = m_ref[w]
        m_new = jnp.maximum(m_old, jnp.max(s, axis=0, keepdims=True))
        p = jnp.exp2(s - m_new)
        alpha = jnp.exp2(m_old - m_new)
        l_ref[w] = l_ref[w] * alpha + jnp.sum(p, axis=0, keepdims=True)
        pv = jnp.dot(vaug_ref[0, u, 0:dv, pl.ds(start, T)], p.astype(BF16),
                     preferred_element_type=F32)
        acc_ref[w] = acc_ref[w] * alpha + pv
        m_ref[w] = m_new

    first, second, both = range(G), range(G, 2 * G), range(2 * G)

    def step(dst_ref, j_next, bias_of, src_ref, j_cur, cur_units=both):
        for w in both:
            logits_into(dst_ref, w, j_next, bias_of(w))
            if w in cur_units:
                consume(src_ref, w, j_cur)

    no_bias = lambda w: None

    def start():
        for w in both:
            sub, u = divmod(w, G)
            q = q_ref[0, sub * T:(sub + 1) * T, u * dv:(u + 1) * dv].astype(F32)
            qt = (q * (d ** -0.5 * LOG2E)).T
            row = lax.broadcasted_iota(jnp.int32, qt.shape, 0)
            qbd_ref[w, :, 0:T] = jnp.where(row < d, qt, 0.0).astype(BF16)
            qbd_ref[w, :, T:2 * T] = jnp.where(row >= d, qt, 0.0).astype(BF16)
        m_ref[...] = jnp.full_like(m_ref, -jnp.inf)
        acc_ref[...] = jnp.zeros_like(acc_ref)
        l_ref[...] = jnp.zeros_like(l_ref)
        for w in second:
            logits_into(sa_ref, w, 2 * i + 1, 0)
        step(sb_ref, 2 * i, lambda w: 0 if w in first else 1,
             sa_ref, 2 * i + 1, cur_units=second)

    def finish(src_ref, j_cur):
        lam = (jnp.exp(jnp.sum(lq1_ref[...] * lk1_ref[...], keepdims=True))
               - jnp.exp(jnp.sum(lq2_ref[...] * lk2_ref[...], keepdims=True))
               + lambda_init)
        out_scale = sg_ref[...] * (1.0 - lambda_init)
        for w in both:
            sub, u = divmod(w, G)
            consume(src_ref, w, j_cur)
            acc = acc_ref[w]
            inv_l = 1.0 / l_ref[w]
            o = (acc[:, :T] * inv_l[:, :T]
                 - acc[:, T:] * (lam * inv_l[:, T:]))
            ms = jnp.mean(o * o, axis=0, keepdims=True)
            on = (o * lax.rsqrt(ms + GN_EPS)) * out_scale
            head_rows = slice(u * dv, (u + 1) * dv)
            sub_cols = slice(sub * T, (sub + 1) * T)
            gate = gt_ref[0, head_rows, sub_cols].astype(F32)
            o_ref[0, head_rows, sub_cols] = (gate * on).astype(o_ref.dtype)

    @pl.when(i == 0)
    def _():
        start()
        finish(sb_ref, 2 * i)

    @pl.when(i >= 1)
    def _():
        def prologue():
            start()
            step(sa_ref, 2 * i - 1, lambda w: 1 if w in first else None,
                 sb_ref, 2 * i)
            step(sb_ref, 0, no_bias, sa_ref, 2 * i - 1)

        def pair(r):
            step(sa_ref, 2 * r - 1, no_bias, sb_ref, 2 * r - 2)
            step(sb_ref, 2 * r, no_bias, sa_ref, 2 * r - 1)

        n_pairs = i - 1
        n_trips = n_pairs // 4

        def trip(t):
            for e in range(1, 5):
                pair(4 * t + e)

        @pl.when(n_trips == 0)
        def _():
            prologue()

        @pl.when(n_trips >= 1)
        def _():
            prologue()
            trip(0)

        def trip_body(t, carry):
            trip(t)
            return carry

        lax.fori_loop(1, n_trips, trip_body, 0)

        @pl.when(n_pairs % 4 >= 2)
        def _():
            first_left = (n_pairs // 4) * 4 + 1
            pair(first_left)
            pair(first_left + 1)

        @pl.when((i - 1) % 2 == 1)
        def _():
            pair(i - 1)
            finish(sb_ref, 2 * i - 2)

        @pl.when((i - 1) % 2 == 0)
        def _():
            finish(sb_ref, 2 * i - 2)


def _t5_bucket(n):
    max_exact = REL_BUCKETS // 2
    nf = jnp.maximum(n, max_exact).astype(F32)
    large = max_exact + (jnp.log(nf / max_exact) / math.log(REL_MAX_DIST / max_exact)
                         * (REL_BUCKETS - max_exact)).astype(jnp.int32)
    large = jnp.minimum(large, REL_BUCKETS - 1)
    return jnp.where(n < max_exact, n, large)


def _diff_attn(qk, vaug, gt, lq1, lk1, lq2, lk2, subln_g, rel_bias, lambda_init):
    B, S, _ = qk.shape
    H, T, G = DIFF_HEADS, ATT_TILE, ATT_HEADS_PER_STEP
    d_inner = gt.shape[1]
    dv = d_inner // H
    d = dv // 2
    gw = G * dv
    n_groups = H // G

    bkd = _t5_bucket(jnp.arange(2 * T, dtype=jnp.int32)[None, :])
    bfar = _t5_bucket(jnp.full((1,), T + 1, jnp.int32))

    smem = pl.BlockSpec(memory_space=pltpu.SMEM)
    vec = pl.BlockSpec((1, d), lambda b, h, i: (0, 0))
    return pl.pallas_call(
        functools.partial(_diff_attn_kernel, d=d, lambda_init=lambda_init),
        grid=(B, n_groups, S // (2 * T)),
        in_specs=[
            smem, smem,
            pl.BlockSpec((1, 2 * T, gw), lambda b, h, i: (b, i, h)),
            pl.BlockSpec((1, S, gw), lambda b, h, i: (b, 0, n_groups + h)),
            pl.BlockSpec((1, G, dv + ONES_ROWS, S), lambda b, h, i: (b, h, 0, 0)),
            pl.BlockSpec((1, gw, 2 * T), lambda b, h, i: (b, h, i)),
            pl.BlockSpec((1, 2 * T), lambda b, h, i: (0, 0)),
            vec, vec, vec, vec,
            pl.BlockSpec((dv, 1), lambda b, h, i: (0, 0)),
        ],
        out_specs=pl.BlockSpec((1, gw, 2 * T), lambda b, h, i: (b, h, i)),
        out_shape=jax.ShapeDtypeStruct((B, d_inner, S), BF16),
        scratch_shapes=[pltpu.VMEM((2 * G, dv, 2 * T), BF16),
                        pltpu.VMEM((G, 2, T, T), F32),
                        pltpu.VMEM((2 * G, 1, 2 * T), F32),
                        pltpu.VMEM((2 * G, 1, 2 * T), F32),
                        pltpu.VMEM((2 * G, dv, 2 * T), F32),
                        pltpu.VMEM((2 * G, T, 2 * T), F32),
                        pltpu.VMEM((2 * G, T, 2 * T), F32)],
        compiler_params=_compiler_params(("parallel", "arbitrary", "arbitrary")),
        name="diff_attn",
    )(rel_bias.astype(F32).reshape(-1), bfar, qk, qk, vaug, gt, bkd,
      lq1.reshape(1, d), lk1.reshape(1, d), lq2.reshape(1, d), lk2.reshape(1, d),
      subln_g.reshape(dv, 1))


def kernel(x, pre_norm_g, post_norm_g, ret_w_in, ret_w_out, diff_w_in, diff_w_out,
           diff_lambda_q1, diff_lambda_k1, diff_lambda_q2, diff_lambda_k2,
           diff_subln_g, rel_bias):
    d_model = x.shape[-1]
    d_inner = ret_w_out.shape[1]
    for i in range(DEPTH):
        j = i // N_MIXERS
        if i % N_MIXERS == 0:
            proj = _norm_matmul_ret(x, pre_norm_g[i], ret_w_in[j].astype(BF16),
                                    d_inner, RET_HEADS)
            y = _retention(proj, d_model, d_inner)
            x = _out_proj(y, ret_w_out[j].astype(BF16), post_norm_g[i], x,
                          transposed_in=False)
        else:
            lambda_init = 0.8 - 0.6 * math.exp(-0.3 * i)
            qk, vaug, gt = _norm_matmul_diff(
                x, pre_norm_g[i], diff_w_in[j].astype(BF16), DIFF_HEADS)
            yt = _diff_attn(qk, vaug, gt, diff_lambda_q1[j], diff_lambda_k1[j],
                            diff_lambda_q2[j], diff_lambda_k2[j], diff_subln_g[j],
                            rel_bias, lambda_init)
            x = _out_proj(yt, diff_w_out[j].astype(BF16), post_norm_g[i], x,
                          transposed_in=True)
    return x
```

```python
import functools
import math

import jax
import jax.numpy as jnp
import numpy as np
from jax import lax
from jax.experimental import pallas as pl
from jax.experimental.pallas import tpu as pltpu

F32 = jnp.float32
BF16 = jnp.bfloat16

DEPTH = 2
N_MIXERS = 2
RET_HEADS = 4
DIFF_HEADS = 16
REL_BUCKETS = 32
REL_MAX_DIST = 128
ROPE_BASE = 10000.0
RMS_EPS = 1e-6
GN_EPS = 1e-5

V7X_VMEM_BYTES = 64 * 1024 * 1024
VMEM_LIMIT_BYTES = V7X_VMEM_BYTES * 15 // 16

PROJ_ROWS = 1024
PROJ_COLS = 2048
OUT_ROWS = 1024
RET_CHUNK = 256
RET_CHUNKS_PER_STEP = 8
RET_NORM_ROWS = 32
ATT_TILE = 256
ATT_HEADS_PER_STEP = 4
ONES_ROWS = 16
LOG2E = math.log2(math.e)


def _compiler_params(semantics):
    return pltpu.CompilerParams(dimension_semantics=semantics,
                                vmem_limit_bytes=VMEM_LIMIT_BYTES)


def _store_normed(x_ref, g_ref, h_ref):
    x = x_ref[0]
    ms = jnp.mean(x * x, axis=-1, keepdims=True)
    h_ref[...] = (x * lax.rsqrt(ms + RMS_EPS) * g_ref[...]).astype(BF16)


def _silu(x):
    return x * jax.nn.sigmoid(x)


def _norm_matmul_ret_kernel(x_ref, g_ref, w_ref, cos_ref, sin_ref, o_ref, h_ref,
                            *, dk):
    j = pl.program_id(2)

    def project(cols=slice(None)):
        return jnp.dot(h_ref[...], w_ref[:, cols], preferred_element_type=F32)

    @pl.when(j == 0)
    def _():
        _store_normed(x_ref, g_ref, h_ref)
        half = dk // 2
        n_blocks = w_ref.shape[1] // dk
        c, s = cos_ref[...], sin_ref[...]
        for blk in range(n_blocks):
            o = project(slice(blk * dk, (blk + 1) * dk))
            scale = 1.0 if blk < n_blocks // 2 else dk ** -0.5
            x1, x2 = o[:, :half], o[:, half:]
            o_ref[0, :, blk * dk:blk * dk + half] = (
                (x1 * c - x2 * s) * scale).astype(BF16)
            o_ref[0, :, blk * dk + half:(blk + 1) * dk] = (
                (x1 * s + x2 * c) * scale).astype(BF16)

    @pl.when(j == 1)
    def _():
        o_ref[0] = project().astype(BF16)

    @pl.when(j == 2)
    def _():
        o_ref[0] = _silu(project()).astype(BF16)


def _norm_matmul_ret(x, g, w, d_inner, n_heads):
    B, S, D = x.shape
    N = w.shape[1]
    ts, tn = PROJ_ROWS, d_inner
    dk = (N - 2 * d_inner) // 2 // n_heads
    half = dk // 2
    assert N == 3 * tn, "q | k must fill exactly one column tile"

    pos = np.arange(S, dtype=np.float64)
    inv = ROPE_BASE ** (-np.arange(half, dtype=np.float64) / half)
    ang = pos[:, None] * inv[None, :]
    cos = jnp.asarray(np.cos(ang), dtype=F32)
    sin = jnp.asarray(np.sin(ang), dtype=F32)

    return pl.pallas_call(
        functools.partial(_norm_matmul_ret_kernel, dk=dk),
        grid=(B, S // ts, N // tn),
        in_specs=[pl.BlockSpec((1, ts, D), lambda b, i, j: (b, i, 0)),
                  pl.BlockSpec((1, D), lambda b, i, j: (0, 0)),
                  pl.BlockSpec((D, tn), lambda b, i, j: (0, j)),
                  pl.BlockSpec((ts, half), lambda b, i, j: (i, 0)),
                  pl.BlockSpec((ts, half), lambda b, i, j: (i, 0))],
        out_specs=pl.BlockSpec((1, ts, tn), lambda b, i, j: (b, i, j)),
        out_shape=jax.ShapeDtypeStruct((B, S, N), BF16),
        scratch_shapes=[pltpu.VMEM((ts, D), BF16)],
        compiler_params=_compiler_params(("parallel", "parallel", "arbitrary")),
        name="norm_matmul_ret",
    )(x, g.reshape(1, D), w, cos, sin)


def _norm_matmul_diff_kernel(x_ref, g_ref, w_ref, qk_ref, vaug_ref, gt_ref, h_ref):
    j = pl.program_id(2)

    @pl.when(j == 0)
    def _():
        _store_normed(x_ref, g_ref, h_ref)

    def project():
        return jnp.dot(h_ref[...], w_ref[...], preferred_element_type=F32)

    @pl.when(j < 2)
    def _():
        qk_ref[0] = project().astype(BF16)

    @pl.when(j == 2)
    def _():
        n_heads, rows_aug, ts = vaug_ref.shape[1:]
        dv = w_ref.shape[1] // n_heads
        for h in range(0, n_heads, 2):
            ot = jnp.dot(h_ref[...], w_ref[:, h * dv:(h + 2) * dv],
                         preferred_element_type=F32).T.astype(BF16)
            for e in range(2):
                vaug_ref[0, h + e, 0:dv, :] = ot[e * dv:(e + 1) * dv, :]
                vaug_ref[0, h + e, dv:rows_aug, :] = jnp.ones(
                    (rows_aug - dv, ts), BF16)

    @pl.when(j == 3)
    def _():
        gt_ref[0] = _silu(project()).T.astype(BF16)


def _norm_matmul_diff(x, g, w, n_heads):
    B, S, D = x.shape
    Di = w.shape[1] // 4
    dv = Di // n_heads
    ts = PROJ_ROWS
    return pl.pallas_call(
        _norm_matmul_diff_kernel,
        grid=(B, S // ts, 4),
        in_specs=[pl.BlockSpec((1, ts, D), lambda b, i, j: (b, i, 0)),
                  pl.BlockSpec((1, D), lambda b, i, j: (0, 0)),
                  pl.BlockSpec((D, Di), lambda b, i, j: (0, j))],
        out_specs=[
            pl.BlockSpec((1, ts, Di), lambda b, i, j: (b, i, jnp.minimum(j, 1))),
            pl.BlockSpec((1, n_heads, dv + ONES_ROWS, ts),
                         lambda b, i, j: (b, 0, 0, i)),
            pl.BlockSpec((1, Di, ts), lambda b, i, j: (b, 0, i)),
        ],
        out_shape=[jax.ShapeDtypeStruct((B, S, 2 * Di), BF16),
                   jax.ShapeDtypeStruct((B, n_heads, dv + ONES_ROWS, S), BF16),
                   jax.ShapeDtypeStruct((B, Di, S), BF16)],
        scratch_shapes=[pltpu.VMEM((ts, D), BF16)],
        compiler_params=_compiler_params(("parallel", "parallel", "arbitrary")),
        name="norm_matmul_diff",
    )(x, g.reshape(1, D), w)


def _out_proj_kernel(y_ref, w_ref, g_ref, x_ref, o_ref, *, transposed_in):
    y = y_ref[0]
    if transposed_in:
        o = lax.dot_general(y, w_ref[...], (((0,), (0,)), ((), ())),
                            preferred_element_type=F32)
    else:
        o = jnp.dot(y, w_ref[...], preferred_element_type=F32)
    ms = jnp.mean(o * o, axis=-1, keepdims=True)
    o_ref[0] = x_ref[0] + o * lax.rsqrt(ms + RMS_EPS) * g_ref[...]


def _out_proj(y, w, g, x, *, transposed_in):
    B, S, D = x.shape
    Di = w.shape[0]
    ts = OUT_ROWS
    if transposed_in:
        y_spec = pl.BlockSpec((1, Di, ts), lambda b, i: (b, 0, i))
    else:
        y_spec = pl.BlockSpec((1, ts, Di), lambda b, i: (b, i, 0))
    return pl.pallas_call(
        functools.partial(_out_proj_kernel, transposed_in=transposed_in),
        grid=(B, S // ts),
        in_specs=[y_spec,
                  pl.BlockSpec((Di, D), lambda b, i: (0, 0)),
                  pl.BlockSpec((1, D), lambda b, i: (0, 0)),
                  pl.BlockSpec((1, ts, D), lambda b, i: (b, i, 0))],
        out_specs=pl.BlockSpec((1, ts, D), lambda b, i: (b, i, 0)),
        out_shape=jax.ShapeDtypeStruct((B, S, D), F32),
        compiler_params=_compiler_params(("parallel", "parallel")),
        name="out_proj_t" if transposed_in else "out_proj",
    )(y, w, g.reshape(1, D), x)


def _retention_kernel(q_ref, k_ref, v_ref, g_ref, dmat_ref, xi_ref, zeta_ref,
                      cdec_ref, o_ref, state_ref, state_bf_ref):
    C = RET_CHUNK

    @pl.when(pl.program_id(2) == 0)
    def _():
        state_ref[...] = jnp.zeros_like(state_ref)
        state_bf_ref[...] = jnp.zeros_like(state_bf_ref)

    dmat = dmat_ref[0]
    xi = xi_ref[0]
    zeta = zeta_ref[0]
    cdec = cdec_ref[0]

    for ci in range(RET_CHUNKS_PER_STEP):
        rows = pl.ds(ci * C, C)
        q, k, v = q_ref[0, rows, :], k_ref[0, rows, :], v_ref[0, rows, :]
        scores = lax.dot_general(q, k, (((1,), (1,)), ((), ())),
                                 preferred_element_type=F32) * dmat
        cross = jnp.dot(q, state_bf_ref[...], preferred_element_type=F32) * xi
        kz = (k.astype(F32) * zeta).astype(BF16)
        state = state_ref[...] * cdec + lax.dot_general(
            kz, v, (((0,), (0,)), ((), ())), preferred_element_type=F32)
        state_ref[...] = state
        state_bf_ref[...] = state.astype(BF16)
        inner = jnp.dot(scores.astype(BF16), v, preferred_element_type=F32)
        for r0 in range(0, C, RET_NORM_ROWS):
            blk = slice(r0, r0 + RET_NORM_ROWS)
            ret = inner[blk] + cross[blk]
            mu = jnp.mean(ret, axis=-1, keepdims=True)
            cen = ret - mu
            var = jnp.mean(cen * cen, axis=-1, keepdims=True)
            y = cen * lax.rsqrt(var + GN_EPS)
            out_rows = pl.ds(ci * C + r0, RET_NORM_ROWS)
            o_ref[0, out_rows, :] = (
                g_ref[0, out_rows, :].astype(F32) * y).astype(o_ref.dtype)


def _retention(proj, d_model, d_inner):
    B, S, _ = proj.shape
    H, C = RET_HEADS, RET_CHUNK
    dk, dv = d_model // H, d_inner // H
    rows = C * RET_CHUNKS_PER_STEP
    k_off = d_model // dk
    v_off = 2 * d_model // dv
    g_off = (2 * d_model + d_inner) // dv

    log_gamma = jnp.log1p(-jnp.exp2(-5.0 - jnp.arange(H, dtype=F32)))
    idx = jnp.arange(C, dtype=F32)
    rel = idx[:, None] - idx[None, :]
    dmat = jnp.exp(log_gamma[:, None, None] * jnp.maximum(rel, 0.0)) * (rel >= 0)
    xi = jnp.exp(log_gamma[:, None] * (idx + 1.0))
    zeta = jnp.exp(log_gamma[:, None] * (C - 1.0 - idx))
    cdec = jnp.exp(log_gamma * C)
    xi = jnp.broadcast_to(xi[:, :, None], (H, C, dv))
    zeta = jnp.broadcast_to(zeta[:, :, None], (H, C, dk))
    cdec = jnp.broadcast_to(cdec[:, None, None], (H, 1, dv))

    return pl.pallas_call(
        _retention_kernel,
        grid=(B, H, S // rows),
        in_specs=[
            pl.BlockSpec((1, rows, dk), lambda b, h, n: (b, n, h)),
            pl.BlockSpec((1, rows, dk), lambda b, h, n: (b, n, k_off + h)),
            pl.BlockSpec((1, rows, dv), lambda b, h, n: (b, n, v_off + h)),
            pl.BlockSpec((1, rows, dv), lambda b, h, n: (b, n, g_off + h)),
            pl.BlockSpec((1, C, C), lambda b, h, n: (h, 0, 0)),
            pl.BlockSpec((1, C, dv), lambda b, h, n: (h, 0, 0)),
            pl.BlockSpec((1, C, dk), lambda b, h, n: (h, 0, 0)),
            pl.BlockSpec((1, 1, dv), lambda b, h, n: (h, 0, 0)),
        ],
        out_specs=pl.BlockSpec((1, rows, dv), lambda b, h, n: (b, n, h)),
        out_shape=jax.ShapeDtypeStruct((B, S, d_inner), BF16),
        scratch_shapes=[pltpu.VMEM((dk, dv), F32), pltpu.VMEM((dk, dv), BF16)],
        compiler_params=_compiler_params(("parallel", "parallel", "arbitrary")),
        name="retention",
    )(proj, proj, proj, proj, dmat, xi, zeta, cdec)


def _diff_attn_kernel(tab_ref, bfar_ref, q_ref, k_ref, vaug_ref, gt_ref, bkd_ref,
                      lq1_ref, lk1_ref, lq2_ref, lk2_ref, sg_ref, o_ref,
                      qbd_ref, bias_ref, m_ref, acc_ref, sa_ref, sb_ref,
                      *, d, lambda_init):
    T, G, H = ATT_TILE, ATT_HEADS_PER_STEP, DIFF_HEADS
    dv = 2 * d
    hp = pl.program_id(1)
    i = pl.program_id(2)

    @pl.when(i == 0)
    def _():
        rows = lax.broadcasted_iota(jnp.int32, (T, T), 0)
        cols = lax.broadcasted_iota(jnp.int32, (T, T), 1)
        bucket_of_distance = bkd_ref[...]
        for u in range(G):
            h = hp * G + u
            cfar = tab_ref[bfar_ref[0] * H + h]
            by_distance = jnp.full((1, 2 * T), tab_ref[h], F32)
            for bucket in range(1, REL_BUCKETS):
                by_distance = jnp.where(bucket_of_distance >= bucket,
                                        tab_ref[bucket * H + h], by_distance)
            by_distance = (by_distance - cfar) * LOG2E
            toeplitz = pltpu.roll(jnp.broadcast_to(by_distance, (T, 2 * T)),
                                  0, 1, stride=1, stride_axis=0)
            bias_ref[u, 0] = jnp.where(rows <= cols, toeplitz[:, 0:T], -jnp.inf)
            bias_ref[u, 1] = toeplitz[:, T:2 * T]

    def logits_into(dst_ref, w, j, bias_index):
        u = w % G
        start = pl.multiple_of(j * T, T)
        kt = k_ref[0, pl.ds(start, T), u * dv:(u + 1) * dv]
        s = jnp.dot(kt, qbd_ref[w], preferred_element_type=F32)
        if bias_index is None:
            dst_ref[w] = s
        else:
            bias = bias_ref[u, bias_index]
            dst_ref[w, :, 0:T] = s[:, 0:T] + bias
            dst_ref[w, :, T:2 * T] = s[:, T:2 * T] + bias

    def consume(src_ref, w, j):
        u = w % G
        start = pl.multiple_of(j * T, T)
        s = src_ref[w]
        m_old = m_ref[w]
        m_new = jnp.maximum(m_old, jnp.max(s, axis=0, keepdims=True))
        p = jnp.exp2((s - m_new).astype(BF16))
        alpha = jnp.exp2(m_old - m_new)
        pv = jnp.dot(vaug_ref[0, u, :, pl.ds(start, T)], p,
                     preferred_element_type=F32)
        acc_ref[w] = acc_ref[w] * alpha + pv
        m_ref[w] = m_new

    first, second, both = range(G), range(G, 2 * G), range(2 * G)

    def step(dst_ref, j_next, bias_of, src_ref, j_cur, cur_units=both):
        for w in both:
            logits_into(dst_ref, w, j_next, bias_of(w))
            if w in cur_units:
                consume(src_ref, w, j_cur)

    no_bias = lambda w: None

    def start():
        for w in both:
            sub, u = divmod(w, G)
            q = q_ref[0, sub * T:(sub + 1) * T, u * dv:(u + 1) * dv].astype(F32)
            qt = (q * (d ** -0.5 * LOG2E)).T
            row = lax.broadcasted_iota(jnp.int32, qt.shape, 0)
            qbd_ref[w, :, 0:T] = jnp.where(row < d, qt, 0.0).astype(BF16)
            qbd_ref[w, :, T:2 * T] = jnp.where(row >= d, qt, 0.0).astype(BF16)
        m_ref[...] = jnp.full_like(m_ref, -jnp.inf)
        acc_ref[...] = jnp.zeros_like(acc_ref)
        for w in second:
            logits_into(sa_ref, w, 2 * i + 1, 0)
        step(sb_ref, 2 * i, lambda w: 0 if w in first else 1,
             sa_ref, 2 * i + 1, cur_units=second)

    def finish(src_ref, j_cur):
        lam = (jnp.exp(jnp.sum(lq1_ref[...] * lk1_ref[...], keepdims=True))
               - jnp.exp(jnp.sum(lq2_ref[...] * lk2_ref[...], keepdims=True))
               + lambda_init)
        out_scale = sg_ref[...] * (1.0 - lambda_init)
        for w in both:
            sub, u = divmod(w, G)
            consume(src_ref, w, j_cur)
            acc = acc_ref[w]
            inv_l = 1.0 / acc[dv:dv + 1, :]
            o = (acc[:dv, :T] * inv_l[:, :T]
                 - acc[:dv, T:] * (lam * inv_l[:, T:]))
            ms = jnp.mean(o * o, axis=0, keepdims=True)
            on = (o * lax.rsqrt(ms + GN_EPS)) * out_scale
            head_rows = slice(u * dv, (u + 1) * dv)
            sub_cols = slice(sub * T, (sub + 1) * T)
            gate = gt_ref[0, head_rows, sub_cols].astype(F32)
            o_ref[0, head_rows, sub_cols] = (gate * on).astype(o_ref.dtype)

    @pl.when(i == 0)
    def _():
        start()
        finish(sb_ref, 2 * i)

    @pl.when(i >= 1)
    def _():
        def prologue():
            start()
            step(sa_ref, 2 * i - 1, lambda w: 1 if w in first else None,
                 sb_ref, 2 * i)
            step(sb_ref, 0, no_bias, sa_ref, 2 * i - 1)

        def pair(r):
            step(sa_ref, 2 * r - 1, no_bias, sb_ref, 2 * r - 2)
            step(sb_ref, 2 * r, no_bias, sa_ref, 2 * r - 1)

        n_pairs = i - 1
        n_trips = n_pairs // 4

        def trip(t):
            for e in range(1, 5):
                pair(4 * t + e)

        @pl.when(n_trips == 0)
        def _():
            prologue()

        @pl.when(n_trips >= 1)
        def _():
            prologue()
            trip(0)

        def trip_body(t, carry):
            trip(t)
            return carry

        lax.fori_loop(1, n_trips, trip_body, 0)

        @pl.when(n_pairs % 4 >= 2)
        def _():
            first_left = (n_pairs // 4) * 4 + 1
            pair(first_left)
            pair(first_left + 1)

        @pl.when((i - 1) % 2 == 1)
        def _():
            pair(i - 1)
            finish(sb_ref, 2 * i - 2)

        @pl.when((i - 1) % 2 == 0)
        def _():
            finish(sb_ref, 2 * i - 2)


def _t5_bucket(n):
    max_exact = REL_BUCKETS // 2
    nf = jnp.maximum(n, max_exact).astype(F32)
    large = max_exact + (jnp.log(nf / max_exact) / math.log(REL_MAX_DIST / max_exact)
                         * (REL_BUCKETS - max_exact)).astype(jnp.int32)
    large = jnp.minimum(large, REL_BUCKETS - 1)
    return jnp.where(n < max_exact, n, large)


def _diff_attn(qk, vaug, gt, lq1, lk1, lq2, lk2, subln_g, rel_bias, lambda_init):
    B, S, _ = qk.shape
    H, T, G = DIFF_HEADS, ATT_TILE, ATT_HEADS_PER_STEP
    d_inner = gt.shape[1]
    dv = d_inner // H
    d = dv // 2
    gw = G * dv
    n_groups = H // G

    bkd = _t5_bucket(jnp.arange(2 * T, dtype=jnp.int32)[None, :])
    bfar = _t5_bucket(jnp.full((1,), T + 1, jnp.int32))

    smem = pl.BlockSpec(memory_space=pltpu.SMEM)
    vec = pl.BlockSpec((1, d), lambda b, h, i: (0, 0))
    return pl.pallas_call(
        functools.partial(_diff_attn_kernel, d=d, lambda_init=lambda_init),
        grid=(B, n_groups, S // (2 * T)),
        in_specs=[
            smem, smem,
            pl.BlockSpec((1, 2 * T, gw), lambda b, h, i: (b, i, h)),
            pl.BlockSpec((1, S, gw), lambda b, h, i: (b, 0, n_groups + h)),
            pl.BlockSpec((1, G, dv + ONES_ROWS, S), lambda b, h, i: (b, h, 0, 0)),
            pl.BlockSpec((1, gw, 2 * T), lambda b, h, i: (b, h, i)),
            pl.BlockSpec((1, 2 * T), lambda b, h, i: (0, 0)),
            vec, vec, vec, vec,
            pl.BlockSpec((dv, 1), lambda b, h, i: (0, 0)),
        ],
        out_specs=pl.BlockSpec((1, gw, 2 * T), lambda b, h, i: (b, h, i)),
        out_shape=jax.ShapeDtypeStruct((B, d_inner, S), BF16),
        scratch_shapes=[pltpu.VMEM((2 * G, dv, 2 * T), BF16),
                        pltpu.VMEM((G, 2, T, T), F32),
                        pltpu.VMEM((2 * G, 1, 2 * T), F32),
                        pltpu.VMEM((2 * G, dv + ONES_ROWS, 2 * T), F32),
                        pltpu.VMEM((2 * G, T, 2 * T), F32),
                        pltpu.VMEM((2 * G, T, 2 * T), F32)],
        compiler_params=_compiler_params(("parallel", "arbitrary", "arbitrary")),
        name="diff_attn",
    )(rel_bias.astype(F32).reshape(-1), bfar, qk, qk, vaug, gt, bkd,
      lq1.reshape(1, d), lk1.reshape(1, d), lq2.reshape(1, d), lk2.reshape(1, d),
      subln_g.reshape(dv, 1))


def kernel(x, pre_norm_g, post_norm_g, ret_w_in, ret_w_out, diff_w_in, diff_w_out,
           diff_lambda_q1, diff_lambda_k1, diff_lambda_q2, diff_lambda_k2,
           diff_subln_g, rel_bias):
    d_model = x.shape[-1]
    d_inner = ret_w_out.shape[1]
    for i in range(DEPTH):
        j = i // N_MIXERS
        if i % N_MIXERS == 0:
            proj = _norm_matmul_ret(x, pre_norm_g[i], ret_w_in[j].astype(BF16),
                                    d_inner, RET_HEADS)
            y = _retention(proj, d_model, d_inner)
            x = _out_proj(y, ret_w_out[j].astype(BF16), post_norm_g[i], x,
                          transposed_in=False)
        else:
            lambda_init = 0.8 - 0.6 * math.exp(-0.3 * i)
            qk, vaug, gt = _norm_matmul_diff(
                x, pre_norm_g[i], diff_w_in[j].astype(BF16), DIFF_HEADS)
            yt = _diff_attn(qk, vaug, gt, diff_lambda_q1[j], diff_lambda_k1[j],
                            diff_lambda_q2[j], diff_lambda_k2[j], diff_subln_g[j],
                            rel_bias, lambda_init)
            x = _out_proj(yt, diff_w_out[j].astype(BF16), post_norm_g[i], x,
                          transposed_in=True)
    return x
```

```python
import functools
import math

import jax
import jax.numpy as jnp
import numpy as np
from jax import lax
from jax.experimental import pallas as pl
from jax.experimental.pallas import tpu as pltpu

F32 = jnp.float32
BF16 = jnp.bfloat16

DEPTH = 2
N_MIXERS = 2
RET_HEADS = 4
DIFF_HEADS = 16
REL_BUCKETS = 32
REL_MAX_DIST = 128
ROPE_BASE = 10000.0
RMS_EPS = 1e-6
GN_EPS = 1e-5

V7X_VMEM_BYTES = 64 * 1024 * 1024
VMEM_LIMIT_BYTES = V7X_VMEM_BYTES * 15 // 16

PROJ_ROWS = 1024
PROJ_COLS = 2048
OUT_ROWS = 1024
RET_CHUNK = 256
RET_CHUNKS_PER_STEP = 8
RET_NORM_ROWS = 32
ATT_TILE = 256
ATT_HEADS_PER_STEP = 4
ONES_ROWS = 16
LOG2E = math.log2(math.e)


def _compiler_params(semantics):
    return pltpu.CompilerParams(dimension_semantics=semantics,
                                vmem_limit_bytes=VMEM_LIMIT_BYTES)


def _store_normed(x_ref, g_ref, h_ref):
    x = x_ref[0]
    ms = jnp.mean(x * x, axis=-1, keepdims=True)
    h_ref[...] = (x * lax.rsqrt(ms + RMS_EPS) * g_ref[...]).astype(BF16)


def _silu(x):
    return x * jax.nn.sigmoid(x)


def _norm_matmul_ret_kernel(x_ref, g_ref, w_ref, cos_ref, sin_ref, o_ref, h_ref,
                            *, dk):
    j = pl.program_id(2)

    def project(cols=slice(None)):
        return jnp.dot(h_ref[...], w_ref[:, cols], preferred_element_type=F32)

    @pl.when(j == 0)
    def _():
        _store_normed(x_ref, g_ref, h_ref)
        half = dk // 2
        n_blocks = w_ref.shape[1] // dk
        c, s = cos_ref[...], sin_ref[...]
        for blk in range(n_blocks):
            o = project(slice(blk * dk, (blk + 1) * dk))
            scale = 1.0 if blk < n_blocks // 2 else dk ** -0.5
            x1, x2 = o[:, :half], o[:, half:]
            o_ref[0, :, blk * dk:blk * dk + half] = (
                (x1 * c - x2 * s) * scale).astype(BF16)
            o_ref[0, :, blk * dk + half:(blk + 1) * dk] = (
                (x1 * s + x2 * c) * scale).astype(BF16)

    @pl.when(j == 1)
    def _():
        o_ref[0] = project().astype(BF16)

    @pl.when(j == 2)
    def _():
        o_ref[0] = _silu(project()).astype(BF16)


def _norm_matmul_ret(x, g, w, d_inner, n_heads):
    B, S, D = x.shape
    N = w.shape[1]
    ts, tn = PROJ_ROWS, d_inner
    dk = (N - 2 * d_inner) // 2 // n_heads
    half = dk // 2
    assert N == 3 * tn, "q | k must fill exactly one column tile"

    pos = np.arange(S, dtype=np.float64)
    inv = ROPE_BASE ** (-np.arange(half, dtype=np.float64) / half)
    ang = pos[:, None] * inv[None, :]
    cos = jnp.asarray(np.cos(ang), dtype=F32)
    sin = jnp.asarray(np.sin(ang), dtype=F32)

    return pl.pallas_call(
        functools.partial(_norm_matmul_ret_kernel, dk=dk),
        grid=(B, S // ts, N // tn),
        in_specs=[pl.BlockSpec((1, ts, D), lambda b, i, j: (b, i, 0)),
                  pl.BlockSpec((1, D), lambda b, i, j: (0, 0)),
                  pl.BlockSpec((D, tn), lambda b, i, j: (0, j)),
                  pl.BlockSpec((ts, half), lambda b, i, j: (i, 0)),
                  pl.BlockSpec((ts, half), lambda b, i, j: (i, 0))],
        out_specs=pl.BlockSpec((1, ts, tn), lambda b, i, j: (b, i, j)),
        out_shape=jax.ShapeDtypeStruct((B, S, N), BF16),
        scratch_shapes=[pltpu.VMEM((ts, D), BF16)],
        compiler_params=_compiler_params(("parallel", "parallel", "arbitrary")),
        name="norm_matmul_ret",
    )(x, g.reshape(1, D), w, cos, sin)


def _norm_matmul_diff_kernel(x_ref, g_ref, w_ref, qk_ref, vaug_ref, gt_ref, h_ref):
    j = pl.program_id(2)

    @pl.when(j == 0)
    def _():
        _store_normed(x_ref, g_ref, h_ref)

    def project():
        return jnp.dot(h_ref[...], w_ref[...], preferred_element_type=F32)

    @pl.when(j < 2)
    def _():
        qk_ref[0] = project().astype(BF16)

    @pl.when(j == 2)
    def _():
        n_heads, rows_aug, ts = vaug_ref.shape[1:]
        dv = w_ref.shape[1] // n_heads
        for h in range(0, n_heads, 2):
            ot = jnp.dot(h_ref[...], w_ref[:, h * dv:(h + 2) * dv],
                         preferred_element_type=F32).T.astype(BF16)
            for e in range(2):
                vaug_ref[0, h + e, 0:dv, :] = ot[e * dv:(e + 1) * dv, :]
                vaug_ref[0, h + e, dv:rows_aug, :] = jnp.ones(
                    (rows_aug - dv, ts), BF16)

    @pl.when(j == 3)
    def _():
        gt_ref[0] = _silu(project()).T.astype(BF16)


def _norm_matmul_diff(x, g, w, n_heads):
    B, S, D = x.shape
    Di = w.shape[1] // 4
    dv = Di // n_heads
    ts = PROJ_ROWS
    return pl.pallas_call(
        _norm_matmul_diff_kernel,
        grid=(B, S // ts, 4),
        in_specs=[pl.BlockSpec((1, ts, D), lambda b, i, j: (b, i, 0)),
                  pl.BlockSpec((1, D), lambda b, i, j: (0, 0)),
                  pl.BlockSpec((D, Di), lambda b, i, j: (0, j))],
        out_specs=[
            pl.BlockSpec((1, ts, Di), lambda b, i, j: (b, i, jnp.minimum(j, 1))),
            pl.BlockSpec((1, n_heads, dv + ONES_ROWS, ts),
                         lambda b, i, j: (b, 0, 0, i)),
            pl.BlockSpec((1, Di, ts), lambda b, i, j: (b, 0, i)),
        ],
        out_shape=[jax.ShapeDtypeStruct((B, S, 2 * Di), BF16),
                   jax.ShapeDtypeStruct((B, n_heads, dv + ONES_ROWS, S), BF16),
                   jax.ShapeDtypeStruct((B, Di, S), BF16)],
        scratch_shapes=[pltpu.VMEM((ts, D), BF16)],
        compiler_params=_compiler_params(("parallel", "parallel", "arbitrary")),
        name="norm_matmul_diff",
    )(x, g.reshape(1, D), w)


def _out_proj_kernel(y_ref, w_ref, g_ref, x_ref, o_ref, *, transposed_in):
    y = y_ref[0]
    if transposed_in:
        o = lax.dot_general(y, w_ref[...], (((0,), (0,)), ((), ())),
                            preferred_element_type=F32)
    else:
        o = jnp.dot(y, w_ref[...], preferred_element_type=F32)
    ms = jnp.mean(o * o, axis=-1, keepdims=True)
    o_ref[0] = x_ref[0] + o * lax.rsqrt(ms + RMS_EPS) * g_ref[...]


def _out_proj(y, w, g, x, *, transposed_in):
    B, S, D = x.shape
    Di = w.shape[0]
    ts = OUT_ROWS
    if transposed_in:
        y_spec = pl.BlockSpec((1, Di, ts), lambda b, i: (b, 0, i))
    else:
        y_spec = pl.BlockSpec((1, ts, Di), lambda b, i: (b, i, 0))
    return pl.pallas_call(
        functools.partial(_out_proj_kernel, transposed_in=transposed_in),
        grid=(B, S // ts),
        in_specs=[y_spec,
                  pl.BlockSpec((Di, D), lambda b, i: (0, 0)),
                  pl.BlockSpec((1, D), lambda b, i: (0, 0)),
                  pl.BlockSpec((1, ts, D), lambda b, i: (b, i, 0))],
        out_specs=pl.BlockSpec((1, ts, D), lambda b, i: (b, i, 0)),
        out_shape=jax.ShapeDtypeStruct((B, S, D), F32),
        compiler_params=_compiler_params(("parallel", "parallel")),
        name="out_proj_t" if transposed_in else "out_proj",
    )(y, w, g.reshape(1, D), x)


def _retention_kernel(q_ref, k_ref, v_ref, g_ref, dmat_ref, xi_ref, zeta_ref,
                      cdec_ref, o_ref, state_ref, state_bf_ref):
    C = RET_CHUNK

    @pl.when(pl.program_id(2) == 0)
    def _():
        state_ref[...] = jnp.zeros_like(state_ref)
        state_bf_ref[...] = jnp.zeros_like(state_bf_ref)

    dmat = dmat_ref[0]
    xi = xi_ref[0]
    zeta = zeta_ref[0]
    cdec = cdec_ref[0]

    for ci in range(RET_CHUNKS_PER_STEP):
        rows = pl.ds(ci * C, C)
        q, k, v = q_ref[0, rows, :], k_ref[0, rows, :], v_ref[0, rows, :]
        scores = lax.dot_general(q, k, (((1,), (1,)), ((), ())),
                                 preferred_element_type=F32) * dmat
        cross = jnp.dot(q, state_bf_ref[...], preferred_element_type=F32) * xi
        kz = (k.astype(F32) * zeta).astype(BF16)
        state = state_ref[...] * cdec + lax.dot_general(
            kz, v, (((0,), (0,)), ((), ())), preferred_element_type=F32)
        state_ref[...] = state
        state_bf_ref[...] = state.astype(BF16)
        inner = jnp.dot(scores.astype(BF16), v, preferred_element_type=F32)
        for r0 in range(0, C, RET_NORM_ROWS):
            blk = slice(r0, r0 + RET_NORM_ROWS)
            ret = inner[blk] + cross[blk]
            mu = jnp.mean(ret, axis=-1, keepdims=True)
            cen = ret - mu
            var = jnp.mean(cen * cen, axis=-1, keepdims=True)
            y = cen * lax.rsqrt(var + GN_EPS)
            out_rows = pl.ds(ci * C + r0, RET_NORM_ROWS)
            o_ref[0, out_rows, :] = (
                g_ref[0, out_rows, :].astype(F32) * y).astype(o_ref.dtype)


def _retention(proj, d_model, d_inner):
    B, S, _ = proj.shape
    H, C = RET_HEADS, RET_CHUNK
    dk, dv = d_model // H, d_inner // H
    rows = C * RET_CHUNKS_PER_STEP
    k_off = d_model // dk
    v_off = 2 * d_model // dv
    g_off = (2 * d_model + d_inner) // dv

    log_gamma = jnp.log1p(-jnp.exp2(-5.0 - jnp.arange(H, dtype=F32)))
    idx = jnp.arange(C, dtype=F32)
    rel = idx[:, None] - idx[None, :]
    dmat = jnp.exp(log_gamma[:, None, None] * jnp.maximum(rel, 0.0)) * (rel >= 0)
    xi = jnp.exp(log_gamma[:, None] * (idx + 1.0))
    zeta = jnp.exp(log_gamma[:, None] * (C - 1.0 - idx))
    cdec = jnp.exp(log_gamma * C)
    xi = jnp.broadcast_to(xi[:, :, None], (H, C, dv))
    zeta = jnp.broadcast_to(zeta[:, :, None], (H, C, dk))
    cdec = jnp.broadcast_to(cdec[:, None, None], (H, 1, dv))

    return pl.pallas_call(
        _retention_kernel,
        grid=(B, H, S // rows),
        in_specs=[
            pl.BlockSpec((1, rows, dk), lambda b, h, n: (b, n, h)),
            pl.BlockSpec((1, rows, dk), lambda b, h, n: (b, n, k_off + h)),
            pl.BlockSpec((1, rows, dv), lambda b, h, n: (b, n, v_off + h)),
            pl.BlockSpec((1, rows, dv), lambda b, h, n: (b, n, g_off + h)),
            pl.BlockSpec((1, C, C), lambda b, h, n: (h, 0, 0)),
            pl.BlockSpec((1, C, dv), lambda b, h, n: (h, 0, 0)),
            pl.BlockSpec((1, C, dk), lambda b, h, n: (h, 0, 0)),
            pl.BlockSpec((1, 1, dv), lambda b, h, n: (h, 0, 0)),
        ],
        out_specs=pl.BlockSpec((1, rows, dv), lambda b, h, n: (b, n, h)),
        out_shape=jax.ShapeDtypeStruct((B, S, d_inner), BF16),
        scratch_shapes=[pltpu.VMEM((dk, dv), F32), pltpu.VMEM((dk, dv), BF16)],
        compiler_params=_compiler_params(("parallel", "parallel", "arbitrary")),
        name="retention",
    )(proj, proj, proj, proj, dmat, xi, zeta, cdec)


def _diff_attn_kernel(tab_ref, bfar_ref, q_ref, k_ref, vaug_ref, gt_ref, bkd_ref,
                      lq1_ref, lk1_ref, lq2_ref, lk2_ref, sg_ref, o_ref,
                      qbd_ref, bias_ref, m_ref, acc_ref, sa_ref, sb_ref,
                      *, d, lambda_init):
    T, G, H = ATT_TILE, ATT_HEADS_PER_STEP, DIFF_HEADS
    dv = 2 * d
    hp = pl.program_id(1)
    i = pl.program_id(2)

    @pl.when(i == 0)
    def _():
        rows = lax.broadcasted_iota(jnp.int32, (T, T), 0)
        cols = lax.broadcasted_iota(jnp.int32, (T, T), 1)
        bucket_of_distance = bkd_ref[...]
        for u in range(G):
            h = hp * G + u
            cfar = tab_ref[bfar_ref[0] * H + h]
            by_distance = jnp.full((1, 2 * T), tab_ref[h], F32)
            for bucket in range(1, REL_BUCKETS):
                by_distance = jnp.where(bucket_of_distance >= bucket,
                                        tab_ref[bucket * H + h], by_distance)
            by_distance = (by_distance - cfar) * LOG2E
            toeplitz = pltpu.roll(jnp.broadcast_to(by_distance, (T, 2 * T)),
                                  0, 1, stride=1, stride_axis=0)
            bias_ref[u, 0] = jnp.where(rows <= cols, toeplitz[:, 0:T], -jnp.inf)
            bias_ref[u, 1] = toeplitz[:, T:2 * T]

    def logits_into(dst_ref, w, j, bias_index):
        u = w % G
        start = pl.multiple_of(j * T, T)
        kt = k_ref[0, pl.ds(start, T), u * dv:(u + 1) * dv]
        s = jnp.dot(kt, qbd_ref[w], preferred_element_type=F32)
        if bias_index is None:
            dst_ref[w] = s
        else:
            bias = bias_ref[u, bias_index]
            dst_ref[w, :, 0:T] = s[:, 0:T] + bias
            dst_ref[w, :, T:2 * T] = s[:, T:2 * T] + bias

    def consume(src_ref, w, j):
        u = w % G
        start = pl.multiple_of(j * T, T)
        s = src_ref[w]
        m_old = m_ref[w]
        m_new = jnp.maximum(m_old, jnp.max(s, axis=0, keepdims=True))
        p = jnp.exp2((s - m_new).astype(BF16))
        alpha = jnp.exp2(m_old - m_new)
        pv = jnp.dot(vaug_ref[0, u, :, pl.ds(start, T)], p,
                     preferred_element_type=F32)
        acc_ref[w] = acc_ref[w] * alpha + pv
        m_ref[w] = m_new

    first, second, both = range(G), range(G, 2 * G), range(2 * G)

    def step(dst_ref, j_next, bias_of, src_ref, j_cur, cur_units=both):
        for w in both:
            logits_into(dst_ref, w, j_next, bias_of(w))
            if w in cur_units:
                consume(src_ref, w, j_cur)

    no_bias = lambda w: None

    def start():
        for w in both:
            sub, u = divmod(w, G)
            q = q_ref[0, sub * T:(sub + 1) * T, u * dv:(u + 1) * dv].astype(F32)
            qt = (q * (d ** -0.5 * LOG2E)).T
            row = lax.broadcasted_iota(jnp.int32, qt.shape, 0)
            qbd_ref[w, :, 0:T] = jnp.where(row < d, qt, 0.0).astype(BF16)
            qbd_ref[w, :, T:2 * T] = jnp.where(row >= d, qt, 0.0).astype(BF16)
        m_ref[...] = jnp.full_like(m_ref, -jnp.inf)
        acc_ref[...] = jnp.zeros_like(acc_ref)
        for w in second:
            logits_into(sa_ref, w, 2 * i + 1, 0)
        step(sb_ref, 2 * i, lambda w: 0 if w in first else 1,
             sa_ref, 2 * i + 1, cur_units=second)

    def finish(src_ref, j_cur):
        lam = (jnp.exp(jnp.sum(lq1_ref[...] * lk1_ref[...], keepdims=True))
               - jnp.exp(jnp.sum(lq2_ref[...] * lk2_ref[...], keepdims=True))
               + lambda_init)
        out_scale = sg_ref[...] * (1.0 - lambda_init)
        for w in both:
            sub, u = divmod(w, G)
            consume(src_ref, w, j_cur)
            acc = acc_ref[w]
            inv_l = 1.0 / acc[dv:dv + 1, :]
            o = (acc[:dv, :T] * inv_l[:, :T]
                 - acc[:dv, T:] * (lam * inv_l[:, T:]))
            ms = jnp.mean(o * o, axis=0, keepdims=True)
            on = (o * lax.rsqrt(ms + GN_EPS)) * out_scale
            head_rows = slice(u * dv, (u + 1) * dv)
            sub_cols = slice(sub * T, (sub + 1) * T)
            gate = gt_ref[0, head_rows, sub_cols].astype(F32)
            o_ref[0, head_rows, sub_cols] = (gate * on).astype(o_ref.dtype)

    @pl.when(i == 0)
    def _():
        start()
        finish(sb_ref, 2 * i)

    @pl.when(i >= 1)
    def _():
        def prologue():
            start()
            step(sa_ref, 2 * i - 1, lambda w: 1 if w in first else None,
                 sb_ref, 2 * i)
            step(sb_ref, 0, no_bias, sa_ref, 2 * i - 1)

        def pair(r):
            step(sa_ref, 2 * r - 1, no_bias, sb_ref, 2 * r - 2)
            step(sb_ref, 2 * r, no_bias, sa_ref, 2 * r - 1)

        n_pairs = i - 1
        n_trips = n_pairs // 4

        def trip(t):
            for e in range(1, 5):
                pair(4 * t + e)

        prologue()

        def trip_body(t, carry):
            trip(t)
            return carry

        lax.fori_loop(0, n_trips, trip_body, 0)

        @pl.when(n_pairs % 4 >= 2)
        def _():
            first_left = (n_pairs // 4) * 4 + 1
            pair(first_left)
            pair(first_left + 1)

        @pl.when((i - 1) % 2 == 1)
        def _():
            pair(i - 1)
            finish(sb_ref, 2 * i - 2)

        @pl.when((i - 1) % 2 == 0)
        def _():
            finish(sb_ref, 2 * i - 2)


def _t5_bucket(n):
    max_exact = REL_BUCKETS // 2
    nf = jnp.maximum(n, max_exact).astype(F32)
    large = max_exact + (jnp.log(nf / max_exact) / math.log(REL_MAX_DIST / max_exact)
                         * (REL_BUCKETS - max_exact)).astype(jnp.int32)
    large = jnp.minimum(large, REL_BUCKETS - 1)
    return jnp.where(n < max_exact, n, large)


def _diff_attn(qk, vaug, gt, lq1, lk1, lq2, lk2, subln_g, rel_bias, lambda_init):
    B, S, _ = qk.shape
    H, T, G = DIFF_HEADS, ATT_TILE, ATT_HEADS_PER_STEP
    d_inner = gt.shape[1]
    dv = d_inner // H
    d = dv // 2
    gw = G * dv
    n_groups = H // G

    bkd = _t5_bucket(jnp.arange(2 * T, dtype=jnp.int32)[None, :])
    bfar = _t5_bucket(jnp.full((1,), T + 1, jnp.int32))

    smem = pl.BlockSpec(memory_space=pltpu.SMEM)
    vec = pl.BlockSpec((1, d), lambda b, h, i: (0, 0))
    return pl.pallas_call(
        functools.partial(_diff_attn_kernel, d=d, lambda_init=lambda_init),
        grid=(B, n_groups, S // (2 * T)),
        in_specs=[
            smem, smem,
            pl.BlockSpec((1, 2 * T, gw), lambda b, h, i: (b, i, h)),
            pl.BlockSpec((1, S, gw), lambda b, h, i: (b, 0, n_groups + h)),
            pl.BlockSpec((1, G, dv + ONES_ROWS, S), lambda b, h, i: (b, h, 0, 0)),
            pl.BlockSpec((1, gw, 2 * T), lambda b, h, i: (b, h, i)),
            pl.BlockSpec((1, 2 * T), lambda b, h, i: (0, 0)),
            vec, vec, vec, vec,
            pl.BlockSpec((dv, 1), lambda b, h, i: (0, 0)),
        ],
        out_specs=pl.BlockSpec((1, gw, 2 * T), lambda b, h, i: (b, h, i)),
        out_shape=jax.ShapeDtypeStruct((B, d_inner, S), BF16),
        scratch_shapes=[pltpu.VMEM((2 * G, dv, 2 * T), BF16),
                        pltpu.VMEM((G, 2, T, T), F32),
                        pltpu.VMEM((2 * G, 1, 2 * T), F32),
                        pltpu.VMEM((2 * G, dv + ONES_ROWS, 2 * T), F32),
                        pltpu.VMEM((2 * G, T, 2 * T), F32),
                        pltpu.VMEM((2 * G, T, 2 * T), F32)],
        compiler_params=_compiler_params(("parallel", "arbitrary", "arbitrary")),
        name="diff_attn",
    )(rel_bias.astype(F32).reshape(-1), bfar, qk, qk, vaug, gt, bkd,
      lq1.reshape(1, d), lk1.reshape(1, d), lq2.reshape(1, d), lk2.reshape(1, d),
      subln_g.reshape(dv, 1))


def kernel(x, pre_norm_g, post_norm_g, ret_w_in, ret_w_out, diff_w_in, diff_w_out,
           diff_lambda_q1, diff_lambda_k1, diff_lambda_q2, diff_lambda_k2,
           diff_subln_g, rel_bias):
    d_model = x.shape[-1]
    d_inner = ret_w_out.shape[1]
    for i in range(DEPTH):
        j = i // N_MIXERS
        if i % N_MIXERS == 0:
            proj = _norm_matmul_ret(x, pre_norm_g[i], ret_w_in[j].astype(BF16),
                                    d_inner, RET_HEADS)
            y = _retention(proj, d_model, d_inner)
            x = _out_proj(y, ret_w_out[j].astype(BF16), post_norm_g[i], x,
                          transposed_in=False)
        else:
            lambda_init = 0.8 - 0.6 * math.exp(-0.3 * i)
            qk, vaug, gt = _norm_matmul_diff(
                x, pre_norm_g[i], diff_w_in[j].astype(BF16), DIFF_HEADS)
            yt = _diff_attn(qk, vaug, gt, diff_lambda_q1[j], diff_lambda_k1[j],
                            diff_lambda_q2[j], diff_lambda_k2[j], diff_subln_g[j],
                            rel_bias, lambda_init)
            x = _out_proj(yt, diff_w_out[j].astype(BF16), post_norm_g[i], x,
                          transposed_in=True)
    return x
```

```python
import functools
import math

import jax
import jax.numpy as jnp
import numpy as np
from jax import lax
from jax.experimental import pallas as pl
from jax.experimental.pallas import tpu as pltpu

F32 = jnp.float32
BF16 = jnp.bfloat16

DEPTH = 2
N_MIXERS = 2
RET_HEADS = 4
DIFF_HEADS = 16
REL_BUCKETS = 32
REL_MAX_DIST = 128
ROPE_BASE = 10000.0
RMS_EPS = 1e-6
GN_EPS = 1e-5

V7X_VMEM_BYTES = 64 * 1024 * 1024
VMEM_LIMIT_BYTES = V7X_VMEM_BYTES * 15 // 16

PROJ_ROWS = 1024
OUT_ROWS = 1024
RET_CHUNK = 256
RET_CHUNKS_PER_STEP = 8
RET_NORM_ROWS = 32
ATT_TILE = 256
ATT_HEADS_PER_STEP = 4
ONES_ROWS = 16
LOG2E = math.log2(math.e)


def _compiler_params(semantics):
    return pltpu.CompilerParams(dimension_semantics=semantics,
                                vmem_limit_bytes=VMEM_LIMIT_BYTES)


def _store_normed(x_ref, g_ref, h_ref):
    x = x_ref[0]
    ms = jnp.mean(x * x, axis=-1, keepdims=True)
    h_ref[...] = (x * lax.rsqrt(ms + RMS_EPS) * g_ref[...]).astype(BF16)


def _silu(x):
    return x * jax.nn.sigmoid(x)


def _norm_matmul_ret_kernel(x_ref, g_ref, w_ref, cos_ref, sin_ref, o_ref, h_ref,
                            *, dk):
    j = pl.program_id(2)

    def project(cols=slice(None)):
        return jnp.dot(h_ref[...], w_ref[:, cols], preferred_element_type=F32)

    @pl.when(j == 0)
    def _():
        _store_normed(x_ref, g_ref, h_ref)
        half = dk // 2
        n_blocks = w_ref.shape[1] // dk
        c, s = cos_ref[...], sin_ref[...]
        for blk in range(n_blocks):
            o = project(slice(blk * dk, (blk + 1) * dk))
            scale = 1.0 if blk < n_blocks // 2 else dk ** -0.5
            x1, x2 = o[:, :half], o[:, half:]
            o_ref[0, :, blk * dk:blk * dk + half] = (
                (x1 * c - x2 * s) * scale).astype(BF16)
            o_ref[0, :, blk * dk + half:(blk + 1) * dk] = (
                (x1 * s + x2 * c) * scale).astype(BF16)

    @pl.when(j == 1)
    def _():
        o_ref[0] = project().astype(BF16)

    @pl.when(j == 2)
    def _():
        o_ref[0] = _silu(project()).astype(BF16)


def _norm_matmul_ret(x, g, w, d_inner, n_heads):
    B, S, D = x.shape
    N = w.shape[1]
    ts, tn = PROJ_ROWS, d_inner
    dk = (N - 2 * d_inner) // 2 // n_heads
    half = dk // 2
    assert N == 3 * tn, "q | k must fill exactly one column tile"

    pos = np.arange(S, dtype=np.float64)
    inv = ROPE_BASE ** (-np.arange(half, dtype=np.float64) / half)
    ang = pos[:, None] * inv[None, :]
    cos = jnp.asarray(np.cos(ang), dtype=F32)
    sin = jnp.asarray(np.sin(ang), dtype=F32)

    return pl.pallas_call(
        functools.partial(_norm_matmul_ret_kernel, dk=dk),
        grid=(B, S // ts, N // tn),
        in_specs=[pl.BlockSpec((1, ts, D), lambda b, i, j: (b, i, 0)),
                  pl.BlockSpec((1, D), lambda b, i, j: (0, 0)),
                  pl.BlockSpec((D, tn), lambda b, i, j: (0, j)),
                  pl.BlockSpec((ts, half), lambda b, i, j: (i, 0)),
                  pl.BlockSpec((ts, half), lambda b, i, j: (i, 0))],
        out_specs=pl.BlockSpec((1, ts, tn), lambda b, i, j: (b, i, j)),
        out_shape=jax.ShapeDtypeStruct((B, S, N), BF16),
        scratch_shapes=[pltpu.VMEM((ts, D), BF16)],
        compiler_params=_compiler_params(("parallel", "parallel", "arbitrary")),
        name="norm_matmul_ret",
    )(x, g.reshape(1, D), w, cos, sin)


def _norm_matmul_diff_kernel(x_ref, g_ref, w_ref, qk_ref, vaug_ref, gt_ref, h_ref):
    j = pl.program_id(2)

    @pl.when(j == 0)
    def _():
        _store_normed(x_ref, g_ref, h_ref)

    def project():
        return jnp.dot(h_ref[...], w_ref[...], preferred_element_type=F32)

    @pl.when(j < 2)
    def _():
        qk_ref[0] = project().astype(BF16)

    @pl.when(j == 2)
    def _():
        n_heads, rows_aug, ts = vaug_ref.shape[1:]
        dv = w_ref.shape[1] // n_heads
        for h in range(0, n_heads, 2):
            ot = jnp.dot(h_ref[...], w_ref[:, h * dv:(h + 2) * dv],
                         preferred_element_type=F32).T.astype(BF16)
            for e in range(2):
                vaug_ref[0, h + e, 0:dv, :] = ot[e * dv:(e + 1) * dv, :]
                vaug_ref[0, h + e, dv:rows_aug, :] = jnp.ones(
                    (rows_aug - dv, ts), BF16)

    @pl.when(j == 3)
    def _():
        gt_ref[0] = _silu(project()).T.astype(BF16)


def _norm_matmul_diff(x, g, w, n_heads):
    B, S, D = x.shape
    Di = w.shape[1] // 4
    dv = Di // n_heads
    ts = PROJ_ROWS
    return pl.pallas_call(
        _norm_matmul_diff_kernel,
        grid=(B, S // ts, 4),
        in_specs=[pl.BlockSpec((1, ts, D), lambda b, i, j: (b, i, 0)),
                  pl.BlockSpec((1, D), lambda b, i, j: (0, 0)),
                  pl.BlockSpec((D, Di), lambda b, i, j: (0, j))],
        out_specs=[
            pl.BlockSpec((1, ts, Di), lambda b, i, j: (b, i, jnp.minimum(j, 1))),
            pl.BlockSpec((1, n_heads, dv + ONES_ROWS, ts),
                         lambda b, i, j: (b, 0, 0, i)),
            pl.BlockSpec((1, Di, ts), lambda b, i, j: (b, 0, i)),
        ],
        out_shape=[jax.ShapeDtypeStruct((B, S, 2 * Di), BF16),
                   jax.ShapeDtypeStruct((B, n_heads, dv + ONES_ROWS, S), BF16),
                   jax.ShapeDtypeStruct((B, Di, S), BF16)],
        scratch_shapes=[pltpu.VMEM((ts, D), BF16)],
        compiler_params=_compiler_params(("parallel", "parallel", "arbitrary")),
        name="norm_matmul_diff",
    )(x, g.reshape(1, D), w)


def _out_proj_kernel(y_ref, w_ref, g_ref, x_ref, o_ref, *, transposed_in):
    y = y_ref[0]
    if transposed_in:
        o = lax.dot_general(y, w_ref[...], (((0,), (0,)), ((), ())),
                            preferred_element_type=F32)
    else:
        o = jnp.dot(y, w_ref[...], preferred_element_type=F32)
    ms = jnp.mean(o * o, axis=-1, keepdims=True)
    o_ref[0] = x_ref[0] + o * lax.rsqrt(ms + RMS_EPS) * g_ref[...]


def _out_proj(y, w, g, x, *, transposed_in):
    B, S, D = x.shape
    Di = w.shape[0]
    ts = OUT_ROWS
    if transposed_in:
        y_spec = pl.BlockSpec((1, Di, ts), lambda b, i: (b, 0, i))
    else:
        y_spec = pl.BlockSpec((1, ts, Di), lambda b, i: (b, i, 0))
    return pl.pallas_call(
        functools.partial(_out_proj_kernel, transposed_in=transposed_in),
        grid=(B, S // ts),
        in_specs=[y_spec,
                  pl.BlockSpec((Di, D), lambda b, i: (0, 0)),
                  pl.BlockSpec((1, D), lambda b, i: (0, 0)),
                  pl.BlockSpec((1, ts, D), lambda b, i: (b, i, 0))],
        out_specs=pl.BlockSpec((1, ts, D), lambda b, i: (b, i, 0)),
        out_shape=jax.ShapeDtypeStruct((B, S, D), F32),
        compiler_params=_compiler_params(("parallel", "parallel")),
        name="out_proj_t" if transposed_in else "out_proj",
    )(y, w, g.reshape(1, D), x)


def _retention_kernel(q_ref, k_ref, v_ref, g_ref, dmat_ref, xi_ref, zeta_ref,
                      cdec_ref, o_ref, state_ref, state_bf_ref):
    C = RET_CHUNK

    @pl.when(pl.program_id(2) == 0)
    def _():
        state_ref[...] = jnp.zeros_like(state_ref)
        state_bf_ref[...] = jnp.zeros_like(state_bf_ref)

    dmat = dmat_ref[0]
    xi = xi_ref[0]
    zeta = zeta_ref[0]
    cdec = cdec_ref[0]

    for ci in range(RET_CHUNKS_PER_STEP):
        rows = pl.ds(ci * C, C)
        q, k, v = q_ref[0, rows, :], k_ref[0, rows, :], v_ref[0, rows, :]
        scores = lax.dot_general(q, k, (((1,), (1,)), ((), ())),
                                 preferred_element_type=F32) * dmat
        cross = jnp.dot(q, state_bf_ref[...], preferred_element_type=F32) * xi
        kz = (k.astype(F32) * zeta).astype(BF16)
        state = state_ref[...] * cdec + lax.dot_general(
            kz, v, (((0,), (0,)), ((), ())), preferred_element_type=F32)
        state_ref[...] = state
        state_bf_ref[...] = state.astype(BF16)
        inner = jnp.dot(scores.astype(BF16), v, preferred_element_type=F32)
        for r0 in range(0, C, RET_NORM_ROWS):
            blk = slice(r0, r0 + RET_NORM_ROWS)
            ret = inner[blk] + cross[blk]
            mu = jnp.mean(ret, axis=-1, keepdims=True)
            cen = ret - mu
            var = jnp.mean(cen * cen, axis=-1, keepdims=True)
            y = cen * lax.rsqrt(var + GN_EPS)
            out_rows = pl.ds(ci * C + r0, RET_NORM_ROWS)
            o_ref[0, out_rows, :] = (
                g_ref[0, out_rows, :].astype(F32) * y).astype(o_ref.dtype)


def _retention(proj, d_model, d_inner):
    B, S, _ = proj.shape
    H, C = RET_HEADS, RET_CHUNK
    dk, dv = d_model // H, d_inner // H
    rows = C * RET_CHUNKS_PER_STEP
    k_off = d_model // dk
    v_off = 2 * d_model // dv
    g_off = (2 * d_model + d_inner) // dv

    log_gamma = jnp.log1p(-jnp.exp2(-5.0 - jnp.arange(H, dtype=F32)))
    idx = jnp.arange(C, dtype=F32)
    rel = idx[:, None] - idx[None, :]
    dmat = jnp.exp(log_gamma[:, None, None] * jnp.maximum(rel, 0.0)) * (rel >= 0)
    xi = jnp.exp(log_gamma[:, None] * (idx + 1.0))
    zeta = jnp.exp(log_gamma[:, None] * (C - 1.0 - idx))
    cdec = jnp.exp(log_gamma * C)
    xi = jnp.broadcast_to(xi[:, :, None], (H, C, dv))
    zeta = jnp.broadcast_to(zeta[:, :, None], (H, C, dk))
    cdec = jnp.broadcast_to(cdec[:, None, None], (H, 1, dv))

    return pl.pallas_call(
        _retention_kernel,
        grid=(B, H, S // rows),
        in_specs=[
            pl.BlockSpec((1, rows, dk), lambda b, h, n: (b, n, h)),
            pl.BlockSpec((1, rows, dk), lambda b, h, n: (b, n, k_off + h)),
            pl.BlockSpec((1, rows, dv), lambda b, h, n: (b, n, v_off + h)),
            pl.BlockSpec((1, rows, dv), lambda b, h, n: (b, n, g_off + h)),
            pl.BlockSpec((1, C, C), lambda b, h, n: (h, 0, 0)),
            pl.BlockSpec((1, C, dv), lambda b, h, n: (h, 0, 0)),
            pl.BlockSpec((1, C, dk), lambda b, h, n: (h, 0, 0)),
            pl.BlockSpec((1, 1, dv), lambda b, h, n: (h, 0, 0)),
        ],
        out_specs=pl.BlockSpec((1, rows, dv), lambda b, h, n: (b, n, h)),
        out_shape=jax.ShapeDtypeStruct((B, S, d_inner), BF16),
        scratch_shapes=[pltpu.VMEM((dk, dv), F32), pltpu.VMEM((dk, dv), BF16)],
        compiler_params=_compiler_params(("parallel", "parallel", "arbitrary")),
        name="retention",
    )(proj, proj, proj, proj, dmat, xi, zeta, cdec)


def _diff_attn_kernel(tab_ref, bfar_ref, q_ref, k_ref, vaug_ref, gt_ref, bkd_ref,
                      lq1_ref, lk1_ref, lq2_ref, lk2_ref, sg_ref, o_ref,
                      qbd_ref, bias_ref, m_ref, acc_ref, sa_ref, sb_ref,
                      *, d, lambda_init):
    T, G, H = ATT_TILE, ATT_HEADS_PER_STEP, DIFF_HEADS
    dv = 2 * d
    hp = pl.program_id(1)
    i = pl.program_id(2)

    @pl.when(i == 0)
    def _():
        rows = lax.broadcasted_iota(jnp.int32, (T, T), 0)
        cols = lax.broadcasted_iota(jnp.int32, (T, T), 1)
        bucket_of_distance = bkd_ref[...]
        for u in range(G):
            h = hp * G + u
            cfar = tab_ref[bfar_ref[0] * H + h]
            by_distance = jnp.full((1, 2 * T), tab_ref[h], F32)
            for bucket in range(1, REL_BUCKETS):
                by_distance = jnp.where(bucket_of_distance >= bucket,
                                        tab_ref[bucket * H + h], by_distance)
            by_distance = (by_distance - cfar) * LOG2E
            toeplitz = pltpu.roll(jnp.broadcast_to(by_distance, (T, 2 * T)),
                                  0, 1, stride=1, stride_axis=0)
            bias_ref[u, 0] = jnp.where(rows <= cols, toeplitz[:, 0:T], -jnp.inf)
            bias_ref[u, 1] = toeplitz[:, T:2 * T]

    def logits_into(dst_ref, w, j, bias_index):
        u = w % G
        start = pl.multiple_of(j * T, T)
        kt = k_ref[0, pl.ds(start, T), u * dv:(u + 1) * dv]
        s = jnp.dot(kt, qbd_ref[w], preferred_element_type=F32)
        if bias_index is None:
            dst_ref[w] = s
        else:
            bias = bias_ref[u, bias_index]
            dst_ref[w, :, 0:T] = s[:, 0:T] + bias
            dst_ref[w, :, T:2 * T] = s[:, T:2 * T] + bias

    def consume(src_ref, w, j):
        u = w % G
        start = pl.multiple_of(j * T, T)
        s = src_ref[w]
        m_old = m_ref[w]
        m_new = jnp.maximum(m_old, jnp.max(s, axis=0, keepdims=True))
        p = jnp.exp2(s - m_new).astype(BF16)
        alpha = jnp.exp2(m_old - m_new)
        pv = jnp.dot(vaug_ref[0, u, :, pl.ds(start, T)], p,
                     preferred_element_type=F32)
        acc_ref[w] = acc_ref[w] * alpha + pv
        m_ref[w] = m_new

    first, second, both = range(G), range(G, 2 * G), range(2 * G)

    def step(dst_ref, j_next, bias_of, src_ref, j_cur, cur_units=both):
        for w in both:
            logits_into(dst_ref, w, j_next, bias_of(w))
            if w in cur_units:
                consume(src_ref, w, j_cur)

    no_bias = lambda w: None

    def start():
        for w in both:
            sub, u = divmod(w, G)
            q = q_ref[0, sub * T:(sub + 1) * T, u * dv:(u + 1) * dv].astype(F32)
            qt = (q * (d ** -0.5 * LOG2E)).T
            row = lax.broadcasted_iota(jnp.int32, qt.shape, 0)
            qbd_ref[w, :, 0:T] = jnp.where(row < d, qt, 0.0).astype(BF16)
            qbd_ref[w, :, T:2 * T] = jnp.where(row >= d, qt, 0.0).astype(BF16)
        m_ref[...] = jnp.full_like(m_ref, -jnp.inf)
        acc_ref[...] = jnp.zeros_like(acc_ref)
        for w in second:
            logits_into(sa_ref, w, 2 * i + 1, 0)
        step(sb_ref, 2 * i, lambda w: 0 if w in first else 1,
             sa_ref, 2 * i + 1, cur_units=second)

    def finish(src_ref, j_cur):
        lam = (jnp.exp(jnp.sum(lq1_ref[...] * lk1_ref[...], keepdims=True))
               - jnp.exp(jnp.sum(lq2_ref[...] * lk2_ref[...], keepdims=True))
               + lambda_init)
        out_scale = sg_ref[...] * (1.0 - lambda_init)
        for w in both:
            sub, u = divmod(w, G)
            consume(src_ref, w, j_cur)
            acc = acc_ref[w]
            inv_l = 1.0 / acc[dv:dv + 1, :]
            o = (acc[:dv, :T] * inv_l[:, :T]
                 - acc[:dv, T:] * (lam * inv_l[:, T:]))
            ms = jnp.mean(o * o, axis=0, keepdims=True)
            on = (o * lax.rsqrt(ms + GN_EPS)) * out_scale
            head_rows = slice(u * dv, (u + 1) * dv)
            sub_cols = slice(sub * T, (sub + 1) * T)
            gate = gt_ref[0, head_rows, sub_cols].astype(F32)
            o_ref[0, head_rows, sub_cols] = (gate * on).astype(o_ref.dtype)

    @pl.when(i == 0)
    def _():
        start()
        finish(sb_ref, 2 * i)

    @pl.when(i >= 1)
    def _():
        def prologue():
            start()
            step(sa_ref, 2 * i - 1, lambda w: 1 if w in first else None,
                 sb_ref, 2 * i)
            step(sb_ref, 0, no_bias, sa_ref, 2 * i - 1)

        def pair(r):
            step(sa_ref, 2 * r - 1, no_bias, sb_ref, 2 * r - 2)
            step(sb_ref, 2 * r, no_bias, sa_ref, 2 * r - 1)

        n_pairs = i - 1
        n_trips = n_pairs // 4

        def trip(t):
            for e in range(1, 5):
                pair(4 * t + e)

        @pl.when(n_trips == 0)
        def _():
            prologue()

        @pl.when(n_trips >= 1)
        def _():
            prologue()
            trip(0)

        def trip_body(t, carry):
            trip(t)
            return carry

        lax.fori_loop(1, n_trips, trip_body, 0)

        @pl.when(n_pairs % 4 >= 2)
        def _():
            first_left = (n_pairs // 4) * 4 + 1
            pair(first_left)
            pair(first_left + 1)

        @pl.when((i - 1) % 2 == 1)
        def _():
            pair(i - 1)
            finish(sb_ref, 2 * i - 2)

        @pl.when((i - 1) % 2 == 0)
        def _():
            finish(sb_ref, 2 * i - 2)


def _t5_bucket(n):
    max_exact = REL_BUCKETS // 2
    nf = jnp.maximum(n, max_exact).astype(F32)
    large = max_exact + (jnp.log(nf / max_exact) / math.log(REL_MAX_DIST / max_exact)
                         * (REL_BUCKETS - max_exact)).astype(jnp.int32)
    large = jnp.minimum(large, REL_BUCKETS - 1)
    return jnp.where(n < max_exact, n, large)


def _diff_attn(qk, vaug, gt, lq1, lk1, lq2, lk2, subln_g, rel_bias, lambda_init):
    B, S, _ = qk.shape
    H, T, G = DIFF_HEADS, ATT_TILE, ATT_HEADS_PER_STEP
    d_inner = gt.shape[1]
    dv = d_inner // H
    d = dv // 2
    gw = G * dv
    n_groups = H // G

    bkd = _t5_bucket(jnp.arange(2 * T, dtype=jnp.int32)[None, :])
    bfar = _t5_bucket(jnp.full((1,), T + 1, jnp.int32))

    smem = pl.BlockSpec(memory_space=pltpu.SMEM)
    vec = pl.BlockSpec((1, d), lambda b, h, i: (0, 0))
    return pl.pallas_call(
        functools.partial(_diff_attn_kernel, d=d, lambda_init=lambda_init),
        grid=(B, n_groups, S // (2 * T)),
        in_specs=[
            smem, smem,
            pl.BlockSpec((1, 2 * T, gw), lambda b, h, i: (b, i, h)),
            pl.BlockSpec((1, S, gw), lambda b, h, i: (b, 0, n_groups + h)),
            pl.BlockSpec((1, G, dv + ONES_ROWS, S), lambda b, h, i: (b, h, 0, 0)),
            pl.BlockSpec((1, gw, 2 * T), lambda b, h, i: (b, h, i)),
            pl.BlockSpec((1, 2 * T), lambda b, h, i: (0, 0)),
            vec, vec, vec, vec,
            pl.BlockSpec((dv, 1), lambda b, h, i: (0, 0)),
        ],
        out_specs=pl.BlockSpec((1, gw, 2 * T), lambda b, h, i: (b, h, i)),
        out_shape=jax.ShapeDtypeStruct((B, d_inner, S), BF16),
        scratch_shapes=[pltpu.VMEM((2 * G, dv, 2 * T), BF16),
                        pltpu.VMEM((G, 2, T, T), F32),
                        pltpu.VMEM((2 * G, 1, 2 * T), F32),
                        pltpu.VMEM((2 * G, dv + ONES_ROWS, 2 * T), F32),
                        pltpu.VMEM((2 * G, T, 2 * T), F32),
                        pltpu.VMEM((2 * G, T, 2 * T), F32)],
        compiler_params=_compiler_params(("parallel", "arbitrary", "arbitrary")),
        name="diff_attn",
    )(rel_bias.astype(F32).reshape(-1), bfar, qk, qk, vaug, gt, bkd,
      lq1.reshape(1, d), lk1.reshape(1, d), lq2.reshape(1, d), lk2.reshape(1, d),
      subln_g.reshape(dv, 1))


def kernel(x, pre_norm_g, post_norm_g, ret_w_in, ret_w_out, diff_w_in, diff_w_out,
           diff_lambda_q1, diff_lambda_k1, diff_lambda_q2, diff_lambda_k2,
           diff_subln_g, rel_bias):
    d_model = x.shape[-1]
    d_inner = ret_w_out.shape[1]
    for i in range(DEPTH):
        j = i // N_MIXERS
        if i % N_MIXERS == 0:
            proj = _norm_matmul_ret(x, pre_norm_g[i], ret_w_in[j].astype(BF16),
                                    d_inner, RET_HEADS)
            y = _retention(proj, d_model, d_inner)
            x = _out_proj(y, ret_w_out[j].astype(BF16), post_norm_g[i], x,
                          transposed_in=False)
        else:
            lambda_init = 0.8 - 0.6 * math.exp(-0.3 * i)
            qk, vaug, gt = _norm_matmul_diff(
                x, pre_norm_g[i], diff_w_in[j].astype(BF16), DIFF_HEADS)
            yt = _diff_attn(qk, vaug, gt, diff_lambda_q1[j], diff_lambda_k1[j],
                            diff_lambda_q2[j], diff_lambda_k2[j], diff_subln_g[j],
                            rel_bias, lambda_init)
            x = _out_proj(yt, diff_w_out[j].astype(BF16), post_norm_g[i], x,
                          transposed_in=True)
    return x
```

```python
import functools
import math

import jax
import jax.numpy as jnp
import numpy as np
from jax import lax
from jax.experimental import pallas as pl
from jax.experimental.pallas import tpu as pltpu

F32 = jnp.float32
BF16 = jnp.bfloat16

DEPTH = 2
N_MIXERS = 2
RET_HEADS = 4
DIFF_HEADS = 16
REL_BUCKETS = 32
REL_MAX_DIST = 128
ROPE_BASE = 10000.0
RMS_EPS = 1e-6
GN_EPS = 1e-5

V7X_VMEM_BYTES = 64 * 1024 * 1024
VMEM_LIMIT_BYTES = V7X_VMEM_BYTES * 15 // 16

PROJ_ROWS = 1024
GATE_COLS = 512
OUT_ROWS = 1024
RET_CHUNK = 256
RET_CHUNKS_PER_STEP = 8
RET_NORM_ROWS = 32
ATT_TILE = 256
ATT_HEADS_PER_STEP = 4
ONES_ROWS = 16
LOG2E = math.log2(math.e)


def _compiler_params(semantics):
    return pltpu.CompilerParams(dimension_semantics=semantics,
                                vmem_limit_bytes=VMEM_LIMIT_BYTES)


def _store_normed(x_ref, g_ref, h_ref):
    x = x_ref[0]
    ms = jnp.mean(x * x, axis=-1, keepdims=True)
    h_ref[...] = (x * lax.rsqrt(ms + RMS_EPS) * g_ref[...]).astype(BF16)


def _silu(x):
    return x * jax.nn.sigmoid(x)


def _norm_matmul_ret_kernel(x_ref, g_ref, w_ref, cos_ref, sin_ref, o_ref, h_ref,
                            *, dk):
    j = pl.program_id(2)

    def project(cols=slice(None)):
        return jnp.dot(h_ref[...], w_ref[:, cols], preferred_element_type=F32)

    @pl.when(j == 0)
    def _():
        _store_normed(x_ref, g_ref, h_ref)
        half = dk // 2
        n_blocks = w_ref.shape[1] // dk
        c, s = cos_ref[...], sin_ref[...]
        for blk in range(n_blocks):
            o = project(slice(blk * dk, (blk + 1) * dk))
            scale = 1.0 if blk < n_blocks // 2 else dk ** -0.5
            x1, x2 = o[:, :half], o[:, half:]
            o_ref[0, :, blk * dk:blk * dk + half] = (
                (x1 * c - x2 * s) * scale).astype(BF16)
            o_ref[0, :, blk * dk + half:(blk + 1) * dk] = (
                (x1 * s + x2 * c) * scale).astype(BF16)

    @pl.when(j == 1)
    def _():
        o_ref[0] = project().astype(BF16)

    @pl.when(j == 2)
    def _():
        n_cols = w_ref.shape[1]
        for c0 in range(0, n_cols, GATE_COLS):
            cols = slice(c0, c0 + GATE_COLS)
            o_ref[0, :, cols] = _silu(project(cols)).astype(BF16)


def _norm_matmul_ret(x, g, w, d_inner, n_heads):
    B, S, D = x.shape
    N = w.shape[1]
    ts, tn = PROJ_ROWS, d_inner
    dk = (N - 2 * d_inner) // 2 // n_heads
    half = dk // 2
    assert N == 3 * tn, "q | k must fill exactly one column tile"

    pos = np.arange(S, dtype=np.float64)
    inv = ROPE_BASE ** (-np.arange(half, dtype=np.float64) / half)
    ang = pos[:, None] * inv[None, :]
    cos = jnp.asarray(np.cos(ang), dtype=F32)
    sin = jnp.asarray(np.sin(ang), dtype=F32)

    return pl.pallas_call(
        functools.partial(_norm_matmul_ret_kernel, dk=dk),
        grid=(B, S // ts, N // tn),
        in_specs=[pl.BlockSpec((1, ts, D), lambda b, i, j: (b, i, 0)),
                  pl.BlockSpec((1, D), lambda b, i, j: (0, 0)),
                  pl.BlockSpec((D, tn), lambda b, i, j: (0, j)),
                  pl.BlockSpec((ts, half), lambda b, i, j: (i, 0)),
                  pl.BlockSpec((ts, half), lambda b, i, j: (i, 0))],
        out_specs=pl.BlockSpec((1, ts, tn), lambda b, i, j: (b, i, j)),
        out_shape=jax.ShapeDtypeStruct((B, S, N), BF16),
        scratch_shapes=[pltpu.VMEM((ts, D), BF16)],
        compiler_params=_compiler_params(("parallel", "parallel", "arbitrary")),
        name="norm_matmul_ret",
    )(x, g.reshape(1, D), w, cos, sin)


def _norm_matmul_diff_kernel(x_ref, g_ref, w_ref, qk_ref, vaug_ref, gt_ref, h_ref):
    j = pl.program_id(2)

    @pl.when(j == 0)
    def _():
        _store_normed(x_ref, g_ref, h_ref)

    def project():
        return jnp.dot(h_ref[...], w_ref[...], preferred_element_type=F32)

    @pl.when(j < 2)
    def _():
        qk_ref[0] = project().astype(BF16)

    @pl.when(j == 2)
    def _():
        n_heads, rows_aug, ts = vaug_ref.shape[1:]
        dv = w_ref.shape[1] // n_heads
        for h in range(0, n_heads, 2):
            ot = jnp.dot(h_ref[...], w_ref[:, h * dv:(h + 2) * dv],
                         preferred_element_type=F32).T.astype(BF16)
            for e in range(2):
                vaug_ref[0, h + e, 0:dv, :] = ot[e * dv:(e + 1) * dv, :]
                vaug_ref[0, h + e, dv:rows_aug, :] = jnp.ones(
                    (rows_aug - dv, ts), BF16)

    @pl.when(j == 3)
    def _():
        gt_ref[0] = _silu(project()).T.astype(BF16)


def _norm_matmul_diff(x, g, w, n_heads):
    B, S, D = x.shape
    Di = w.shape[1] // 4
    dv = Di // n_heads
    ts = PROJ_ROWS
    return pl.pallas_call(
        _norm_matmul_diff_kernel,
        grid=(B, S // ts, 4),
        in_specs=[pl.BlockSpec((1, ts, D), lambda b, i, j: (b, i, 0)),
                  pl.BlockSpec((1, D), lambda b, i, j: (0, 0)),
                  pl.BlockSpec((D, Di), lambda b, i, j: (0, j))],
        out_specs=[
            pl.BlockSpec((1, ts, Di), lambda b, i, j: (b, i, jnp.minimum(j, 1))),
            pl.BlockSpec((1, n_heads, dv + ONES_ROWS, ts),
                         lambda b, i, j: (b, 0, 0, i)),
            pl.BlockSpec((1, Di, ts), lambda b, i, j: (b, 0, i)),
        ],
        out_shape=[jax.ShapeDtypeStruct((B, S, 2 * Di), BF16),
                   jax.ShapeDtypeStruct((B, n_heads, dv + ONES_ROWS, S), BF16),
                   jax.ShapeDtypeStruct((B, Di, S), BF16)],
        scratch_shapes=[pltpu.VMEM((ts, D), BF16)],
        compiler_params=_compiler_params(("parallel", "parallel", "arbitrary")),
        name="norm_matmul_diff",
    )(x, g.reshape(1, D), w)


def _out_proj_kernel(y_ref, w_ref, g_ref, x_ref, o_ref, *, transposed_in):
    y = y_ref[0]
    if transposed_in:
        o = lax.dot_general(y, w_ref[...], (((0,), (0,)), ((), ())),
                            preferred_element_type=F32)
    else:
        o = jnp.dot(y, w_ref[...], preferred_element_type=F32)
    ms = jnp.mean(o * o, axis=-1, keepdims=True)
    o_ref[0] = x_ref[0] + o * lax.rsqrt(ms + RMS_EPS) * g_ref[...]


def _out_proj(y, w, g, x, *, transposed_in):
    B, S, D = x.shape
    Di = w.shape[0]
    ts = OUT_ROWS
    if transposed_in:
        y_spec = pl.BlockSpec((1, Di, ts), lambda b, i: (b, 0, i))
    else:
        y_spec = pl.BlockSpec((1, ts, Di), lambda b, i: (b, i, 0))
    return pl.pallas_call(
        functools.partial(_out_proj_kernel, transposed_in=transposed_in),
        grid=(B, S // ts),
        in_specs=[y_spec,
                  pl.BlockSpec((Di, D), lambda b, i: (0, 0)),
                  pl.BlockSpec((1, D), lambda b, i: (0, 0)),
                  pl.BlockSpec((1, ts, D), lambda b, i: (b, i, 0))],
        out_specs=pl.BlockSpec((1, ts, D), lambda b, i: (b, i, 0)),
        out_shape=jax.ShapeDtypeStruct((B, S, D), F32),
        compiler_params=_compiler_params(("parallel", "parallel")),
        name="out_proj_t" if transposed_in else "out_proj",
    )(y, w, g.reshape(1, D), x)


def _retention_kernel(q_ref, k_ref, v_ref, g_ref, dmat_ref, xi_ref, zeta_ref,
                      cdec_ref, o_ref, state_ref, state_bf_ref):
    C = RET_CHUNK

    @pl.when(pl.program_id(2) == 0)
    def _():
        state_ref[...] = jnp.zeros_like(state_ref)
        state_bf_ref[...] = jnp.zeros_like(state_bf_ref)

    dmat = dmat_ref[0]
    xi = xi_ref[0]
    zeta = zeta_ref[0]
    cdec = cdec_ref[0]

    for ci in range(RET_CHUNKS_PER_STEP):
        rows = pl.ds(ci * C, C)
        q, k, v = q_ref[0, rows, :], k_ref[0, rows, :], v_ref[0, rows, :]
        scores = lax.dot_general(q, k, (((1,), (1,)), ((), ())),
                                 preferred_element_type=F32) * dmat
        cross = jnp.dot(q, state_bf_ref[...], preferred_element_type=F32) * xi
        kz = (k.astype(F32) * zeta).astype(BF16)
        state = state_ref[...] * cdec + lax.dot_general(
            kz, v, (((0,), (0,)), ((), ())), preferred_element_type=F32)
        state_ref[...] = state
        state_bf_ref[...] = state.astype(BF16)
        inner = jnp.dot(scores.astype(BF16), v, preferred_element_type=F32)
        for r0 in range(0, C, RET_NORM_ROWS):
            blk = slice(r0, r0 + RET_NORM_ROWS)
            ret = inner[blk] + cross[blk]
            mu = jnp.mean(ret, axis=-1, keepdims=True)
            cen = ret - mu
            var = jnp.mean(cen * cen, axis=-1, keepdims=True)
            y = cen * lax.rsqrt(var + GN_EPS)
            out_rows = pl.ds(ci * C + r0, RET_NORM_ROWS)
            o_ref[0, out_rows, :] = (
                g_ref[0, out_rows, :].astype(F32) * y).astype(o_ref.dtype)


def _retention(proj, d_model, d_inner):
    B, S, _ = proj.shape
    H, C = RET_HEADS, RET_CHUNK
    dk, dv = d_model // H, d_inner // H
    rows = C * RET_CHUNKS_PER_STEP
    k_off = d_model // dk
    v_off = 2 * d_model // dv
    g_off = (2 * d_model + d_inner) // dv

    log_gamma = jnp.log1p(-jnp.exp2(-5.0 - jnp.arange(H, dtype=F32)))
    idx = jnp.arange(C, dtype=F32)
    rel = idx[:, None] - idx[None, :]
    dmat = jnp.exp(log_gamma[:, None, None] * jnp.maximum(rel, 0.0)) * (rel >= 0)
    xi = jnp.exp(log_gamma[:, None] * (idx + 1.0))
    zeta = jnp.exp(log_gamma[:, None] * (C - 1.0 - idx))
    cdec = jnp.exp(log_gamma * C)
    xi = jnp.broadcast_to(xi[:, :, None], (H, C, dv))
    zeta = jnp.broadcast_to(zeta[:, :, None], (H, C, dk))
    cdec = jnp.broadcast_to(cdec[:, None, None], (H, 1, dv))

    return pl.pallas_call(
        _retention_kernel,
        grid=(B, H, S // rows),
        in_specs=[
            pl.BlockSpec((1, rows, dk), lambda b, h, n: (b, n, h)),
            pl.BlockSpec((1, rows, dk), lambda b, h, n: (b, n, k_off + h)),
            pl.BlockSpec((1, rows, dv), lambda b, h, n: (b, n, v_off + h)),
            pl.BlockSpec((1, rows, dv), lambda b, h, n: (b, n, g_off + h)),
            pl.BlockSpec((1, C, C), lambda b, h, n: (h, 0, 0)),
            pl.BlockSpec((1, C, dv), lambda b, h, n: (h, 0, 0)),
            pl.BlockSpec((1, C, dk), lambda b, h, n: (h, 0, 0)),
            pl.BlockSpec((1, 1, dv), lambda b, h, n: (h, 0, 0)),
        ],
        out_specs=pl.BlockSpec((1, rows, dv), lambda b, h, n: (b, n, h)),
        out_shape=jax.ShapeDtypeStruct((B, S, d_inner), BF16),
        scratch_shapes=[pltpu.VMEM((dk, dv), F32), pltpu.VMEM((dk, dv), BF16)],
        compiler_params=_compiler_params(("parallel", "parallel", "arbitrary")),
        name="retention",
    )(proj, proj, proj, proj, dmat, xi, zeta, cdec)


def _diff_attn_kernel(tab_ref, bfar_ref, q_ref, k_ref, vaug_ref, gt_ref, bkd_ref,
                      lq1_ref, lk1_ref, lq2_ref, lk2_ref, sg_ref, o_ref,
                      qbd_ref, bias_ref, m_ref, acc_ref, sa_ref, sb_ref,
                      *, d, lambda_init):
    T, G, H = ATT_TILE, ATT_HEADS_PER_STEP, DIFF_HEADS
    dv = 2 * d
    hp = pl.program_id(1)
    i = pl.program_id(2)

    @pl.when(i == 0)
    def _():
        rows = lax.broadcasted_iota(jnp.int32, (T, T), 0)
        cols = lax.broadcasted_iota(jnp.int32, (T, T), 1)
        bucket_of_distance = bkd_ref[...]
        for u in range(G):
            h = hp * G + u
            cfar = tab_ref[bfar_ref[0] * H + h]
            by_distance = jnp.full((1, 2 * T), tab_ref[h], F32)
            for bucket in range(1, REL_BUCKETS):
                by_distance = jnp.where(bucket_of_distance >= bucket,
                                        tab_ref[bucket * H + h], by_distance)
            by_distance = (by_distance - cfar) * LOG2E
            toeplitz = pltpu.roll(jnp.broadcast_to(by_distance, (T, 2 * T)),
                                  0, 1, stride=1, stride_axis=0)
            bias_ref[u, 0] = jnp.where(rows <= cols, toeplitz[:, 0:T], -jnp.inf)
            bias_ref[u, 1] = toeplitz[:, T:2 * T]

    def logits_into(dst_ref, w, j, bias_index):
        u = w % G
        start = pl.multiple_of(j * T, T)
        kt = k_ref[0, pl.ds(start, T), u * dv:(u + 1) * dv]
        s = jnp.dot(kt, qbd_ref[w], preferred_element_type=F32)
        if bias_index is None:
            dst_ref[w] = s
        else:
            bias = bias_ref[u, bias_index]
            dst_ref[w, :, 0:T] = s[:, 0:T] + bias
            dst_ref[w, :, T:2 * T] = s[:, T:2 * T] + bias

    def consume(src_ref, w, j):
        u = w % G
        start = pl.multiple_of(j * T, T)
        s = src_ref[w]
        m_old = m_ref[w]
        m_new = jnp.maximum(m_old, jnp.max(s, axis=0, keepdims=True))
        p = jnp.exp2(s - m_new).astype(BF16)
        alpha = jnp.exp2(m_old - m_new)
        pv = jnp.dot(vaug_ref[0, u, :, pl.ds(start, T)], p,
                     preferred_element_type=F32)
        acc_ref[w] = acc_ref[w] * alpha + pv
        m_ref[w] = m_new

    first, second, both = range(G), range(G, 2 * G), range(2 * G)

    def step(dst_ref, j_next, bias_of, src_ref, j_cur, cur_units=both):
        for w in both:
            logits_into(dst_ref, w, j_next, bias_of(w))
            if w in cur_units:
                consume(src_ref, w, j_cur)

    no_bias = lambda w: None

    def start():
        for w in both:
            sub, u = divmod(w, G)
            q = q_ref[0, sub * T:(sub + 1) * T, u * dv:(u + 1) * dv].astype(F32)
            qt = (q * (d ** -0.5 * LOG2E)).T
            row = lax.broadcasted_iota(jnp.int32, qt.shape, 0)
            qbd_ref[w, :, 0:T] = jnp.where(row < d, qt, 0.0).astype(BF16)
            qbd_ref[w, :, T:2 * T] = jnp.where(row >= d, qt, 0.0).astype(BF16)
        m_ref[...] = jnp.full_like(m_ref, -jnp.inf)
        acc_ref[...] = jnp.zeros_like(acc_ref)
        for w in second:
            logits_into(sa_ref, w, 2 * i + 1, 0)
        step(sb_ref, 2 * i, lambda w: 0 if w in first else 1,
             sa_ref, 2 * i + 1, cur_units=second)

    def finish(src_ref, j_cur):
        lam = (jnp.exp(jnp.sum(lq1_ref[...] * lk1_ref[...], keepdims=True))
               - jnp.exp(jnp.sum(lq2_ref[...] * lk2_ref[...], keepdims=True))
               + lambda_init)
        out_scale = sg_ref[...] * (1.0 - lambda_init)
        for w in both:
            sub, u = divmod(w, G)
            consume(src_ref, w, j_cur)
            acc = acc_ref[w]
            inv_l = 1.0 / acc[dv:dv + 1, :]
            o = (acc[:dv, :T] * inv_l[:, :T]
                 - acc[:dv, T:] * (lam * inv_l[:, T:]))
            ms = jnp.mean(o * o, axis=0, keepdims=True)
            on = (o * lax.rsqrt(ms + GN_EPS)) * out_scale
            head_rows = slice(u * dv, (u + 1) * dv)
            sub_cols = slice(sub * T, (sub + 1) * T)
            gate = gt_ref[0, head_rows, sub_cols].astype(F32)
            o_ref[0, head_rows, sub_cols] = (gate * on).astype(o_ref.dtype)

    @pl.when(i == 0)
    def _():
        start()
        finish(sb_ref, 2 * i)

    @pl.when(i >= 1)
    def _():
        def prologue():
            start()
            step(sa_ref, 2 * i - 1, lambda w: 1 if w in first else None,
                 sb_ref, 2 * i)
            step(sb_ref, 0, no_bias, sa_ref, 2 * i - 1)

        def pair(r):
            step(sa_ref, 2 * r - 1, no_bias, sb_ref, 2 * r - 2)
            step(sb_ref, 2 * r, no_bias, sa_ref, 2 * r - 1)

        n_pairs = i - 1
        n_trips = n_pairs // 4

        def trip(t):
            for e in range(1, 5):
                pair(4 * t + e)

        @pl.when(n_trips == 0)
        def _():
            prologue()

        @pl.when(n_trips >= 1)
        def _():
            prologue()
            trip(0)

        def trip_body(t, carry):
            trip(t)
            return carry

        lax.fori_loop(1, n_trips, trip_body, 0)

        @pl.when(n_pairs % 4 >= 2)
        def _():
            first_left = (n_pairs // 4) * 4 + 1
            pair(first_left)
            pair(first_left + 1)

        @pl.when((i - 1) % 2 == 1)
        def _():
            pair(i - 1)
            finish(sb_ref, 2 * i - 2)

        @pl.when((i - 1) % 2 == 0)
        def _():
            finish(sb_ref, 2 * i - 2)


def _t5_bucket(n):
    max_exact = REL_BUCKETS // 2
    nf = jnp.maximum(n, max_exact).astype(F32)
    large = max_exact + (jnp.log(nf / max_exact) / math.log(REL_MAX_DIST / max_exact)
                         * (REL_BUCKETS - max_exact)).astype(jnp.int32)
    large = jnp.minimum(large, REL_BUCKETS - 1)
    return jnp.where(n < max_exact, n, large)


def _diff_attn(qk, vaug, gt, lq1, lk1, lq2, lk2, subln_g, rel_bias, lambda_init):
    B, S, _ = qk.shape
    H, T, G = DIFF_HEADS, ATT_TILE, ATT_HEADS_PER_STEP
    d_inner = gt.shape[1]
    dv = d_inner // H
    d = dv // 2
    gw = G * dv
    n_groups = H // G

    bkd = _t5_bucket(jnp.arange(2 * T, dtype=jnp.int32)[None, :])
    bfar = _t5_bucket(jnp.full((1,), T + 1, jnp.int32))

    smem = pl.BlockSpec(memory_space=pltpu.SMEM)
    vec = pl.BlockSpec((1, d), lambda b, h, i: (0, 0))
    return pl.pallas_call(
        functools.partial(_diff_attn_kernel, d=d, lambda_init=lambda_init),
        grid=(B, n_groups, S // (2 * T)),
        in_specs=[
            smem, smem,
            pl.BlockSpec((1, 2 * T, gw), lambda b, h, i: (b, i, h)),
            pl.BlockSpec((1, S, gw), lambda b, h, i: (b, 0, n_groups + h)),
            pl.BlockSpec((1, G, dv + ONES_ROWS, S), lambda b, h, i: (b, h, 0, 0)),
            pl.BlockSpec((1, gw, 2 * T), lambda b, h, i: (b, h, i)),
            pl.BlockSpec((1, 2 * T), lambda b, h, i: (0, 0)),
            vec, vec, vec, vec,
            pl.BlockSpec((dv, 1), lambda b, h, i: (0, 0)),
        ],
        out_specs=pl.BlockSpec((1, gw, 2 * T), lambda b, h, i: (b, h, i)),
        out_shape=jax.ShapeDtypeStruct((B, d_inner, S), BF16),
        scratch_shapes=[pltpu.VMEM((2 * G, dv, 2 * T), BF16),
                        pltpu.VMEM((G, 2, T, T), F32),
                        pltpu.VMEM((2 * G, 1, 2 * T), F32),
                        pltpu.VMEM((2 * G, dv + ONES_ROWS, 2 * T), F32),
                        pltpu.VMEM((2 * G, T, 2 * T), F32),
                        pltpu.VMEM((2 * G, T, 2 * T), F32)],
        compiler_params=_compiler_params(("parallel", "arbitrary", "arbitrary")),
        name="diff_attn",
    )(rel_bias.astype(F32).reshape(-1), bfar, qk, qk, vaug, gt, bkd,
      lq1.reshape(1, d), lk1.reshape(1, d), lq2.reshape(1, d), lk2.reshape(1, d),
      subln_g.reshape(dv, 1))


def kernel(x, pre_norm_g, post_norm_g, ret_w_in, ret_w_out, diff_w_in, diff_w_out,
           diff_lambda_q1, diff_lambda_k1, diff_lambda_q2, diff_lambda_k2,
           diff_subln_g, rel_bias):
    d_model = x.shape[-1]
    d_inner = ret_w_out.shape[1]
    for i in range(DEPTH):
        j = i // N_MIXERS
        if i % N_MIXERS == 0:
            proj = _norm_matmul_ret(x, pre_norm_g[i], ret_w_in[j].astype(BF16),
                                    d_inner, RET_HEADS)
            y = _retention(proj, d_model, d_inner)
            x = _out_proj(y, ret_w_out[j].astype(BF16), post_norm_g[i], x,
                          transposed_in=False)
        else:
            lambda_init = 0.8 - 0.6 * math.exp(-0.3 * i)
            qk, vaug, gt = _norm_matmul_diff(
                x, pre_norm_g[i], diff_w_in[j].astype(BF16), DIFF_HEADS)
            yt = _diff_attn(qk, vaug, gt, diff_lambda_q1[j], diff_lambda_k1[j],
                            diff_lambda_q2[j], diff_lambda_k2[j], diff_subln_g[j],
                            rel_bias, lambda_init)
            x = _out_proj(yt, diff_w_out[j].astype(BF16), post_norm_g[i], x,
                          transposed_in=True)
    return x
```

```python
import functools
import math

import jax
import jax.numpy as jnp
import numpy as np
from jax import lax
from jax.experimental import pallas as pl
from jax.experimental.pallas import tpu as pltpu

F32 = jnp.float32
BF16 = jnp.bfloat16

DEPTH = 2
N_MIXERS = 2
RET_HEADS = 4
DIFF_HEADS = 16
REL_BUCKETS = 32
REL_MAX_DIST = 128
ROPE_BASE = 10000.0
RMS_EPS = 1e-6
GN_EPS = 1e-5

V7X_VMEM_BYTES = 64 * 1024 * 1024
VMEM_LIMIT_BYTES = V7X_VMEM_BYTES * 15 // 16

PROJ_ROWS = 1024
OUT_ROWS = 1024
RET_CHUNK = 256
RET_CHUNKS_PER_STEP = 8
RET_NORM_ROWS = 64
ATT_TILE = 256
ATT_HEADS_PER_STEP = 4
ONES_ROWS = 16
LOG2E = math.log2(math.e)


def _compiler_params(semantics):
    return pltpu.CompilerParams(dimension_semantics=semantics,
                                vmem_limit_bytes=VMEM_LIMIT_BYTES)


def _store_normed(x_ref, g_ref, h_ref):
    x = x_ref[0]
    ms = jnp.mean(x * x, axis=-1, keepdims=True)
    h_ref[...] = (x * lax.rsqrt(ms + RMS_EPS) * g_ref[...]).astype(BF16)


def _silu(x):
    return x * jax.nn.sigmoid(x)


def _norm_matmul_ret_kernel(x_ref, g_ref, w_ref, cos_ref, sin_ref, o_ref, h_ref,
                            *, dk):
    j = pl.program_id(2)

    def project(cols=slice(None)):
        return jnp.dot(h_ref[...], w_ref[:, cols], preferred_element_type=F32)

    @pl.when(j == 0)
    def _():
        _store_normed(x_ref, g_ref, h_ref)
        half = dk // 2
        n_blocks = w_ref.shape[1] // dk
        c, s = cos_ref[...], sin_ref[...]
        for blk in range(n_blocks):
            o = project(slice(blk * dk, (blk + 1) * dk))
            scale = 1.0 if blk < n_blocks // 2 else dk ** -0.5
            x1, x2 = o[:, :half], o[:, half:]
            o_ref[0, :, blk * dk:blk * dk + half] = (
                (x1 * c - x2 * s) * scale).astype(BF16)
            o_ref[0, :, blk * dk + half:(blk + 1) * dk] = (
                (x1 * s + x2 * c) * scale).astype(BF16)

    @pl.when(j == 1)
    def _():
        o_ref[0] = project().astype(BF16)

    @pl.when(j == 2)
    def _():
        o_ref[0] = _silu(project()).astype(BF16)


def _norm_matmul_ret(x, g, w, d_inner, n_heads):
    B, S, D = x.shape
    N = w.shape[1]
    ts, tn = PROJ_ROWS, d_inner
    dk = (N - 2 * d_inner) // 2 // n_heads
    half = dk // 2
    assert N == 3 * tn, "q | k must fill exactly one column tile"

    pos = np.arange(S, dtype=np.float64)
    inv = ROPE_BASE ** (-np.arange(half, dtype=np.float64) / half)
    ang = pos[:, None] * inv[None, :]
    cos = jnp.asarray(np.cos(ang), dtype=F32)
    sin = jnp.asarray(np.sin(ang), dtype=F32)

    return pl.pallas_call(
        functools.partial(_norm_matmul_ret_kernel, dk=dk),
        grid=(B, S // ts, N // tn),
        in_specs=[pl.BlockSpec((1, ts, D), lambda b, i, j: (b, i, 0)),
                  pl.BlockSpec((1, D), lambda b, i, j: (0, 0)),
                  pl.BlockSpec((D, tn), lambda b, i, j: (0, j)),
                  pl.BlockSpec((ts, half), lambda b, i, j: (i, 0)),
                  pl.BlockSpec((ts, half), lambda b, i, j: (i, 0))],
        out_specs=pl.BlockSpec((1, ts, tn), lambda b, i, j: (b, i, j)),
        out_shape=jax.ShapeDtypeStruct((B, S, N), BF16),
        scratch_shapes=[pltpu.VMEM((ts, D), BF16)],
        compiler_params=_compiler_params(("parallel", "parallel", "arbitrary")),
        name="norm_matmul_ret",
    )(x, g.reshape(1, D), w, cos, sin)


def _norm_matmul_diff_kernel(x_ref, g_ref, w_ref, qk_ref, vaug_ref, gt_ref, h_ref):
    j = pl.program_id(2)

    @pl.when(j == 0)
    def _():
        _store_normed(x_ref, g_ref, h_ref)

    def project():
        return jnp.dot(h_ref[...], w_ref[...], preferred_element_type=F32)

    @pl.when(j < 2)
    def _():
        qk_ref[0] = project().astype(BF16)

    @pl.when(j == 2)
    def _():
        n_heads, rows_aug, ts = vaug_ref.shape[1:]
        dv = w_ref.shape[1] // n_heads
        for h in range(0, n_heads, 2):
            ot = jnp.dot(h_ref[...], w_ref[:, h * dv:(h + 2) * dv],
                         preferred_element_type=F32).T.astype(BF16)
            for e in range(2):
                vaug_ref[0, h + e, 0:dv, :] = ot[e * dv:(e + 1) * dv, :]
                vaug_ref[0, h + e, dv:rows_aug, :] = jnp.ones(
                    (rows_aug - dv, ts), BF16)

    @pl.when(j == 3)
    def _():
        gt_ref[0] = _silu(project()).T.astype(BF16)


def _norm_matmul_diff(x, g, w, n_heads):
    B, S, D = x.shape
    Di = w.shape[1] // 4
    dv = Di // n_heads
    ts = PROJ_ROWS
    return pl.pallas_call(
        _norm_matmul_diff_kernel,
        grid=(B, S // ts, 4),
        in_specs=[pl.BlockSpec((1, ts, D), lambda b, i, j: (b, i, 0)),
                  pl.BlockSpec((1, D), lambda b, i, j: (0, 0)),
                  pl.BlockSpec((D, Di), lambda b, i, j: (0, j))],
        out_specs=[
            pl.BlockSpec((1, ts, Di), lambda b, i, j: (b, i, jnp.minimum(j, 1))),
            pl.BlockSpec((1, n_heads, dv + ONES_ROWS, ts),
                         lambda b, i, j: (b, 0, 0, i)),
            pl.BlockSpec((1, Di, ts), lambda b, i, j: (b, 0, i)),
        ],
        out_shape=[jax.ShapeDtypeStruct((B, S, 2 * Di), BF16),
                   jax.ShapeDtypeStruct((B, n_heads, dv + ONES_ROWS, S), BF16),
                   jax.ShapeDtypeStruct((B, Di, S), BF16)],
        scratch_shapes=[pltpu.VMEM((ts, D), BF16)],
        compiler_params=_compiler_params(("parallel", "parallel", "arbitrary")),
        name="norm_matmul_diff",
    )(x, g.reshape(1, D), w)


def _out_proj_kernel(y_ref, w_ref, g_ref, x_ref, o_ref, *, transposed_in):
    y = y_ref[0]
    if transposed_in:
        o = lax.dot_general(y, w_ref[...], (((0,), (0,)), ((), ())),
                            preferred_element_type=F32)
    else:
        o = jnp.dot(y, w_ref[...], preferred_element_type=F32)
    ms = jnp.mean(o * o, axis=-1, keepdims=True)
    o_ref[0] = x_ref[0] + o * lax.rsqrt(ms + RMS_EPS) * g_ref[...]


def _out_proj(y, w, g, x, *, transposed_in):
    B, S, D = x.shape
    Di = w.shape[0]
    ts = OUT_ROWS
    if transposed_in:
        y_spec = pl.BlockSpec((1, Di, ts), lambda b, i: (b, 0, i))
    else:
        y_spec = pl.BlockSpec((1, ts, Di), lambda b, i: (b, i, 0))
    return pl.pallas_call(
        functools.partial(_out_proj_kernel, transposed_in=transposed_in),
        grid=(B, S // ts),
        in_specs=[y_spec,
                  pl.BlockSpec((Di, D), lambda b, i: (0, 0)),
                  pl.BlockSpec((1, D), lambda b, i: (0, 0)),
                  pl.BlockSpec((1, ts, D), lambda b, i: (b, i, 0))],
        out_specs=pl.BlockSpec((1, ts, D), lambda b, i: (b, i, 0)),
        out_shape=jax.ShapeDtypeStruct((B, S, D), F32),
        compiler_params=_compiler_params(("parallel", "parallel")),
        name="out_proj_t" if transposed_in else "out_proj",
    )(y, w, g.reshape(1, D), x)


def _retention_kernel(q_ref, k_ref, v_ref, g_ref, dmat_ref, xi_ref, zeta_ref,
                      cdec_ref, o_ref, state_ref, state_bf_ref):
    C = RET_CHUNK

    @pl.when(pl.program_id(2) == 0)
    def _():
        state_ref[...] = jnp.zeros_like(state_ref)
        state_bf_ref[...] = jnp.zeros_like(state_bf_ref)

    dmat = dmat_ref[0]
    xi = xi_ref[0]
    zeta = zeta_ref[0]
    cdec = cdec_ref[0]

    for ci in range(RET_CHUNKS_PER_STEP):
        rows = pl.ds(ci * C, C)
        q, k, v = q_ref[0, rows, :], k_ref[0, rows, :], v_ref[0, rows, :]
        scores = lax.dot_general(q, k, (((1,), (1,)), ((), ())),
                                 preferred_element_type=F32) * dmat
        cross = jnp.dot(q, state_bf_ref[...], preferred_element_type=F32) * xi
        kz = (k.astype(F32) * zeta).astype(BF16)
        state = state_ref[...] * cdec + lax.dot_general(
            kz, v, (((0,), (0,)), ((), ())), preferred_element_type=F32)
        state_ref[...] = state
        state_bf_ref[...] = state.astype(BF16)
        inner = jnp.dot(scores.astype(BF16), v, preferred_element_type=F32)
        for r0 in range(0, C, RET_NORM_ROWS):
            blk = slice(r0, r0 + RET_NORM_ROWS)
            ret = inner[blk] + cross[blk]
            mu = jnp.mean(ret, axis=-1, keepdims=True)
            cen = ret - mu
            var = jnp.mean(cen * cen, axis=-1, keepdims=True)
            y = cen * lax.rsqrt(var + GN_EPS)
            out_rows = pl.ds(ci * C + r0, RET_NORM_ROWS)
            o_ref[0, out_rows, :] = (
                g_ref[0, out_rows, :].astype(F32) * y).astype(o_ref.dtype)


def _retention(proj, d_model, d_inner):
    B, S, _ = proj.shape
    H, C = RET_HEADS, RET_CHUNK
    dk, dv = d_model // H, d_inner // H
    rows = C * RET_CHUNKS_PER_STEP
    k_off = d_model // dk
    v_off = 2 * d_model // dv
    g_off = (2 * d_model + d_inner) // dv

    log_gamma = jnp.log1p(-jnp.exp2(-5.0 - jnp.arange(H, dtype=F32)))
    idx = jnp.arange(C, dtype=F32)
    rel = idx[:, None] - idx[None, :]
    dmat = jnp.exp(log_gamma[:, None, None] * jnp.maximum(rel, 0.0)) * (rel >= 0)
    xi = jnp.exp(log_gamma[:, None] * (idx + 1.0))
    zeta = jnp.exp(log_gamma[:, None] * (C - 1.0 - idx))
    cdec = jnp.exp(log_gamma * C)
    xi = jnp.broadcast_to(xi[:, :, None], (H, C, dv))
    zeta = jnp.broadcast_to(zeta[:, :, None], (H, C, dk))
    cdec = jnp.broadcast_to(cdec[:, None, None], (H, 1, dv))

    return pl.pallas_call(
        _retention_kernel,
        grid=(B, H, S // rows),
        in_specs=[
            pl.BlockSpec((1, rows, dk), lambda b, h, n: (b, n, h)),
            pl.BlockSpec((1, rows, dk), lambda b, h, n: (b, n, k_off + h)),
            pl.BlockSpec((1, rows, dv), lambda b, h, n: (b, n, v_off + h)),
            pl.BlockSpec((1, rows, dv), lambda b, h, n: (b, n, g_off + h)),
            pl.BlockSpec((1, C, C), lambda b, h, n: (h, 0, 0)),
            pl.BlockSpec((1, C, dv), lambda b, h, n: (h, 0, 0)),
            pl.BlockSpec((1, C, dk), lambda b, h, n: (h, 0, 0)),
            pl.BlockSpec((1, 1, dv), lambda b, h, n: (h, 0, 0)),
        ],
        out_specs=pl.BlockSpec((1, rows, dv), lambda b, h, n: (b, n, h)),
        out_shape=jax.ShapeDtypeStruct((B, S, d_inner), BF16),
        scratch_shapes=[pltpu.VMEM((dk, dv), F32), pltpu.VMEM((dk, dv), BF16)],
        compiler_params=_compiler_params(("parallel", "parallel", "arbitrary")),
        name="retention",
    )(proj, proj, proj, proj, dmat, xi, zeta, cdec)


def _diff_attn_kernel(tab_ref, bfar_ref, q_ref, k_ref, vaug_ref, gt_ref, bkd_ref,
                      lq1_ref, lk1_ref, lq2_ref, lk2_ref, sg_ref, o_ref,
                      qbd_ref, bias_ref, m_ref, acc_ref, sa_ref, sb_ref,
                      *, d, lambda_init):
    T, G, H = ATT_TILE, ATT_HEADS_PER_STEP, DIFF_HEADS
    dv = 2 * d
    hp = pl.program_id(1)
    i = pl.program_id(2)

    @pl.when(i == 0)
    def _():
        rows = lax.broadcasted_iota(jnp.int32, (T, T), 0)
        cols = lax.broadcasted_iota(jnp.int32, (T, T), 1)
        bucket_of_distance = bkd_ref[...]
        for u in range(G):
            h = hp * G + u
            cfar = tab_ref[bfar_ref[0] * H + h]
            by_distance = jnp.full((1, 2 * T), tab_ref[h], F32)
            for bucket in range(1, REL_BUCKETS):
                by_distance = jnp.where(bucket_of_distance >= bucket,
                                        tab_ref[bucket * H + h], by_distance)
            by_distance = (by_distance - cfar) * LOG2E
            toeplitz = pltpu.roll(jnp.broadcast_to(by_distance, (T, 2 * T)),
                                  0, 1, stride=1, stride_axis=0)
            bias_ref[u, 0] = jnp.where(rows <= cols, toeplitz[:, 0:T], -jnp.inf)
            bias_ref[u, 1] = toeplitz[:, T:2 * T]

    def logits_into(dst_ref, w, j, bias_index):
        u = w % G
        start = pl.multiple_of(j * T, T)
        kt = k_ref[0, pl.ds(start, T), u * dv:(u + 1) * dv]
        s = jnp.dot(kt, qbd_ref[w], preferred_element_type=F32)
        if bias_index is None:
            dst_ref[w] = s
        else:
            bias = bias_ref[u, bias_index]
            dst_ref[w, :, 0:T] = s[:, 0:T] + bias
            dst_ref[w, :, T:2 * T] = s[:, T:2 * T] + bias

    def consume(src_ref, w, j):
        u = w % G
        start = pl.multiple_of(j * T, T)
        s = src_ref[w]
        m_old = m_ref[w]
        m_new = jnp.maximum(m_old, jnp.max(s, axis=0, keepdims=True))
        p = jnp.exp2(s - m_new).astype(BF16)
        alpha = jnp.exp2(m_old - m_new)
        pv = jnp.dot(vaug_ref[0, u, :, pl.ds(start, T)], p,
                     preferred_element_type=F32)
        acc_ref[w] = acc_ref[w] * alpha + pv
        m_ref[w] = m_new

    first, second, both = range(G), range(G, 2 * G), range(2 * G)

    def step(dst_ref, j_next, bias_of, src_ref, j_cur, cur_units=both):
        for w in both:
            logits_into(dst_ref, w, j_next, bias_of(w))
            if w in cur_units:
                consume(src_ref, w, j_cur)

    no_bias = lambda w: None

    def start():
        for w in both:
            sub, u = divmod(w, G)
            q = q_ref[0, sub * T:(sub + 1) * T, u * dv:(u + 1) * dv].astype(F32)
            qt = (q * (d ** -0.5 * LOG2E)).T
            row = lax.broadcasted_iota(jnp.int32, qt.shape, 0)
            qbd_ref[w, :, 0:T] = jnp.where(row < d, qt, 0.0).astype(BF16)
            qbd_ref[w, :, T:2 * T] = jnp.where(row >= d, qt, 0.0).astype(BF16)
        m_ref[...] = jnp.full_like(m_ref, -jnp.inf)
        acc_ref[...] = jnp.zeros_like(acc_ref)
        for w in second:
            logits_into(sa_ref, w, 2 * i + 1, 0)
        step(sb_ref, 2 * i, lambda w: 0 if w in first else 1,
             sa_ref, 2 * i + 1, cur_units=second)

    def finish(src_ref, j_cur):
        lam = (jnp.exp(jnp.sum(lq1_ref[...] * lk1_ref[...], keepdims=True))
               - jnp.exp(jnp.sum(lq2_ref[...] * lk2_ref[...], keepdims=True))
               + lambda_init)
        out_scale = sg_ref[...] * (1.0 - lambda_init)
        for w in both:
            sub, u = divmod(w, G)
            consume(src_ref, w, j_cur)
            acc = acc_ref[w]
            inv_l = 1.0 / acc[dv:dv + 1, :]
            o = (acc[:dv, :T] * inv_l[:, :T]
                 - acc[:dv, T:] * (lam * inv_l[:, T:]))
            ms = jnp.mean(o * o, axis=0, keepdims=True)
            on = (o * lax.rsqrt(ms + GN_EPS)) * out_scale
            head_rows = slice(u * dv, (u + 1) * dv)
            sub_cols = slice(sub * T, (sub + 1) * T)
            gate = gt_ref[0, head_rows, sub_cols].astype(F32)
            o_ref[0, head_rows, sub_cols] = (gate * on).astype(o_ref.dtype)

    @pl.when(i == 0)
    def _():
        start()
        finish(sb_ref, 2 * i)

    @pl.when(i >= 1)
    def _():
        def prologue():
            start()
            step(sa_ref, 2 * i - 1, lambda w: 1 if w in first else None,
                 sb_ref, 2 * i)
            step(sb_ref, 0, no_bias, sa_ref, 2 * i - 1)

        def pair(r):
            step(sa_ref, 2 * r - 1, no_bias, sb_ref, 2 * r - 2)
            step(sb_ref, 2 * r, no_bias, sa_ref, 2 * r - 1)

        n_pairs = i - 1
        n_trips = n_pairs // 4

        def trip(t):
            for e in range(1, 5):
                pair(4 * t + e)

        @pl.when(n_trips == 0)
        def _():
            prologue()

        @pl.when(n_trips >= 1)
        def _():
            prologue()
            trip(0)

        def trip_body(t, carry):
            trip(t)
            return carry

        lax.fori_loop(1, n_trips, trip_body, 0)

        @pl.when(n_pairs % 4 >= 2)
        def _():
            first_left = (n_pairs // 4) * 4 + 1
            pair(first_left)
            pair(first_left + 1)

        @pl.when((i - 1) % 2 == 1)
        def _():
            pair(i - 1)
            finish(sb_ref, 2 * i - 2)

        @pl.when((i - 1) % 2 == 0)
        def _():
            finish(sb_ref, 2 * i - 2)


def _t5_bucket(n):
    max_exact = REL_BUCKETS // 2
    nf = jnp.maximum(n, max_exact).astype(F32)
    large = max_exact + (jnp.log(nf / max_exact) / math.log(REL_MAX_DIST / max_exact)
                         * (REL_BUCKETS - max_exact)).astype(jnp.int32)
    large = jnp.minimum(large, REL_BUCKETS - 1)
    return jnp.where(n < max_exact, n, large)


def _diff_attn(qk, vaug, gt, lq1, lk1, lq2, lk2, subln_g, rel_bias, lambda_init):
    B, S, _ = qk.shape
    H, T, G = DIFF_HEADS, ATT_TILE, ATT_HEADS_PER_STEP
    d_inner = gt.shape[1]
    dv = d_inner // H
    d = dv // 2
    gw = G * dv
    n_groups = H // G

    bkd = _t5_bucket(jnp.arange(2 * T, dtype=jnp.int32)[None, :])
    bfar = _t5_bucket(jnp.full((1,), T + 1, jnp.int32))

    smem = pl.BlockSpec(memory_space=pltpu.SMEM)
    vec = pl.BlockSpec((1, d), lambda b, h, i: (0, 0))
    return pl.pallas_call(
        functools.partial(_diff_attn_kernel, d=d, lambda_init=lambda_init),
        grid=(B, n_groups, S // (2 * T)),
        in_specs=[
            smem, smem,
            pl.BlockSpec((1, 2 * T, gw), lambda b, h, i: (b, i, h)),
            pl.BlockSpec((1, S, gw), lambda b, h, i: (b, 0, n_groups + h)),
            pl.BlockSpec((1, G, dv + ONES_ROWS, S), lambda b, h, i: (b, h, 0, 0)),
            pl.BlockSpec((1, gw, 2 * T), lambda b, h, i: (b, h, i)),
            pl.BlockSpec((1, 2 * T), lambda b, h, i: (0, 0)),
            vec, vec, vec, vec,
            pl.BlockSpec((dv, 1), lambda b, h, i: (0, 0)),
        ],
        out_specs=pl.BlockSpec((1, gw, 2 * T), lambda b, h, i: (b, h, i)),
        out_shape=jax.ShapeDtypeStruct((B, d_inner, S), BF16),
        scratch_shapes=[pltpu.VMEM((2 * G, dv, 2 * T), BF16),
                        pltpu.VMEM((G, 2, T, T), F32),
                        pltpu.VMEM((2 * G, 1, 2 * T), F32),
                        pltpu.VMEM((2 * G, dv + ONES_ROWS, 2 * T), F32),
                        pltpu.VMEM((2 * G, T, 2 * T), F32),
                        pltpu.VMEM((2 * G, T, 2 * T), F32)],
        compiler_params=_compiler_params(("parallel", "arbitrary", "arbitrary")),
        name="diff_attn",
    )(rel_bias.astype(F32).reshape(-1), bfar, qk, qk, vaug, gt, bkd,
      lq1.reshape(1, d), lk1.reshape(1, d), lq2.reshape(1, d), lk2.reshape(1, d),
      subln_g.reshape(dv, 1))


def kernel(x, pre_norm_g, post_norm_g, ret_w_in, ret_w_out, diff_w_in, diff_w_out,
           diff_lambda_q1, diff_lambda_k1, diff_lambda_q2, diff_lambda_k2,
           diff_subln_g, rel_bias):
    d_model = x.shape[-1]
    d_inner = ret_w_out.shape[1]
    for i in range(DEPTH):
        j = i // N_MIXERS
        if i % N_MIXERS == 0:
            proj = _norm_matmul_ret(x, pre_norm_g[i], ret_w_in[j].astype(BF16),
                                    d_inner, RET_HEADS)
            y = _retention(proj, d_model, d_inner)
            x = _out_proj(y, ret_w_out[j].astype(BF16), post_norm_g[i], x,
                          transposed_in=False)
        else:
            lambda_init = 0.8 - 0.6 * math.exp(-0.3 * i)
            qk, vaug, gt = _norm_matmul_diff(
                x, pre_norm_g[i], diff_w_in[j].astype(BF16), DIFF_HEADS)
            yt = _diff_attn(qk, vaug, gt, diff_lambda_q1[j], diff_lambda_k1[j],
                            diff_lambda_q2[j], diff_lambda_k2[j], diff_subln_g[j],
                            rel_bias, lambda_init)
            x = _out_proj(yt, diff_w_out[j].astype(BF16), post_norm_g[i], x,
                          transposed_in=True)
    return x
```

```python
import functools
import math

import jax
import jax.numpy as jnp
import numpy as np
from jax import lax
from jax.experimental import pallas as pl
from jax.experimental.pallas import tpu as pltpu

F32 = jnp.float32
BF16 = jnp.bfloat16

DEPTH = 2
N_MIXERS = 2
RET_HEADS = 4
DIFF_HEADS = 16
REL_BUCKETS = 32
REL_MAX_DIST = 128
ROPE_BASE = 10000.0
RMS_EPS = 1e-6
GN_EPS = 1e-5

V7X_VMEM_BYTES = 64 * 1024 * 1024
VMEM_LIMIT_BYTES = V7X_VMEM_BYTES * 15 // 16

PROJ_ROWS = 1024
OUT_ROWS = 1024
RET_CHUNK = 256
RET_CHUNKS_PER_STEP = 8
RET_NORM_ROWS = 32
ATT_TILE = 256
ATT_HEADS_PER_STEP = 4
ONES_ROWS = 16
LOG2E = math.log2(math.e)


def _compiler_params(semantics):
    return pltpu.CompilerParams(dimension_semantics=semantics,
                                vmem_limit_bytes=VMEM_LIMIT_BYTES)


def _store_normed(x_ref, g_ref, h_ref):
    x = x_ref[0]
    ms = jnp.mean(x * x, axis=-1, keepdims=True)
    h_ref[...] = (x * lax.rsqrt(ms + RMS_EPS) * g_ref[...]).astype(BF16)


def _silu(x):
    return x * jax.nn.sigmoid(x)


def _norm_matmul_ret_kernel(x_ref, g_ref, w_ref, cos_ref, sin_ref, o_ref, h_ref,
                            *, dk):
    j = pl.program_id(2)

    def project(cols=slice(None)):
        return jnp.dot(h_ref[...], w_ref[:, cols], preferred_element_type=F32)

    @pl.when(j == 0)
    def _():
        _store_normed(x_ref, g_ref, h_ref)
        half = dk // 2
        n_blocks = w_ref.shape[1] // dk
        c, s = cos_ref[...], sin_ref[...]
        for blk in range(n_blocks):
            o = project(slice(blk * dk, (blk + 1) * dk))
            scale = 1.0 if blk < n_blocks // 2 else dk ** -0.5
            x1, x2 = o[:, :half], o[:, half:]
            o_ref[0, :, blk * dk:blk * dk + half] = (
                (x1 * c - x2 * s) * scale).astype(BF16)
            o_ref[0, :, blk * dk + half:(blk + 1) * dk] = (
                (x1 * s + x2 * c) * scale).astype(BF16)

    @pl.when(j == 1)
    def _():
        o_ref[0] = project().astype(BF16)

    @pl.when(j == 2)
    def _():
        o_ref[0] = _silu(project()).astype(BF16)


def _norm_matmul_ret(x, g, w, d_inner, n_heads):
    B, S, D = x.shape
    N = w.shape[1]
    ts, tn = PROJ_ROWS, d_inner
    dk = (N - 2 * d_inner) // 2 // n_heads
    half = dk // 2
    assert N == 3 * tn, "q | k must fill exactly one column tile"

    pos = np.arange(S, dtype=np.float64)
    inv = ROPE_BASE ** (-np.arange(half, dtype=np.float64) / half)
    ang = pos[:, None] * inv[None, :]
    cos = jnp.asarray(np.cos(ang), dtype=F32)
    sin = jnp.asarray(np.sin(ang), dtype=F32)

    return pl.pallas_call(
        functools.partial(_norm_matmul_ret_kernel, dk=dk),
        grid=(B, S // ts, N // tn),
        in_specs=[pl.BlockSpec((1, ts, D), lambda b, i, j: (b, i, 0)),
                  pl.BlockSpec((1, D), lambda b, i, j: (0, 0)),
                  pl.BlockSpec((D, tn), lambda b, i, j: (0, j)),
                  pl.BlockSpec((ts, half), lambda b, i, j: (i, 0)),
                  pl.BlockSpec((ts, half), lambda b, i, j: (i, 0))],
        out_specs=pl.BlockSpec((1, ts, tn), lambda b, i, j: (b, i, j)),
        out_shape=jax.ShapeDtypeStruct((B, S, N), BF16),
        scratch_shapes=[pltpu.VMEM((ts, D), BF16)],
        compiler_params=_compiler_params(("parallel", "parallel", "arbitrary")),
        name="norm_matmul_ret",
    )(x, g.reshape(1, D), w, cos, sin)


def _norm_matmul_diff_kernel(x_ref, g_ref, w_ref, qk_ref, vaug_ref, gt_ref, h_ref):
    j = pl.program_id(2)

    def project():
        return jnp.dot(h_ref[...], w_ref[...], preferred_element_type=F32)

    @pl.when(j == 0)
    def _():
        _store_normed(x_ref, g_ref, h_ref)
        qk_ref[0] = project().astype(BF16)

    @pl.when(j == 1)
    def _():
        qk_ref[0] = project().astype(BF16)

    @pl.when(j == 2)
    def _():
        n_heads, rows_aug, ts = vaug_ref.shape[1:]
        dv = w_ref.shape[1] // n_heads
        for h in range(0, n_heads, 2):
            ot = jnp.dot(h_ref[...], w_ref[:, h * dv:(h + 2) * dv],
                         preferred_element_type=F32).T.astype(BF16)
            for e in range(2):
                vaug_ref[0, h + e, 0:dv, :] = ot[e * dv:(e + 1) * dv, :]
                vaug_ref[0, h + e, dv:rows_aug, :] = jnp.ones(
                    (rows_aug - dv, ts), BF16)

    @pl.when(j == 3)
    def _():
        gt_ref[0] = _silu(project()).T.astype(BF16)


def _norm_matmul_diff(x, g, w, n_heads):
    B, S, D = x.shape
    Di = w.shape[1] // 4
    dv = Di // n_heads
    ts = PROJ_ROWS
    return pl.pallas_call(
        _norm_matmul_diff_kernel,
        grid=(B, S // ts, 4),
        in_specs=[pl.BlockSpec((1, ts, D), lambda b, i, j: (b, i, 0)),
                  pl.BlockSpec((1, D), lambda b, i, j: (0, 0)),
                  pl.BlockSpec((D, Di), lambda b, i, j: (0, j))],
        out_specs=[
            pl.BlockSpec((1, ts, Di), lambda b, i, j: (b, i, jnp.minimum(j, 1))),
            pl.BlockSpec((1, n_heads, dv + ONES_ROWS, ts),
                         lambda b, i, j: (b, 0, 0, i)),
            pl.BlockSpec((1, Di, ts), lambda b, i, j: (b, 0, i)),
        ],
        out_shape=[jax.ShapeDtypeStruct((B, S, 2 * Di), BF16),
                   jax.ShapeDtypeStruct((B, n_heads, dv + ONES_ROWS, S), BF16),
                   jax.ShapeDtypeStruct((B, Di, S), BF16)],
        scratch_shapes=[pltpu.VMEM((ts, D), BF16)],
        compiler_params=_compiler_params(("parallel", "parallel", "arbitrary")),
        name="norm_matmul_diff",
    )(x, g.reshape(1, D), w)


def _out_proj_kernel(y_ref, w_ref, g_ref, x_ref, o_ref, *, transposed_in):
    y = y_ref[0]
    if transposed_in:
        o = lax.dot_general(y, w_ref[...], (((0,), (0,)), ((), ())),
                            preferred_element_type=F32)
    else:
        o = jnp.dot(y, w_ref[...], preferred_element_type=F32)
    ms = jnp.mean(o * o, axis=-1, keepdims=True)
    o_ref[0] = x_ref[0] + o * lax.rsqrt(ms + RMS_EPS) * g_ref[...]


def _out_proj(y, w, g, x, *, transposed_in):
    B, S, D = x.shape
    Di = w.shape[0]
    ts = OUT_ROWS
    if transposed_in:
        y_spec = pl.BlockSpec((1, Di, ts), lambda b, i: (b, 0, i))
    else:
        y_spec = pl.BlockSpec((1, ts, Di), lambda b, i: (b, i, 0))
    return pl.pallas_call(
        functools.partial(_out_proj_kernel, transposed_in=transposed_in),
        grid=(B, S // ts),
        in_specs=[y_spec,
                  pl.BlockSpec((Di, D), lambda b, i: (0, 0)),
                  pl.BlockSpec((1, D), lambda b, i: (0, 0)),
                  pl.BlockSpec((1, ts, D), lambda b, i: (b, i, 0))],
        out_specs=pl.BlockSpec((1, ts, D), lambda b, i: (b, i, 0)),
        out_shape=jax.ShapeDtypeStruct((B, S, D), F32),
        compiler_params=_compiler_params(("parallel", "parallel")),
        name="out_proj_t" if transposed_in else "out_proj",
    )(y, w, g.reshape(1, D), x)


def _retention_kernel(q_ref, k_ref, v_ref, g_ref, dmat_ref, xi_ref, zeta_ref,
                      cdec_ref, o_ref, state_ref, state_bf_ref):
    C = RET_CHUNK

    @pl.when(pl.program_id(2) == 0)
    def _():
        state_ref[...] = jnp.zeros_like(state_ref)
        state_bf_ref[...] = jnp.zeros_like(state_bf_ref)

    dmat = dmat_ref[0]
    xi = xi_ref[0]
    zeta = zeta_ref[0]
    cdec = cdec_ref[0]

    for ci in range(RET_CHUNKS_PER_STEP):
        rows = pl.ds(ci * C, C)
        q, k, v = q_ref[0, rows, :], k_ref[0, rows, :], v_ref[0, rows, :]
        scores = lax.dot_general(q, k, (((1,), (1,)), ((), ())),
                                 preferred_element_type=F32) * dmat
        cross = jnp.dot(q, state_bf_ref[...], preferred_element_type=F32) * xi
        kz = (k.astype(F32) * zeta).astype(BF16)
        state = state_ref[...] * cdec + lax.dot_general(
            kz, v, (((0,), (0,)), ((), ())), preferred_element_type=F32)
        state_ref[...] = state
        state_bf_ref[...] = state.astype(BF16)
        inner = jnp.dot(scores.astype(BF16), v, preferred_element_type=F32)
        for r0 in range(0, C, RET_NORM_ROWS):
            blk = slice(r0, r0 + RET_NORM_ROWS)
            ret = inner[blk] + cross[blk]
            mu = jnp.mean(ret, axis=-1, keepdims=True)
            cen = ret - mu
            var = jnp.mean(cen * cen, axis=-1, keepdims=True)
            y = cen * lax.rsqrt(var + GN_EPS)
            out_rows = pl.ds(ci * C + r0, RET_NORM_ROWS)
            o_ref[0, out_rows, :] = (
                g_ref[0, out_rows, :].astype(F32) * y).astype(o_ref.dtype)


def _retention(proj, d_model, d_inner):
    B, S, _ = proj.shape
    H, C = RET_HEADS, RET_CHUNK
    dk, dv = d_model // H, d_inner // H
    rows = C * RET_CHUNKS_PER_STEP
    k_off = d_model // dk
    v_off = 2 * d_model // dv
    g_off = (2 * d_model + d_inner) // dv

    log_gamma = jnp.log1p(-jnp.exp2(-5.0 - jnp.arange(H, dtype=F32)))
    idx = jnp.arange(C, dtype=F32)
    rel = idx[:, None] - idx[None, :]
    dmat = jnp.exp(log_gamma[:, None, None] * jnp.maximum(rel, 0.0)) * (rel >= 0)
    xi = jnp.exp(log_gamma[:, None] * (idx + 1.0))
    zeta = jnp.exp(log_gamma[:, None] * (C - 1.0 - idx))
    cdec = jnp.exp(log_gamma * C)
    xi = jnp.broadcast_to(xi[:, :, None], (H, C, dv))
    zeta = jnp.broadcast_to(zeta[:, :, None], (H, C, dk))
    cdec = jnp.broadcast_to(cdec[:, None, None], (H, 1, dv))

    return pl.pallas_call(
        _retention_kernel,
        grid=(B, H, S // rows),
        in_specs=[
            pl.BlockSpec((1, rows, dk), lambda b, h, n: (b, n, h)),
            pl.BlockSpec((1, rows, dk), lambda b, h, n: (b, n, k_off + h)),
            pl.BlockSpec((1, rows, dv), lambda b, h, n: (b, n, v_off + h)),
            pl.BlockSpec((1, rows, dv), lambda b, h, n: (b, n, g_off + h)),
            pl.BlockSpec((1, C, C), lambda b, h, n: (h, 0, 0)),
            pl.BlockSpec((1, C, dv), lambda b, h, n: (h, 0, 0)),
            pl.BlockSpec((1, C, dk), lambda b, h, n: (h, 0, 0)),
            pl.BlockSpec((1, 1, dv), lambda b, h, n: (h, 0, 0)),
        ],
        out_specs=pl.BlockSpec((1, rows, dv), lambda b, h, n: (b, n, h)),
        out_shape=jax.ShapeDtypeStruct((B, S, d_inner), BF16),
        scratch_shapes=[pltpu.VMEM((dk, dv), F32), pltpu.VMEM((dk, dv), BF16)],
        compiler_params=_compiler_params(("parallel", "parallel", "arbitrary")),
        name="retention",
    )(proj, proj, proj, proj, dmat, xi, zeta, cdec)


def _diff_attn_kernel(tab_ref, bfar_ref, q_ref, k_ref, vaug_ref, gt_ref, bkd_ref,
                      lq1_ref, lk1_ref, lq2_ref, lk2_ref, sg_ref, o_ref,
                      qbd_ref, bias_ref, m_ref, acc_ref, sa_ref, sb_ref,
                      *, d, lambda_init):
    T, G, H = ATT_TILE, ATT_HEADS_PER_STEP, DIFF_HEADS
    dv = 2 * d
    hp = pl.program_id(1)
    i = pl.program_id(2)

    @pl.when(i == 0)
    def _():
        rows = lax.broadcasted_iota(jnp.int32, (T, T), 0)
        cols = lax.broadcasted_iota(jnp.int32, (T, T), 1)
        bucket_of_distance = bkd_ref[...]
        for u in range(G):
            h = hp * G + u
            cfar = tab_ref[bfar_ref[0] * H + h]
            by_distance = jnp.full((1, 2 * T), tab_ref[h], F32)
            for bucket in range(1, REL_BUCKETS):
                by_distance = jnp.where(bucket_of_distance >= bucket,
                                        tab_ref[bucket * H + h], by_distance)
            by_distance = (by_distance - cfar) * LOG2E
            toeplitz = pltpu.roll(jnp.broadcast_to(by_distance, (T, 2 * T)),
                                  0, 1, stride=1, stride_axis=0)
            bias_ref[u, 0] = jnp.where(rows <= cols, toeplitz[:, 0:T], -jnp.inf)
            bias_ref[u, 1] = toeplitz[:, T:2 * T]

    def logits_into(dst_ref, w, j, bias_index):
        u = w % G
        start = pl.multiple_of(j * T, T)
        kt = k_ref[0, pl.ds(start, T), u * dv:(u + 1) * dv]
        s = jnp.dot(kt, qbd_ref[w], preferred_element_type=F32)
        if bias_index is None:
            dst_ref[w] = s
        else:
            bias = bias_ref[u, bias_index]
            dst_ref[w, :, 0:T] = s[:, 0:T] + bias
            dst_ref[w, :, T:2 * T] = s[:, T:2 * T] + bias

    def consume(src_ref, w, j):
        u = w % G
        start = pl.multiple_of(j * T, T)
        s = src_ref[w]
        m_old = m_ref[w]
        m_new = jnp.maximum(m_old, jnp.max(s, axis=0, keepdims=True))
        p = jnp.exp2(s - m_new).astype(BF16)
        alpha = jnp.exp2(m_old - m_new)
        pv = jnp.dot(vaug_ref[0, u, :, pl.ds(start, T)], p,
                     preferred_element_type=F32)
        acc_ref[w] = acc_ref[w] * alpha + pv
        m_ref[w] = m_new

    first, second, both = range(G), range(G, 2 * G), range(2 * G)

    def step(dst_ref, j_next, bias_of, src_ref, j_cur, cur_units=both):
        for w in both:
            logits_into(dst_ref, w, j_next, bias_of(w))
            if w in cur_units:
                consume(src_ref, w, j_cur)

    no_bias = lambda w: None

    def start():
        for w in both:
            sub, u = divmod(w, G)
            q = q_ref[0, sub * T:(sub + 1) * T, u * dv:(u + 1) * dv].astype(F32)
            qt = (q * (d ** -0.5 * LOG2E)).T
            row = lax.broadcasted_iota(jnp.int32, qt.shape, 0)
            qbd_ref[w, :, 0:T] = jnp.where(row < d, qt, 0.0).astype(BF16)
            qbd_ref[w, :, T:2 * T] = jnp.where(row >= d, qt, 0.0).astype(BF16)
        m_ref[...] = jnp.full_like(m_ref, -jnp.inf)
        acc_ref[...] = jnp.zeros_like(acc_ref)
        for w in second:
            logits_into(sa_ref, w, 2 * i + 1, 0)
        step(sb_ref, 2 * i, lambda w: 0 if w in first else 1,
             sa_ref, 2 * i + 1, cur_units=second)

    def finish(src_ref, j_cur):
        lam = (jnp.exp(jnp.sum(lq1_ref[...] * lk1_ref[...], keepdims=True))
               - jnp.exp(jnp.sum(lq2_ref[...] * lk2_ref[...], keepdims=True))
               + lambda_init)
        out_scale = sg_ref[...] * (1.0 - lambda_init)
        for w in both:
            sub, u = divmod(w, G)
            consume(src_ref, w, j_cur)
            acc = acc_ref[w]
            inv_l = 1.0 / acc[dv:dv + 1, :]
            o = (acc[:dv, :T] * inv_l[:, :T]
                 - acc[:dv, T:] * (lam * inv_l[:, T:]))
            ms = jnp.mean(o * o, axis=0, keepdims=True)
            on = (o * lax.rsqrt(ms + GN_EPS)) * out_scale
            head_rows = slice(u * dv, (u + 1) * dv)
            sub_cols = slice(sub * T, (sub + 1) * T)
            gate = gt_ref[0, head_rows, sub_cols].astype(F32)
            o_ref[0, head_rows, sub_cols] = (gate * on).astype(o_ref.dtype)

    @pl.when(i == 0)
    def _():
        start()
        finish(sb_ref, 2 * i)

    @pl.when(i >= 1)
    def _():
        def prologue():
            start()
            step(sa_ref, 2 * i - 1, lambda w: 1 if w in first else None,
                 sb_ref, 2 * i)
            step(sb_ref, 0, no_bias, sa_ref, 2 * i - 1)

        def pair(r):
            step(sa_ref, 2 * r - 1, no_bias, sb_ref, 2 * r - 2)
            step(sb_ref, 2 * r, no_bias, sa_ref, 2 * r - 1)

        n_pairs = i - 1
        n_trips = n_pairs // 4

        def trip(t):
            for e in range(1, 5):
                pair(4 * t + e)

        @pl.when(n_trips == 0)
        def _():
            prologue()

        @pl.when(n_trips >= 1)
        def _():
            prologue()
            trip(0)

        def trip_body(t, carry):
            trip(t)
            return carry

        lax.fori_loop(1, n_trips, trip_body, 0)

        @pl.when(n_pairs % 4 >= 2)
        def _():
            first_left = (n_pairs // 4) * 4 + 1
            pair(first_left)
            pair(first_left + 1)

        @pl.when((i - 1) % 2 == 1)
        def _():
            pair(i - 1)
            finish(sb_ref, 2 * i - 2)

        @pl.when((i - 1) % 2 == 0)
        def _():
            finish(sb_ref, 2 * i - 2)


def _t5_bucket(n):
    max_exact = REL_BUCKETS // 2
    nf = jnp.maximum(n, max_exact).astype(F32)
    large = max_exact + (jnp.log(nf / max_exact) / math.log(REL_MAX_DIST / max_exact)
                         * (REL_BUCKETS - max_exact)).astype(jnp.int32)
    large = jnp.minimum(large, REL_BUCKETS - 1)
    return jnp.where(n < max_exact, n, large)


def _diff_attn(qk, vaug, gt, lq1, lk1, lq2, lk2, subln_g, rel_bias, lambda_init):
    B, S, _ = qk.shape
    H, T, G = DIFF_HEADS, ATT_TILE, ATT_HEADS_PER_STEP
    d_inner = gt.shape[1]
    dv = d_inner // H
    d = dv // 2
    gw = G * dv
    n_groups = H // G

    bkd = _t5_bucket(jnp.arange(2 * T, dtype=jnp.int32)[None, :])
    bfar = _t5_bucket(jnp.full((1,), T + 1, jnp.int32))

    smem = pl.BlockSpec(memory_space=pltpu.SMEM)
    vec = pl.BlockSpec((1, d), lambda b, h, i: (0, 0))
    return pl.pallas_call(
        functools.partial(_diff_attn_kernel, d=d, lambda_init=lambda_init),
        grid=(B, n_groups, S // (2 * T)),
        in_specs=[
            smem, smem,
            pl.BlockSpec((1, 2 * T, gw), lambda b, h, i: (b, i, h)),
            pl.BlockSpec((1, S, gw), lambda b, h, i: (b, 0, n_groups + h)),
            pl.BlockSpec((1, G, dv + ONES_ROWS, S), lambda b, h, i: (b, h, 0, 0)),
            pl.BlockSpec((1, gw, 2 * T), lambda b, h, i: (b, h, i)),
            pl.BlockSpec((1, 2 * T), lambda b, h, i: (0, 0)),
            vec, vec, vec, vec,
            pl.BlockSpec((dv, 1), lambda b, h, i: (0, 0)),
        ],
        out_specs=pl.BlockSpec((1, gw, 2 * T), lambda b, h, i: (b, h, i)),
        out_shape=jax.ShapeDtypeStruct((B, d_inner, S), BF16),
        scratch_shapes=[pltpu.VMEM((2 * G, dv, 2 * T), BF16),
                        pltpu.VMEM((G, 2, T, T), F32),
                        pltpu.VMEM((2 * G, 1, 2 * T), F32),
                        pltpu.VMEM((2 * G, dv + ONES_ROWS, 2 * T), F32),
                        pltpu.VMEM((2 * G, T, 2 * T), F32),
                        pltpu.VMEM((2 * G, T, 2 * T), F32)],
        compiler_params=_compiler_params(("parallel", "arbitrary", "arbitrary")),
        name="diff_attn",
    )(rel_bias.astype(F32).reshape(-1), bfar, qk, qk, vaug, gt, bkd,
      lq1.reshape(1, d), lk1.reshape(1, d), lq2.reshape(1, d), lk2.reshape(1, d),
      subln_g.reshape(dv, 1))


def kernel(x, pre_norm_g, post_norm_g, ret_w_in, ret_w_out, diff_w_in, diff_w_out,
           diff_lambda_q1, diff_lambda_k1, diff_lambda_q2, diff_lambda_k2,
           diff_subln_g, rel_bias):
    d_model = x.shape[-1]
    d_inner = ret_w_out.shape[1]
    for i in range(DEPTH):
        j = i // N_MIXERS
        if i % N_MIXERS == 0:
            proj = _norm_matmul_ret(x, pre_norm_g[i], ret_w_in[j].astype(BF16),
                                    d_inner, RET_HEADS)
            y = _retention(proj, d_model, d_inner)
            x = _out_proj(y, ret_w_out[j].astype(BF16), post_norm_g[i], x,
                          transposed_in=False)
        else:
            lambda_init = 0.8 - 0.6 * math.exp(-0.3 * i)
            qk, vaug, gt = _norm_matmul_diff(
                x, pre_norm_g[i], diff_w_in[j].astype(BF16), DIFF_HEADS)
            yt = _diff_attn(qk, vaug, gt, diff_lambda_q1[j], diff_lambda_k1[j],
                            diff_lambda_q2[j], diff_lambda_k2[j], diff_subln_g[j],
                            rel_bias, lambda_init)
            x = _out_proj(yt, diff_w_out[j].astype(BF16), post_norm_g[i], x,
                          transposed_in=True)
    return x
```

```python
import functools
import math

import jax
import jax.numpy as jnp
import numpy as np
from jax import lax
from jax.experimental import pallas as pl
from jax.experimental.pallas import tpu as pltpu

F32 = jnp.float32
BF16 = jnp.bfloat16

DEPTH = 2
N_MIXERS = 2
RET_HEADS = 4
DIFF_HEADS = 16
REL_BUCKETS = 32
REL_MAX_DIST = 128
ROPE_BASE = 10000.0
RMS_EPS = 1e-6
GN_EPS = 1e-5

V7X_VMEM_BYTES = 64 * 1024 * 1024
VMEM_LIMIT_BYTES = V7X_VMEM_BYTES * 15 // 16

PROJ_ROWS = 1024
OUT_ROWS = 1024
RET_CHUNK = 256
RET_CHUNKS_PER_STEP = 8
RET_NORM_ROWS = 32
ATT_TILE = 256
ATT_HEADS_PER_STEP = 4
ONES_ROWS = 16
LOG2E = math.log2(math.e)


def _compiler_params(semantics):
    return pltpu.CompilerParams(dimension_semantics=semantics,
                                vmem_limit_bytes=VMEM_LIMIT_BYTES)


def _store_normed(x_ref, g_ref, h_ref):
    x = x_ref[0]
    ms = jnp.mean(x * x, axis=-1, keepdims=True)
    h_ref[...] = (x * lax.rsqrt(ms + RMS_EPS) * g_ref[...]).astype(BF16)


def _silu(x):
    return x * jax.nn.sigmoid(x)


def _norm_matmul_ret_kernel(x_ref, g_ref, w_ref, cos_ref, sin_ref, o_ref, h_ref,
                            *, dk):
    j = pl.program_id(2)

    def project(cols=slice(None)):
        return jnp.dot(h_ref[...], w_ref[:, cols], preferred_element_type=F32)

    @pl.when(j == 0)
    def _():
        _store_normed(x_ref, g_ref, h_ref)
        half = dk // 2
        n_blocks = w_ref.shape[1] // dk
        c, s = cos_ref[...], sin_ref[...]
        for blk in range(n_blocks):
            o = project(slice(blk * dk, (blk + 1) * dk))
            scale = 1.0 if blk < n_blocks // 2 else dk ** -0.5
            x1, x2 = o[:, :half], o[:, half:]
            o_ref[0, :, blk * dk:blk * dk + half] = (
                (x1 * c - x2 * s) * scale).astype(BF16)
            o_ref[0, :, blk * dk + half:(blk + 1) * dk] = (
                (x1 * s + x2 * c) * scale).astype(BF16)

    @pl.when(j == 1)
    def _():
        o_ref[0] = project().astype(BF16)

    @pl.when(j == 2)
    def _():
        o_ref[0] = _silu(project()).astype(BF16)


def _norm_matmul_ret(x, g, w, d_inner, n_heads):
    B, S, D = x.shape
    N = w.shape[1]
    ts, tn = PROJ_ROWS, d_inner
    dk = (N - 2 * d_inner) // 2 // n_heads
    half = dk // 2
    assert N == 3 * tn, "q | k must fill exactly one column tile"

    pos = np.arange(S, dtype=np.float64)
    inv = ROPE_BASE ** (-np.arange(half, dtype=np.float64) / half)
    ang = pos[:, None] * inv[None, :]
    cos = jnp.asarray(np.cos(ang), dtype=F32)
    sin = jnp.asarray(np.sin(ang), dtype=F32)

    return pl.pallas_call(
        functools.partial(_norm_matmul_ret_kernel, dk=dk),
        grid=(B, S // ts, N // tn),
        in_specs=[pl.BlockSpec((1, ts, D), lambda b, i, j: (b, i, 0)),
                  pl.BlockSpec((1, D), lambda b, i, j: (0, 0)),
                  pl.BlockSpec((D, tn), lambda b, i, j: (0, j)),
                  pl.BlockSpec((ts, half), lambda b, i, j: (i, 0)),
                  pl.BlockSpec((ts, half), lambda b, i, j: (i, 0))],
        out_specs=pl.BlockSpec((1, ts, tn), lambda b, i, j: (b, i, j)),
        out_shape=jax.ShapeDtypeStruct((B, S, N), BF16),
        scratch_shapes=[pltpu.VMEM((ts, D), BF16)],
        compiler_params=_compiler_params(("parallel", "parallel", "arbitrary")),
        name="norm_matmul_ret",
    )(x, g.reshape(1, D), w, cos, sin)


def _norm_matmul_diff_kernel(x_ref, g_ref, w_ref, qk_ref, vaug_ref, gt_ref, h_ref):
    j = pl.program_id(2)

    @pl.when(j == 0)
    def _():
        _store_normed(x_ref, g_ref, h_ref)

    def project():
        return jnp.dot(h_ref[...], w_ref[...], preferred_element_type=F32)

    @pl.when(j < 2)
    def _():
        qk_ref[0] = project().astype(BF16)

    @pl.when(j == 2)
    def _():
        n_heads, rows_aug, ts = vaug_ref.shape[1:]
        dv = w_ref.shape[1] // n_heads
        for h in range(0, n_heads, 2):
            ot = jnp.dot(h_ref[...], w_ref[:, h * dv:(h + 2) * dv],
                         preferred_element_type=F32).T.astype(BF16)
            for e in range(2):
                vaug_ref[0, h + e, 0:dv, :] = ot[e * dv:(e + 1) * dv, :]
                vaug_ref[0, h + e, dv:rows_aug, :] = jnp.ones(
                    (rows_aug - dv, ts), BF16)

    @pl.when(j == 3)
    def _():
        gt_ref[0] = _silu(project()).T.astype(BF16)


def _norm_matmul_diff(x, g, w, n_heads):
    B, S, D = x.shape
    Di = w.shape[1] // 4
    dv = Di // n_heads
    ts = PROJ_ROWS
    return pl.pallas_call(
        _norm_matmul_diff_kernel,
        grid=(B, S // ts, 4),
        in_specs=[pl.BlockSpec((1, ts, D), lambda b, i, j: (b, i, 0)),
                  pl.BlockSpec((1, D), lambda b, i, j: (0, 0)),
                  pl.BlockSpec((D, Di), lambda b, i, j: (0, j))],
        out_specs=[
            pl.BlockSpec((1, ts, Di), lambda b, i, j: (b, i, jnp.minimum(j, 1))),
            pl.BlockSpec((1, n_heads, dv + ONES_ROWS, ts),
                         lambda b, i, j: (b, 0, 0, i)),
            pl.BlockSpec((1, Di, ts), lambda b, i, j: (b, 0, i)),
        ],
        out_shape=[jax.ShapeDtypeStruct((B, S, 2 * Di), BF16),
                   jax.ShapeDtypeStruct((B, n_heads, dv + ONES_ROWS, S), BF16),
                   jax.ShapeDtypeStruct((B, Di, S), BF16)],
        scratch_shapes=[pltpu.VMEM((ts, D), BF16)],
        compiler_params=_compiler_params(("parallel", "parallel", "arbitrary")),
        name="norm_matmul_diff",
    )(x, g.reshape(1, D), w)


def _out_proj_kernel(y_ref, w_ref, g_ref, x_ref, o_ref, *, transposed_in):
    y = y_ref[0]
    if transposed_in:
        o = lax.dot_general(y, w_ref[...], (((0,), (0,)), ((), ())),
                            preferred_element_type=F32)
    else:
        o = jnp.dot(y, w_ref[...], preferred_element_type=F32)
    ms = jnp.mean(o * o, axis=-1, keepdims=True)
    o_ref[0] = x_ref[0] + o * lax.rsqrt(ms + RMS_EPS) * g_ref[...]


def _out_proj(y, w, g, x, *, transposed_in):
    B, S, D = x.shape
    Di = w.shape[0]
    ts = OUT_ROWS
    if transposed_in:
        y_spec = pl.BlockSpec((1, Di, ts), lambda b, i: (b, 0, i))
    else:
        y_spec = pl.BlockSpec((1, ts, Di), lambda b, i: (b, i, 0))
    return pl.pallas_call(
        functools.partial(_out_proj_kernel, transposed_in=transposed_in),
        grid=(B, S // ts),
        in_specs=[y_spec,
                  pl.BlockSpec((Di, D), lambda b, i: (0, 0)),
                  pl.BlockSpec((1, D), lambda b, i: (0, 0)),
                  pl.BlockSpec((1, ts, D), lambda b, i: (b, i, 0))],
        out_specs=pl.BlockSpec((1, ts, D), lambda b, i: (b, i, 0)),
        out_shape=jax.ShapeDtypeStruct((B, S, D), F32),
        compiler_params=_compiler_params(("parallel", "parallel")),
        name="out_proj_t" if transposed_in else "out_proj",
    )(y, w, g.reshape(1, D), x)


def _retention_kernel(q_ref, k_ref, v_ref, g_ref, dmat_ref, xi_ref, zeta_ref,
                      cdec_ref, o_ref, state_ref, state_bf_ref):
    C = RET_CHUNK

    @pl.when(pl.program_id(2) == 0)
    def _():
        state_ref[...] = jnp.zeros_like(state_ref)
        state_bf_ref[...] = jnp.zeros_like(state_bf_ref)

    dmat = dmat_ref[0]
    xi = xi_ref[0]
    zeta = zeta_ref[0]
    cdec = cdec_ref[0]

    for ci in range(RET_CHUNKS_PER_STEP):
        rows = pl.ds(ci * C, C)
        q, k, v = q_ref[0, rows, :], k_ref[0, rows, :], v_ref[0, rows, :]
        scores = lax.dot_general(q, k, (((1,), (1,)), ((), ())),
                                 preferred_element_type=F32) * dmat
        cross = jnp.dot(q, state_bf_ref[...], preferred_element_type=F32) * xi
        kz = (k.astype(F32) * zeta).astype(BF16)
        state = state_ref[...] * cdec + lax.dot_general(
            kz, v, (((0,), (0,)), ((), ())), preferred_element_type=F32)
        state_ref[...] = state
        state_bf_ref[...] = state.astype(BF16)
        inner = jnp.dot(scores.astype(BF16), v, preferred_element_type=F32)
        for r0 in range(0, C, RET_NORM_ROWS):
            blk = slice(r0, r0 + RET_NORM_ROWS)
            ret = inner[blk] + cross[blk]
            mu = jnp.mean(ret, axis=-1, keepdims=True)
            cen = ret - mu
            var = jnp.mean(cen * cen, axis=-1, keepdims=True)
            y = cen * lax.rsqrt(var + GN_EPS)
            out_rows = pl.ds(ci * C + r0, RET_NORM_ROWS)
            o_ref[0, out_rows, :] = (
                g_ref[0, out_rows, :].astype(F32) * y).astype(o_ref.dtype)


def _retention(proj, d_model, d_inner):
    B, S, _ = proj.shape
    H, C = RET_HEADS, RET_CHUNK
    dk, dv = d_model // H, d_inner // H
    rows = C * RET_CHUNKS_PER_STEP
    k_off = d_model // dk
    v_off = 2 * d_model // dv
    g_off = (2 * d_model + d_inner) // dv

    log_gamma = jnp.log1p(-jnp.exp2(-5.0 - jnp.arange(H, dtype=F32)))
    idx = jnp.arange(C, dtype=F32)
    rel = idx[:, None] - idx[None, :]
    dmat = jnp.exp(log_gamma[:, None, None] * jnp.maximum(rel, 0.0)) * (rel >= 0)
    xi = jnp.exp(log_gamma[:, None] * (idx + 1.0))
    zeta = jnp.exp(log_gamma[:, None] * (C - 1.0 - idx))
    cdec = jnp.exp(log_gamma * C)
    xi = jnp.broadcast_to(xi[:, :, None], (H, C, dv))
    zeta = jnp.broadcast_to(zeta[:, :, None], (H, C, dk))
    cdec = jnp.broadcast_to(cdec[:, None, None], (H, 1, dv))

    return pl.pallas_call(
        _retention_kernel,
        grid=(B, H, S // rows),
        in_specs=[
            pl.BlockSpec((1, rows, dk), lambda b, h, n: (b, n, h)),
            pl.BlockSpec((1, rows, dk), lambda b, h, n: (b, n, k_off + h)),
            pl.BlockSpec((1, rows, dv), lambda b, h, n: (b, n, v_off + h)),
            pl.BlockSpec((1, rows, dv), lambda b, h, n: (b, n, g_off + h)),
            pl.BlockSpec((1, C, C), lambda b, h, n: (h, 0, 0)),
            pl.BlockSpec((1, C, dv), lambda b, h, n: (h, 0, 0)),
            pl.BlockSpec((1, C, dk), lambda b, h, n: (h, 0, 0)),
            pl.BlockSpec((1, 1, dv), lambda b, h, n: (h, 0, 0)),
        ],
        out_specs=pl.BlockSpec((1, rows, dv), lambda b, h, n: (b, n, h)),
        out_shape=jax.ShapeDtypeStruct((B, S, d_inner), BF16),
        scratch_shapes=[pltpu.VMEM((dk, dv), F32), pltpu.VMEM((dk, dv), BF16)],
        compiler_params=_compiler_params(("parallel", "parallel", "arbitrary")),
        name="retention",
    )(proj, proj, proj, proj, dmat, xi, zeta, cdec)


def _diff_attn_kernel(tab_ref, bfar_ref, q_ref, k_ref, vaug_ref, gt_ref, bkd_ref,
                      lq1_ref, lk1_ref, lq2_ref, lk2_ref, sg_ref, o_ref,
                      qbd_ref, bias_ref, m_ref, acc_ref, sa_ref, sb_ref,
                      *, d, lambda_init):
    T, G, H = ATT_TILE, ATT_HEADS_PER_STEP, DIFF_HEADS
    dv = 2 * d
    hp = pl.program_id(1)
    i = pl.program_id(2)

    @pl.when(i == 0)
    def _():
        rows = lax.broadcasted_iota(jnp.int32, (T, T), 0)
        cols = lax.broadcasted_iota(jnp.int32, (T, T), 1)
        bucket_of_distance = bkd_ref[...]
        for u in range(G):
            h = hp * G + u
            cfar = tab_ref[bfar_ref[0] * H + h]
            by_distance = jnp.full((1, 2 * T), tab_ref[h], F32)
            for bucket in range(1, REL_BUCKETS):
                by_distance = jnp.where(bucket_of_distance >= bucket,
                                        tab_ref[bucket * H + h], by_distance)
            by_distance = (by_distance - cfar) * LOG2E
            toeplitz = pltpu.roll(jnp.broadcast_to(by_distance, (T, 2 * T)),
                                  0, 1, stride=1, stride_axis=0)
            bias_ref[u, 0] = jnp.where(rows <= cols, toeplitz[:, 0:T], -jnp.inf)
            bias_ref[u, 1] = toeplitz[:, T:2 * T]

    def logits_into(dst_ref, w, j, bias_index):
        u = w % G
        start = pl.multiple_of(j * T, T)
        kt = k_ref[0, pl.ds(start, T), u * dv:(u + 1) * dv]
        s = jnp.dot(kt, qbd_ref[w], preferred_element_type=F32)
        if bias_index is None:
            dst_ref[w] = s
        else:
            bias = bias_ref[u, bias_index]
            dst_ref[w, :, 0:T] = s[:, 0:T] + bias
            dst_ref[w, :, T:2 * T] = s[:, T:2 * T] + bias

    def consume(src_ref, w, j):
        u = w % G
        start = pl.multiple_of(j * T, T)
        values = vaug_ref[0, u, :, pl.ds(start, T)]
        for cols in (slice(0, T), slice(T, 2 * T)):
            s = src_ref[w, :, cols]
            m_old = m_ref[w, :, cols]
            m_new = jnp.maximum(m_old, jnp.max(s, axis=0, keepdims=True))
            p = jnp.exp2(s - m_new).astype(BF16)
            alpha = jnp.exp2(m_old - m_new)
            pv = jnp.dot(values, p, preferred_element_type=F32)
            acc_ref[w, :, cols] = acc_ref[w, :, cols] * alpha + pv
            m_ref[w, :, cols] = m_new

    first, second, both = range(G), range(G, 2 * G), range(2 * G)

    def step(dst_ref, j_next, bias_of, src_ref, j_cur, cur_units=both):
        for w in both:
            logits_into(dst_ref, w, j_next, bias_of(w))
            if w in cur_units:
                consume(src_ref, w, j_cur)

    no_bias = lambda w: None

    def start():
        for w in both:
            sub, u = divmod(w, G)
            q = q_ref[0, sub * T:(sub + 1) * T, u * dv:(u + 1) * dv].astype(F32)
            qt = (q * (d ** -0.5 * LOG2E)).T
            row = lax.broadcasted_iota(jnp.int32, qt.shape, 0)
            qbd_ref[w, :, 0:T] = jnp.where(row < d, qt, 0.0).astype(BF16)
            qbd_ref[w, :, T:2 * T] = jnp.where(row >= d, qt, 0.0).astype(BF16)
        m_ref[...] = jnp.full_like(m_ref, -jnp.inf)
        acc_ref[...] = jnp.zeros_like(acc_ref)
        for w in second:
            logits_into(sa_ref, w, 2 * i + 1, 0)
        step(sb_ref, 2 * i, lambda w: 0 if w in first else 1,
             sa_ref, 2 * i + 1, cur_units=second)

    def finish(src_ref, j_cur):
        lam = (jnp.exp(jnp.sum(lq1_ref[...] * lk1_ref[...], keepdims=True))
               - jnp.exp(jnp.sum(lq2_ref[...] * lk2_ref[...], keepdims=True))
               + lambda_init)
        out_scale = sg_ref[...] * (1.0 - lambda_init)
        for w in both:
            sub, u = divmod(w, G)
            consume(src_ref, w, j_cur)
            acc = acc_ref[w]
            inv_l = 1.0 / acc[dv:dv + 1, :]
            o = (acc[:dv, :T] * inv_l[:, :T]
                 - acc[:dv, T:] * (lam * inv_l[:, T:]))
            ms = jnp.mean(o * o, axis=0, keepdims=True)
            on = (o * lax.rsqrt(ms + GN_EPS)) * out_scale
            head_rows = slice(u * dv, (u + 1) * dv)
            sub_cols = slice(sub * T, (sub + 1) * T)
            gate = gt_ref[0, head_rows, sub_cols].astype(F32)
            o_ref[0, head_rows, sub_cols] = (gate * on).astype(o_ref.dtype)

    @pl.when(i == 0)
    def _():
        start()
        finish(sb_ref, 2 * i)

    @pl.when(i >= 1)
    def _():
        def prologue():
            start()
            step(sa_ref, 2 * i - 1, lambda w: 1 if w in first else None,
                 sb_ref, 2 * i)
            step(sb_ref, 0, no_bias, sa_ref, 2 * i - 1)

        def pair(r):
            step(sa_ref, 2 * r - 1, no_bias, sb_ref, 2 * r - 2)
            step(sb_ref, 2 * r, no_bias, sa_ref, 2 * r - 1)

        n_pairs = i - 1
        n_trips = n_pairs // 4

        def trip(t):
            for e in range(1, 5):
                pair(4 * t + e)

        @pl.when(n_trips == 0)
        def _():
            prologue()

        @pl.when(n_trips >= 1)
        def _():
            prologue()
            trip(0)

        def trip_body(t, carry):
            trip(t)
            return carry

        lax.fori_loop(1, n_trips, trip_body, 0)

        @pl.when(n_pairs % 4 >= 2)
        def _():
            first_left = (n_pairs // 4) * 4 + 1
            pair(first_left)
            pair(first_left + 1)

        @pl.when((i - 1) % 2 == 1)
        def _():
            pair(i - 1)
            finish(sb_ref, 2 * i - 2)

        @pl.when((i - 1) % 2 == 0)
        def _():
            finish(sb_ref, 2 * i - 2)


def _t5_bucket(n):
    max_exact = REL_BUCKETS // 2
    nf = jnp.maximum(n, max_exact).astype(F32)
    large = max_exact + (jnp.log(nf / max_exact) / math.log(REL_MAX_DIST / max_exact)
                         * (REL_BUCKETS - max_exact)).astype(jnp.int32)
    large = jnp.minimum(large, REL_BUCKETS - 1)
    return jnp.where(n < max_exact, n, large)


def _diff_attn(qk, vaug, gt, lq1, lk1, lq2, lk2, subln_g, rel_bias, lambda_init):
    B, S, _ = qk.shape
    H, T, G = DIFF_HEADS, ATT_TILE, ATT_HEADS_PER_STEP
    d_inner = gt.shape[1]
    dv = d_inner // H
    d = dv // 2
    gw = G * dv
    n_groups = H // G

    bkd = _t5_bucket(jnp.arange(2 * T, dtype=jnp.int32)[None, :])
    bfar = _t5_bucket(jnp.full((1,), T + 1, jnp.int32))

    smem = pl.BlockSpec(memory_space=pltpu.SMEM)
    vec = pl.BlockSpec((1, d), lambda b, h, i: (0, 0))
    return pl.pallas_call(
        functools.partial(_diff_attn_kernel, d=d, lambda_init=lambda_init),
        grid=(B, n_groups, S // (2 * T)),
        in_specs=[
            smem, smem,
            pl.BlockSpec((1, 2 * T, gw), lambda b, h, i: (b, i, h)),
            pl.BlockSpec((1, S, gw), lambda b, h, i: (b, 0, n_groups + h)),
            pl.BlockSpec((1, G, dv + ONES_ROWS, S), lambda b, h, i: (b, h, 0, 0)),
            pl.BlockSpec((1, gw, 2 * T), lambda b, h, i: (b, h, i)),
            pl.BlockSpec((1, 2 * T), lambda b, h, i: (0, 0)),
            vec, vec, vec, vec,
            pl.BlockSpec((dv, 1), lambda b, h, i: (0, 0)),
        ],
        out_specs=pl.BlockSpec((1, gw, 2 * T), lambda b, h, i: (b, h, i)),
        out_shape=jax.ShapeDtypeStruct((B, d_inner, S), BF16),
        scratch_shapes=[pltpu.VMEM((2 * G, dv, 2 * T), BF16),
                        pltpu.VMEM((G, 2, T, T), F32),
                        pltpu.VMEM((2 * G, 1, 2 * T), F32),
                        pltpu.VMEM((2 * G, dv + ONES_ROWS, 2 * T), F32),
                        pltpu.VMEM((2 * G, T, 2 * T), F32),
                        pltpu.VMEM((2 * G, T, 2 * T), F32)],
        compiler_params=_compiler_params(("parallel", "arbitrary", "arbitrary")),
        name="diff_attn",
    )(rel_bias.astype(F32).reshape(-1), bfar, qk, qk, vaug, gt, bkd,
      lq1.reshape(1, d), lk1.reshape(1, d), lq2.reshape(1, d), lk2.reshape(1, d),
      subln_g.reshape(dv, 1))


def kernel(x, pre_norm_g, post_norm_g, ret_w_in, ret_w_out, diff_w_in, diff_w_out,
           diff_lambda_q1, diff_lambda_k1, diff_lambda_q2, diff_lambda_k2,
           diff_subln_g, rel_bias):
    d_model = x.shape[-1]
    d_inner = ret_w_out.shape[1]
    for i in range(DEPTH):
        j = i // N_MIXERS
        if i % N_MIXERS == 0:
            proj = _norm_matmul_ret(x, pre_norm_g[i], ret_w_in[j].astype(BF16),
                                    d_inner, RET_HEADS)
            y = _retention(proj, d_model, d_inner)
            x = _out_proj(y, ret_w_out[j].astype(BF16), post_norm_g[i], x,
                          transposed_in=False)
        else:
            lambda_init = 0.8 - 0.6 * math.exp(-0.3 * i)
            qk, vaug, gt = _norm_matmul_diff(
                x, pre_norm_g[i], diff_w_in[j].astype(BF16), DIFF_HEADS)
            yt = _diff_attn(qk, vaug, gt, diff_lambda_q1[j], diff_lambda_k1[j],
                            diff_lambda_q2[j], diff_lambda_k2[j], diff_subln_g[j],
                            rel_bias, lambda_init)
            x = _out_proj(yt, diff_w_out[j].astype(BF16), post_norm_g[i], x,
                          transposed_in=True)
    return x
```

```python
import functools
import math

import jax
import jax.numpy as jnp
import numpy as np
from jax import lax
from jax.experimental import pallas as pl
from jax.experimental.pallas import tpu as pltpu

F32 = jnp.float32
BF16 = jnp.bfloat16

DEPTH = 2
N_MIXERS = 2
RET_HEADS = 4
DIFF_HEADS = 16
REL_BUCKETS = 32
REL_MAX_DIST = 128
ROPE_BASE = 10000.0
RMS_EPS = 1e-6
GN_EPS = 1e-5

V7X_VMEM_BYTES = 64 * 1024 * 1024
VMEM_LIMIT_BYTES = V7X_VMEM_BYTES * 15 // 16

PROJ_ROWS = 1024
OUT_ROWS = 1024
RET_CHUNK = 256
RET_CHUNKS_PER_STEP = 8
RET_NORM_ROWS = 32
ATT_TILE = 256
ATT_HEADS_PER_STEP = 4
ONES_ROWS = 16
LOG2E = math.log2(math.e)


def _compiler_params(semantics):
    return pltpu.CompilerParams(dimension_semantics=semantics,
                                vmem_limit_bytes=VMEM_LIMIT_BYTES)


def _store_normed(x_ref, g_ref, h_ref):
    x = x_ref[0]
    ms = jnp.mean(x * x, axis=-1, keepdims=True)
    h_ref[...] = (x * lax.rsqrt(ms + RMS_EPS) * g_ref[...]).astype(BF16)


def _silu(x):
    return x * jax.nn.sigmoid(x)


def _norm_matmul_ret_kernel(x_ref, g_ref, w_ref, cos_ref, sin_ref, o_ref, h_ref,
                            *, dk):
    j = pl.program_id(2)

    def project(cols=slice(None)):
        return jnp.dot(h_ref[...], w_ref[:, cols], preferred_element_type=F32)

    @pl.when(j == 0)
    def _():
        _store_normed(x_ref, g_ref, h_ref)
        half = dk // 2
        n_blocks = w_ref.shape[1] // dk
        c, s = cos_ref[...], sin_ref[...]
        for blk in range(n_blocks):
            o = project(slice(blk * dk, (blk + 1) * dk))
            scale = 1.0 if blk < n_blocks // 2 else dk ** -0.5
            x1, x2 = o[:, :half], o[:, half:]
            o_ref[0, :, blk * dk:blk * dk + half] = (
                (x1 * c - x2 * s) * scale).astype(BF16)
            o_ref[0, :, blk * dk + half:(blk + 1) * dk] = (
                (x1 * s + x2 * c) * scale).astype(BF16)

    @pl.when(j == 1)
    def _():
        o_ref[0] = project().astype(BF16)

    @pl.when(j == 2)
    def _():
        o_ref[0] = _silu(project()).astype(BF16)


def _norm_matmul_ret(x, g, w, d_inner, n_heads):
    B, S, D = x.shape
    N = w.shape[1]
    ts, tn = PROJ_ROWS, d_inner
    dk = (N - 2 * d_inner) // 2 // n_heads
    half = dk // 2
    assert N == 3 * tn, "q | k must fill exactly one column tile"

    pos = np.arange(S, dtype=np.float64)
    inv = ROPE_BASE ** (-np.arange(half, dtype=np.float64) / half)
    ang = pos[:, None] * inv[None, :]
    cos = jnp.asarray(np.cos(ang), dtype=F32)
    sin = jnp.asarray(np.sin(ang), dtype=F32)

    return pl.pallas_call(
        functools.partial(_norm_matmul_ret_kernel, dk=dk),
        grid=(B, S // ts, N // tn),
        in_specs=[pl.BlockSpec((1, ts, D), lambda b, i, j: (b, i, 0)),
                  pl.BlockSpec((1, D), lambda b, i, j: (0, 0)),
                  pl.BlockSpec((D, tn), lambda b, i, j: (0, j)),
                  pl.BlockSpec((ts, half), lambda b, i, j: (i, 0)),
                  pl.BlockSpec((ts, half), lambda b, i, j: (i, 0))],
        out_specs=pl.BlockSpec((1, ts, tn), lambda b, i, j: (b, i, j)),
        out_shape=jax.ShapeDtypeStruct((B, S, N), BF16),
        scratch_shapes=[pltpu.VMEM((ts, D), BF16)],
        compiler_params=_compiler_params(("parallel", "parallel", "arbitrary")),
        name="norm_matmul_ret",
    )(x, g.reshape(1, D), w, cos, sin)


def _norm_matmul_diff_kernel(x_ref, g_ref, w_ref, qk_ref, vaug_ref, gt_ref, h_ref):
    j = pl.program_id(2)

    @pl.when(j == 0)
    def _():
        _store_normed(x_ref, g_ref, h_ref)

    def project():
        return jnp.dot(h_ref[...], w_ref[...], preferred_element_type=F32)

    @pl.when(j < 2)
    def _():
        qk_ref[0] = project().astype(BF16)

    @pl.when(j == 2)
    def _():
        n_heads, rows_aug, ts = vaug_ref.shape[1:]
        dv = w_ref.shape[1] // n_heads
        for h in range(0, n_heads, 2):
            ot = jnp.dot(h_ref[...], w_ref[:, h * dv:(h + 2) * dv],
                         preferred_element_type=F32).T.astype(BF16)
            for e in range(2):
                vaug_ref[0, h + e, 0:dv, :] = ot[e * dv:(e + 1) * dv, :]
                vaug_ref[0, h + e, dv:rows_aug, :] = jnp.ones(
                    (rows_aug - dv, ts), BF16)

    @pl.when(j == 3)
    def _():
        gt_ref[0] = _silu(project()).T.astype(BF16)


def _norm_matmul_diff(x, g, w, n_heads):
    B, S, D = x.shape
    Di = w.shape[1] // 4
    dv = Di // n_heads
    ts = PROJ_ROWS
    return pl.pallas_call(
        _norm_matmul_diff_kernel,
        grid=(B, S // ts, 4),
        in_specs=[pl.BlockSpec((1, ts, D), lambda b, i, j: (b, i, 0)),
                  pl.BlockSpec((1, D), lambda b, i, j: (0, 0)),
                  pl.BlockSpec((D, Di), lambda b, i, j: (0, j))],
        out_specs=[
            pl.BlockSpec((1, ts, Di), lambda b, i, j: (b, i, jnp.minimum(j, 1))),
            pl.BlockSpec((1, n_heads, dv + ONES_ROWS, ts),
                         lambda b, i, j: (b, 0, 0, i)),
            pl.BlockSpec((1, Di, ts), lambda b, i, j: (b, 0, i)),
        ],
        out_shape=[jax.ShapeDtypeStruct((B, S, 2 * Di), BF16),
                   jax.ShapeDtypeStruct((B, n_heads, dv + ONES_ROWS, S), BF16),
                   jax.ShapeDtypeStruct((B, Di, S), BF16)],
        scratch_shapes=[pltpu.VMEM((ts, D), BF16)],
        compiler_params=_compiler_params(("parallel", "parallel", "arbitrary")),
        name="norm_matmul_diff",
    )(x, g.reshape(1, D), w)


def _out_proj_kernel(y_ref, w_ref, g_ref, x_ref, o_ref, *, transposed_in):
    y = y_ref[0]
    if transposed_in:
        o = lax.dot_general(y, w_ref[...], (((0,), (0,)), ((), ())),
                            preferred_element_type=F32)
    else:
        o = jnp.dot(y, w_ref[...], preferred_element_type=F32)
    ms = jnp.mean(o * o, axis=-1, keepdims=True)
    o_ref[0] = x_ref[0] + o * lax.rsqrt(ms + RMS_EPS) * g_ref[...]


def _out_proj(y, w, g, x, *, transposed_in):
    B, S, D = x.shape
    Di = w.shape[0]
    ts = OUT_ROWS
    if transposed_in:
        y_spec = pl.BlockSpec((1, Di, ts), lambda b, i: (b, 0, i))
    else:
        y_spec = pl.BlockSpec((1, ts, Di), lambda b, i: (b, i, 0))
    return pl.pallas_call(
        functools.partial(_out_proj_kernel, transposed_in=transposed_in),
        grid=(B, S // ts),
        in_specs=[y_spec,
                  pl.BlockSpec((Di, D), lambda b, i: (0, 0)),
                  pl.BlockSpec((1, D), lambda b, i: (0, 0)),
                  pl.BlockSpec((1, ts, D), lambda b, i: (b, i, 0))],
        out_specs=pl.BlockSpec((1, ts, D), lambda b, i: (b, i, 0)),
        out_shape=jax.ShapeDtypeStruct((B, S, D), F32),
        compiler_params=_compiler_params(("parallel", "parallel")),
        name="out_proj_t" if transposed_in else "out_proj",
    )(y, w, g.reshape(1, D), x)


def _retention_kernel(q_ref, k_ref, v_ref, g_ref, dmat_ref, xi_ref, zeta_ref,
                      cdec_ref, o_ref, state_ref, state_bf_ref):
    C = RET_CHUNK

    @pl.when(pl.program_id(2) == 0)
    def _():
        state_ref[...] = jnp.zeros_like(state_ref)
        state_bf_ref[...] = jnp.zeros_like(state_bf_ref)

    dmat = dmat_ref[0]
    xi = xi_ref[0]
    zeta = zeta_ref[0]
    cdec = cdec_ref[0]

    for ci in range(RET_CHUNKS_PER_STEP):
        rows = pl.ds(ci * C, C)
        q, k, v = q_ref[0, rows, :], k_ref[0, rows, :], v_ref[0, rows, :]
        scores = lax.dot_general(q, k, (((1,), (1,)), ((), ())),
                                 preferred_element_type=F32) * dmat
        cross = jnp.dot(q, state_bf_ref[...], preferred_element_type=F32) * xi
        kz = (k.astype(F32) * zeta).astype(BF16)
        state = state_ref[...] * cdec + lax.dot_general(
            kz, v, (((0,), (0,)), ((), ())), preferred_element_type=F32)
        state_ref[...] = state
        state_bf_ref[...] = state.astype(BF16)
        inner = jnp.dot(scores.astype(BF16), v, preferred_element_type=F32)
        for r0 in range(0, C, RET_NORM_ROWS):
            blk = slice(r0, r0 + RET_NORM_ROWS)
            ret = inner[blk] + cross[blk]
            mu = jnp.mean(ret, axis=-1, keepdims=True)
            cen = ret - mu
            var = jnp.mean(cen * cen, axis=-1, keepdims=True)
            y = cen * lax.rsqrt(var + GN_EPS)
            out_rows = pl.ds(ci * C + r0, RET_NORM_ROWS)
            o_ref[0, out_rows, :] = (
                g_ref[0, out_rows, :].astype(F32) * y).astype(o_ref.dtype)


def _retention(proj, d_model, d_inner):
    B, S, _ = proj.shape
    H, C = RET_HEADS, RET_CHUNK
    dk, dv = d_model // H, d_inner // H
    rows = C * RET_CHUNKS_PER_STEP
    k_off = d_model // dk
    v_off = 2 * d_model // dv
    g_off = (2 * d_model + d_inner) // dv

    log_gamma = jnp.log1p(-jnp.exp2(-5.0 - jnp.arange(H, dtype=F32)))
    idx = jnp.arange(C, dtype=F32)
    rel = idx[:, None] - idx[None, :]
    dmat = jnp.exp(log_gamma[:, None, None] * jnp.maximum(rel, 0.0)) * (rel >= 0)
    xi = jnp.exp(log_gamma[:, None] * (idx + 1.0))
    zeta = jnp.exp(log_gamma[:, None] * (C - 1.0 - idx))
    cdec = jnp.exp(log_gamma * C)
    xi = jnp.broadcast_to(xi[:, :, None], (H, C, dv))
    zeta = jnp.broadcast_to(zeta[:, :, None], (H, C, dk))
    cdec = jnp.broadcast_to(cdec[:, None, None], (H, 1, dv))

    return pl.pallas_call(
        _retention_kernel,
        grid=(B, H, S // rows),
        in_specs=[
            pl.BlockSpec((1, rows, dk), lambda b, h, n: (b, n, h)),
            pl.BlockSpec((1, rows, dk), lambda b, h, n: (b, n, k_off + h)),
            pl.BlockSpec((1, rows, dv), lambda b, h, n: (b, n, v_off + h)),
            pl.BlockSpec((1, rows, dv), lambda b, h, n: (b, n, g_off + h)),
            pl.BlockSpec((1, C, C), lambda b, h, n: (h, 0, 0)),
            pl.BlockSpec((1, C, dv), lambda b, h, n: (h, 0, 0)),
            pl.BlockSpec((1, C, dk), lambda b, h, n: (h, 0, 0)),
            pl.BlockSpec((1, 1, dv), lambda b, h, n: (h, 0, 0)),
        ],
        out_specs=pl.BlockSpec((1, rows, dv), lambda b, h, n: (b, n, h)),
        out_shape=jax.ShapeDtypeStruct((B, S, d_inner), BF16),
        scratch_shapes=[pltpu.VMEM((dk, dv), F32), pltpu.VMEM((dk, dv), BF16)],
        compiler_params=_compiler_params(("parallel", "parallel", "arbitrary")),
        name="retention",
    )(proj, proj, proj, proj, dmat, xi, zeta, cdec)


def _diff_attn_kernel(tab_ref, bfar_ref, q_ref, k_ref, vaug_ref, gt_ref, bkd_ref,
                      lq1_ref, lk1_ref, lq2_ref, lk2_ref, sg_ref, o_ref,
                      qbd_ref, bias_ref, m_ref, acc_ref, sa_ref, sb_ref,
                      *, d, lambda_init):
    T, G, H = ATT_TILE, ATT_HEADS_PER_STEP, DIFF_HEADS
    dv = 2 * d
    hp = pl.program_id(1)
    i = pl.program_id(2)

    @pl.when(i == 0)
    def _():
        rows = lax.broadcasted_iota(jnp.int32, (T, T), 0)
        cols = lax.broadcasted_iota(jnp.int32, (T, T), 1)
        bucket_of_distance = bkd_ref[...]
        for u in range(G):
            h = hp * G + u
            cfar = tab_ref[bfar_ref[0] * H + h]
            by_distance = jnp.full((1, 2 * T), tab_ref[h], F32)
            for bucket in range(1, REL_BUCKETS):
                by_distance = jnp.where(bucket_of_distance >= bucket,
                                        tab_ref[bucket * H + h], by_distance)
            by_distance = (by_distance - cfar) * LOG2E
            toeplitz = pltpu.roll(jnp.broadcast_to(by_distance, (T, 2 * T)),
                                  0, 1, stride=1, stride_axis=0)
            bias_ref[u, 0] = jnp.where(rows <= cols, toeplitz[:, 0:T], -jnp.inf)
            bias_ref[u, 1] = toeplitz[:, T:2 * T]

    def logits_into(dst_ref, w, j, bias_index):
        u = w % G
        start = pl.multiple_of(j * T, T)
        kt = k_ref[0, pl.ds(start, T), u * dv:(u + 1) * dv]
        s = jnp.dot(kt, qbd_ref[w], preferred_element_type=F32)
        if bias_index is None:
            dst_ref[w] = s
        else:
            bias = bias_ref[u, bias_index]
            dst_ref[w, :, 0:T] = s[:, 0:T] + bias
            dst_ref[w, :, T:2 * T] = s[:, T:2 * T] + bias

    def consume(src_ref, w, j):
        u = w % G
        start = pl.multiple_of(j * T, T)
        s = src_ref[w]
        m_old = m_ref[w]
        m_new = jnp.maximum(m_old, jnp.max(s, axis=0, keepdims=True))
        p = jnp.exp2(s - m_new).astype(BF16)
        alpha = jnp.exp2(m_old - m_new)
        pv = jnp.dot(vaug_ref[0, u, :, pl.ds(start, T)], p,
                     preferred_element_type=F32)
        acc_ref[w] = acc_ref[w] * alpha + pv
        m_ref[w] = m_new

    first, second, both = range(G), range(G, 2 * G), range(2 * G)

    def step(dst_ref, j_next, bias_of, src_ref, j_cur, cur_units=both):
        for w in [w for u in range(G) for w in (u, G + u)]:
            logits_into(dst_ref, w, j_next, bias_of(w))
            if w in cur_units:
                consume(src_ref, w, j_cur)

    no_bias = lambda w: None

    def start():
        for w in both:
            sub, u = divmod(w, G)
            q = q_ref[0, sub * T:(sub + 1) * T, u * dv:(u + 1) * dv].astype(F32)
            qt = (q * (d ** -0.5 * LOG2E)).T
            row = lax.broadcasted_iota(jnp.int32, qt.shape, 0)
            qbd_ref[w, :, 0:T] = jnp.where(row < d, qt, 0.0).astype(BF16)
            qbd_ref[w, :, T:2 * T] = jnp.where(row >= d, qt, 0.0).astype(BF16)
        m_ref[...] = jnp.full_like(m_ref, -jnp.inf)
        acc_ref[...] = jnp.zeros_like(acc_ref)
        for w in second:
            logits_into(sa_ref, w, 2 * i + 1, 0)
        step(sb_ref, 2 * i, lambda w: 0 if w in first else 1,
             sa_ref, 2 * i + 1, cur_units=second)

    def finish(src_ref, j_cur):
        lam = (jnp.exp(jnp.sum(lq1_ref[...] * lk1_ref[...], keepdims=True))
               - jnp.exp(jnp.sum(lq2_ref[...] * lk2_ref[...], keepdims=True))
               + lambda_init)
        out_scale = sg_ref[...] * (1.0 - lambda_init)
        for w in both:
            sub, u = divmod(w, G)
            consume(src_ref, w, j_cur)
            acc = acc_ref[w]
            inv_l = 1.0 / acc[dv:dv + 1, :]
            o = (acc[:dv, :T] * inv_l[:, :T]
                 - acc[:dv, T:] * (lam * inv_l[:, T:]))
            ms = jnp.mean(o * o, axis=0, keepdims=True)
            on = (o * lax.rsqrt(ms + GN_EPS)) * out_scale
            head_rows = slice(u * dv, (u + 1) * dv)
            sub_cols = slice(sub * T, (sub + 1) * T)
            gate = gt_ref[0, head_rows, sub_cols].astype(F32)
            o_ref[0, head_rows, sub_cols] = (gate * on).astype(o_ref.dtype)

    @pl.when(i == 0)
    def _():
        start()
        finish(sb_ref, 2 * i)

    @pl.when(i >= 1)
    def _():
        def prologue():
            start()
            step(sa_ref, 2 * i - 1, lambda w: 1 if w in first else None,
                 sb_ref, 2 * i)
            step(sb_ref, 0, no_bias, sa_ref, 2 * i - 1)

        def pair(r):
            step(sa_ref, 2 * r - 1, no_bias, sb_ref, 2 * r - 2)
            step(sb_ref, 2 * r, no_bias, sa_ref, 2 * r - 1)

        n_pairs = i - 1
        n_trips = n_pairs // 4

        def trip(t):
            for e in range(1, 5):
                pair(4 * t + e)

        @pl.when(n_trips == 0)
        def _():
            prologue()

        @pl.when(n_trips >= 1)
        def _():
            prologue()
            trip(0)

        def trip_body(t, carry):
            trip(t)
            return carry

        lax.fori_loop(1, n_trips, trip_body, 0)

        @pl.when(n_pairs % 4 >= 2)
        def _():
            first_left = (n_pairs // 4) * 4 + 1
            pair(first_left)
            pair(first_left + 1)

        @pl.when((i - 1) % 2 == 1)
        def _():
            pair(i - 1)
            finish(sb_ref, 2 * i - 2)

        @pl.when((i - 1) % 2 == 0)
        def _():
            finish(sb_ref, 2 * i - 2)


def _t5_bucket(n):
    max_exact = REL_BUCKETS // 2
    nf = jnp.maximum(n, max_exact).astype(F32)
    large = max_exact + (jnp.log(nf / max_exact) / math.log(REL_MAX_DIST / max_exact)
                         * (REL_BUCKETS - max_exact)).astype(jnp.int32)
    large = jnp.minimum(large, REL_BUCKETS - 1)
    return jnp.where(n < max_exact, n, large)


def _diff_attn(qk, vaug, gt, lq1, lk1, lq2, lk2, subln_g, rel_bias, lambda_init):
    B, S, _ = qk.shape
    H, T, G = DIFF_HEADS, ATT_TILE, ATT_HEADS_PER_STEP
    d_inner = gt.shape[1]
    dv = d_inner // H
    d = dv // 2
    gw = G * dv
    n_groups = H // G

    bkd = _t5_bucket(jnp.arange(2 * T, dtype=jnp.int32)[None, :])
    bfar = _t5_bucket(jnp.full((1,), T + 1, jnp.int32))

    smem = pl.BlockSpec(memory_space=pltpu.SMEM)
    vec = pl.BlockSpec((1, d), lambda b, h, i: (0, 0))
    return pl.pallas_call(
        functools.partial(_diff_attn_kernel, d=d, lambda_init=lambda_init),
        grid=(B, n_groups, S // (2 * T)),
        in_specs=[
            smem, smem,
            pl.BlockSpec((1, 2 * T, gw), lambda b, h, i: (b, i, h)),
            pl.BlockSpec((1, S, gw), lambda b, h, i: (b, 0, n_groups + h)),
            pl.BlockSpec((1, G, dv + ONES_ROWS, S), lambda b, h, i: (b, h, 0, 0)),
            pl.BlockSpec((1, gw, 2 * T), lambda b, h, i: (b, h, i)),
            pl.BlockSpec((1, 2 * T), lambda b, h, i: (0, 0)),
            vec, vec, vec, vec,
            pl.BlockSpec((dv, 1), lambda b, h, i: (0, 0)),
        ],
        out_specs=pl.BlockSpec((1, gw, 2 * T), lambda b, h, i: (b, h, i)),
        out_shape=jax.ShapeDtypeStruct((B, d_inner, S), BF16),
        scratch_shapes=[pltpu.VMEM((2 * G, dv, 2 * T), BF16),
                        pltpu.VMEM((G, 2, T, T), F32),
                        pltpu.VMEM((2 * G, 1, 2 * T), F32),
                        pltpu.VMEM((2 * G, dv + ONES_ROWS, 2 * T), F32),
                        pltpu.VMEM((2 * G, T, 2 * T), F32),
                        pltpu.VMEM((2 * G, T, 2 * T), F32)],
        compiler_params=_compiler_params(("parallel", "arbitrary", "arbitrary")),
        name="diff_attn",
    )(rel_bias.astype(F32).reshape(-1), bfar, qk, qk, vaug, gt, bkd,
      lq1.reshape(1, d), lk1.reshape(1, d), lq2.reshape(1, d), lk2.reshape(1, d),
      subln_g.reshape(dv, 1))


def kernel(x, pre_norm_g, post_norm_g, ret_w_in, ret_w_out, diff_w_in, diff_w_out,
           diff_lambda_q1, diff_lambda_k1, diff_lambda_q2, diff_lambda_k2,
           diff_subln_g, rel_bias):
    d_model = x.shape[-1]
    d_inner = ret_w_out.shape[1]
    for i in range(DEPTH):
        j = i // N_MIXERS
        if i % N_MIXERS == 0:
            proj = _norm_matmul_ret(x, pre_norm_g[i], ret_w_in[j].astype(BF16),
                                    d_inner, RET_HEADS)
            y = _retention(proj, d_model, d_inner)
            x = _out_proj(y, ret_w_out[j].astype(BF16), post_norm_g[i], x,
                          transposed_in=False)
        else:
            lambda_init = 0.8 - 0.6 * math.exp(-0.3 * i)
            qk, vaug, gt = _norm_matmul_diff(
                x, pre_norm_g[i], diff_w_in[j].astype(BF16), DIFF_HEADS)
            yt = _diff_attn(qk, vaug, gt, diff_lambda_q1[j], diff_lambda_k1[j],
                            diff_lambda_q2[j], diff_lambda_k2[j], diff_subln_g[j],
                            rel_bias, lambda_init)
            x = _out_proj(yt, diff_w_out[j].astype(BF16), post_norm_g[i], x,
                          transposed_in=True)
    return x
```
